```python
import math
import jax, jax.numpy as jnp
from jax import lax
import numpy as np

D_MODEL = 2048
BATCH = 8
SEQ = 2048
DEPTH = 4

N_MIXERS = 3
HEAD_DIM = 128
N_HEADS = D_MODEL // HEAD_DIM
ATTN_WIDTH = N_HEADS * HEAD_DIM
BLOCK_Q = 128
D_FF = 4 * D_MODEL
MLA_Q_RANK = 3 * D_MODEL // 8
MLA_KV_RANK = D_MODEL // 4
MLA_NOPE = 128
MLA_ROPE = 64
MLA_V = 128
MLA_QK = MLA_NOPE + MLA_ROPE
ROPE_THETA = 10000.0
EPS = 1e-6
N_SB = (DEPTH + 2) // 3
N_FOX = (DEPTH + 1) // 3
N_MLA = DEPTH // 3

kernel_name = "interleaved_sb_fox_mla_trunk"


def rmsnorm(x, g):
    xf = x.astype(jnp.float32)
    y = xf * lax.rsqrt(jnp.mean(xf * xf, axis=-1, keepdims=True) + EPS)
    return (y * g.astype(jnp.float32)).astype(x.dtype)


def sweep_query_blocks(block_fn, seq_len):
    outs = [block_fn(qs, qs + BLOCK_Q) for qs in range(0, seq_len, BLOCK_Q)]
    return jnp.concatenate(outs, axis=2)


def split_heads_qkv(qkv, b, s):
    t = qkv.reshape(b, s, 3, N_HEADS, HEAD_DIM).transpose(2, 0, 3, 1, 4)
    return t[0], t[1], t[2]


def merge_heads(o):
    b, h, s, d = o.shape
    return o.transpose(0, 2, 1, 3).reshape(b, s, h * d)


def stick_breaking_block(q_blk, k_pre, v_pre, q_start):
    n_q, n_k = q_blk.shape[2], k_pre.shape[2]
    z = jnp.einsum('bhtd,bhsd->bhts', q_blk, k_pre).astype(jnp.float32) * (1.0 / math.sqrt(HEAD_DIM))
    strict = jnp.arange(n_k)[None, :] < (q_start + jnp.arange(n_q))[:, None]
    log_keep = jnp.where(strict, jax.nn.log_sigmoid(-z), 0.0)
    later = lax.cumsum(log_keep, axis=3, reverse=True) - log_keep
    a = jnp.where(strict, jnp.exp(jax.nn.log_sigmoid(z) + later), 0.0)
    return jnp.einsum('bhts,bhsd->bhtd', a.astype(v_pre.dtype), v_pre)


def causal_softmax_block(q_blk, k_pre, v_pre, q_start, scale, bias=None):
    n_q, n_k = q_blk.shape[2], k_pre.shape[2]
    logits = jnp.einsum('bhtd,bhsd->bhts', q_blk, k_pre).astype(jnp.float32) * scale
    if bias is not None:
        logits = logits + bias
    causal = jnp.arange(n_k)[None, :] <= (q_start + jnp.arange(n_q))[:, None]
    p = jax.nn.softmax(jnp.where(causal, logits, -jnp.inf), axis=-1)
    return jnp.einsum('bhts,bhsd->bhtd', p.astype(v_pre.dtype), v_pre)


def stick_breaking_mixer(h, w_in, w_out):
    b, s, _ = h.shape
    q, k, v = split_heads_qkv(h @ w_in, b, s)
    o = sweep_query_blocks(
        lambda qs, qe: stick_breaking_block(q[:, :, qs:qe], k[:, :, :qe], v[:, :, :qe], qs), s)
    return merge_heads(o) @ w_out


def forgetting_mixer(h, w_in, b_f, q_gain, k_gain, w_out):
    b, s, _ = h.shape
    proj = h @ w_in
    q, k, v = split_heads_qkv(proj[..., :3 * ATTN_WIDTH], b, s)
    log_f = jax.nn.log_sigmoid((proj[..., 3 * ATTN_WIDTH:] + b_f).astype(jnp.float32))
    cf = jnp.cumsum(log_f, axis=1).transpose(0, 2, 1)
    q = rmsnorm(q, q_gain)
    k = rmsnorm(k, k_gain)
    scale = 1.0 / math.sqrt(HEAD_DIM)

    def blk(qs, qe):
        bias = cf[:, :, qs:qe, None] - cf[:, :, None, :qe]
        return causal_softmax_block(q[:, :, qs:qe], k[:, :, :qe], v[:, :, :qe], qs, scale, bias)

    return merge_heads(sweep_query_blocks(blk, s)) @ w_out


def rope(x, positions):
    half = x.shape[-1] // 2
    inv_freq = ROPE_THETA ** (-jnp.arange(0, half, dtype=jnp.float32) * 2.0 / x.shape[-1])
    ang = positions.astype(jnp.float32)[..., None] * inv_freq
    if x.ndim == 4:
        ang = ang[:, :, None, :]
    cos, sin = jnp.cos(ang), jnp.sin(ang)
    xf = x.astype(jnp.float32)
    x1, x2 = xf[..., :half], xf[..., half:]
    return jnp.concatenate([x1 * cos - x2 * sin, x1 * sin + x2 * cos], axis=-1).astype(x.dtype)


def mla_mixer(h, positions, w_in, q_norm, kv_norm, w_uq, w_ukv, q_gain, k_gain, w_out):
    b, s, _ = h.shape
    down = h @ w_in
    c_q = rmsnorm(down[..., :MLA_Q_RANK], q_norm)
    c_kv = rmsnorm(down[..., MLA_Q_RANK:MLA_Q_RANK + MLA_KV_RANK], kv_norm)
    k_rope = rope(down[..., MLA_Q_RANK + MLA_KV_RANK:], positions)
    q = (c_q @ w_uq).reshape(b, s, N_HEADS, MLA_QK)
    q = jnp.concatenate([q[..., :MLA_NOPE], rope(q[..., MLA_NOPE:], positions)], axis=-1)
    kv = (c_kv @ w_ukv).reshape(b, s, N_HEADS, MLA_NOPE + MLA_V)
    k = jnp.concatenate(
        [kv[..., :MLA_NOPE], jnp.broadcast_to(k_rope[:, :, None, :], (b, s, N_HEADS, MLA_ROPE))], axis=-1)
    v = kv[..., MLA_NOPE:].transpose(0, 2, 1, 3)
    q = rmsnorm(q, q_gain).transpose(0, 2, 1, 3)
    k = rmsnorm(k, k_gain).transpose(0, 2, 1, 3)
    scale = 1.0 / math.sqrt(MLA_QK)
    o = sweep_query_blocks(
        lambda qs, qe: causal_softmax_block(q[:, :, qs:qe], k[:, :, :qe], v[:, :, :qe], qs, scale), s)
    return merge_heads(o) @ w_out


def sq_relu_mlp(h, w1, w2):
    a = jax.nn.relu(h @ w1)
    return (a * a) @ w2


def _fwd_setup_inputs(seed: int = 0) -> dict:
    key = jax.random.key(seed)
    ks = jax.random.split(key, 24)
    f32 = jnp.float32

    def nrm(k, shape, fan_in):
        return jax.random.normal(k, shape, f32) * (fan_in ** -0.5)

    def gain(k, shape):
        return 1.0 + 0.02 * jax.random.normal(k, shape, f32)

    return {
        "x": jax.random.normal(ks[0], (BATCH, SEQ, D_MODEL), f32),
        "positions": jnp.broadcast_to(jnp.arange(SEQ, dtype=jnp.int32)[None, :], (BATCH, SEQ)),
        "mix_norm": gain(ks[1], (DEPTH, D_MODEL)),
        "mlp_norm": gain(ks[2], (DEPTH, D_MODEL)),
        "sb_w_in": nrm(ks[3], (N_SB, D_MODEL, 3 * ATTN_WIDTH), D_MODEL),
        "sb_w_out": nrm(ks[4], (N_SB, ATTN_WIDTH, D_MODEL), ATTN_WIDTH),
        "fox_w_in": nrm(ks[5], (N_FOX, D_MODEL, 3 * ATTN_WIDTH + N_HEADS), D_MODEL),
        "fox_b_f": jax.random.uniform(ks[6], (N_FOX, N_HEADS), f32, 1.0, 5.0),
        "fox_q_gain": gain(ks[7], (N_FOX, HEAD_DIM)),
        "fox_k_gain": gain(ks[8], (N_FOX, HEAD_DIM)),
        "fox_w_out": nrm(ks[9], (N_FOX, ATTN_WIDTH, D_MODEL), ATTN_WIDTH),
        "mla_w_in": nrm(ks[10], (N_MLA, D_MODEL, MLA_Q_RANK + MLA_KV_RANK + MLA_ROPE), D_MODEL),
        "mla_q_norm": gain(ks[11], (N_MLA, MLA_Q_RANK)),
        "mla_kv_norm": gain(ks[12], (N_MLA, MLA_KV_RANK)),
        "mla_w_uq": nrm(ks[13], (N_MLA, MLA_Q_RANK, N_HEADS * MLA_QK), MLA_Q_RANK),
        "mla_w_ukv": nrm(ks[14], (N_MLA, MLA_KV_RANK, N_HEADS * (MLA_NOPE + MLA_V)), MLA_KV_RANK),
        "mla_q_gain": gain(ks[15], (N_MLA, MLA_QK)),
        "mla_k_gain": gain(ks[16], (N_MLA, MLA_QK)),
        "mla_w_out": nrm(ks[17], (N_MLA, N_HEADS * MLA_V, D_MODEL), N_HEADS * MLA_V),
        "mlp_w1": nrm(ks[18], (DEPTH, D_MODEL, D_FF), D_MODEL),
        "mlp_w2": nrm(ks[19], (DEPTH, D_FF, D_MODEL), D_FF),
    }


def _fwd_reference(x, positions, mix_norm, mlp_norm, sb_w_in, sb_w_out, fox_w_in, fox_b_f, fox_q_gain,
              fox_k_gain, fox_w_out, mla_w_in, mla_q_norm, mla_kv_norm, mla_w_uq, mla_w_ukv,
              mla_q_gain, mla_k_gain, mla_w_out, mlp_w1, mlp_w2):
    for i in range(DEPTH):
        kind, j = i % N_MIXERS, i // N_MIXERS
        h = rmsnorm(x, mix_norm[i])
        if kind == 0:
            y = stick_breaking_mixer(h, sb_w_in[j], sb_w_out[j])
        elif kind == 1:
            y = forgetting_mixer(h, fox_w_in[j], fox_b_f[j], fox_q_gain[j], fox_k_gain[j], fox_w_out[j])
        else:
            y = mla_mixer(h, positions, mla_w_in[j], mla_q_norm[j], mla_kv_norm[j], mla_w_uq[j],
                          mla_w_ukv[j], mla_q_gain[j], mla_k_gain[j], mla_w_out[j])
        x = x + y
        x = x + sq_relu_mlp(rmsnorm(x, mlp_norm[i]), mlp_w1[i], mlp_w2[i])
    return x


import jax as _jax
import jax.numpy as _jnp

TWIN_FORMAT = 'train_step'
FWD_PARAMS = ['x', 'positions', 'mix_norm', 'mlp_norm', 'sb_w_in', 'sb_w_out', 'fox_w_in', 'fox_b_f', 'fox_q_gain', 'fox_k_gain', 'fox_w_out', 'mla_w_in', 'mla_q_norm', 'mla_kv_norm', 'mla_w_uq', 'mla_w_ukv', 'mla_q_gain', 'mla_k_gain', 'mla_w_out', 'mlp_w1', 'mlp_w2']
TWIN_WEIGHTS = ['mix_norm', 'mlp_norm', 'sb_w_in', 'sb_w_out', 'fox_w_in', 'fox_b_f', 'fox_q_gain', 'fox_k_gain', 'fox_w_out', 'mla_w_in', 'mla_q_norm', 'mla_kv_norm', 'mla_w_uq', 'mla_w_ukv', 'mla_q_gain', 'mla_k_gain', 'mla_w_out', 'mlp_w1', 'mlp_w2']
TWIN_DIFF_INPUT = 'x'
TWIN_INPUTS = ['x', 'positions', 'mix_norm', 'mlp_norm', 'sb_w_in', 'sb_w_out', 'fox_w_in', 'fox_b_f', 'fox_q_gain', 'fox_k_gain', 'fox_w_out', 'mla_w_in', 'mla_q_norm', 'mla_kv_norm', 'mla_w_uq', 'mla_w_ukv', 'mla_q_gain', 'mla_k_gain', 'mla_w_out', 'mlp_w1', 'mlp_w2', 'loss_target', 'm_mix_norm', 'm_mlp_norm', 'm_sb_w_in', 'm_sb_w_out', 'm_fox_w_in', 'm_fox_b_f', 'm_fox_q_gain', 'm_fox_k_gain', 'm_fox_w_out', 'm_mla_w_in', 'm_mla_q_norm', 'm_mla_kv_norm', 'm_mla_w_uq', 'm_mla_w_ukv', 'm_mla_q_gain', 'm_mla_k_gain', 'm_mla_w_out', 'm_mlp_w1', 'm_mlp_w2', 'v_mix_norm', 'v_mlp_norm', 'v_sb_w_in', 'v_sb_w_out', 'v_fox_w_in', 'v_fox_b_f', 'v_fox_q_gain', 'v_fox_k_gain', 'v_fox_w_out', 'v_mla_w_in', 'v_mla_q_norm', 'v_mla_kv_norm', 'v_mla_w_uq', 'v_mla_w_ukv', 'v_mla_q_gain', 'v_mla_k_gain', 'v_mla_w_out', 'v_mlp_w1', 'v_mlp_w2']
TWIN_OUTPUTS = ['loss', 'grad_x', 'grad_mix_norm', 'grad_mlp_norm', 'grad_sb_w_in', 'grad_sb_w_out', 'grad_fox_w_in', 'grad_fox_b_f', 'grad_fox_q_gain', 'grad_fox_k_gain', 'grad_fox_w_out', 'grad_mla_w_in', 'grad_mla_q_norm', 'grad_mla_kv_norm', 'grad_mla_w_uq', 'grad_mla_w_ukv', 'grad_mla_q_gain', 'grad_mla_k_gain', 'grad_mla_w_out', 'grad_mlp_w1', 'grad_mlp_w2', 'delta_mix_norm', 'delta_mlp_norm', 'delta_sb_w_in', 'delta_sb_w_out', 'delta_fox_w_in', 'delta_fox_b_f', 'delta_fox_q_gain', 'delta_fox_k_gain', 'delta_fox_w_out', 'delta_mla_w_in', 'delta_mla_q_norm', 'delta_mla_kv_norm', 'delta_mla_w_uq', 'delta_mla_w_ukv', 'delta_mla_q_gain', 'delta_mla_k_gain', 'delta_mla_w_out', 'delta_mlp_w1', 'delta_mlp_w2', 'new_m_mix_norm', 'new_m_mlp_norm', 'new_m_sb_w_in', 'new_m_sb_w_out', 'new_m_fox_w_in', 'new_m_fox_b_f', 'new_m_fox_q_gain', 'new_m_fox_k_gain', 'new_m_fox_w_out', 'new_m_mla_w_in', 'new_m_mla_q_norm', 'new_m_mla_kv_norm', 'new_m_mla_w_uq', 'new_m_mla_w_ukv', 'new_m_mla_q_gain', 'new_m_mla_k_gain', 'new_m_mla_w_out', 'new_m_mlp_w1', 'new_m_mlp_w2', 'new_v_mix_norm', 'new_v_mlp_norm', 'new_v_sb_w_in', 'new_v_sb_w_out', 'new_v_fox_w_in', 'new_v_fox_b_f', 'new_v_fox_q_gain', 'new_v_fox_k_gain', 'new_v_fox_w_out', 'new_v_mla_w_in', 'new_v_mla_q_norm', 'new_v_mla_kv_norm', 'new_v_mla_w_uq', 'new_v_mla_w_ukv', 'new_v_mla_q_gain', 'new_v_mla_k_gain', 'new_v_mla_w_out', 'new_v_mlp_w1', 'new_v_mlp_w2']
TWIN_LEAF_KINDS = {'loss': 'loss', 'grad_x': 'grad_x', 'grad_mix_norm': 'grad_w', 'grad_mlp_norm': 'grad_w', 'grad_sb_w_in': 'grad_w', 'grad_sb_w_out': 'grad_w', 'grad_fox_w_in': 'grad_w', 'grad_fox_b_f': 'grad_w', 'grad_fox_q_gain': 'grad_w', 'grad_fox_k_gain': 'grad_w', 'grad_fox_w_out': 'grad_w', 'grad_mla_w_in': 'grad_w', 'grad_mla_q_norm': 'grad_w', 'grad_mla_kv_norm': 'grad_w', 'grad_mla_w_uq': 'grad_w', 'grad_mla_w_ukv': 'grad_w', 'grad_mla_q_gain': 'grad_w', 'grad_mla_k_gain': 'grad_w', 'grad_mla_w_out': 'grad_w', 'grad_mlp_w1': 'grad_w', 'grad_mlp_w2': 'grad_w', 'delta_mix_norm': 'delta_w', 'delta_mlp_norm': 'delta_w', 'delta_sb_w_in': 'delta_w', 'delta_sb_w_out': 'delta_w', 'delta_fox_w_in': 'delta_w', 'delta_fox_b_f': 'delta_w', 'delta_fox_q_gain': 'delta_w', 'delta_fox_k_gain': 'delta_w', 'delta_fox_w_out': 'delta_w', 'delta_mla_w_in': 'delta_w', 'delta_mla_q_norm': 'delta_w', 'delta_mla_kv_norm': 'delta_w', 'delta_mla_w_uq': 'delta_w', 'delta_mla_w_ukv': 'delta_w', 'delta_mla_q_gain': 'delta_w', 'delta_mla_k_gain': 'delta_w', 'delta_mla_w_out': 'delta_w', 'delta_mlp_w1': 'delta_w', 'delta_mlp_w2': 'delta_w', 'new_m_mix_norm': 'new_m', 'new_m_mlp_norm': 'new_m', 'new_m_sb_w_in': 'new_m', 'new_m_sb_w_out': 'new_m', 'new_m_fox_w_in': 'new_m', 'new_m_fox_b_f': 'new_m', 'new_m_fox_q_gain': 'new_m', 'new_m_fox_k_gain': 'new_m', 'new_m_fox_w_out': 'new_m', 'new_m_mla_w_in': 'new_m', 'new_m_mla_q_norm': 'new_m', 'new_m_mla_kv_norm': 'new_m', 'new_m_mla_w_uq': 'new_m', 'new_m_mla_w_ukv': 'new_m', 'new_m_mla_q_gain': 'new_m', 'new_m_mla_k_gain': 'new_m', 'new_m_mla_w_out': 'new_m', 'new_m_mlp_w1': 'new_m', 'new_m_mlp_w2': 'new_m', 'new_v_mix_norm': 'new_v', 'new_v_mlp_norm': 'new_v', 'new_v_sb_w_in': 'new_v', 'new_v_sb_w_out': 'new_v', 'new_v_fox_w_in': 'new_v', 'new_v_fox_b_f': 'new_v', 'new_v_fox_q_gain': 'new_v', 'new_v_fox_k_gain': 'new_v', 'new_v_fox_w_out': 'new_v', 'new_v_mla_w_in': 'new_v', 'new_v_mla_q_norm': 'new_v', 'new_v_mla_kv_norm': 'new_v', 'new_v_mla_w_uq': 'new_v', 'new_v_mla_w_ukv': 'new_v', 'new_v_mla_q_gain': 'new_v', 'new_v_mla_k_gain': 'new_v', 'new_v_mla_w_out': 'new_v', 'new_v_mlp_w1': 'new_v', 'new_v_mlp_w2': 'new_v'}


def _forward(args):
    return _fwd_reference(*[args[k] for k in FWD_PARAMS])


def _output_shape():
    out = _jax.eval_shape(lambda: _forward(_fwd_setup_inputs(0)))
    return out.shape, out.dtype

N_MICROBATCH = 1
ADAM_LR = 0.001
ADAM_B1 = 0.9
ADAM_B2 = 0.999
ADAM_EPS = 1e-08
ADAM_WD = 0.01
ADAM_STEP = 10
PER_EXAMPLE_BATCH_AXIS = {'x': 0, 'positions': 0, 'loss_target': 0}
SHARED_INPUTS = []
_WEIGHT_DTYPES = {'mix_norm': _jnp.float32, 'mlp_norm': _jnp.float32, 'sb_w_in': _jnp.float32, 'sb_w_out': _jnp.float32, 'fox_w_in': _jnp.float32, 'fox_b_f': _jnp.float32, 'fox_q_gain': _jnp.float32, 'fox_k_gain': _jnp.float32, 'fox_w_out': _jnp.float32, 'mla_w_in': _jnp.float32, 'mla_q_norm': _jnp.float32, 'mla_kv_norm': _jnp.float32, 'mla_w_uq': _jnp.float32, 'mla_w_ukv': _jnp.float32, 'mla_q_gain': _jnp.float32, 'mla_k_gain': _jnp.float32, 'mla_w_out': _jnp.float32, 'mlp_w1': _jnp.float32, 'mlp_w2': _jnp.float32}
MOMENT_SCALE = {'mix_norm': 6.176773e+00, 'mlp_norm': 2.567061e+01, 'sb_w_in': 2.228990e+00, 'sb_w_out': 3.538603e+00, 'fox_w_in': 3.349129e+00, 'fox_b_f': 2.932106e+01, 'fox_q_gain': 2.935149e+00, 'fox_k_gain': 2.929030e+00, 'fox_w_out': 5.932490e+00, 'mla_w_in': 7.551298e+00, 'mla_q_norm': 2.470593e-01, 'mla_kv_norm': 1.340645e+01, 'mla_w_uq': 1.429121e-01, 'mla_w_ukv': 3.997841e+00, 'mla_q_gain': 5.583010e-01, 'mla_k_gain': 5.565171e-01, 'mla_w_out': 5.455371e+00, 'mlp_w1': 2.519122e+00, 'mlp_w2': 9.492600e+00}


def _to_microbatches(a, axis):
    t = _jnp.moveaxis(a, axis, 0)
    t = t.reshape((N_MICROBATCH, t.shape[0] // N_MICROBATCH) + t.shape[1:])
    return _jnp.moveaxis(t, 1, axis + 1)


def setup_inputs(seed: int = 0) -> dict:
    inp = _fwd_setup_inputs(seed)
    key = _jax.random.fold_in(_jax.random.key(seed), 7919)
    shape, _ = _output_shape()
    out = dict(inp)
    out["loss_target"] = _jax.random.normal(_jax.random.fold_in(key, 0), shape, _jnp.float32)
    for i, name in enumerate(TWIN_WEIGHTS):
        w = inp[name].astype(_jnp.float32)
        if MOMENT_SCALE is None:
            s = _jnp.sqrt(_jnp.mean(_jnp.square(w)) + 1e-30)
        else:
            s = MOMENT_SCALE[name]
        km, kv = _jax.random.split(_jax.random.fold_in(key, i + 1))
        out[name] = w
        out["m_" + name] = s * _jax.random.normal(km, w.shape, _jnp.float32)
        out["v_" + name] = (s * s) * _jax.random.uniform(kv, w.shape, _jnp.float32, 0.5, 1.5)
    if N_MICROBATCH > 1:
        for name, axis in PER_EXAMPLE_BATCH_AXIS.items():
            out[name] = _to_microbatches(out[name], axis)
    return {'x': out['x'], 'positions': out['positions'], 'mix_norm': out['mix_norm'], 'mlp_norm': out['mlp_norm'], 'sb_w_in': out['sb_w_in'], 'sb_w_out': out['sb_w_out'], 'fox_w_in': out['fox_w_in'], 'fox_b_f': out['fox_b_f'], 'fox_q_gain': out['fox_q_gain'], 'fox_k_gain': out['fox_k_gain'], 'fox_w_out': out['fox_w_out'], 'mla_w_in': out['mla_w_in'], 'mla_q_norm': out['mla_q_norm'], 'mla_kv_norm': out['mla_kv_norm'], 'mla_w_uq': out['mla_w_uq'], 'mla_w_ukv': out['mla_w_ukv'], 'mla_q_gain': out['mla_q_gain'], 'mla_k_gain': out['mla_k_gain'], 'mla_w_out': out['mla_w_out'], 'mlp_w1': out['mlp_w1'], 'mlp_w2': out['mlp_w2'], 'loss_target': out['loss_target'], 'm_mix_norm': out['m_mix_norm'], 'm_mlp_norm': out['m_mlp_norm'], 'm_sb_w_in': out['m_sb_w_in'], 'm_sb_w_out': out['m_sb_w_out'], 'm_fox_w_in': out['m_fox_w_in'], 'm_fox_b_f': out['m_fox_b_f'], 'm_fox_q_gain': out['m_fox_q_gain'], 'm_fox_k_gain': out['m_fox_k_gain'], 'm_fox_w_out': out['m_fox_w_out'], 'm_mla_w_in': out['m_mla_w_in'], 'm_mla_q_norm': out['m_mla_q_norm'], 'm_mla_kv_norm': out['m_mla_kv_norm'], 'm_mla_w_uq': out['m_mla_w_uq'], 'm_mla_w_ukv': out['m_mla_w_ukv'], 'm_mla_q_gain': out['m_mla_q_gain'], 'm_mla_k_gain': out['m_mla_k_gain'], 'm_mla_w_out': out['m_mla_w_out'], 'm_mlp_w1': out['m_mlp_w1'], 'm_mlp_w2': out['m_mlp_w2'], 'v_mix_norm': out['v_mix_norm'], 'v_mlp_norm': out['v_mlp_norm'], 'v_sb_w_in': out['v_sb_w_in'], 'v_sb_w_out': out['v_sb_w_out'], 'v_fox_w_in': out['v_fox_w_in'], 'v_fox_b_f': out['v_fox_b_f'], 'v_fox_q_gain': out['v_fox_q_gain'], 'v_fox_k_gain': out['v_fox_k_gain'], 'v_fox_w_out': out['v_fox_w_out'], 'v_mla_w_in': out['v_mla_w_in'], 'v_mla_q_norm': out['v_mla_q_norm'], 'v_mla_kv_norm': out['v_mla_kv_norm'], 'v_mla_w_uq': out['v_mla_w_uq'], 'v_mla_w_ukv': out['v_mla_w_ukv'], 'v_mla_q_gain': out['v_mla_q_gain'], 'v_mla_k_gain': out['v_mla_k_gain'], 'v_mla_w_out': out['v_mla_w_out'], 'v_mlp_w1': out['v_mlp_w1'], 'v_mlp_w2': out['v_mlp_w2']}


def _loss(weights, diff, rest, loss_target):
    with _jax.named_scope("forward"):
        args = {**rest, TWIN_DIFF_INPUT: diff, **{k: w.astype(_WEIGHT_DTYPES[k]) for k, w in weights.items()}}
        y = _forward(args)
    with _jax.named_scope("loss_head"):
        err = _jnp.square(y.astype(_jnp.float32) - loss_target)
        return 0.5 * _jnp.sum(_jnp.mean(err, axis=-1)) if err.ndim else 0.5 * err


def _adamw(w, g, m, v):
    m = ADAM_B1 * m + (1.0 - ADAM_B1) * g
    v = ADAM_B2 * v + (1.0 - ADAM_B2) * _jnp.square(g)
    m_hat = m / (1.0 - ADAM_B1 ** ADAM_STEP)
    v_hat = v / (1.0 - ADAM_B2 ** ADAM_STEP)
    delta = -ADAM_LR * (m_hat / (_jnp.sqrt(v_hat) + ADAM_EPS) + ADAM_WD * w)
    return delta, m, v


def reference(x, positions, mix_norm, mlp_norm, sb_w_in, sb_w_out, fox_w_in, fox_b_f, fox_q_gain, fox_k_gain, fox_w_out, mla_w_in, mla_q_norm, mla_kv_norm, mla_w_uq, mla_w_ukv, mla_q_gain, mla_k_gain, mla_w_out, mlp_w1, mlp_w2, loss_target, m_mix_norm, m_mlp_norm, m_sb_w_in, m_sb_w_out, m_fox_w_in, m_fox_b_f, m_fox_q_gain, m_fox_k_gain, m_fox_w_out, m_mla_w_in, m_mla_q_norm, m_mla_kv_norm, m_mla_w_uq, m_mla_w_ukv, m_mla_q_gain, m_mla_k_gain, m_mla_w_out, m_mlp_w1, m_mlp_w2, v_mix_norm, v_mlp_norm, v_sb_w_in, v_sb_w_out, v_fox_w_in, v_fox_b_f, v_fox_q_gain, v_fox_k_gain, v_fox_w_out, v_mla_w_in, v_mla_q_norm, v_mla_kv_norm, v_mla_w_uq, v_mla_w_ukv, v_mla_q_gain, v_mla_k_gain, v_mla_w_out, v_mlp_w1, v_mlp_w2):
    given = dict(x=x, positions=positions, mix_norm=mix_norm, mlp_norm=mlp_norm, sb_w_in=sb_w_in, sb_w_out=sb_w_out, fox_w_in=fox_w_in, fox_b_f=fox_b_f, fox_q_gain=fox_q_gain, fox_k_gain=fox_k_gain, fox_w_out=fox_w_out, mla_w_in=mla_w_in, mla_q_norm=mla_q_norm, mla_kv_norm=mla_kv_norm, mla_w_uq=mla_w_uq, mla_w_ukv=mla_w_ukv, mla_q_gain=mla_q_gain, mla_k_gain=mla_k_gain, mla_w_out=mla_w_out, mlp_w1=mlp_w1, mlp_w2=mlp_w2, loss_target=loss_target, m_mix_norm=m_mix_norm, m_mlp_norm=m_mlp_norm, m_sb_w_in=m_sb_w_in, m_sb_w_out=m_sb_w_out, m_fox_w_in=m_fox_w_in, m_fox_b_f=m_fox_b_f, m_fox_q_gain=m_fox_q_gain, m_fox_k_gain=m_fox_k_gain, m_fox_w_out=m_fox_w_out, m_mla_w_in=m_mla_w_in, m_mla_q_norm=m_mla_q_norm, m_mla_kv_norm=m_mla_kv_norm, m_mla_w_uq=m_mla_w_uq, m_mla_w_ukv=m_mla_w_ukv, m_mla_q_gain=m_mla_q_gain, m_mla_k_gain=m_mla_k_gain, m_mla_w_out=m_mla_w_out, m_mlp_w1=m_mlp_w1, m_mlp_w2=m_mlp_w2, v_mix_norm=v_mix_norm, v_mlp_norm=v_mlp_norm, v_sb_w_in=v_sb_w_in, v_sb_w_out=v_sb_w_out, v_fox_w_in=v_fox_w_in, v_fox_b_f=v_fox_b_f, v_fox_q_gain=v_fox_q_gain, v_fox_k_gain=v_fox_k_gain, v_fox_w_out=v_fox_w_out, v_mla_w_in=v_mla_w_in, v_mla_q_norm=v_mla_q_norm, v_mla_kv_norm=v_mla_kv_norm, v_mla_w_uq=v_mla_w_uq, v_mla_w_ukv=v_mla_w_ukv, v_mla_q_gain=v_mla_q_gain, v_mla_k_gain=v_mla_k_gain, v_mla_w_out=v_mla_w_out, v_mlp_w1=v_mlp_w1, v_mlp_w2=v_mlp_w2)
    weights = {n: given[n] for n in TWIN_WEIGHTS}
    shared = {n: given[n] for n in SHARED_INPUTS}
    per_example = {n: given[n] for n in ['x', 'positions']}
    grad_fn = _jax.value_and_grad(_loss, argnums=(0, 1))

    def one_microbatch(ex, loss_target):
        ex = dict(ex)
        diff = ex.pop(TWIN_DIFF_INPUT)
        return grad_fn(weights, diff, {**shared, **ex}, loss_target)

    if N_MICROBATCH == 1:
        loss, (grad_w, grad_x) = one_microbatch(per_example, given["loss_target"])
    else:
        def body(carry, xs):
            loss_sum, grad_sum = carry
            l_k, (gw_k, gx_k) = one_microbatch(xs[0], xs[1])
            with _jax.named_scope("update"):
                return (loss_sum + l_k, _jax.tree.map(_jnp.add, grad_sum, gw_k)), gx_k

        init = (_jnp.zeros((), _jnp.float32), _jax.tree.map(_jnp.zeros_like, weights))
        (loss, grad_w), grad_x = _jax.lax.scan(body, init, (per_example, given["loss_target"]))
    with _jax.named_scope("update"):
        delta_w, new_m, new_v = {}, {}, {}
        for n in TWIN_WEIGHTS:
            delta_w[n], new_m[n], new_v[n] = _adamw(weights[n], grad_w[n], given["m_" + n], given["v_" + n])
    return (loss, grad_x, *[grad_w[n] for n in TWIN_WEIGHTS], *[delta_w[n] for n in TWIN_WEIGHTS],
            *[new_m[n] for n in TWIN_WEIGHTS], *[new_v[n] for n in TWIN_WEIGHTS])
```

```python
import functools
import math

import numpy as np
import jax
import jax.numpy as jnp
from jax import lax
from jax.experimental import pallas as pl
from jax.experimental.pallas import tpu as pltpu

F32 = jnp.float32
BF16 = jnp.bfloat16
MESH = pl.DeviceIdType.MESH

EPS = 1e-6
HEAD_DIM = 128
MLA_NOPE = 128
MLA_ROPE = 64
MLA_V = 128
MLA_QK = MLA_NOPE + MLA_ROPE
MLA_QK_PAD = 256
ROPE_THETA = 10000.0
N_MIXERS = 3
ADAM_LR = 0.001
ADAM_B1 = 0.9
ADAM_B2 = 0.999
ADAM_EPS = 1e-08
ADAM_WD = 0.01
ADAM_STEP = 10

LANE = 128
N_CHIPS = 4
VMEM_LIMIT = 48 * 1024 * 1024
NEG = -1e30


def _cp(sem):
    return pltpu.CompilerParams(dimension_semantics=sem, vmem_limit_bytes=VMEM_LIMIT)


def _pick(dim, cap):
    best = None
    b = LANE
    while b <= min(dim, cap):
        if dim % b == 0:
            best = b
        b += LANE
    return best if best is not None else dim


def _pick_rows(dim, cap):
    b = min(dim, cap)
    while dim % b:
        b -= 8
    return b


def _matmul(name, a, b, a_blk, a_map, b_blk, b_map, dn, grid, acc_shape, outs, extras=(), epilogue=None):
    nk = grid[2]
    n_ex, n_out = len(extras), len(outs)

    def body(*refs):
        a_ref, b_ref = refs[0], refs[1]
        ex_refs = refs[2:2 + n_ex]
        out_refs = refs[2 + n_ex:2 + n_ex + n_out]
        acc = refs[-1]
        k = pl.program_id(2)

        @pl.when(k == 0)
        def _():
            acc[...] = jnp.zeros_like(acc)

        acc[...] += lax.dot_general(a_ref[...], b_ref[...], dn, preferred_element_type=F32)

        @pl.when(k == nk - 1)
        def _():
            res = acc[...]
            vals = epilogue(res, *[e[...] for e in ex_refs]) if epilogue is not None else (res,)
            for o, v in zip(out_refs, vals):
                o[...] = v.astype(o.dtype)

    in_specs = [pl.BlockSpec(a_blk, a_map), pl.BlockSpec(b_blk, b_map)]
    in_specs += [pl.BlockSpec(blk, lambda i, j, k, m=m: m(i, j)) for (_, blk, m) in extras]
    out_specs = [pl.BlockSpec(blk, lambda i, j, k, m=m: m(i, j)) for (_, _, blk, m) in outs]
    out_shape = [jax.ShapeDtypeStruct(s, d) for (s, d, _, _) in outs]
    res = pl.pallas_call(
        body, name=name, grid=grid, in_specs=in_specs, out_specs=out_specs, out_shape=out_shape,
        scratch_shapes=[pltpu.VMEM(acc_shape, F32)],
        compiler_params=_cp(("parallel", "parallel", "arbitrary")),
    )(a, b, *[e[0] for e in extras])
    return res


BM, BN, BK = 1024, 1024, 512


def mm_nn(name, a, w, out_dtype, epilogue=None, extras=(), n_out=1, out_dtypes=None):
    M, K = a.shape
    stacked = w.ndim == 3
    n4 = w.shape[-1]
    N = n4 * (N_CHIPS if stacked else 1)
    bm, bn, bk = _pick_rows(M, BM), _pick(n4, BN), _pick(K, BK)
    nb = n4 // bn
    if stacked:
        b_blk, b_map = (None, bk, bn), (lambda i, j, k: (j // nb, k, j % nb))
    else:
        b_blk, b_map = (bk, bn), (lambda i, j, k: (k, j))
    dts = out_dtypes if out_dtypes is not None else [out_dtype] * n_out
    outs = [((M, N), d, (bm, bn), lambda i, j: (i, j)) for d in dts]
    exs = [(e, (bm, bn), lambda i, j: (i, j)) for e in extras]
    return _matmul(name, a, w, (bm, bk), lambda i, j, k: (i, k), b_blk, b_map,
                   (((1,), (0,)), ((), ())), (M // bm, N // bn, K // bk), (bm, bn), outs, exs, epilogue)


def mm_nt(name, dy, w, out_dtype, epilogue=None, extras=()):
    M, N = dy.shape
    stacked = w.ndim == 3
    K, n4 = w.shape[-2], w.shape[-1]
    bm, bn, bk = _pick_rows(M, BM), _pick(K, BN), _pick(n4, BK)
    nb = n4 // bk
    if stacked:
        b_blk, b_map = (None, bn, bk), (lambda i, j, k: (k // nb, j, k % nb))
    else:
        b_blk, b_map = (bn, bk), (lambda i, j, k: (j, k))
    outs = [((M, K), out_dtype, (bm, bn), lambda i, j: (i, j))]
    exs = [(e, (bm, bn), lambda i, j: (i, j)) for e in extras]
    return _matmul(name, dy, w, (bm, bk), lambda i, j, k: (i, k), b_blk, b_map,
                   (((1,), (1,)), ((), ())), (M // bm, K // bn, N // bk), (bm, bn), outs, exs, epilogue)[0]


def mm_tn(name, a, dy, out_dtype, stacked):
    M, K = a.shape
    N = dy.shape[1]
    n4 = N // N_CHIPS if stacked else N
    bm, bn, bk = _pick(K, BM), _pick(n4, BN), _pick_rows(M, BK)
    nb = n4 // bn
    if stacked:
        outs = [((N_CHIPS, K, n4), out_dtype, (None, bm, bn), lambda i, j: (j // nb, i, j % nb))]
    else:
        outs = [((K, N), out_dtype, (bm, bn), lambda i, j: (i, j))]
    return _matmul(name, a, dy, (bk, bm), lambda i, j, k: (k, i), (bk, bn), lambda i, j, k: (k, j),
                   (((0,), (0,)), ((), ())), (K // bm, N // bn, M // bk), (bm, bn), outs)[0]


def _split3(x):
    hi = x.astype(BF16)
    r = x - hi.astype(F32)
    mid = r.astype(BF16)
    lo = (r - mid.astype(F32)).astype(BF16)
    return hi, mid, lo


def _exact_dot(x, m, n):
    out = None
    for p in _split3(x)[:n]:
        d = jnp.dot(p, m, preferred_element_type=F32)
        out = d if out is None else out + d
    return out


def _row_spec(tr, width, cmap):
    return pl.BlockSpec((tr, width), lambda i, h: (i, cmap(h)))


def _full_spec(p):
    return pl.BlockSpec(p.shape, lambda i, h: (0,) * p.ndim)


def rowwise_fwd(name, fn, rows, params, outs, tr, nh=1):
    S = rows[0][0].shape[0]
    nr, npar = len(rows), len(params)

    def body(*refs):
        vals = fn(*[r[...].astype(F32) for r in refs[:nr]], *[p[...] for p in refs[nr:nr + npar]])
        for o, v in zip(refs[nr + npar:], vals):
            o[...] = v.astype(o.dtype)

    return pl.pallas_call(
        body, name=name, grid=(S // tr, nh),
        in_specs=[_row_spec(tr, w, cm) for (_, w, cm) in rows] + [_full_spec(p) for p in params],
        out_specs=[_row_spec(tr, w, cm) for (_, w, cm, _) in outs],
        out_shape=[jax.ShapeDtypeStruct((S, c), d) for (c, _, _, d) in outs],
        compiler_params=_cp(("parallel", "arbitrary")),
    )(*[r[0] for r in rows], *params)


def rowwise_bwd(name, fn, rows, params, cts, grads, tr, nh=1, n_diff=None, add=None):
    S = rows[0][0].shape[0]
    nr, npar, nct = len(rows), len(params), len(cts)
    n_diff = nr if n_diff is None else n_diff
    n_add = 1 if add is not None else 0

    def body(*refs):
        row_refs = refs[:nr]
        par_refs = refs[nr:nr + npar]
        ct_refs = refs[nr + npar:nr + npar + nct]
        add_refs = refs[nr + npar + nct:nr + npar + nct + n_add]
        o = nr + npar + nct + n_add
        g_refs = refs[o:o + n_diff]
        cp_refs = refs[o + n_diff:o + n_diff + n_add]
        pg_refs = refs[o + n_diff + n_add:]
        i, h = pl.program_id(0), pl.program_id(1)
        rv = [r[...].astype(F32) for r in row_refs]
        pv = [p[...] for p in par_refs]
        aux = rv[n_diff:]

        def f(*dp):
            return fn(*dp[:n_diff], *aux, *dp[n_diff:])

        _, vjp = jax.vjp(f, *rv[:n_diff], *pv)
        gs = vjp(tuple(c[...].astype(F32) for c in ct_refs))
        for n, (g_ref, (_, _, _, _, over)) in enumerate(zip(g_refs, grads)):
            g = gs[n]
            if n == 0 and add is not None:
                g = g + add_refs[0][...]
                cp_refs[0][...] = g.astype(BF16)
            if over:
                @pl.when(h == 0)
                def _(g_ref=g_ref):
                    g_ref[...] = jnp.zeros_like(g_ref)
                g_ref[...] += g.astype(g_ref.dtype)
            else:
                g_ref[...] = g.astype(g_ref.dtype)
        for pg_ref, g in zip(pg_refs, gs[n_diff:]):
            @pl.when((i == 0) & (h == 0))
            def _(pg_ref=pg_ref):
                pg_ref[...] = jnp.zeros_like(pg_ref)
            pg_ref[...] += g

    in_specs = [_row_spec(tr, w, cm) for (_, w, cm) in rows] + [_full_spec(p) for p in params]
    in_specs += [_row_spec(tr, w, cm) for (_, w, cm) in cts]
    operands = [r[0] for r in rows] + list(params) + [c[0] for c in cts]
    out_specs = [_row_spec(tr, w, cm) for (_, w, cm, _, _) in grads]
    out_shape = [jax.ShapeDtypeStruct((S, c), d) for (c, _, _, d, _) in grads]
    if add is not None:
        in_specs.append(_row_spec(tr, add[1], add[2]))
        operands.append(add[0])
        out_specs.append(_row_spec(tr, add[1], add[2]))
        out_shape.append(jax.ShapeDtypeStruct(add[0].shape, BF16))
    out_specs += [_full_spec(p) for p in params]
    out_shape += [jax.ShapeDtypeStruct(p.shape, F32) for p in params]
    return pl.pallas_call(
        body, name=name, grid=(S // tr, nh), in_specs=in_specs, out_specs=out_specs, out_shape=out_shape,
        compiler_params=_cp(("arbitrary", "arbitrary")),
    )(*operands)


def _c0(h):
    return 0


def _rms(x, g, n):
    return x * lax.rsqrt(jnp.sum(x * x, axis=-1, keepdims=True) * (1.0 / n) + EPS) * g


def _rmsnorm_fn(d):
    def fn(x, g):
        return (_rms(x, g, d),)
    return fn


def rmsnorm_fwd(name, x, g):
    S, D = x.shape
    return rowwise_fwd(name, _rmsnorm_fn(D), [(x, D, _c0)], [g], [(D, D, _c0, BF16)], _pick_rows(S, 256))[0]


def rmsnorm_bwd(name, x, g, dh, dres):
    S, D = x.shape
    return rowwise_bwd(name, _rmsnorm_fn(D), [(x, D, _c0)], [g], [(dh, D, _c0)], [(D, D, _c0, F32, False)],
                       _pick_rows(S, 256), add=(dres, D, _c0))


def loss_head(y, t):
    S, D = y.shape
    tr = _pick_rows(S, 256)

    def body(y_ref, t_ref, l_ref, d_ref, db_ref):
        @pl.when(pl.program_id(0) == 0)
        def _():
            l_ref[...] = jnp.zeros_like(l_ref)
        e = y_ref[...] - t_ref[...]
        l_ref[...] += 0.5 * jnp.sum(jnp.sum(e * e, axis=1, keepdims=True), axis=0, keepdims=True) * (1.0 / D)
        d = e * (1.0 / D)
        d_ref[...] = d
        db_ref[...] = d.astype(BF16)

    row = pl.BlockSpec((tr, D), lambda i: (i, 0))
    return pl.pallas_call(
        body, name="loss_head", grid=(S // tr,), in_specs=[row, row],
        out_specs=[pl.BlockSpec((1, 1), lambda i: (0, 0)), row, row],
        out_shape=[jax.ShapeDtypeStruct((1, 1), F32), jax.ShapeDtypeStruct((S, D), F32), jax.ShapeDtypeStruct((S, D), BF16)],
        compiler_params=_cp(("arbitrary",)),
    )(y, t)


def _dot_nt(a, b):
    return lax.dot_general(a, b, (((1,), (1,)), ((), ())), preferred_element_type=F32)


def _dot_tn(a, b):
    return lax.dot_general(a, b, (((0,), (0,)), ((), ())), preferred_element_type=F32)


def attn_fwd(name, q, k, v, nh, dq, qoff, koff, voff, vstep, scale, cf_col=None, cf_row=None):
    S = q.shape[0]
    tq = _pick_rows(S, 256)
    tk = tq
    nq = S // tq
    bias = cf_col is not None

    def body(*refs):
        if bias:
            q_ref, k_ref, v_ref, cfc_ref, cfr_ref, o_ref, of_ref, lse_ref = refs
        else:
            q_ref, k_ref, v_ref, o_ref, of_ref, lse_ref = refs
        i = pl.program_id(1)
        qv = q_ref[...].astype(BF16)
        row = i * tq + lax.broadcasted_iota(jnp.int32, (tq, tk), 0)
        coli = lax.broadcasted_iota(jnp.int32, (tq, tk), 1)

        def step(j, carry):
            m, l, acc, acc_lo = carry
            off = pl.multiple_of(j * tk, tk)
            ks = k_ref[pl.ds(off, tk), :].astype(BF16)
            vs = v_ref[pl.ds(off, tk), :].astype(BF16)
            s = _dot_nt(qv, ks) * scale
            if bias:
                s = s + cfc_ref[0] - cfr_ref[0, pl.ds(j, 1), :]
            s = jnp.where(j * tk + coli <= row, s, NEG)
            m_new = jnp.maximum(m, jnp.max(s, axis=1, keepdims=True))
            alpha = jnp.exp(m - m_new)
            p = jnp.exp(s - m_new)
            pb = p.astype(BF16)
            l = alpha * l + jnp.sum(p, axis=1, keepdims=True)
            acc = alpha * acc + jnp.dot(pb, vs, preferred_element_type=F32)
            acc_lo = alpha * acc_lo + jnp.dot((p - pb.astype(F32)).astype(BF16), vs, preferred_element_type=F32)
            return m_new, l, acc, acc_lo

        z = jnp.zeros((tq, HEAD_DIM), F32)
        m, l, acc, acc_lo = lax.fori_loop(0, i + 1, step, (jnp.full((tq, 1), NEG, F32), jnp.zeros((tq, 1), F32), z, z))
        o_ref[...] = (acc / l).astype(o_ref.dtype)
        of_ref[...] = (acc + acc_lo) / l
        lse_ref[0] = m + jnp.log(l)

    in_specs = [pl.BlockSpec((tq, dq), lambda h, i: (i, qoff + h)),
                pl.BlockSpec((S, dq), lambda h, i: (0, koff + h)),
                pl.BlockSpec((S, HEAD_DIM), lambda h, i: (0, voff + vstep * h))]
    operands = [q, k, v]
    if bias:
        in_specs += [pl.BlockSpec((1, tq, 1), lambda h, i: (h, i, 0)), pl.BlockSpec((1, S // tk, tk), lambda h, i: (h, 0, 0))]
        operands += [cf_col, cf_row]
    return pl.pallas_call(
        body, name=name, grid=(nh, nq), in_specs=in_specs,
        out_specs=[pl.BlockSpec((tq, HEAD_DIM), lambda h, i: (i, h)), pl.BlockSpec((tq, HEAD_DIM), lambda h, i: (i, h)),
                   pl.BlockSpec((1, tq, 1), lambda h, i: (h, i, 0))],
        out_shape=[jax.ShapeDtypeStruct((S, nh * HEAD_DIM), BF16), jax.ShapeDtypeStruct((S, nh * HEAD_DIM), F32),
                   jax.ShapeDtypeStruct((nh, S, 1), F32)],
        compiler_params=_cp(("parallel", "arbitrary")),
    )(*operands)


def attn_bwd(name, q, k, v, o, do, lse, nh, dq, qoff, koff, voff, vstep, scale, cf_col=None, cf_row=None):
    S = q.shape[0]
    tq = _pick_rows(S, 256)
    tk = tq
    nq = S // tq
    bias = cf_col is not None

    def body(*refs):
        if bias:
            q_ref, k_ref, v_ref, o_ref, do_ref, lse_ref, cfc_ref, cfr_ref, dq_ref, dk_ref, dv_ref, dcf_ref = refs
        else:
            q_ref, k_ref, v_ref, o_ref, do_ref, lse_ref, dq_ref, dk_ref, dv_ref = refs
        dk_ref[...] = jnp.zeros_like(dk_ref)
        dv_ref[...] = jnp.zeros_like(dv_ref)
        if bias:
            dcf_ref[...] = jnp.zeros_like(dcf_ref)
        rowi = lax.broadcasted_iota(jnp.int32, (tq, tk), 0)
        coli = lax.broadcasted_iota(jnp.int32, (tq, tk), 1)

        def outer(i, _):
            roff = pl.multiple_of(i * tq, tq)
            qv = q_ref[pl.ds(roff, tq), :].astype(BF16)
            dov = do_ref[pl.ds(roff, tq), :]
            delta = jnp.sum(dov.astype(F32) * o_ref[pl.ds(roff, tq), :].astype(F32), axis=1, keepdims=True)
            lse = lse_ref[0, pl.ds(roff, tq), :]
            if bias:
                cq = cfc_ref[0, pl.ds(roff, tq), :]

            def inner(j, dq_acc):
                off = pl.multiple_of(j * tk, tk)
                ks = k_ref[pl.ds(off, tk), :].astype(BF16)
                vs = v_ref[pl.ds(off, tk), :].astype(BF16)
                s = _dot_nt(qv, ks) * scale
                if bias:
                    s = s + cq - cfr_ref[0, pl.ds(j, 1), :]
                p = jnp.where(j * tk + coli <= i * tq + rowi, jnp.exp(s - lse), 0.0)
                dp = _dot_nt(dov, vs)
                ds = p * (dp - delta)
                if bias:
                    dcf_ref[0, pl.ds(j, 1), :] -= jnp.sum(ds, axis=0, keepdims=True)
                dsb = (ds * scale).astype(BF16)
                dv_ref[pl.ds(off, tk), :] += _dot_tn(p.astype(BF16), dov)
                dk_ref[pl.ds(off, tk), :] += _dot_tn(dsb, qv)
                return dq_acc + jnp.dot(dsb, ks, preferred_element_type=F32)

            dq_ref[pl.ds(roff, tq), :] = lax.fori_loop(0, i + 1, inner, jnp.zeros((tq, dq), F32))
            return 0

        lax.fori_loop(0, nq, outer, 0)

    in_specs = [pl.BlockSpec((S, dq), lambda h: (0, qoff + h)),
                pl.BlockSpec((S, dq), lambda h: (0, koff + h)),
                pl.BlockSpec((S, HEAD_DIM), lambda h: (0, voff + vstep * h)),
                pl.BlockSpec((S, HEAD_DIM), lambda h: (0, h)),
                pl.BlockSpec((S, HEAD_DIM), lambda h: (0, h)),
                pl.BlockSpec((1, S, 1), lambda h: (h, 0, 0))]
    operands = [q, k, v, o, do, lse]
    out_specs = [pl.BlockSpec((S, dq), lambda h: (0, h)), pl.BlockSpec((S, dq), lambda h: (0, h)),
                 pl.BlockSpec((S, HEAD_DIM), lambda h: (0, h))]
    out_shape = [jax.ShapeDtypeStruct((S, nh * dq), F32), jax.ShapeDtypeStruct((S, nh * dq), F32),
                 jax.ShapeDtypeStruct((S, nh * HEAD_DIM), F32)]
    if bias:
        in_specs += [pl.BlockSpec((1, S, 1), lambda h: (h, 0, 0)), pl.BlockSpec((1, S // tk, tk), lambda h: (h, 0, 0))]
        operands += [cf_col, cf_row]
        out_specs.append(pl.BlockSpec((1, S // tk, tk), lambda h: (h, 0, 0)))
        out_shape.append(jax.ShapeDtypeStruct((nh, S // tk, tk), F32))
    return pl.pallas_call(
        body, name=name, grid=(nh,), in_specs=in_specs, out_specs=out_specs, out_shape=out_shape,
        compiler_params=_cp(("parallel",)),
    )(*operands)


CH = 128


def _sb_chunk(qv, kc, row, col, scale, later_c, tri_after):
    z = _dot_nt(qv, kc) * scale
    strict = col < row
    lsn = jnp.minimum(-z, 0.0) - jnp.log(1.0 + jnp.exp(-jnp.abs(z)))
    lsp = lsn + z
    L = jnp.where(strict, lsn, 0.0)
    later = _exact_dot(L, tri_after, 2) + later_c
    a = jnp.where(strict, jnp.exp(lsp + later), 0.0)
    return strict, lsp, lsn, L, a


def _tri(after_strict):
    r = lax.broadcasted_iota(jnp.int32, (CH, CH), 0)
    c = lax.broadcasted_iota(jnp.int32, (CH, CH), 1)
    return jnp.where(r > c if after_strict else r >= c, 1.0, 0.0).astype(BF16)


def sb_fwd(name, qkv, nh):
    S = qkv.shape[0]
    tq = _pick_rows(S, 256)
    nq = S // tq
    cpq = tq // CH
    scale = 1.0 / math.sqrt(HEAD_DIM)

    def body(q_ref, k_ref, v_ref, o_ref, of_ref):
        i = pl.program_id(1)
        qv = q_ref[...]
        tri = _tri(True)
        row = i * tq + lax.broadcasted_iota(jnp.int32, (tq, CH), 0)
        coli = lax.broadcasted_iota(jnp.int32, (tq, CH), 1)
        nch = (i + 1) * cpq

        def step(jj, carry):
            later_c, acc, acc_lo = carry
            jc = nch - 1 - jj
            off = pl.multiple_of(jc * CH, CH)
            kc = k_ref[pl.ds(off, CH), :]
            vc = v_ref[pl.ds(off, CH), :]
            _, _, _, L, a = _sb_chunk(qv, kc, row, jc * CH + coli, scale, later_c, tri)
            ab = a.astype(BF16)
            acc = acc + jnp.dot(ab, vc, preferred_element_type=F32)
            acc_lo = acc_lo + jnp.dot((a - ab.astype(F32)).astype(BF16), vc, preferred_element_type=F32)
            return later_c + jnp.sum(L, axis=1, keepdims=True), acc, acc_lo

        z = jnp.zeros((tq, HEAD_DIM), F32)
        _, acc, acc_lo = lax.fori_loop(0, nch, step, (jnp.zeros((tq, 1), F32), z, z))
        o_ref[...] = acc.astype(o_ref.dtype)
        of_ref[...] = acc + acc_lo

    blk = pl.BlockSpec((tq, HEAD_DIM), lambda h, i: (i, h))
    return pl.pallas_call(
        body, name=name, grid=(nh, nq),
        in_specs=[blk,
                  pl.BlockSpec((S, HEAD_DIM), lambda h, i: (0, nh + h)),
                  pl.BlockSpec((S, HEAD_DIM), lambda h, i: (0, 2 * nh + h))],
        out_specs=[blk, blk],
        out_shape=[jax.ShapeDtypeStruct((S, nh * HEAD_DIM), BF16), jax.ShapeDtypeStruct((S, nh * HEAD_DIM), F32)],
        compiler_params=_cp(("parallel", "arbitrary")),
    )(qkv, qkv, qkv)


def sb_bwd(name, qkv, o, do, nh):
    S = qkv.shape[0]
    tq = _pick_rows(S, 256)
    nq = S // tq
    cpq = tq // CH
    scale = 1.0 / math.sqrt(HEAD_DIM)

    def body(q_ref, k_ref, v_ref, o_ref, do_ref, dq_ref, dk_ref, dv_ref, dk_acc, dv_acc):
        dk_acc[...] = jnp.zeros_like(dk_acc)
        dv_acc[...] = jnp.zeros_like(dv_acc)
        tri = _tri(True)
        tri_inc = _tri(False)
        rowi = lax.broadcasted_iota(jnp.int32, (tq, CH), 0)
        coli = lax.broadcasted_iota(jnp.int32, (tq, CH), 1)

        def outer(i, _):
            roff = pl.multiple_of(i * tq, tq)
            qv = q_ref[pl.ds(roff, tq), :]
            dov = do_ref[pl.ds(roff, tq), :]
            dtot = jnp.sum(dov.astype(F32) * o_ref[pl.ds(roff, tq), :].astype(F32), axis=1, keepdims=True)
            row = i * tq + rowi
            nch = (i + 1) * cpq

            def inner(jj, carry):
                later_c, suf_c, dq_acc = carry
                jc = nch - 1 - jj
                off = pl.multiple_of(jc * CH, CH)
                kc = k_ref[pl.ds(off, CH), :]
                vc = v_ref[pl.ds(off, CH), :]
                strict, lsp, lsn, L, a = _sb_chunk(qv, kc, row, jc * CH + coli, scale, later_c, tri)
                dl = _dot_nt(dov, vc) * a
                before = dtot - (suf_c + _exact_dot(dl, tri_inc, 3))
                dz = jnp.where(strict, dl * jnp.exp(lsn) - jnp.exp(lsp) * before, 0.0) * scale
                dzb = dz.astype(BF16)
                dv_acc[pl.ds(off, CH), :] += _dot_tn(a.astype(BF16), dov)
                dk_acc[pl.ds(off, CH), :] += _dot_tn(dzb, qv)
                return (later_c + jnp.sum(L, axis=1, keepdims=True), suf_c + jnp.sum(dl, axis=1, keepdims=True),
                        dq_acc + jnp.dot(dzb, kc, preferred_element_type=F32))

            z1 = jnp.zeros((tq, 1), F32)
            _, _, dq_acc = lax.fori_loop(0, nch, inner, (z1, z1, jnp.zeros((tq, HEAD_DIM), F32)))
            dq_ref[pl.ds(roff, tq), :] = dq_acc.astype(dq_ref.dtype)
            return 0

        lax.fori_loop(0, nq, outer, 0)
        dk_ref[...] = dk_acc[...].astype(dk_ref.dtype)
        dv_ref[...] = dv_acc[...].astype(dv_ref.dtype)

    def col(off):
        return pl.BlockSpec((S, HEAD_DIM), lambda h: (0, off + h))

    dq, dk, dv = pl.pallas_call(
        body, name=name, grid=(nh,),
        in_specs=[col(0), col(nh), col(2 * nh), col(0), col(0)],
        out_specs=[col(0), col(0), col(0)],
        out_shape=[jax.ShapeDtypeStruct((S, nh * HEAD_DIM), BF16)] * 3,
        scratch_shapes=[pltpu.VMEM((S, HEAD_DIM), F32), pltpu.VMEM((S, HEAD_DIM), F32)],
        compiler_params=_cp(("parallel",)),
    )(qkv, qkv, qkv, o, do)
    return jnp.concatenate([dq, dk, dv], axis=1)


def seq_cumsum(name, x, reverse):
    S, W = x.shape

    tb = _pick_rows(S, 256)

    def body(x_ref, o_ref):
        parts = _split3(x_ref[...])
        c = lax.broadcasted_iota(jnp.int32, (tb, S), 1)
        for b in range(S // tb):
            r = b * tb + lax.broadcasted_iota(jnp.int32, (tb, S), 0)
            t = jnp.where(r <= c if reverse else r >= c, 1.0, 0.0).astype(BF16)
            out = None
            for p in parts:
                d = jnp.dot(t, p, preferred_element_type=F32)
                out = d if out is None else out + d
            o_ref[b * tb:(b + 1) * tb, :] = out

    return pl.pallas_call(body, name=name, out_shape=jax.ShapeDtypeStruct((S, W), F32),
                          compiler_params=pltpu.CompilerParams(vmem_limit_bytes=VMEM_LIMIT))(x)


def adamw(name, w, g, m, v, layer=None):
    R, C = g.shape
    tr = _pick_rows(R, 256)
    c1 = 1.0 - ADAM_B1 ** ADAM_STEP
    c2 = 1.0 - ADAM_B2 ** ADAM_STEP

    def body(w_ref, g_ref, m_ref, v_ref, d_ref, nm_ref, nv_ref):
        gv = g_ref[...]
        nm = ADAM_B1 * m_ref[...] + (1.0 - ADAM_B1) * gv
        nv = ADAM_B2 * v_ref[...] + (1.0 - ADAM_B2) * (gv * gv)
        d_ref[...] = -ADAM_LR * ((nm / c1) / (jnp.sqrt(nv / c2) + ADAM_EPS) + ADAM_WD * w_ref[...])
        nm_ref[...] = nm
        nv_ref[...] = nv

    row = pl.BlockSpec((tr, C), lambda i: (i, 0))
    sel = row if layer is None else pl.BlockSpec((None, tr, C), lambda i: (layer, i, 0))
    return pl.pallas_call(
        body, name=name, grid=(R // tr,), in_specs=[sel, row, sel, sel], out_specs=[row, row, row],
        out_shape=[jax.ShapeDtypeStruct((R, C), F32)] * 3, compiler_params=_cp(("parallel",)),
    )(w, g, m, v)


ANY = pl.BlockSpec(memory_space=pl.ANY)


def _place():
    x, y, c = lax.axis_index("x"), lax.axis_index("y"), lax.axis_index("c")
    others = [(1 - x, y), (x, 1 - y), (1 - x, 1 - y)]
    return x, y, c, others


def gather_weights(name, shards):
    n = len(shards)

    def body(*refs):
        src, dst = refs[:n], refs[n:2 * n]
        send1, recv1, send2, recv2, local = refs[2 * n:]
        x, y, c, others = _place()
        me = 2 * x + y
        sib = (x, y, 1 - c)

        def half(ref, chip, core, t):
            r2 = shards[t].shape[0] // 2
            return ref.at[chip, pl.ds(core * r2, r2), :]

        def ici(t, j, chip, core, to):
            r2 = shards[t].shape[0] // 2
            return pltpu.make_async_remote_copy(
                src_ref=src[t].at[pl.ds(core * r2, r2), :], dst_ref=half(dst[t], chip, core, t),
                send_sem=send1.at[t, j], recv_sem=recv1.at[t, j], device_id=to, device_id_type=MESH)

        def d2d(t, j, chip, core):
            return pltpu.make_async_remote_copy(
                src_ref=half(dst[t], chip, core, t), dst_ref=half(dst[t], chip, core, t),
                send_sem=send2.at[t, j], recv_sem=recv2.at[t, j], device_id=sib, device_id_type=MESH)

        mine = [pltpu.make_async_copy(src[t], dst[t].at[me], local.at[t]) for t in range(n)]
        for cp in mine:
            cp.start()
        for t in range(n):
            for j, (px, py) in enumerate(others):
                ici(t, j, me, c, (px, py, c)).start()
        for t in range(n):
            for j, (px, py) in enumerate(others):
                ici(t, j, 2 * px + py, c, (px, py, c)).wait_recv()
                d2d(t, j, 2 * px + py, c).start()
        for t in range(n):
            for j, (px, py) in enumerate(others):
                d2d(t, j, 2 * px + py, 1 - c).wait_recv()
        for t in range(n):
            for j, (px, py) in enumerate(others):
                ici(t, j, me, c, (px, py, c)).wait_send()
                d2d(t, j, 2 * px + py, c).wait_send()
        for cp in mine:
            cp.wait()

    return pl.pallas_call(
        body, name=name, in_specs=[ANY] * n, out_specs=[ANY] * n,
        out_shape=[jax.ShapeDtypeStruct((N_CHIPS,) + s.shape, s.dtype) for s in shards],
        scratch_shapes=[pltpu.SemaphoreType.DMA((n, 3))] * 4 + [pltpu.SemaphoreType.DMA((n,))],
    )(*shards)


def swap_halves(name, grads):
    n = len(grads)

    def body(*refs):
        src, dst = refs[:n], refs[n:2 * n]
        send, recv = refs[2 * n:]
        x, y, c, _ = _place()
        cps = []
        for t in range(n):
            r2 = grads[t].shape[1] // 2
            cps.append(pltpu.make_async_remote_copy(
                src_ref=src[t].at[:, pl.ds((1 - c) * r2, r2), :], dst_ref=dst[t],
                send_sem=send.at[t], recv_sem=recv.at[t], device_id=(x, y, 1 - c), device_id_type=MESH))
        for cp in cps:
            cp.start()
        for cp in cps:
            cp.wait()

    return pl.pallas_call(
        body, name=name, in_specs=[ANY] * n, out_specs=[ANY] * n,
        out_shape=[jax.ShapeDtypeStruct((N_CHIPS, g.shape[1] // 2, g.shape[2]), g.dtype) for g in grads],
        scratch_shapes=[pltpu.SemaphoreType.DMA((n,))] * 2,
    )(*grads)


def add_half(name, g, other, c_arr):
    _, R, C = g.shape
    r2 = R // 2
    tr = _pick_rows(r2, 512)
    nb = r2 // tr

    def body(c_ref, g_ref, o_ref, out_ref):
        out_ref[...] = (g_ref[...].astype(F32) + o_ref[...].astype(F32)).astype(out_ref.dtype)

    return pl.pallas_call(
        body, name=name,
        grid_spec=pltpu.PrefetchScalarGridSpec(
            num_scalar_prefetch=1, grid=(N_CHIPS, nb),
            in_specs=[pl.BlockSpec((None, tr, C), lambda s, i, c: (s, c[0] * nb + i, 0)),
                      pl.BlockSpec((None, tr, C), lambda s, i, c: (s, i, 0))],
            out_specs=pl.BlockSpec((None, tr, C), lambda s, i, c: (s, i, 0))),
        out_shape=jax.ShapeDtypeStruct((N_CHIPS, r2, C), BF16),
        compiler_params=_cp(("parallel", "parallel")),
    )(c_arr, g, other)


def exchange_chips(name, parts):
    n = len(parts)

    def body(*refs):
        src, dst = refs[:n], refs[n:2 * n]
        send, recv, local = refs[2 * n:]
        x, y, c, others = _place()
        me = 2 * x + y
        mine = [pltpu.make_async_copy(src[t].at[me], dst[t].at[me], local.at[t]) for t in range(n)]
        for cp in mine:
            cp.start()

        def cp(t, j, chip_from, chip_to, to):
            return pltpu.make_async_remote_copy(
                src_ref=src[t].at[chip_to], dst_ref=dst[t].at[chip_from],
                send_sem=send.at[t, j], recv_sem=recv.at[t, j], device_id=to, device_id_type=MESH)

        for t in range(n):
            for j, (px, py) in enumerate(others):
                cp(t, j, me, 2 * px + py, (px, py, c)).start()
        for t in range(n):
            for j, (px, py) in enumerate(others):
                cp(t, j, 2 * px + py, me, (px, py, c)).wait_recv()
        for t in range(n):
            for j, (px, py) in enumerate(others):
                cp(t, j, me, 2 * px + py, (px, py, c)).wait_send()
        for m in mine:
            m.wait()

    return pl.pallas_call(
        body, name=name, in_specs=[ANY] * n, out_specs=[ANY] * n,
        out_shape=[jax.ShapeDtypeStruct(p.shape, p.dtype) for p in parts],
        scratch_shapes=[pltpu.SemaphoreType.DMA((n, 3))] * 2 + [pltpu.SemaphoreType.DMA((n,))],
    )(*parts)


def sum_slabs(name, parts):
    _, r2, C = parts.shape
    tr = _pick_rows(r2, 512)

    def body(p_ref, out_ref):
        acc = p_ref[0].astype(F32)
        for s in range(1, N_CHIPS):
            acc = acc + p_ref[s].astype(F32)
        out_ref[...] = acc

    return pl.pallas_call(
        body, name=name, grid=(r2 // tr,),
        in_specs=[pl.BlockSpec((N_CHIPS, tr, C), lambda i: (0, i, 0))],
        out_specs=pl.BlockSpec((tr, C), lambda i: (i, 0)),
        out_shape=jax.ShapeDtypeStruct((r2, C), F32), compiler_params=_cp(("parallel",)),
    )(parts)


def join_halves(name, halves):
    n = len(halves)

    def body(*refs):
        src, dst = refs[:n], refs[n:2 * n]
        send, recv, local = refs[2 * n:]
        x, y, c, _ = _place()
        cps, mine = [], []
        for t in range(n):
            r2 = halves[t].shape[0]
            rows = dst[t].at[pl.ds(c * r2, r2), :]
            mine.append(pltpu.make_async_copy(src[t], rows, local.at[t]))
            cps.append(pltpu.make_async_remote_copy(
                src_ref=src[t], dst_ref=rows, send_sem=send.at[t], recv_sem=recv.at[t],
                device_id=(x, y, 1 - c), device_id_type=MESH))
        for cp in mine + cps:
            cp.start()
        for t in range(n):
            r2 = halves[t].shape[0]
            pltpu.make_async_remote_copy(
                src_ref=src[t], dst_ref=dst[t].at[pl.ds((1 - c) * r2, r2), :], send_sem=send.at[t], recv_sem=recv.at[t],
                device_id=(x, y, 1 - c), device_id_type=MESH).wait_recv()
        for cp in cps:
            cp.wait_send()
        for cp in mine:
            cp.wait()

    return pl.pallas_call(
        body, name=name, in_specs=[ANY] * n, out_specs=[ANY] * n,
        out_shape=[jax.ShapeDtypeStruct((2 * h.shape[0], h.shape[1]), h.dtype) for h in halves],
        scratch_shapes=[pltpu.SemaphoreType.DMA((n,))] * 3,
    )(*halves)


def reduce_grads(tag, grads, c_arr):
    theirs = swap_halves(f"swap_{tag}", grads)
    parts = [add_half(f"addh_{tag}_{t}", g, o, c_arr) for t, (g, o) in enumerate(zip(grads, theirs))]
    got = exchange_chips(f"xchg_{tag}", parts)
    halves = [sum_slabs(f"sum4_{tag}_{t}", p) for t, p in enumerate(got)]
    return join_halves(f"join_{tag}", halves)


def all_sum_small(name, v):
    R = v.shape[0]

    def body(v_ref, out_ref, slots, send, recv):
        x, y, c, _ = _place()
        me = 4 * x + 2 * y + c
        slots[me] = v_ref[...]
        cps = []
        for k in range(1, 8):
            dx, dy, dc = (k >> 2) & 1, (k >> 1) & 1, k & 1
            to = (x ^ dx, y ^ dy, c ^ dc)
            cps.append(pltpu.make_async_remote_copy(
                src_ref=v_ref, dst_ref=slots.at[me], send_sem=send.at[k - 1], recv_sem=recv.at[k - 1],
                device_id=to, device_id_type=MESH))
        for cp in cps:
            cp.start()
        for k in range(1, 8):
            dx, dy, dc = (k >> 2) & 1, (k >> 1) & 1, k & 1
            frm = 4 * (x ^ dx) + 2 * (y ^ dy) + (c ^ dc)
            pltpu.make_async_remote_copy(
                src_ref=v_ref, dst_ref=slots.at[frm], send_sem=send.at[k - 1], recv_sem=recv.at[k - 1],
                device_id=(x, y, c), device_id_type=MESH).wait_recv()
        for cp in cps:
            cp.wait_send()
        acc = slots[0]
        for d in range(1, 8):
            acc = acc + slots[d]
        out_ref[...] = acc

    vm = pl.BlockSpec(memory_space=pltpu.VMEM)
    return pl.pallas_call(
        body, name=name, in_specs=[vm], out_specs=vm, out_shape=jax.ShapeDtypeStruct((R, LANE), F32),
        scratch_shapes=[pltpu.VMEM((8, R, LANE), F32), pltpu.SemaphoreType.DMA((7,)), pltpu.SemaphoreType.DMA((7,))],
    )(v)


def _rope_mat(n, lo):
    half = MLA_ROPE // 2
    r = lax.broadcasted_iota(jnp.int32, (n, n), 0)
    c = lax.broadcasted_iota(jnp.int32, (n, n), 1)
    plus = (c >= lo + half) & (c < lo + 2 * half) & (r == c - half)
    minus = (c >= lo) & (c < lo + half) & (r == c + half)
    return (jnp.where(plus, 1.0, 0.0) - jnp.where(minus, 1.0, 0.0)).astype(BF16)


def _rope(v, cos, sin, lo):
    return v * cos + _exact_dot(v, _rope_mat(v.shape[-1], lo), 3) * sin


def rope_tables(pos):
    S = pos.shape[0]
    half = MLA_ROPE // 2
    inv = (np.float32(ROPE_THETA) ** (-np.arange(0, half, dtype=np.float32) * np.float32(2.0 / MLA_ROPE))).astype(np.float32)
    f1 = np.zeros((1, LANE), np.float32)
    f1[0, :MLA_ROPE] = np.tile(inv, 2)
    f2 = np.zeros((1, MLA_QK_PAD), np.float32)
    f2[0, MLA_NOPE:MLA_QK] = np.tile(inv, 2)
    tr = _pick_rows(S, 256)

    def body(p_ref, f1_ref, f2_ref, c1, s1, c2, s2):
        p = p_ref[...].astype(F32)
        a1 = p * f1_ref[...]
        a2 = p * f2_ref[...]
        c1[...] = jnp.cos(a1)
        s1[...] = jnp.sin(a1)
        c2[...] = jnp.cos(a2)
        s2[...] = jnp.sin(a2)

    def row(w):
        return pl.BlockSpec((tr, w), lambda i: (i, 0))

    def full(w):
        return pl.BlockSpec((1, w), lambda i: (0, 0))

    return pl.pallas_call(
        body, name="rope_tables", grid=(S // tr,), in_specs=[row(1), full(LANE), full(MLA_QK_PAD)],
        out_specs=[row(LANE), row(LANE), row(MLA_QK_PAD), row(MLA_QK_PAD)],
        out_shape=[jax.ShapeDtypeStruct((S, LANE), F32)] * 2 + [jax.ShapeDtypeStruct((S, MLA_QK_PAD), F32)] * 2,
        compiler_params=_cp(("parallel",)),
    )(pos, jnp.asarray(f1), jnp.asarray(f2))


def _log_sigmoid(z):
    return jnp.minimum(z, 0.0) - jnp.log(1.0 + jnp.exp(-jnp.abs(z)))


def _fox_qk_fn(q, k, gq, gk):
    return _rms(q, gq, HEAD_DIM), _rms(k, gk, HEAD_DIM)


def _fox_gate_fn(f, b):
    return (_log_sigmoid(f + b),)


def _mla_pre1_fn(q_rank, kv_rank):
    def fn(cq, ckv, kr, cos, sin, qn, kvn):
        return _rms(cq, qn, q_rank), _rms(ckv, kvn, kv_rank), _rope(kr, cos, sin, 0)
    return fn


def _mla_pre2_fn(qb, kn, kr, cos, sin, gq, gk):
    qh = _rms(_rope(qb, cos, sin, MLA_NOPE), gq, MLA_QK)
    kh = _rms(jnp.concatenate([kn, kr], axis=1), gk, MLA_QK)
    return qh, kh


def _pad_cols(a, n):
    return jnp.pad(a, ((0, 0), (0, n - a.shape[1])))


def _relu2(acc):
    r = jnp.maximum(acc, 0.0)
    return r * r, r


def _add(acc, res):
    return (acc + res,)


def _times_2r(acc, r):
    return (acc * (2.0 * r.astype(F32)),)


def kernel(x, positions, mix_norm, mlp_norm, sb_w_in, sb_w_out, fox_w_in, fox_b_f, fox_q_gain, fox_k_gain, fox_w_out, mla_w_in, mla_q_norm, mla_kv_norm, mla_w_uq, mla_w_ukv, mla_q_gain, mla_k_gain, mla_w_out, mlp_w1, mlp_w2, loss_target, m_mix_norm, m_mlp_norm, m_sb_w_in, m_sb_w_out, m_fox_w_in, m_fox_b_f, m_fox_q_gain, m_fox_k_gain, m_fox_w_out, m_mla_w_in, m_mla_q_norm, m_mla_kv_norm, m_mla_w_uq, m_mla_w_ukv, m_mla_q_gain, m_mla_k_gain, m_mla_w_out, m_mlp_w1, m_mlp_w2, v_mix_norm, v_mlp_norm, v_sb_w_in, v_sb_w_out, v_fox_w_in, v_fox_b_f, v_fox_q_gain, v_fox_k_gain, v_fox_w_out, v_mla_w_in, v_mla_q_norm, v_mla_kv_norm, v_mla_w_uq, v_mla_w_ukv, v_mla_q_gain, v_mla_k_gain, v_mla_w_out, v_mlp_w1, v_mlp_w2):
    S, D = x.shape[1], x.shape[2]
    nh = D // HEAD_DIM
    W = nh * HEAD_DIM
    depth = mix_norm.shape[0]
    q_rank, kv_rank = mla_w_uq.shape[1], mla_w_ukv.shape[1]
    n_fox_in = 3 * W + nh
    fox_pad = -(-n_fox_in // LANE) * LANE
    n_down = q_rank + kv_rank + MLA_ROPE
    down_pad = q_rank + kv_rank + LANE
    tr = _pick_rows(S, 256)
    tk = _pick_rows(S, 256)

    ax, ay, ac = lax.axis_index("x"), lax.axis_index("y"), lax.axis_index("c")
    chip = 2 * ax + ay
    c_arr = jnp.reshape(ac, (1,)).astype(jnp.int32)

    def bf(a):
        return a.astype(BF16)

    n_small_in = q_rank + kv_rank
    rows_in = -(-n_small_in // (8 * LANE)) * 8
    placed = jnp.zeros((rows_in * LANE,), F32)
    placed = lax.dynamic_update_slice(placed, mla_q_norm[0], (chip * mla_q_norm.shape[1],))
    placed = lax.dynamic_update_slice(placed, mla_kv_norm[0], (q_rank + chip * mla_kv_norm.shape[1],))
    placed = placed * (ac == 0).astype(F32)
    norms = all_sum_small("gather_norms", placed.reshape(rows_in, LANE)).reshape(-1)
    q_norm_full = norms[:q_rank].reshape(1, q_rank)
    kv_norm_full = norms[q_rank:q_rank + kv_rank].reshape(1, kv_rank)

    layers = []
    for i in range(depth):
        kind, j = i % N_MIXERS, i // N_MIXERS
        if kind == 0:
            names, shards = ["w_in", "w_out"], [bf(sb_w_in[j]), bf(sb_w_out[j])]
        elif kind == 1:
            names, shards = ["w_in", "w_out"], [bf(fox_w_in[j]), bf(fox_w_out[j])]
        else:
            names = ["w_in", "w_uq", "w_ukv", "w_out"]
            shards = [bf(mla_w_in[j]), bf(mla_w_uq[j]), bf(mla_w_ukv[j]), bf(mla_w_out[j])]
        names += ["w1", "w2"]
        shards += [bf(mlp_w1[i]), bf(mlp_w2[i])]
        layers.append(dict(zip(names, gather_weights(f"gather_{i}", shards))))

    def rows_stacked(w):
        return w.reshape(w.shape[0] * w.shape[1], w.shape[2])

    xc = x[0]
    saved = []
    tables = None
    for i in range(depth):
        kind, j = i % N_MIXERS, i // N_MIXERS
        L = layers[i]
        g1 = mix_norm[i:i + 1]
        hb = rmsnorm_fwd(f"norm1_{i}", xc, g1)
        st = dict(x=xc, hb=hb)
        if kind == 0:
            qkv = mm_nn(f"sb_proj_{i}", hb, L["w_in"], BF16)[0]
            o, o_f32 = sb_fwd(f"sb_attn_{i}", qkv, nh)
            st.update(qkv=qkv, o_f32=o_f32)
        elif kind == 1:
            w_in = _pad_cols(jnp.concatenate([L["w_in"][s] for s in range(N_CHIPS)], axis=1), fox_pad)
            proj = mm_nn(f"fox_proj_{i}", hb, w_in, F32)[0]
            gq, gk = fox_q_gain[j:j + 1], fox_k_gain[j:j + 1]
            qk_rows = [(proj, HEAD_DIM, lambda h: h), (proj, HEAD_DIM, lambda h: nh + h)]
            qh, kh = rowwise_fwd(f"fox_qk_{i}", _fox_qk_fn, qk_rows, [gq, gk],
                                 [(W, HEAD_DIM, lambda h: h, BF16)] * 2, tr, nh)
            b_pad = _pad_cols(fox_b_f[j:j + 1], LANE)
            gate_rows = [(proj, LANE, lambda h: 3 * nh)]
            logf = rowwise_fwd(f"fox_gate_{i}", _fox_gate_fn, gate_rows, [b_pad], [(LANE, LANE, _c0, F32)], tr)[0]
            cf = seq_cumsum(f"fox_cf_{i}", logf, False)[:, :nh].T
            cf_col, cf_row = cf.reshape(nh, S, 1), cf.reshape(nh, S // tk, tk)
            scale = 1.0 / math.sqrt(HEAD_DIM)
            o, o_f32, lse = attn_fwd(f"fox_attn_{i}", qh, kh, proj, nh, HEAD_DIM, 0, 0, 2 * nh, 1, scale, cf_col, cf_row)
            st.update(w_in=w_in, proj=proj, qk_rows=qk_rows, gq=gq, gk=gk, qh=qh, kh=kh, b_pad=b_pad, gate_rows=gate_rows,
                      cf_col=cf_col, cf_row=cf_row, lse=lse, scale=scale, o_f32=o_f32)
        else:
            w_in = _pad_cols(rows_stacked(L["w_in"]), down_pad)
            down = mm_nn(f"mla_down_{i}", hb, w_in, F32)[0]
            if tables is None:
                tables = rope_tables(positions.reshape(S, 1))
            cos1, sin1, cos2, sin2 = tables
            pre1_rows = [(down[:, :q_rank], q_rank, _c0), (down[:, q_rank:q_rank + kv_rank], kv_rank, _c0),
                         (down[:, q_rank + kv_rank:], LANE, _c0), (cos1, LANE, _c0), (sin1, LANE, _c0)]
            pre1_fn = _mla_pre1_fn(q_rank, kv_rank)
            c_q, c_kv, k_rope = rowwise_fwd(
                f"mla_pre1_{i}", pre1_fn, pre1_rows, [q_norm_full, kv_norm_full],
                [(q_rank, q_rank, _c0, BF16), (kv_rank, kv_rank, _c0, BF16), (LANE, LANE, _c0, F32)], tr)
            qfull = mm_nn(f"mla_uq_{i}", c_q, L["w_uq"], F32)[0]
            kv = mm_nn(f"mla_ukv_{i}", c_kv, L["w_ukv"], F32)[0]
            qpad = jnp.pad(qfull.reshape(S, nh, MLA_QK), ((0, 0), (0, 0), (0, MLA_QK_PAD - MLA_QK))).reshape(S, nh * MLA_QK_PAD)
            gq, gk = _pad_cols(mla_q_gain[j:j + 1], MLA_QK_PAD), _pad_cols(mla_k_gain[j:j + 1], MLA_QK_PAD)
            pre2_rows = [(qpad, MLA_QK_PAD, lambda h: h), (kv, MLA_NOPE, lambda h: 2 * h), (k_rope, LANE, _c0),
                         (cos2, MLA_QK_PAD, _c0), (sin2, MLA_QK_PAD, _c0)]
            qh, kh = rowwise_fwd(f"mla_pre2_{i}", _mla_pre2_fn, pre2_rows, [gq, gk],
                                 [(nh * MLA_QK_PAD, MLA_QK_PAD, lambda h: h, BF16)] * 2, tr, nh)
            scale = 1.0 / math.sqrt(MLA_QK)
            o, o_f32, lse = attn_fwd(f"mla_attn_{i}", qh, kh, kv, nh, MLA_QK_PAD, 0, 0, 1, 2, scale)
            st.update(w_in=w_in, pre1_rows=pre1_rows, pre1_fn=pre1_fn, c_q=c_q, c_kv=c_kv, pre2_rows=pre2_rows, gq=gq, gk=gk,
                      qh=qh, kh=kh, kv=kv, lse=lse, scale=scale, o_f32=o_f32)
        x1 = mm_nn(f"mix_out_{i}", o, rows_stacked(L["w_out"]), F32, epilogue=_add, extras=(xc,))[0]
        g2 = mlp_norm[i:i + 1]
        h2 = rmsnorm_fwd(f"norm2_{i}", x1, g2)
        a, r = mm_nn(f"mlp_up_{i}", h2, L["w1"], None, epilogue=_relu2, out_dtypes=[BF16, BF16])
        xc = mm_nn(f"mlp_down_{i}", a, rows_stacked(L["w2"]), F32, epilogue=_add, extras=(x1,))[0]
        st.update(o=o, x1=x1, h2=h2, a=a, r=r, g1=g1, g2=g2)
        saved.append(st)

    loss_local, dx, dxb = loss_head(xc, loss_target[0])
    loss = lax.psum(loss_local[0, 0], ("x", "y", "c"))

    big = [None] * depth
    small = {}

    def stack_rows(g):
        return g.reshape(N_CHIPS, g.shape[0] // N_CHIPS, g.shape[1])

    for i in reversed(range(depth)):
        kind, j = i % N_MIXERS, i // N_MIXERS
        L, st = layers[i], saved[i]
        gr = {}
        gr["w2"] = stack_rows(mm_tn(f"mlp_dw2_{i}", st["a"], dxb, BF16, False))
        du = mm_nt(f"mlp_du_{i}", dxb, rows_stacked(L["w2"]), BF16, epilogue=_times_2r, extras=(st["r"],))
        gr["w1"] = mm_tn(f"mlp_dw1_{i}", st["h2"], du, BF16, True)
        dh2 = mm_nt(f"mlp_dh_{i}", du, L["w1"], F32)
        dx1, dx1b, dg2 = rmsnorm_bwd(f"norm2_bwd_{i}", st["x1"], st["g2"], dh2, dx)
        small[("mlp_norm", i)] = dg2
        gr["w_out"] = stack_rows(mm_tn(f"mix_dwout_{i}", st["o"], dx1b, BF16, False))
        do = mm_nt(f"mix_do_{i}", dx1b, rows_stacked(L["w_out"]), BF16)
        if kind == 0:
            dqkv = sb_bwd(f"sb_attn_bwd_{i}", st["qkv"], st["o_f32"], do, nh)
            gr["w_in"] = mm_tn(f"sb_dwin_{i}", st["hb"], dqkv, BF16, True)
            dh = mm_nt(f"sb_dh_{i}", dqkv, L["w_in"], F32)
        elif kind == 1:
            dqh, dkh, dv, dcf = attn_bwd(f"fox_attn_bwd_{i}", st["qh"], st["kh"], st["proj"], st["o_f32"], do, st["lse"], nh,
                                         HEAD_DIM, 0, 0, 2 * nh, 1, st["scale"], st["cf_col"], st["cf_row"])
            dcf_s = _pad_cols(dcf.reshape(nh, S).T, LANE)
            dlogf = seq_cumsum(f"fox_dcf_{i}", dcf_s, True)
            dgate, db = rowwise_bwd(f"fox_gate_bwd_{i}", _fox_gate_fn, st["gate_rows"], [st["b_pad"]], [(dlogf, LANE, _c0)],
                                    [(LANE, LANE, _c0, BF16, False)], tr)
            dq, dk, dgq, dgk = rowwise_bwd(
                f"fox_qk_bwd_{i}", _fox_qk_fn, st["qk_rows"], [st["gq"], st["gk"]],
                [(dqh, HEAD_DIM, lambda h: h), (dkh, HEAD_DIM, lambda h: h)], [(W, HEAD_DIM, lambda h: h, BF16, False)] * 2, tr, nh)
            small[("fox_b_f", j)] = db[:, :nh]
            small[("fox_q_gain", j)] = dgq
            small[("fox_k_gain", j)] = dgk
            dproj = jnp.concatenate([dq, dk, bf(dv), dgate], axis=1)
            dw = mm_tn(f"fox_dwin_{i}", st["hb"], dproj, BF16, False)
            n4 = n_fox_in // N_CHIPS
            gr["w_in"] = jnp.stack([dw[:, s * n4:(s + 1) * n4] for s in range(N_CHIPS)])
            dh = mm_nt(f"fox_dh_{i}", dproj, st["w_in"], F32)
        else:
            dqh, dkh, dv = attn_bwd(f"mla_attn_bwd_{i}", st["qh"], st["kh"], st["kv"], st["o_f32"], do, st["lse"], nh,
                                    MLA_QK_PAD, 0, 0, 1, 2, st["scale"])
            dqpad, dkn, dkr, dgq, dgk = rowwise_bwd(
                f"mla_pre2_bwd_{i}", _mla_pre2_fn, st["pre2_rows"], [st["gq"], st["gk"]],
                [(dqh, MLA_QK_PAD, lambda h: h), (dkh, MLA_QK_PAD, lambda h: h)],
                [(nh * MLA_QK_PAD, MLA_QK_PAD, lambda h: h, BF16, False), (W, MLA_NOPE, lambda h: h, BF16, False),
                 (LANE, LANE, _c0, F32, True)], tr, nh, n_diff=3)
            small[("mla_q_gain", j)] = dgq[:, :MLA_QK]
            small[("mla_k_gain", j)] = dgk[:, :MLA_QK]
            dqfull = dqpad.reshape(S, nh, MLA_QK_PAD)[:, :, :MLA_QK].reshape(S, nh * MLA_QK)
            dkv = jnp.stack([dkn.reshape(S, nh, MLA_NOPE), bf(dv).reshape(S, nh, MLA_V)], axis=2).reshape(S, nh * (MLA_NOPE + MLA_V))
            gr["w_uq"] = mm_tn(f"mla_dwuq_{i}", st["c_q"], dqfull, BF16, True)
            dc_q = mm_nt(f"mla_dcq_{i}", dqfull, L["w_uq"], F32)
            gr["w_ukv"] = mm_tn(f"mla_dwukv_{i}", st["c_kv"], dkv, BF16, True)
            dc_kv = mm_nt(f"mla_dckv_{i}", dkv, L["w_ukv"], F32)
            d1, d2, d3, dqn, dkvn = rowwise_bwd(
                f"mla_pre1_bwd_{i}", st["pre1_fn"], st["pre1_rows"], [q_norm_full, kv_norm_full],
                [(dc_q, q_rank, _c0), (dc_kv, kv_rank, _c0), (dkr, LANE, _c0)],
                [(q_rank, q_rank, _c0, BF16, False), (kv_rank, kv_rank, _c0, BF16, False), (LANE, LANE, _c0, BF16, False)],
                tr, n_diff=3)
            small[("mla_q_norm", j)] = dqn
            small[("mla_kv_norm", j)] = dkvn
            ddown = jnp.concatenate([d1, d2, d3], axis=1)
            dw = mm_tn(f"mla_dwin_{i}", st["hb"], ddown, BF16, False)
            gr["w_in"] = stack_rows(dw[:, :n_down])
            dh = mm_nt(f"mla_dh_{i}", ddown, st["w_in"], F32)
        dx, dxb, dg1 = rmsnorm_bwd(f"norm1_bwd_{i}", st["x"], st["g1"], dh, dx1)
        small[("mix_norm", i)] = dg1
        big[i] = gr

    full = [None] * depth
    for i in range(depth):
        names = list(big[i])
        full[i] = dict(zip(names, reduce_grads(str(i), [big[i][n] for n in names], c_arr)))

    keys = list(small)
    flat = jnp.concatenate([small[k].reshape(-1) for k in keys])
    rows_g = -(-flat.shape[0] // (8 * LANE)) * 8
    flat = jnp.pad(flat, (0, rows_g * LANE - flat.shape[0]))
    summed = all_sum_small("sum_small", flat.reshape(rows_g, LANE)).reshape(-1)
    sg, off = {}, 0
    for k in keys:
        n = small[k].size
        sg[k] = summed[off:off + n].reshape(small[k].shape)
        off += n

    def per_layer(kind_of, key):
        return [full[i][key] for i in range(depth) if kind_of is None or i % N_MIXERS == kind_of]

    def update_big(name, w, m, v, gs):
        outs = [(g,) + tuple(adamw(f"adamw_{name}_{l}", w, g, m, v, layer=l)) for l, g in enumerate(gs)]
        return [jnp.stack([o[k] for o in outs]) for k in range(4)]

    def update_small(name, w, m, v, g):
        return [g] + list(adamw(f"adamw_{name}", w, g, m, v))

    def small_rows(name, count):
        return jnp.concatenate([sg[(name, l)] for l in range(count)], axis=0)

    def my_part(g, n):
        return lax.dynamic_slice(g, (0, chip * n), (g.shape[0], n))

    res = {
        "mix_norm": update_small("mix_norm", mix_norm, m_mix_norm, v_mix_norm, small_rows("mix_norm", depth)),
        "mlp_norm": update_small("mlp_norm", mlp_norm, m_mlp_norm, v_mlp_norm, small_rows("mlp_norm", depth)),
        "sb_w_in": update_big("sb_w_in", sb_w_in, m_sb_w_in, v_sb_w_in, per_layer(0, "w_in")),
        "sb_w_out": update_big("sb_w_out", sb_w_out, m_sb_w_out, v_sb_w_out, per_layer(0, "w_out")),
        "fox_w_in": update_big("fox_w_in", fox_w_in, m_fox_w_in, v_fox_w_in, per_layer(1, "w_in")),
        "fox_b_f": update_small("fox_b_f", fox_b_f, m_fox_b_f, v_fox_b_f, small_rows("fox_b_f", fox_b_f.shape[0])),
        "fox_q_gain": update_small("fox_q_gain", fox_q_gain, m_fox_q_gain, v_fox_q_gain, small_rows("fox_q_gain", fox_q_gain.shape[0])),
        "fox_k_gain": update_small("fox_k_gain", fox_k_gain, m_fox_k_gain, v_fox_k_gain, small_rows("fox_k_gain", fox_k_gain.shape[0])),
        "fox_w_out": update_big("fox_w_out", fox_w_out, m_fox_w_out, v_fox_w_out, per_layer(1, "w_out")),
        "mla_w_in": update_big("mla_w_in", mla_w_in, m_mla_w_in, v_mla_w_in, per_layer(2, "w_in")),
        "mla_q_norm": update_small("mla_q_norm", mla_q_norm, m_mla_q_norm, v_mla_q_norm,
                                   my_part(small_rows("mla_q_norm", mla_q_norm.shape[0]), mla_q_norm.shape[1])),
        "mla_kv_norm": update_small("mla_kv_norm", mla_kv_norm, m_mla_kv_norm, v_mla_kv_norm,
                                    my_part(small_rows("mla_kv_norm", mla_kv_norm.shape[0]), mla_kv_norm.shape[1])),
        "mla_w_uq": update_big("mla_w_uq", mla_w_uq, m_mla_w_uq, v_mla_w_uq, per_layer(2, "w_uq")),
        "mla_w_ukv": update_big("mla_w_ukv", mla_w_ukv, m_mla_w_ukv, v_mla_w_ukv, per_layer(2, "w_ukv")),
        "mla_q_gain": update_small("mla_q_gain", mla_q_gain, m_mla_q_gain, v_mla_q_gain, small_rows("mla_q_gain", mla_q_gain.shape[0])),
        "mla_k_gain": update_small("mla_k_gain", mla_k_gain, m_mla_k_gain, v_mla_k_gain, small_rows("mla_k_gain", mla_k_gain.shape[0])),
        "mla_w_out": update_big("mla_w_out", mla_w_out, m_mla_w_out, v_mla_w_out, per_layer(2, "w_out")),
        "mlp_w1": update_big("mlp_w1", mlp_w1, m_mlp_w1, v_mlp_w1, per_layer(None, "w1")),
        "mlp_w2": update_big("mlp_w2", mlp_w2, m_mlp_w2, v_mlp_w2, per_layer(None, "w2")),
    }
    order = ["mix_norm", "mlp_norm", "sb_w_in", "sb_w_out", "fox_w_in", "fox_b_f", "fox_q_gain", "fox_k_gain", "fox_w_out",
             "mla_w_in", "mla_q_norm", "mla_kv_norm", "mla_w_uq", "mla_w_ukv", "mla_q_gain", "mla_k_gain", "mla_w_out",
             "mlp_w1", "mlp_w2"]
    outs = [loss, dx.reshape(x.shape)]
    for k in range(4):
        outs += [res[n][k] for n in order]
    return tuple(outs)
```

```python
import functools
import math

import numpy as np
import jax
import jax.numpy as jnp
from jax import lax
from jax.experimental import pallas as pl
from jax.experimental.pallas import tpu as pltpu

F32 = jnp.float32
BF16 = jnp.bfloat16
MESH = pl.DeviceIdType.MESH

EPS = 1e-6
HEAD_DIM = 128
MLA_NOPE = 128
MLA_ROPE = 64
MLA_V = 128
MLA_QK = MLA_NOPE + MLA_ROPE
MLA_QK_PAD = 256
ROPE_THETA = 10000.0
N_MIXERS = 3
ADAM_LR = 0.001
ADAM_B1 = 0.9
ADAM_B2 = 0.999
ADAM_EPS = 1e-08
ADAM_WD = 0.01
ADAM_STEP = 10

LANE = 128
N_CHIPS = 4
VMEM_LIMIT = 48 * 1024 * 1024
NEG = -1e30


def _cp(sem):
    return pltpu.CompilerParams(dimension_semantics=sem, vmem_limit_bytes=VMEM_LIMIT)


def _pick(dim, cap):
    best = None
    b = LANE
    while b <= min(dim, cap):
        if dim % b == 0:
            best = b
        b += LANE
    return best if best is not None else dim


def _pick_rows(dim, cap):
    b = min(dim, cap)
    while dim % b:
        b -= 8
    return b


def _matmul(name, a, b, a_blk, a_map, b_blk, b_map, dn, grid, acc_shape, outs, extras=(), epilogue=None):
    nk = grid[2]
    n_ex, n_out = len(extras), len(outs)

    def body(*refs):
        a_ref, b_ref = refs[0], refs[1]
        ex_refs = refs[2:2 + n_ex]
        out_refs = refs[2 + n_ex:2 + n_ex + n_out]
        acc = refs[-1]
        k = pl.program_id(2)

        @pl.when(k == 0)
        def _():
            acc[...] = jnp.zeros_like(acc)

        acc[...] += lax.dot_general(a_ref[...], b_ref[...], dn, preferred_element_type=F32)

        @pl.when(k == nk - 1)
        def _():
            res = acc[...]
            vals = epilogue(res, *[e[...] for e in ex_refs]) if epilogue is not None else (res,)
            for o, v in zip(out_refs, vals):
                o[...] = v.astype(o.dtype)

    in_specs = [pl.BlockSpec(a_blk, a_map), pl.BlockSpec(b_blk, b_map)]
    in_specs += [pl.BlockSpec(blk, lambda i, j, k, m=m: m(i, j)) for (_, blk, m) in extras]
    out_specs = [pl.BlockSpec(blk, lambda i, j, k, m=m: m(i, j)) for (_, _, blk, m) in outs]
    out_shape = [jax.ShapeDtypeStruct(s, d) for (s, d, _, _) in outs]
    res = pl.pallas_call(
        body, name=name, grid=grid, in_specs=in_specs, out_specs=out_specs, out_shape=out_shape,
        scratch_shapes=[pltpu.VMEM(acc_shape, F32)],
        compiler_params=_cp(("parallel", "parallel", "arbitrary")),
    )(a, b, *[e[0] for e in extras])
    return res


BM, BN, BK = 1024, 1024, 512


def mm_nn(name, a, w, out_dtype, epilogue=None, extras=(), n_out=1, out_dtypes=None):
    M, K = a.shape
    stacked = w.ndim == 3
    n4 = w.shape[-1]
    N = n4 * (N_CHIPS if stacked else 1)
    bm, bn, bk = _pick_rows(M, BM), _pick(n4, BN), _pick(K, BK)
    nb = n4 // bn
    if stacked:
        b_blk, b_map = (None, bk, bn), (lambda i, j, k: (j // nb, k, j % nb))
    else:
        b_blk, b_map = (bk, bn), (lambda i, j, k: (k, j))
    dts = out_dtypes if out_dtypes is not None else [out_dtype] * n_out
    outs = [((M, N), d, (bm, bn), lambda i, j: (i, j)) for d in dts]
    exs = [(e, (bm, bn), lambda i, j: (i, j)) for e in extras]
    return _matmul(name, a, w, (bm, bk), lambda i, j, k: (i, k), b_blk, b_map,
                   (((1,), (0,)), ((), ())), (M // bm, N // bn, K // bk), (bm, bn), outs, exs, epilogue)


def mm_nt(name, dy, w, out_dtype, epilogue=None, extras=()):
    M, N = dy.shape
    stacked = w.ndim == 3
    K, n4 = w.shape[-2], w.shape[-1]
    bm, bn, bk = _pick_rows(M, BM), _pick(K, BN), _pick(n4, BK)
    nb = n4 // bk
    if stacked:
        b_blk, b_map = (None, bn, bk), (lambda i, j, k: (k // nb, j, k % nb))
    else:
        b_blk, b_map = (bn, bk), (lambda i, j, k: (j, k))
    outs = [((M, K), out_dtype, (bm, bn), lambda i, j: (i, j))]
    exs = [(e, (bm, bn), lambda i, j: (i, j)) for e in extras]
    return _matmul(name, dy, w, (bm, bk), lambda i, j, k: (i, k), b_blk, b_map,
                   (((1,), (1,)), ((), ())), (M // bm, K // bn, N // bk), (bm, bn), outs, exs, epilogue)[0]


def mm_tn(name, a, dy, out_dtype, stacked):
    M, K = a.shape
    N = dy.shape[1]
    n4 = N // N_CHIPS if stacked else N
    bm, bn, bk = _pick(K, BM), _pick(n4, BN), _pick_rows(M, BK)
    nb = n4 // bn
    if stacked:
        outs = [((N_CHIPS, K, n4), out_dtype, (None, bm, bn), lambda i, j: (j // nb, i, j % nb))]
    else:
        outs = [((K, N), out_dtype, (bm, bn), lambda i, j: (i, j))]
    return _matmul(name, a, dy, (bk, bm), lambda i, j, k: (k, i), (bk, bn), lambda i, j, k: (k, j),
                   (((0,), (0,)), ((), ())), (K // bm, N // bn, M // bk), (bm, bn), outs)[0]


def _split3(x):
    hi = x.astype(BF16)
    r = x - hi.astype(F32)
    mid = r.astype(BF16)
    lo = (r - mid.astype(F32)).astype(BF16)
    return hi, mid, lo


def _exact_dot(x, m, n):
    out = None
    for p in _split3(x)[:n]:
        d = jnp.dot(p, m, preferred_element_type=F32)
        out = d if out is None else out + d
    return out


def _row_spec(tr, width, cmap):
    return pl.BlockSpec((tr, width), lambda i, h: (i, cmap(h)))


def _full_spec(p):
    return pl.BlockSpec(p.shape, lambda i, h: (0,) * p.ndim)


def rowwise_fwd(name, fn, rows, params, outs, tr, nh=1):
    S = rows[0][0].shape[0]
    nr, npar = len(rows), len(params)

    def body(*refs):
        vals = fn(*[r[...].astype(F32) for r in refs[:nr]], *[p[...] for p in refs[nr:nr + npar]])
        for o, v in zip(refs[nr + npar:], vals):
            o[...] = v.astype(o.dtype)

    return pl.pallas_call(
        body, name=name, grid=(S // tr, nh),
        in_specs=[_row_spec(tr, w, cm) for (_, w, cm) in rows] + [_full_spec(p) for p in params],
        out_specs=[_row_spec(tr, w, cm) for (_, w, cm, _) in outs],
        out_shape=[jax.ShapeDtypeStruct((S, c), d) for (c, _, _, d) in outs],
        compiler_params=_cp(("parallel", "arbitrary")),
    )(*[r[0] for r in rows], *params)


def rowwise_bwd(name, fn, rows, params, cts, grads, tr, nh=1, n_diff=None, add=None):
    S = rows[0][0].shape[0]
    nr, npar, nct = len(rows), len(params), len(cts)
    n_diff = nr if n_diff is None else n_diff
    n_add = 1 if add is not None else 0

    def body(*refs):
        row_refs = refs[:nr]
        par_refs = refs[nr:nr + npar]
        ct_refs = refs[nr + npar:nr + npar + nct]
        add_refs = refs[nr + npar + nct:nr + npar + nct + n_add]
        o = nr + npar + nct + n_add
        g_refs = refs[o:o + n_diff]
        cp_refs = refs[o + n_diff:o + n_diff + n_add]
        pg_refs = refs[o + n_diff + n_add:]
        i, h = pl.program_id(0), pl.program_id(1)
        rv = [r[...].astype(F32) for r in row_refs]
        pv = [p[...] for p in par_refs]
        aux = rv[n_diff:]

        def f(*dp):
            return fn(*dp[:n_diff], *aux, *dp[n_diff:])

        _, vjp = jax.vjp(f, *rv[:n_diff], *pv)
        gs = vjp(tuple(c[...].astype(F32) for c in ct_refs))
        for n, (g_ref, (_, _, _, _, over)) in enumerate(zip(g_refs, grads)):
            g = gs[n]
            if n == 0 and add is not None:
                g = g + add_refs[0][...]
                cp_refs[0][...] = g.astype(BF16)
            if over:
                @pl.when(h == 0)
                def _(g_ref=g_ref):
                    g_ref[...] = jnp.zeros_like(g_ref)
                g_ref[...] += g.astype(g_ref.dtype)
            else:
                g_ref[...] = g.astype(g_ref.dtype)
        for pg_ref, g in zip(pg_refs, gs[n_diff:]):
            @pl.when((i == 0) & (h == 0))
            def _(pg_ref=pg_ref):
                pg_ref[...] = jnp.zeros_like(pg_ref)
            pg_ref[...] += g

    in_specs = [_row_spec(tr, w, cm) for (_, w, cm) in rows] + [_full_spec(p) for p in params]
    in_specs += [_row_spec(tr, w, cm) for (_, w, cm) in cts]
    operands = [r[0] for r in rows] + list(params) + [c[0] for c in cts]
    out_specs = [_row_spec(tr, w, cm) for (_, w, cm, _, _) in grads]
    out_shape = [jax.ShapeDtypeStruct((S, c), d) for (c, _, _, d, _) in grads]
    if add is not None:
        in_specs.append(_row_spec(tr, add[1], add[2]))
        operands.append(add[0])
        out_specs.append(_row_spec(tr, add[1], add[2]))
        out_shape.append(jax.ShapeDtypeStruct(add[0].shape, BF16))
    out_specs += [_full_spec(p) for p in params]
    out_shape += [jax.ShapeDtypeStruct(p.shape, F32) for p in params]
    return pl.pallas_call(
        body, name=name, grid=(S // tr, nh), in_specs=in_specs, out_specs=out_specs, out_shape=out_shape,
        compiler_params=_cp(("arbitrary", "arbitrary")),
    )(*operands)


def _c0(h):
    return 0


def _rms(x, g, n):
    return x * lax.rsqrt(jnp.sum(x * x, axis=-1, keepdims=True) * (1.0 / n) + EPS) * g


def _rmsnorm_fn(d):
    def fn(x, g):
        return (_rms(x, g, d),)
    return fn


def rmsnorm_fwd(name, x, g):
    S, D = x.shape
    return rowwise_fwd(name, _rmsnorm_fn(D), [(x, D, _c0)], [g], [(D, D, _c0, BF16)], _pick_rows(S, 256))[0]


def rmsnorm_bwd(name, x, g, dh, dres):
    S, D = x.shape
    return rowwise_bwd(name, _rmsnorm_fn(D), [(x, D, _c0)], [g], [(dh, D, _c0)], [(D, D, _c0, F32, False)],
                       _pick_rows(S, 256), add=(dres, D, _c0))


def loss_head(y, t):
    S, D = y.shape
    tr = _pick_rows(S, 256)

    def body(y_ref, t_ref, l_ref, d_ref, db_ref):
        @pl.when(pl.program_id(0) == 0)
        def _():
            l_ref[...] = jnp.zeros_like(l_ref)
        e = y_ref[...] - t_ref[...]
        l_ref[...] += 0.5 * jnp.sum(jnp.sum(e * e, axis=1, keepdims=True), axis=0, keepdims=True) * (1.0 / D)
        d = e * (1.0 / D)
        d_ref[...] = d
        db_ref[...] = d.astype(BF16)

    row = pl.BlockSpec((tr, D), lambda i: (i, 0))
    return pl.pallas_call(
        body, name="loss_head", grid=(S // tr,), in_specs=[row, row],
        out_specs=[pl.BlockSpec((1, 1), lambda i: (0, 0)), row, row],
        out_shape=[jax.ShapeDtypeStruct((1, 1), F32), jax.ShapeDtypeStruct((S, D), F32), jax.ShapeDtypeStruct((S, D), BF16)],
        compiler_params=_cp(("arbitrary",)),
    )(y, t)


def _dot_nt(a, b):
    return lax.dot_general(a, b, (((1,), (1,)), ((), ())), preferred_element_type=F32)


def _dot_tn(a, b):
    return lax.dot_general(a, b, (((0,), (0,)), ((), ())), preferred_element_type=F32)


def attn_fwd(name, q, k, v, nh, dq, qoff, koff, voff, vstep, scale, cf_col=None, cf_row=None):
    S = q.shape[0]
    tq = _pick_rows(S, 256)
    tk = tq
    nq = S // tq
    bias = cf_col is not None

    def body(*refs):
        if bias:
            q_ref, k_ref, v_ref, cfc_ref, cfr_ref, o_ref, of_ref, lse_ref = refs
        else:
            q_ref, k_ref, v_ref, o_ref, of_ref, lse_ref = refs
        i = pl.program_id(1)
        qv = q_ref[...].astype(BF16)
        row = i * tq + lax.broadcasted_iota(jnp.int32, (tq, tk), 0)
        coli = lax.broadcasted_iota(jnp.int32, (tq, tk), 1)

        def step(j, carry):
            m, l, acc, acc_lo = carry
            off = pl.multiple_of(j * tk, tk)
            ks = k_ref[pl.ds(off, tk), :].astype(BF16)
            vs = v_ref[pl.ds(off, tk), :].astype(BF16)
            s = _dot_nt(qv, ks) * scale
            if bias:
                s = s + cfc_ref[0] - cfr_ref[0, pl.ds(j, 1), :]
            s = jnp.where(j * tk + coli <= row, s, NEG)
            m_new = jnp.maximum(m, jnp.max(s, axis=1, keepdims=True))
            alpha = jnp.exp(m - m_new)
            p = jnp.exp(s - m_new)
            pb = p.astype(BF16)
            l = alpha * l + jnp.sum(p, axis=1, keepdims=True)
            acc = alpha * acc + jnp.dot(pb, vs, preferred_element_type=F32)
            acc_lo = alpha * acc_lo + jnp.dot((p - pb.astype(F32)).astype(BF16), vs, preferred_element_type=F32)
            return m_new, l, acc, acc_lo

        z = jnp.zeros((tq, HEAD_DIM), F32)
        m, l, acc, acc_lo = lax.fori_loop(0, i + 1, step, (jnp.full((tq, 1), NEG, F32), jnp.zeros((tq, 1), F32), z, z))
        o_ref[...] = (acc / l).astype(o_ref.dtype)
        of_ref[...] = (acc + acc_lo) / l
        lse_ref[0] = m + jnp.log(l)

    in_specs = [pl.BlockSpec((tq, dq), lambda h, i: (i, qoff + h)),
                pl.BlockSpec((S, dq), lambda h, i: (0, koff + h)),
                pl.BlockSpec((S, HEAD_DIM), lambda h, i: (0, voff + vstep * h))]
    operands = [q, k, v]
    if bias:
        in_specs += [pl.BlockSpec((1, tq, 1), lambda h, i: (h, i, 0)), pl.BlockSpec((1, S // tk, tk), lambda h, i: (h, 0, 0))]
        operands += [cf_col, cf_row]
    return pl.pallas_call(
        body, name=name, grid=(nh, nq), in_specs=in_specs,
        out_specs=[pl.BlockSpec((tq, HEAD_DIM), lambda h, i: (i, h)), pl.BlockSpec((tq, HEAD_DIM), lambda h, i: (i, h)),
                   pl.BlockSpec((1, tq, 1), lambda h, i: (h, i, 0))],
        out_shape=[jax.ShapeDtypeStruct((S, nh * HEAD_DIM), BF16), jax.ShapeDtypeStruct((S, nh * HEAD_DIM), F32),
                   jax.ShapeDtypeStruct((nh, S, 1), F32)],
        compiler_params=_cp(("parallel", "arbitrary")),
    )(*operands)


def attn_bwd(name, q, k, v, o, do, lse, nh, dq, qoff, koff, voff, vstep, scale, cf_col=None, cf_row=None):
    S = q.shape[0]
    tq = _pick_rows(S, 256)
    tk = tq
    nq = S // tq
    bias = cf_col is not None

    def body(*refs):
        if bias:
            q_ref, k_ref, v_ref, o_ref, do_ref, lse_ref, cfc_ref, cfr_ref, dq_ref, dk_ref, dv_ref, dcf_ref = refs
        else:
            q_ref, k_ref, v_ref, o_ref, do_ref, lse_ref, dq_ref, dk_ref, dv_ref = refs
        dk_ref[...] = jnp.zeros_like(dk_ref)
        dv_ref[...] = jnp.zeros_like(dv_ref)
        if bias:
            dcf_ref[...] = jnp.zeros_like(dcf_ref)
        rowi = lax.broadcasted_iota(jnp.int32, (tq, tk), 0)
        coli = lax.broadcasted_iota(jnp.int32, (tq, tk), 1)

        def outer(i, _):
            roff = pl.multiple_of(i * tq, tq)
            qv = q_ref[pl.ds(roff, tq), :].astype(BF16)
            dov = do_ref[pl.ds(roff, tq), :]
            delta = jnp.sum(dov.astype(F32) * o_ref[pl.ds(roff, tq), :].astype(F32), axis=1, keepdims=True)
            lse = lse_ref[0, pl.ds(roff, tq), :]
            if bias:
                cq = cfc_ref[0, pl.ds(roff, tq), :]

            def inner(j, dq_acc):
                off = pl.multiple_of(j * tk, tk)
                ks = k_ref[pl.ds(off, tk), :].astype(BF16)
                vs = v_ref[pl.ds(off, tk), :].astype(BF16)
                s = _dot_nt(qv, ks) * scale
                if bias:
                    s = s + cq - cfr_ref[0, pl.ds(j, 1), :]
                p = jnp.where(j * tk + coli <= i * tq + rowi, jnp.exp(s - lse), 0.0)
                dp = _dot_nt(dov, vs)
                ds = p * (dp - delta)
                if bias:
                    dcf_ref[0, pl.ds(j, 1), :] -= jnp.sum(ds, axis=0, keepdims=True)
                dsb = (ds * scale).astype(BF16)
                dv_ref[pl.ds(off, tk), :] += _dot_tn(p.astype(BF16), dov)
                dk_ref[pl.ds(off, tk), :] += _dot_tn(dsb, qv)
                return dq_acc + jnp.dot(dsb, ks, preferred_element_type=F32)

            dq_ref[pl.ds(roff, tq), :] = lax.fori_loop(0, i + 1, inner, jnp.zeros((tq, dq), F32))
            return 0

        lax.fori_loop(0, nq, outer, 0)

    in_specs = [pl.BlockSpec((S, dq), lambda h: (0, qoff + h)),
                pl.BlockSpec((S, dq), lambda h: (0, koff + h)),
                pl.BlockSpec((S, HEAD_DIM), lambda h: (0, voff + vstep * h)),
                pl.BlockSpec((S, HEAD_DIM), lambda h: (0, h)),
                pl.BlockSpec((S, HEAD_DIM), lambda h: (0, h)),
                pl.BlockSpec((1, S, 1), lambda h: (h, 0, 0))]
    operands = [q, k, v, o, do, lse]
    out_specs = [pl.BlockSpec((S, dq), lambda h: (0, h)), pl.BlockSpec((S, dq), lambda h: (0, h)),
                 pl.BlockSpec((S, HEAD_DIM), lambda h: (0, h))]
    out_shape = [jax.ShapeDtypeStruct((S, nh * dq), F32), jax.ShapeDtypeStruct((S, nh * dq), F32),
                 jax.ShapeDtypeStruct((S, nh * HEAD_DIM), F32)]
    if bias:
        in_specs += [pl.BlockSpec((1, S, 1), lambda h: (h, 0, 0)), pl.BlockSpec((1, S // tk, tk), lambda h: (h, 0, 0))]
        operands += [cf_col, cf_row]
        out_specs.append(pl.BlockSpec((1, S // tk, tk), lambda h: (h, 0, 0)))
        out_shape.append(jax.ShapeDtypeStruct((nh, S // tk, tk), F32))
    return pl.pallas_call(
        body, name=name, grid=(nh,), in_specs=in_specs, out_specs=out_specs, out_shape=out_shape,
        compiler_params=_cp(("parallel",)),
    )(*operands)


CH = 128


def _sb_chunk(qv, kc, row, col, scale, later_c, tri_after):
    z = _dot_nt(qv, kc) * scale
    strict = col < row
    lsn = jnp.minimum(-z, 0.0) - jnp.log(1.0 + jnp.exp(-jnp.abs(z)))
    lsp = lsn + z
    L = jnp.where(strict, lsn, 0.0)
    later = _exact_dot(L, tri_after, 2) + later_c
    a = jnp.where(strict, jnp.exp(lsp + later), 0.0)
    return strict, lsp, lsn, L, a


def _tri(after_strict):
    r = lax.broadcasted_iota(jnp.int32, (CH, CH), 0)
    c = lax.broadcasted_iota(jnp.int32, (CH, CH), 1)
    return jnp.where(r > c if after_strict else r >= c, 1.0, 0.0).astype(BF16)


def sb_fwd(name, qkv, nh):
    S = qkv.shape[0]
    tq = _pick_rows(S, 256)
    nq = S // tq
    cpq = tq // CH
    scale = 1.0 / math.sqrt(HEAD_DIM)

    def body(q_ref, k_ref, v_ref, o_ref, of_ref):
        i = pl.program_id(1)
        qv = q_ref[...]
        tri = _tri(True)
        row = i * tq + lax.broadcasted_iota(jnp.int32, (tq, CH), 0)
        coli = lax.broadcasted_iota(jnp.int32, (tq, CH), 1)
        nch = (i + 1) * cpq

        def step(jj, carry):
            later_c, acc, acc_lo = carry
            jc = nch - 1 - jj
            off = pl.multiple_of(jc * CH, CH)
            kc = k_ref[pl.ds(off, CH), :]
            vc = v_ref[pl.ds(off, CH), :]
            _, _, _, L, a = _sb_chunk(qv, kc, row, jc * CH + coli, scale, later_c, tri)
            ab = a.astype(BF16)
            acc = acc + jnp.dot(ab, vc, preferred_element_type=F32)
            acc_lo = acc_lo + jnp.dot((a - ab.astype(F32)).astype(BF16), vc, preferred_element_type=F32)
            return later_c + jnp.sum(L, axis=1, keepdims=True), acc, acc_lo

        z = jnp.zeros((tq, HEAD_DIM), F32)
        _, acc, acc_lo = lax.fori_loop(0, nch, step, (jnp.zeros((tq, 1), F32), z, z))
        o_ref[...] = acc.astype(o_ref.dtype)
        of_ref[...] = acc + acc_lo

    blk = pl.BlockSpec((tq, HEAD_DIM), lambda h, i: (i, h))
    return pl.pallas_call(
        body, name=name, grid=(nh, nq),
        in_specs=[blk,
                  pl.BlockSpec((S, HEAD_DIM), lambda h, i: (0, nh + h)),
                  pl.BlockSpec((S, HEAD_DIM), lambda h, i: (0, 2 * nh + h))],
        out_specs=[blk, blk],
        out_shape=[jax.ShapeDtypeStruct((S, nh * HEAD_DIM), BF16), jax.ShapeDtypeStruct((S, nh * HEAD_DIM), F32)],
        compiler_params=_cp(("parallel", "arbitrary")),
    )(qkv, qkv, qkv)


def sb_bwd(name, qkv, o, do, nh):
    S = qkv.shape[0]
    tq = _pick_rows(S, 256)
    nq = S // tq
    cpq = tq // CH
    scale = 1.0 / math.sqrt(HEAD_DIM)

    def body(q_ref, k_ref, v_ref, o_ref, do_ref, dq_ref, dk_ref, dv_ref, dk_acc, dv_acc):
        dk_acc[...] = jnp.zeros_like(dk_acc)
        dv_acc[...] = jnp.zeros_like(dv_acc)
        tri = _tri(True)
        tri_inc = _tri(False)
        rowi = lax.broadcasted_iota(jnp.int32, (tq, CH), 0)
        coli = lax.broadcasted_iota(jnp.int32, (tq, CH), 1)

        def outer(i, _):
            roff = pl.multiple_of(i * tq, tq)
            qv = q_ref[pl.ds(roff, tq), :]
            dov = do_ref[pl.ds(roff, tq), :]
            dtot = jnp.sum(dov.astype(F32) * o_ref[pl.ds(roff, tq), :].astype(F32), axis=1, keepdims=True)
            row = i * tq + rowi
            nch = (i + 1) * cpq

            def inner(jj, carry):
                later_c, suf_c, dq_acc = carry
                jc = nch - 1 - jj
                off = pl.multiple_of(jc * CH, CH)
                kc = k_ref[pl.ds(off, CH), :]
                vc = v_ref[pl.ds(off, CH), :]
                strict, lsp, lsn, L, a = _sb_chunk(qv, kc, row, jc * CH + coli, scale, later_c, tri)
                dl = _dot_nt(dov, vc) * a
                before = dtot - (suf_c + _exact_dot(dl, tri_inc, 3))
                dz = jnp.where(strict, dl * jnp.exp(lsn) - jnp.exp(lsp) * before, 0.0) * scale
                dzb = dz.astype(BF16)
                dv_acc[pl.ds(off, CH), :] += _dot_tn(a.astype(BF16), dov)
                dk_acc[pl.ds(off, CH), :] += _dot_tn(dzb, qv)
                return (later_c + jnp.sum(L, axis=1, keepdims=True), suf_c + jnp.sum(dl, axis=1, keepdims=True),
                        dq_acc + jnp.dot(dzb, kc, preferred_element_type=F32))

            z1 = jnp.zeros((tq, 1), F32)
            _, _, dq_acc = lax.fori_loop(0, nch, inner, (z1, z1, jnp.zeros((tq, HEAD_DIM), F32)))
            dq_ref[pl.ds(roff, tq), :] = dq_acc.astype(dq_ref.dtype)
            return 0

        lax.fori_loop(0, nq, outer, 0)
        dk_ref[...] = dk_acc[...].astype(dk_ref.dtype)
        dv_ref[...] = dv_acc[...].astype(dv_ref.dtype)

    def col(off):
        return pl.BlockSpec((S, HEAD_DIM), lambda h: (0, off + h))

    dq, dk, dv = pl.pallas_call(
        body, name=name, grid=(nh,),
        in_specs=[col(0), col(nh), col(2 * nh), col(0), col(0)],
        out_specs=[col(0), col(0), col(0)],
        out_shape=[jax.ShapeDtypeStruct((S, nh * HEAD_DIM), BF16)] * 3,
        scratch_shapes=[pltpu.VMEM((S, HEAD_DIM), F32), pltpu.VMEM((S, HEAD_DIM), F32)],
        compiler_params=_cp(("parallel",)),
    )(qkv, qkv, qkv, o, do)
    return jnp.concatenate([dq, dk, dv], axis=1)


def seq_cumsum(name, x, reverse):
    S, W = x.shape

    tb = _pick_rows(S, 256)

    def body(x_ref, o_ref):
        parts = _split3(x_ref[...])
        c = lax.broadcasted_iota(jnp.int32, (tb, S), 1)
        for b in range(S // tb):
            r = b * tb + lax.broadcasted_iota(jnp.int32, (tb, S), 0)
            t = jnp.where(r <= c if reverse else r >= c, 1.0, 0.0).astype(BF16)
            out = None
            for p in parts:
                d = jnp.dot(t, p, preferred_element_type=F32)
                out = d if out is None else out + d
            o_ref[b * tb:(b + 1) * tb, :] = out

    return pl.pallas_call(body, name=name, out_shape=jax.ShapeDtypeStruct((S, W), F32),
                          compiler_params=pltpu.CompilerParams(vmem_limit_bytes=VMEM_LIMIT))(x)


def _adamw_math(w, g, m, v):
    c1 = 1.0 - ADAM_B1 ** ADAM_STEP
    c2 = 1.0 - ADAM_B2 ** ADAM_STEP
    nm = ADAM_B1 * m + (1.0 - ADAM_B1) * g
    nv = ADAM_B2 * v + (1.0 - ADAM_B2) * (g * g)
    return -ADAM_LR * ((nm / c1) / (jnp.sqrt(nv / c2) + ADAM_EPS) + ADAM_WD * w), nm, nv


def adamw(name, w, g, m, v):
    R, C = g.shape
    tr = _pick_rows(R, 256)

    def body(w_ref, g_ref, m_ref, v_ref, d_ref, nm_ref, nv_ref):
        d_ref[...], nm_ref[...], nv_ref[...] = _adamw_math(w_ref[...], g_ref[...], m_ref[...], v_ref[...])

    row = pl.BlockSpec((tr, C), lambda i: (i, 0))
    return pl.pallas_call(
        body, name=name, grid=(R // tr,), in_specs=[row] * 4, out_specs=[row] * 3,
        out_shape=[jax.ShapeDtypeStruct((R, C), F32)] * 3, compiler_params=_cp(("parallel",)),
    )(w, g, m, v)


def adamw_big(name, w, m, v, mine, other, c_arr, layer, prev):
    L, R, C = w.shape
    r2 = R // 2
    tr = _pick_rows(r2, 256)
    nb = r2 // tr

    def body(c_ref, w_ref, m_ref, v_ref, mine_ref, other_ref, *rest):
        g_ref, d_ref, nm_ref, nv_ref = rest[-4:]
        g = jnp.where(pl.program_id(0) == c_ref[0], mine_ref[...], other_ref[...])
        g_ref[...] = g
        d_ref[...], nm_ref[...], nv_ref[...] = _adamw_math(w_ref[...], g, m_ref[...], v_ref[...])

    sel = pl.BlockSpec((None, tr, C), lambda hf, i, c: (layer, hf * nb + i, 0))
    in_specs = [sel, sel, sel,
                pl.BlockSpec((tr, C), lambda hf, i, c: (jnp.where(hf == c[0], i, 0), 0)),
                pl.BlockSpec((tr, C), lambda hf, i, c: (jnp.where(hf == c[0], 0, i), 0))]
    operands = [c_arr, w, m, v, mine, other]
    aliases = {}
    if prev is not None:
        in_specs += [ANY] * 4
        aliases = {len(operands) + k: k for k in range(4)}
        operands += list(prev)
    return pl.pallas_call(
        body, name=name,
        grid_spec=pltpu.PrefetchScalarGridSpec(num_scalar_prefetch=1, grid=(2, nb), in_specs=in_specs, out_specs=[sel] * 4),
        out_shape=[jax.ShapeDtypeStruct((L, R, C), F32)] * 4, input_output_aliases=aliases,
        compiler_params=_cp(("arbitrary", "arbitrary")),
    )(*operands)


ANY = pl.BlockSpec(memory_space=pl.ANY)


def _place():
    x, y, c = lax.axis_index("x"), lax.axis_index("y"), lax.axis_index("c")
    others = [(1 - x, y), (x, 1 - y), (1 - x, 1 - y)]
    return x, y, c, others


def swap_halves(name, grads):
    n = len(grads)

    def body(*refs):
        src, dst = refs[:n], refs[n:2 * n]
        send, recv = refs[2 * n:]
        x, y, c, _ = _place()
        cps = []
        for t in range(n):
            r2 = grads[t].shape[1] // 2
            cps.append(pltpu.make_async_remote_copy(
                src_ref=src[t].at[:, pl.ds((1 - c) * r2, r2), :], dst_ref=dst[t],
                send_sem=send.at[t], recv_sem=recv.at[t], device_id=(x, y, 1 - c), device_id_type=MESH))
        for cp in cps:
            cp.start()
        for cp in cps:
            cp.wait()

    return pl.pallas_call(
        body, name=name, in_specs=[ANY] * n, out_specs=[ANY] * n,
        out_shape=[jax.ShapeDtypeStruct((N_CHIPS, g.shape[1] // 2, g.shape[2]), g.dtype) for g in grads],
        scratch_shapes=[pltpu.SemaphoreType.DMA((n,))] * 2,
    )(*grads)


def add_half(name, g, other, c_arr):
    _, R, C = g.shape
    r2 = R // 2
    tr = _pick_rows(r2, 512)
    nb = r2 // tr

    def body(c_ref, g_ref, o_ref, out_ref):
        out_ref[...] = (g_ref[...].astype(F32) + o_ref[...].astype(F32)).astype(out_ref.dtype)

    return pl.pallas_call(
        body, name=name,
        grid_spec=pltpu.PrefetchScalarGridSpec(
            num_scalar_prefetch=1, grid=(N_CHIPS, nb),
            in_specs=[pl.BlockSpec((None, tr, C), lambda s, i, c: (s, c[0] * nb + i, 0)),
                      pl.BlockSpec((None, tr, C), lambda s, i, c: (s, i, 0))],
            out_specs=pl.BlockSpec((None, tr, C), lambda s, i, c: (s, i, 0))),
        out_shape=jax.ShapeDtypeStruct((N_CHIPS, r2, C), BF16),
        compiler_params=_cp(("parallel", "parallel")),
    )(c_arr, g, other)


def sum_slabs(name, got, parts, chip_arr):
    _, r2, C = parts.shape
    tr = _pick_rows(r2, 512)

    def body(chip_ref, *refs):
        own_ref, out_ref = refs[N_CHIPS], refs[N_CHIPS + 1]
        acc = None
        for s in range(N_CHIPS):
            v = jnp.where(chip_ref[0] == s, own_ref[...], refs[s][...]).astype(F32)
            acc = v if acc is None else acc + v
        out_ref[...] = acc

    def slab(s):
        return pl.BlockSpec((None, tr, C), lambda i, ch: (jnp.where(ch[0] == s, (s + 1) % N_CHIPS, s), i, 0))

    return pl.pallas_call(
        body, name=name,
        grid_spec=pltpu.PrefetchScalarGridSpec(
            num_scalar_prefetch=1, grid=(r2 // tr,),
            in_specs=[slab(s) for s in range(N_CHIPS)] + [pl.BlockSpec((None, tr, C), lambda i, ch: (ch[0], i, 0))],
            out_specs=pl.BlockSpec((tr, C), lambda i, ch: (i, 0))),
        out_shape=jax.ShapeDtypeStruct((r2, C), F32), compiler_params=_cp(("parallel",)),
    )(chip_arr, got, got, got, got, parts)


def send_halves(name, halves):
    n = len(halves)

    def body(*refs):
        src, dst = refs[:n], refs[n:2 * n]
        send, recv = refs[2 * n:]
        x, y, c, _ = _place()
        cps = [pltpu.make_async_remote_copy(src_ref=src[t], dst_ref=dst[t], send_sem=send.at[t], recv_sem=recv.at[t],
                                            device_id=(x, y, 1 - c), device_id_type=MESH) for t in range(n)]
        for cp in cps:
            cp.start()
        for cp in cps:
            cp.wait()

    return pl.pallas_call(
        body, name=name, in_specs=[ANY] * n, out_specs=[ANY] * n,
        out_shape=[jax.ShapeDtypeStruct(h.shape, h.dtype) for h in halves],
        scratch_shapes=[pltpu.SemaphoreType.DMA((n,))] * 2,
    )(*halves)


HBM = pl.BlockSpec(memory_space=pltpu.HBM)
SEM = pl.BlockSpec(memory_space=pltpu.SEMAPHORE)
EFFECT = pltpu.SideEffectType.DATAFLOW_SIDE_EFFECTING


def _in_hbm(a):
    return pltpu.with_memory_space_constraint(a, pltpu.HBM)


def _chip_copies(kind, shapes, src, land, send, recv, mine):
    x, y, c, others = _place()
    me = 2 * x + y
    cps = []
    for t, shape in enumerate(shapes):
        for j, (px, py) in enumerate(others):
            slot = me if mine else 2 * px + py
            if kind == "gather":
                r2 = shape[0] // 2
                rows = pl.ds(c * r2, r2)
                s_ref, d_ref = src[t].at[rows, :], land[t].at[slot, rows, :]
            else:
                s_ref, d_ref = src[t].at[2 * px + py], land[t].at[slot]
            cps.append(pltpu.make_async_remote_copy(src_ref=s_ref, dst_ref=d_ref, send_sem=send.at[3 * t + j],
                                                    recv_sem=recv.at[3 * t + j], device_id=(px, py, c), device_id_type=MESH))
    return cps


def chips_start(name, kind, srcs, land_shapes):
    n = len(srcs)
    shapes = [s.shape for s in srcs]

    def body(*refs):
        src, land = refs[:n], refs[n:2 * n]
        send, recv = refs[2 * n], refs[2 * n + 1]
        token = refs[-1]
        for cp in _chip_copies(kind, shapes, src, land, send, recv, True):
            cp.start()
        token[...] = jnp.zeros_like(token)

    lands = [lax.empty(s, srcs[0].dtype) for s in land_shapes]
    out = pl.pallas_call(
        body, name=name,
        out_shape=(pltpu.SemaphoreType.DMA((3 * n,)), pltpu.SemaphoreType.DMA((3 * n,)))
        + tuple(pltpu.HBM(s.shape, s.dtype) for s in srcs) + tuple(pltpu.HBM(l.shape, l.dtype) for l in lands)
        + (jax.ShapeDtypeStruct((8, LANE), F32),),
        in_specs=(HBM,) * (2 * n), out_specs=(SEM, SEM) + (HBM,) * (2 * n) + (pl.BlockSpec(memory_space=pltpu.VMEM),),
        input_output_aliases={k: 2 + k for k in range(2 * n)},
        compiler_params=pltpu.CompilerParams(has_side_effects=EFFECT),
    )(*[_in_hbm(s) for s in srcs], *[_in_hbm(l) for l in lands])
    return (kind, shapes, out[0], out[1], out[2:2 + n], out[2 + n:2 + 2 * n]), out[-1][:1, :1]


def chips_wait(name, handle, after):
    kind, shapes, send, recv, srcs, lands = handle
    n = len(srcs)

    def body(*refs):
        src, land = refs[:n], refs[n:2 * n]
        for cp in _chip_copies(kind, shapes, src, land, refs[2 * n], refs[2 * n + 1], True):
            cp.wait_send()
        for cp in _chip_copies(kind, shapes, src, land, refs[2 * n], refs[2 * n + 1], False):
            cp.wait_recv()

    out = pl.pallas_call(
        body, name=name,
        out_shape=tuple(pltpu.HBM(s.shape, s.dtype) for s in srcs) + tuple(pltpu.HBM(l.shape, l.dtype) for l in lands),
        in_specs=(HBM,) * (2 * n) + (SEM, SEM, ANY), out_specs=(HBM,) * (2 * n),
        input_output_aliases={k: k for k in range(2 * n)},
        compiler_params=pltpu.CompilerParams(has_side_effects=EFFECT),
    )(*srcs, *lands, send, recv, after)
    return list(out[n:]), list(out[:n])


def pass_to_sibling(name, lands, shards):
    n = len(lands)

    def body(*refs):
        buf = refs[n:2 * n]
        send, recv = refs[2 * n:]
        x, y, c, others = _place()

        def cp(t, j, core):
            r2 = lands[t].shape[1] // 2
            px, py = others[j]
            rows = buf[t].at[2 * px + py, pl.ds(core * r2, r2), :]
            return pltpu.make_async_remote_copy(src_ref=rows, dst_ref=rows, send_sem=send.at[t, j], recv_sem=recv.at[t, j],
                                                device_id=(x, y, 1 - c), device_id_type=MESH)

        for t in range(n):
            for j in range(3):
                cp(t, j, c).start()
        for t in range(n):
            for j in range(3):
                cp(t, j, 1 - c).wait_recv()
        for t in range(n):
            for j in range(3):
                cp(t, j, c).wait_send()

    got = pl.pallas_call(
        body, name=name, in_specs=[ANY] * n, out_specs=[ANY] * n,
        out_shape=[jax.ShapeDtypeStruct(l.shape, l.dtype) for l in lands],
        input_output_aliases={k: k for k in range(n)},
        scratch_shapes=[pltpu.SemaphoreType.DMA((n, 3))] * 2,
    )(*lands)
    chip = 2 * lax.axis_index("x") + lax.axis_index("y")
    return [lax.dynamic_update_slice(g, s[None], (chip, 0, 0)) for g, s in zip(got, shards)]


def all_sum_small(name, v):
    R = v.shape[0]

    def body(v_ref, out_ref, slots, send, recv):
        x, y, c, _ = _place()
        me = 4 * x + 2 * y + c
        slots[me] = v_ref[...]
        cps = []
        for k in range(1, 8):
            dx, dy, dc = (k >> 2) & 1, (k >> 1) & 1, k & 1
            to = (x ^ dx, y ^ dy, c ^ dc)
            cps.append(pltpu.make_async_remote_copy(
                src_ref=v_ref, dst_ref=slots.at[me], send_sem=send.at[k - 1], recv_sem=recv.at[k - 1],
                device_id=to, device_id_type=MESH))
        for cp in cps:
            cp.start()
        for k in range(1, 8):
            dx, dy, dc = (k >> 2) & 1, (k >> 1) & 1, k & 1
            frm = 4 * (x ^ dx) + 2 * (y ^ dy) + (c ^ dc)
            pltpu.make_async_remote_copy(
                src_ref=v_ref, dst_ref=slots.at[frm], send_sem=send.at[k - 1], recv_sem=recv.at[k - 1],
                device_id=(x, y, c), device_id_type=MESH).wait_recv()
        for cp in cps:
            cp.wait_send()
        acc = slots[0]
        for d in range(1, 8):
            acc = acc + slots[d]
        out_ref[...] = acc

    vm = pl.BlockSpec(memory_space=pltpu.VMEM)
    return pl.pallas_call(
        body, name=name, in_specs=[vm], out_specs=vm, out_shape=jax.ShapeDtypeStruct((R, LANE), F32),
        scratch_shapes=[pltpu.VMEM((8, R, LANE), F32), pltpu.SemaphoreType.DMA((7,)), pltpu.SemaphoreType.DMA((7,))],
    )(v)


def _rope_mat(n, lo):
    half = MLA_ROPE // 2
    r = lax.broadcasted_iota(jnp.int32, (n, n), 0)
    c = lax.broadcasted_iota(jnp.int32, (n, n), 1)
    plus = (c >= lo + half) & (c < lo + 2 * half) & (r == c - half)
    minus = (c >= lo) & (c < lo + half) & (r == c + half)
    return (jnp.where(plus, 1.0, 0.0) - jnp.where(minus, 1.0, 0.0)).astype(BF16)


def _rope(v, cos, sin, lo):
    return v * cos + _exact_dot(v, _rope_mat(v.shape[-1], lo), 3) * sin


def rope_tables(pos):
    S = pos.shape[0]
    half = MLA_ROPE // 2
    inv = (np.float32(ROPE_THETA) ** (-np.arange(0, half, dtype=np.float32) * np.float32(2.0 / MLA_ROPE))).astype(np.float32)
    f1 = np.zeros((1, LANE), np.float32)
    f1[0, :MLA_ROPE] = np.tile(inv, 2)
    f2 = np.zeros((1, MLA_QK_PAD), np.float32)
    f2[0, MLA_NOPE:MLA_QK] = np.tile(inv, 2)
    tr = _pick_rows(S, 256)

    def body(p_ref, f1_ref, f2_ref, c1, s1, c2, s2):
        p = p_ref[...].astype(F32)
        a1 = p * f1_ref[...]
        a2 = p * f2_ref[...]
        c1[...] = jnp.cos(a1)
        s1[...] = jnp.sin(a1)
        c2[...] = jnp.cos(a2)
        s2[...] = jnp.sin(a2)

    def row(w):
        return pl.BlockSpec((tr, w), lambda i: (i, 0))

    def full(w):
        return pl.BlockSpec((1, w), lambda i: (0, 0))

    return pl.pallas_call(
        body, name="rope_tables", grid=(S // tr,), in_specs=[row(1), full(LANE), full(MLA_QK_PAD)],
        out_specs=[row(LANE), row(LANE), row(MLA_QK_PAD), row(MLA_QK_PAD)],
        out_shape=[jax.ShapeDtypeStruct((S, LANE), F32)] * 2 + [jax.ShapeDtypeStruct((S, MLA_QK_PAD), F32)] * 2,
        compiler_params=_cp(("parallel",)),
    )(pos, jnp.asarray(f1), jnp.asarray(f2))


def _log_sigmoid(z):
    return jnp.minimum(z, 0.0) - jnp.log(1.0 + jnp.exp(-jnp.abs(z)))


def _fox_qk_fn(q, k, gq, gk):
    return _rms(q, gq, HEAD_DIM), _rms(k, gk, HEAD_DIM)


def _fox_gate_fn(f, b):
    return (_log_sigmoid(f + b),)


def _mla_pre1_fn(q_rank, kv_rank):
    def fn(cq, ckv, kr, cos, sin, qn, kvn):
        return _rms(cq, qn, q_rank), _rms(ckv, kvn, kv_rank), _rope(kr, cos, sin, 0)
    return fn


def _mla_pre2_fn(qb, kn, kr, cos, sin, gq, gk):
    qh = _rms(_rope(qb, cos, sin, MLA_NOPE), gq, MLA_QK)
    kh = _rms(jnp.concatenate([kn, kr], axis=1), gk, MLA_QK)
    return qh, kh


def _pad_cols(a, n):
    return jnp.pad(a, ((0, 0), (0, n - a.shape[1])))


def _relu2(acc):
    r = jnp.maximum(acc, 0.0)
    return r * r, r


def _add(acc, res):
    return (acc + res,)


def _times_2r(acc, r):
    return (acc * (2.0 * r.astype(F32)),)


def kernel(x, positions, mix_norm, mlp_norm, sb_w_in, sb_w_out, fox_w_in, fox_b_f, fox_q_gain, fox_k_gain, fox_w_out, mla_w_in, mla_q_norm, mla_kv_norm, mla_w_uq, mla_w_ukv, mla_q_gain, mla_k_gain, mla_w_out, mlp_w1, mlp_w2, loss_target, m_mix_norm, m_mlp_norm, m_sb_w_in, m_sb_w_out, m_fox_w_in, m_fox_b_f, m_fox_q_gain, m_fox_k_gain, m_fox_w_out, m_mla_w_in, m_mla_q_norm, m_mla_kv_norm, m_mla_w_uq, m_mla_w_ukv, m_mla_q_gain, m_mla_k_gain, m_mla_w_out, m_mlp_w1, m_mlp_w2, v_mix_norm, v_mlp_norm, v_sb_w_in, v_sb_w_out, v_fox_w_in, v_fox_b_f, v_fox_q_gain, v_fox_k_gain, v_fox_w_out, v_mla_w_in, v_mla_q_norm, v_mla_kv_norm, v_mla_w_uq, v_mla_w_ukv, v_mla_q_gain, v_mla_k_gain, v_mla_w_out, v_mlp_w1, v_mlp_w2):
    S, D = x.shape[1], x.shape[2]
    nh = D // HEAD_DIM
    W = nh * HEAD_DIM
    depth = mix_norm.shape[0]
    q_rank, kv_rank = mla_w_uq.shape[1], mla_w_ukv.shape[1]
    n_fox_in = 3 * W + nh
    fox_pad = -(-n_fox_in // LANE) * LANE
    n_down = q_rank + kv_rank + MLA_ROPE
    down_pad = q_rank + kv_rank + LANE
    tr = _pick_rows(S, 256)
    tk = _pick_rows(S, 256)

    ax, ay, ac = lax.axis_index("x"), lax.axis_index("y"), lax.axis_index("c")
    chip = 2 * ax + ay
    c_arr = jnp.reshape(ac, (1,)).astype(jnp.int32)
    chip_arr = jnp.reshape(chip, (1,)).astype(jnp.int32)

    def bf(a):
        return a.astype(BF16)

    n_small_in = q_rank + kv_rank
    rows_in = -(-n_small_in // (8 * LANE)) * 8
    placed = jnp.zeros((rows_in * LANE,), F32)
    placed = lax.dynamic_update_slice(placed, mla_q_norm[0], (chip * mla_q_norm.shape[1],))
    placed = lax.dynamic_update_slice(placed, mla_kv_norm[0], (q_rank + chip * mla_kv_norm.shape[1],))
    placed = placed * (ac == 0).astype(F32)
    norms = all_sum_small("gather_norms", placed.reshape(rows_in, LANE)).reshape(-1)
    q_norm_full = norms[:q_rank].reshape(1, q_rank)
    kv_norm_full = norms[q_rank:q_rank + kv_rank].reshape(1, kv_rank)

    def gather_start(i):
        kind, j = i % N_MIXERS, i // N_MIXERS
        if kind == 0:
            names, shards = ["w_in", "w_out"], [bf(sb_w_in[j]), bf(sb_w_out[j])]
        elif kind == 1:
            names, shards = ["w_in", "w_out"], [bf(fox_w_in[j]), bf(fox_w_out[j])]
        else:
            names = ["w_in", "w_uq", "w_ukv", "w_out"]
            shards = [bf(mla_w_in[j]), bf(mla_w_uq[j]), bf(mla_w_ukv[j]), bf(mla_w_out[j])]
        names += ["w1", "w2"]
        shards += [bf(mlp_w1[i]), bf(mlp_w2[i])]
        handle, token = chips_start(f"gather_start_{i}", "gather", shards, [(N_CHIPS,) + s.shape for s in shards])
        return names, handle, token

    def gather_finish(i, names, handle, after):
        lands, shards = chips_wait(f"gather_wait_{i}", handle, after)
        return dict(zip(names, pass_to_sibling(f"gather_pass_{i}", lands, shards)))

    def rows_stacked(w):
        return w.reshape(w.shape[0] * w.shape[1], w.shape[2])

    xc = x[0]
    saved = []
    layers = []
    tables = None
    pending = gather_start(0)
    for i in range(depth):
        kind, j = i % N_MIXERS, i // N_MIXERS
        L = gather_finish(i, pending[0], pending[1], xc)
        layers.append(L)
        g1 = mix_norm[i:i + 1]
        if i + 1 < depth:
            pending = gather_start(i + 1)
            g1 = g1 + pending[2]
        hb = rmsnorm_fwd(f"norm1_{i}", xc, g1)
        st = dict(x=xc, hb=hb)
        if kind == 0:
            qkv = mm_nn(f"sb_proj_{i}", hb, L["w_in"], BF16)[0]
            o, o_f32 = sb_fwd(f"sb_attn_{i}", qkv, nh)
            st.update(qkv=qkv, o_f32=o_f32)
        elif kind == 1:
            w_in = _pad_cols(jnp.concatenate([L["w_in"][s] for s in range(N_CHIPS)], axis=1), fox_pad)
            proj = mm_nn(f"fox_proj_{i}", hb, w_in, F32)[0]
            gq, gk = fox_q_gain[j:j + 1], fox_k_gain[j:j + 1]
            qk_rows = [(proj, HEAD_DIM, lambda h: h), (proj, HEAD_DIM, lambda h: nh + h)]
            qh, kh = rowwise_fwd(f"fox_qk_{i}", _fox_qk_fn, qk_rows, [gq, gk],
                                 [(W, HEAD_DIM, lambda h: h, BF16)] * 2, tr, nh)
            b_pad = _pad_cols(fox_b_f[j:j + 1], LANE)
            gate_rows = [(proj, LANE, lambda h: 3 * nh)]
            logf = rowwise_fwd(f"fox_gate_{i}", _fox_gate_fn, gate_rows, [b_pad], [(LANE, LANE, _c0, F32)], tr)[0]
            cf = seq_cumsum(f"fox_cf_{i}", logf, False)[:, :nh].T
            cf_col, cf_row = cf.reshape(nh, S, 1), cf.reshape(nh, S // tk, tk)
            scale = 1.0 / math.sqrt(HEAD_DIM)
            o, o_f32, lse = attn_fwd(f"fox_attn_{i}", qh, kh, proj, nh, HEAD_DIM, 0, 0, 2 * nh, 1, scale, cf_col, cf_row)
            st.update(w_in=w_in, proj=proj, qk_rows=qk_rows, gq=gq, gk=gk, qh=qh, kh=kh, b_pad=b_pad, gate_rows=gate_rows,
                      cf_col=cf_col, cf_row=cf_row, lse=lse, scale=scale, o_f32=o_f32)
        else:
            w_in = _pad_cols(rows_stacked(L["w_in"]), down_pad)
            down = mm_nn(f"mla_down_{i}", hb, w_in, F32)[0]
            if tables is None:
                tables = rope_tables(positions.reshape(S, 1))
            cos1, sin1, cos2, sin2 = tables
            pre1_rows = [(down[:, :q_rank], q_rank, _c0), (down[:, q_rank:q_rank + kv_rank], kv_rank, _c0),
                         (down[:, q_rank + kv_rank:], LANE, _c0), (cos1, LANE, _c0), (sin1, LANE, _c0)]
            pre1_fn = _mla_pre1_fn(q_rank, kv_rank)
            c_q, c_kv, k_rope = rowwise_fwd(
                f"mla_pre1_{i}", pre1_fn, pre1_rows, [q_norm_full, kv_norm_full],
                [(q_rank, q_rank, _c0, BF16), (kv_rank, kv_rank, _c0, BF16), (LANE, LANE, _c0, F32)], tr)
            qfull = mm_nn(f"mla_uq_{i}", c_q, L["w_uq"], F32)[0]
            kv = mm_nn(f"mla_ukv_{i}", c_kv, L["w_ukv"], F32)[0]
            qpad = jnp.pad(qfull.reshape(S, nh, MLA_QK), ((0, 0), (0, 0), (0, MLA_QK_PAD - MLA_QK))).reshape(S, nh * MLA_QK_PAD)
            gq, gk = _pad_cols(mla_q_gain[j:j + 1], MLA_QK_PAD), _pad_cols(mla_k_gain[j:j + 1], MLA_QK_PAD)
            pre2_rows = [(qpad, MLA_QK_PAD, lambda h: h), (kv, MLA_NOPE, lambda h: 2 * h), (k_rope, LANE, _c0),
                         (cos2, MLA_QK_PAD, _c0), (sin2, MLA_QK_PAD, _c0)]
            qh, kh = rowwise_fwd(f"mla_pre2_{i}", _mla_pre2_fn, pre2_rows, [gq, gk],
                                 [(nh * MLA_QK_PAD, MLA_QK_PAD, lambda h: h, BF16)] * 2, tr, nh)
            scale = 1.0 / math.sqrt(MLA_QK)
            o, o_f32, lse = attn_fwd(f"mla_attn_{i}", qh, kh, kv, nh, MLA_QK_PAD, 0, 0, 1, 2, scale)
            st.update(w_in=w_in, pre1_rows=pre1_rows, pre1_fn=pre1_fn, c_q=c_q, c_kv=c_kv, pre2_rows=pre2_rows, gq=gq, gk=gk,
                      qh=qh, kh=kh, kv=kv, lse=lse, scale=scale, o_f32=o_f32)
        x1 = mm_nn(f"mix_out_{i}", o, rows_stacked(L["w_out"]), F32, epilogue=_add, extras=(xc,))[0]
        g2 = mlp_norm[i:i + 1]
        h2 = rmsnorm_fwd(f"norm2_{i}", x1, g2)
        a, r = mm_nn(f"mlp_up_{i}", h2, L["w1"], None, epilogue=_relu2, out_dtypes=[BF16, BF16])
        xc = mm_nn(f"mlp_down_{i}", a, rows_stacked(L["w2"]), F32, epilogue=_add, extras=(x1,))[0]
        st.update(o=o, x1=x1, h2=h2, a=a, r=r, g1=g1, g2=g2)
        saved.append(st)

    loss_local, dx, dxb = loss_head(xc, loss_target[0])
    loss = lax.psum(loss_local[0, 0], ("x", "y", "c"))

    small = {}

    def stack_rows(g):
        return g.reshape(N_CHIPS, g.shape[0] // N_CHIPS, g.shape[1])

    full = [None] * depth
    flying = None

    def exchange_start(i, gr):
        names = list(gr)
        grads = [gr[n] for n in names]
        theirs = swap_halves(f"swap_{i}", grads)
        parts = [add_half(f"addh_{i}_{t}", g, o, c_arr) for t, (g, o) in enumerate(zip(grads, theirs))]
        handle, token = chips_start(f"xchg_start_{i}", "scatter", parts, [p.shape for p in parts])
        return i, names, handle, token

    def exchange_finish(fly, after):
        i, names, handle, _ = fly
        got, parts = chips_wait(f"xchg_wait_{i}", handle, after)
        halves = [sum_slabs(f"sum4_{i}_{t}", g, p, chip_arr) for t, (g, p) in enumerate(zip(got, parts))]
        full[i] = dict(zip(names, zip(halves, send_halves(f"sendh_{i}", halves))))

    for i in reversed(range(depth)):
        kind, j = i % N_MIXERS, i // N_MIXERS
        L, st = layers[i], saved[i]
        gr = {}
        if flying is not None:
            st["g2"] = st["g2"] + flying[3]
        gr["w2"] = stack_rows(mm_tn(f"mlp_dw2_{i}", st["a"], dxb, BF16, False))
        du = mm_nt(f"mlp_du_{i}", dxb, rows_stacked(L["w2"]), BF16, epilogue=_times_2r, extras=(st["r"],))
        gr["w1"] = mm_tn(f"mlp_dw1_{i}", st["h2"], du, BF16, True)
        dh2 = mm_nt(f"mlp_dh_{i}", du, L["w1"], F32)
        dx1, dx1b, dg2 = rmsnorm_bwd(f"norm2_bwd_{i}", st["x1"], st["g2"], dh2, dx)
        small[("mlp_norm", i)] = dg2
        gr["w_out"] = stack_rows(mm_tn(f"mix_dwout_{i}", st["o"], dx1b, BF16, False))
        do = mm_nt(f"mix_do_{i}", dx1b, rows_stacked(L["w_out"]), BF16)
        if kind == 0:
            dqkv = sb_bwd(f"sb_attn_bwd_{i}", st["qkv"], st["o_f32"], do, nh)
            gr["w_in"] = mm_tn(f"sb_dwin_{i}", st["hb"], dqkv, BF16, True)
            dh = mm_nt(f"sb_dh_{i}", dqkv, L["w_in"], F32)
        elif kind == 1:
            dqh, dkh, dv, dcf = attn_bwd(f"fox_attn_bwd_{i}", st["qh"], st["kh"], st["proj"], st["o_f32"], do, st["lse"], nh,
                                         HEAD_DIM, 0, 0, 2 * nh, 1, st["scale"], st["cf_col"], st["cf_row"])
            dcf_s = _pad_cols(dcf.reshape(nh, S).T, LANE)
            dlogf = seq_cumsum(f"fox_dcf_{i}", dcf_s, True)
            dgate, db = rowwise_bwd(f"fox_gate_bwd_{i}", _fox_gate_fn, st["gate_rows"], [st["b_pad"]], [(dlogf, LANE, _c0)],
                                    [(LANE, LANE, _c0, BF16, False)], tr)
            dq, dk, dgq, dgk = rowwise_bwd(
                f"fox_qk_bwd_{i}", _fox_qk_fn, st["qk_rows"], [st["gq"], st["gk"]],
                [(dqh, HEAD_DIM, lambda h: h), (dkh, HEAD_DIM, lambda h: h)], [(W, HEAD_DIM, lambda h: h, BF16, False)] * 2, tr, nh)
            small[("fox_b_f", j)] = db[:, :nh]
            small[("fox_q_gain", j)] = dgq
            small[("fox_k_gain", j)] = dgk
            dproj = jnp.concatenate([dq, dk, bf(dv), dgate], axis=1)
            dw = mm_tn(f"fox_dwin_{i}", st["hb"], dproj, BF16, False)
            n4 = n_fox_in // N_CHIPS
            gr["w_in"] = jnp.stack([dw[:, s * n4:(s + 1) * n4] for s in range(N_CHIPS)])
            dh = mm_nt(f"fox_dh_{i}", dproj, st["w_in"], F32)
        else:
            dqh, dkh, dv = attn_bwd(f"mla_attn_bwd_{i}", st["qh"], st["kh"], st["kv"], st["o_f32"], do, st["lse"], nh,
                                    MLA_QK_PAD, 0, 0, 1, 2, st["scale"])
            dqpad, dkn, dkr, dgq, dgk = rowwise_bwd(
                f"mla_pre2_bwd_{i}", _mla_pre2_fn, st["pre2_rows"], [st["gq"], st["gk"]],
                [(dqh, MLA_QK_PAD, lambda h: h), (dkh, MLA_QK_PAD, lambda h: h)],
                [(nh * MLA_QK_PAD, MLA_QK_PAD, lambda h: h, BF16, False), (W, MLA_NOPE, lambda h: h, BF16, False),
                 (LANE, LANE, _c0, F32, True)], tr, nh, n_diff=3)
            small[("mla_q_gain", j)] = dgq[:, :MLA_QK]
            small[("mla_k_gain", j)] = dgk[:, :MLA_QK]
            dqfull = dqpad.reshape(S, nh, MLA_QK_PAD)[:, :, :MLA_QK].reshape(S, nh * MLA_QK)
            dkv = jnp.stack([dkn.reshape(S, nh, MLA_NOPE), bf(dv).reshape(S, nh, MLA_V)], axis=2).reshape(S, nh * (MLA_NOPE + MLA_V))
            gr["w_uq"] = mm_tn(f"mla_dwuq_{i}", st["c_q"], dqfull, BF16, True)
            dc_q = mm_nt(f"mla_dcq_{i}", dqfull, L["w_uq"], F32)
            gr["w_ukv"] = mm_tn(f"mla_dwukv_{i}", st["c_kv"], dkv, BF16, True)
            dc_kv = mm_nt(f"mla_dckv_{i}", dkv, L["w_ukv"], F32)
            d1, d2, d3, dqn, dkvn = rowwise_bwd(
                f"mla_pre1_bwd_{i}", st["pre1_fn"], st["pre1_rows"], [q_norm_full, kv_norm_full],
                [(dc_q, q_rank, _c0), (dc_kv, kv_rank, _c0), (dkr, LANE, _c0)],
                [(q_rank, q_rank, _c0, BF16, False), (kv_rank, kv_rank, _c0, BF16, False), (LANE, LANE, _c0, BF16, False)],
                tr, n_diff=3)
            small[("mla_q_norm", j)] = dqn
            small[("mla_kv_norm", j)] = dkvn
            ddown = jnp.concatenate([d1, d2, d3], axis=1)
            dw = mm_tn(f"mla_dwin_{i}", st["hb"], ddown, BF16, False)
            gr["w_in"] = stack_rows(dw[:, :n_down])
            dh = mm_nt(f"mla_dh_{i}", ddown, st["w_in"], F32)
        dx, dxb, dg1 = rmsnorm_bwd(f"norm1_bwd_{i}", st["x"], st["g1"], dh, dx1)
        small[("mix_norm", i)] = dg1
        if flying is not None:
            exchange_finish(flying, dx)
        flying = exchange_start(i, gr)

    keys = list(small)
    flat = jnp.concatenate([small[k].reshape(-1) for k in keys])
    rows_g = -(-flat.shape[0] // (8 * LANE)) * 8
    flat = jnp.pad(flat, (0, rows_g * LANE - flat.shape[0]))
    summed = all_sum_small("sum_small", flat.reshape(rows_g, LANE)).reshape(-1)
    sg, off = {}, 0
    for k in keys:
        n = small[k].size
        sg[k] = summed[off:off + n].reshape(small[k].shape)
        off += n

    def holders(kind_of):
        return [i for i in range(depth) if kind_of is None or i % N_MIXERS == kind_of]

    tensors = {
        "sb_w_in": (sb_w_in, m_sb_w_in, v_sb_w_in, "w_in", holders(0)),
        "sb_w_out": (sb_w_out, m_sb_w_out, v_sb_w_out, "w_out", holders(0)),
        "fox_w_in": (fox_w_in, m_fox_w_in, v_fox_w_in, "w_in", holders(1)),
        "fox_w_out": (fox_w_out, m_fox_w_out, v_fox_w_out, "w_out", holders(1)),
        "mla_w_in": (mla_w_in, m_mla_w_in, v_mla_w_in, "w_in", holders(2)),
        "mla_w_uq": (mla_w_uq, m_mla_w_uq, v_mla_w_uq, "w_uq", holders(2)),
        "mla_w_ukv": (mla_w_ukv, m_mla_w_ukv, v_mla_w_ukv, "w_ukv", holders(2)),
        "mla_w_out": (mla_w_out, m_mla_w_out, v_mla_w_out, "w_out", holders(2)),
        "mlp_w1": (mlp_w1, m_mlp_w1, v_mlp_w1, "w1", holders(None)),
        "mlp_w2": (mlp_w2, m_mlp_w2, v_mlp_w2, "w2", holders(None)),
    }
    updated = {n: None for n in tensors}

    def update_layers(first):
        for n, (w, m, v, key, held) in tensors.items():
            for l in reversed(range(len(held))):
                if (held[l] == 0) == first:
                    mine, other = full[held[l]][key]
                    updated[n] = adamw_big(f"adamw_{n}_{l}", w, m, v, mine, other, c_arr, l, updated[n])

    update_layers(False)
    exchange_finish(flying, updated["mlp_w2"][0])
    update_layers(True)

    def update_big(name):
        return list(updated[name])

    def update_small(name, w, m, v, g):
        return [g] + list(adamw(f"adamw_{name}", w, g, m, v))

    def small_rows(name, count):
        return jnp.concatenate([sg[(name, l)] for l in range(count)], axis=0)

    def my_part(g, n):
        return lax.dynamic_slice(g, (0, chip * n), (g.shape[0], n))

    res = {
        "mix_norm": update_small("mix_norm", mix_norm, m_mix_norm, v_mix_norm, small_rows("mix_norm", depth)),
        "mlp_norm": update_small("mlp_norm", mlp_norm, m_mlp_norm, v_mlp_norm, small_rows("mlp_norm", depth)),
        "sb_w_in": update_big("sb_w_in"),
        "sb_w_out": update_big("sb_w_out"),
        "fox_w_in": update_big("fox_w_in"),
        "fox_b_f": update_small("fox_b_f", fox_b_f, m_fox_b_f, v_fox_b_f, small_rows("fox_b_f", fox_b_f.shape[0])),
        "fox_q_gain": update_small("fox_q_gain", fox_q_gain, m_fox_q_gain, v_fox_q_gain, small_rows("fox_q_gain", fox_q_gain.shape[0])),
        "fox_k_gain": update_small("fox_k_gain", fox_k_gain, m_fox_k_gain, v_fox_k_gain, small_rows("fox_k_gain", fox_k_gain.shape[0])),
        "fox_w_out": update_big("fox_w_out"),
        "mla_w_in": update_big("mla_w_in"),
        "mla_q_norm": update_small("mla_q_norm", mla_q_norm, m_mla_q_norm, v_mla_q_norm,
                                   my_part(small_rows("mla_q_norm", mla_q_norm.shape[0]), mla_q_norm.shape[1])),
        "mla_kv_norm": update_small("mla_kv_norm", mla_kv_norm, m_mla_kv_norm, v_mla_kv_norm,
                                    my_part(small_rows("mla_kv_norm", mla_kv_norm.shape[0]), mla_kv_norm.shape[1])),
        "mla_w_uq": update_big("mla_w_uq"),
        "mla_w_ukv": update_big("mla_w_ukv"),
        "mla_q_gain": update_small("mla_q_gain", mla_q_gain, m_mla_q_gain, v_mla_q_gain, small_rows("mla_q_gain", mla_q_gain.shape[0])),
        "mla_k_gain": update_small("mla_k_gain", mla_k_gain, m_mla_k_gain, v_mla_k_gain, small_rows("mla_k_gain", mla_k_gain.shape[0])),
        "mla_w_out": update_big("mla_w_out"),
        "mlp_w1": update_big("mlp_w1"),
        "mlp_w2": update_big("mlp_w2"),
    }
    order = ["mix_norm", "mlp_norm", "sb_w_in", "sb_w_out", "fox_w_in", "fox_b_f", "fox_q_gain", "fox_k_gain", "fox_w_out",
             "mla_w_in", "mla_q_norm", "mla_kv_norm", "mla_w_uq", "mla_w_ukv", "mla_q_gain", "mla_k_gain", "mla_w_out",
             "mlp_w1", "mlp_w2"]
    outs = [loss, dx.reshape(x.shape)]
    for k in range(4):
        outs += [res[n][k] for n in order]
    return tuple(outs)
```

```python
import functools
import math

import numpy as np
import jax
import jax.numpy as jnp
from jax import lax
from jax.experimental import pallas as pl
from jax.experimental.pallas import tpu as pltpu

F32 = jnp.float32
BF16 = jnp.bfloat16
MESH = pl.DeviceIdType.MESH

EPS = 1e-6
HEAD_DIM = 128
MLA_NOPE = 128
MLA_ROPE = 64
MLA_V = 128
MLA_QK = MLA_NOPE + MLA_ROPE
MLA_QK_PAD = 256
ROPE_THETA = 10000.0
N_MIXERS = 3
ADAM_LR = 0.001
ADAM_B1 = 0.9
ADAM_B2 = 0.999
ADAM_EPS = 1e-08
ADAM_WD = 0.01
ADAM_STEP = 10

LANE = 128
N_CHIPS = 4
VMEM_LIMIT = 48 * 1024 * 1024
NEG = -1e30


def _cp(sem):
    return pltpu.CompilerParams(dimension_semantics=sem, vmem_limit_bytes=VMEM_LIMIT)


def _pick(dim, cap):
    best = None
    b = LANE
    while b <= min(dim, cap):
        if dim % b == 0:
            best = b
        b += LANE
    return best if best is not None else dim


def _pick_rows(dim, cap):
    b = min(dim, cap)
    while dim % b:
        b -= 8
    return b


def _matmul(name, a, b, a_blk, a_map, b_blk, b_map, dn, grid, acc_shape, outs, extras=(), epilogue=None):
    nk = grid[2]
    n_ex, n_out = len(extras), len(outs)

    def body(*refs):
        a_ref, b_ref = refs[0], refs[1]
        ex_refs = refs[2:2 + n_ex]
        out_refs = refs[2 + n_ex:2 + n_ex + n_out]
        acc = refs[-1]
        k = pl.program_id(2)

        @pl.when(k == 0)
        def _():
            acc[...] = jnp.zeros_like(acc)

        acc[...] += lax.dot_general(a_ref[...], b_ref[...], dn, preferred_element_type=F32)

        @pl.when(k == nk - 1)
        def _():
            res = acc[...]
            vals = epilogue(res, *[e[...] for e in ex_refs]) if epilogue is not None else (res,)
            for o, v in zip(out_refs, vals):
                o[...] = v.astype(o.dtype)

    in_specs = [pl.BlockSpec(a_blk, a_map), pl.BlockSpec(b_blk, b_map)]
    in_specs += [pl.BlockSpec(blk, lambda i, j, k, m=m: m(i, j)) for (_, blk, m) in extras]
    out_specs = [pl.BlockSpec(blk, lambda i, j, k, m=m: m(i, j)) for (_, _, blk, m) in outs]
    out_shape = [jax.ShapeDtypeStruct(s, d) for (s, d, _, _) in outs]
    res = pl.pallas_call(
        body, name=name, grid=grid, in_specs=in_specs, out_specs=out_specs, out_shape=out_shape,
        scratch_shapes=[pltpu.VMEM(acc_shape, F32)],
        compiler_params=_cp(("parallel", "parallel", "arbitrary")),
    )(a, b, *[e[0] for e in extras])
    return res


BM, BN, BK = 1024, 1024, 512


def mm_nn(name, a, w, out_dtype, epilogue=None, extras=(), n_out=1, out_dtypes=None):
    M, K = a.shape
    stacked = w.ndim == 3
    n4 = w.shape[-1]
    N = n4 * (N_CHIPS if stacked else 1)
    bm, bn, bk = _pick_rows(M, BM), _pick(n4, BN), _pick(K, BK)
    nb = n4 // bn
    if stacked:
        b_blk, b_map = (None, bk, bn), (lambda i, j, k: (j // nb, k, j % nb))
    else:
        b_blk, b_map = (bk, bn), (lambda i, j, k: (k, j))
    dts = out_dtypes if out_dtypes is not None else [out_dtype] * n_out
    outs = [((M, N), d, (bm, bn), lambda i, j: (i, j)) for d in dts]
    exs = [(e, (bm, bn), lambda i, j: (i, j)) for e in extras]
    return _matmul(name, a, w, (bm, bk), lambda i, j, k: (i, k), b_blk, b_map,
                   (((1,), (0,)), ((), ())), (M // bm, N // bn, K // bk), (bm, bn), outs, exs, epilogue)


def mm_nt(name, dy, w, out_dtype, epilogue=None, extras=()):
    M, N = dy.shape
    stacked = w.ndim == 3
    K, n4 = w.shape[-2], w.shape[-1]
    bm, bn, bk = _pick_rows(M, BM), _pick(K, BN), _pick(n4, BK)
    nb = n4 // bk
    if stacked:
        b_blk, b_map = (None, bn, bk), (lambda i, j, k: (k // nb, j, k % nb))
    else:
        b_blk, b_map = (bn, bk), (lambda i, j, k: (j, k))
    outs = [((M, K), out_dtype, (bm, bn), lambda i, j: (i, j))]
    exs = [(e, (bm, bn), lambda i, j: (i, j)) for e in extras]
    return _matmul(name, dy, w, (bm, bk), lambda i, j, k: (i, k), b_blk, b_map,
                   (((1,), (1,)), ((), ())), (M // bm, K // bn, N // bk), (bm, bn), outs, exs, epilogue)[0]


def mm_tn(name, a, dy, out_dtype, stacked):
    M, K = a.shape
    N = dy.shape[1]
    n4 = N // N_CHIPS if stacked else N
    bm, bn, bk = _pick(K, BM), _pick(n4, BN), _pick_rows(M, BK)
    nb = n4 // bn
    if stacked:
        outs = [((N_CHIPS, K, n4), out_dtype, (None, bm, bn), lambda i, j: (j // nb, i, j % nb))]
    else:
        outs = [((K, N), out_dtype, (bm, bn), lambda i, j: (i, j))]
    return _matmul(name, a, dy, (bk, bm), lambda i, j, k: (k, i), (bk, bn), lambda i, j, k: (k, j),
                   (((0,), (0,)), ((), ())), (K // bm, N // bn, M // bk), (bm, bn), outs)[0]


def _split3(x):
    hi = x.astype(BF16)
    r = x - hi.astype(F32)
    mid = r.astype(BF16)
    lo = (r - mid.astype(F32)).astype(BF16)
    return hi, mid, lo


def _exact_dot(x, m, n):
    out = None
    for p in _split3(x)[:n]:
        d = jnp.dot(p, m, preferred_element_type=F32)
        out = d if out is None else out + d
    return out


def _row_spec(tr, width, cmap):
    return pl.BlockSpec((tr, width), lambda i, h: (i, cmap(h)))


def _full_spec(p):
    return pl.BlockSpec(p.shape, lambda i, h: (0,) * p.ndim)


def rowwise_fwd(name, fn, rows, params, outs, tr, nh=1):
    S = rows[0][0].shape[0]
    nr, npar = len(rows), len(params)

    def body(*refs):
        vals = fn(*[r[...].astype(F32) for r in refs[:nr]], *[p[...] for p in refs[nr:nr + npar]])
        for o, v in zip(refs[nr + npar:], vals):
            o[...] = v.astype(o.dtype)

    return pl.pallas_call(
        body, name=name, grid=(S // tr, nh),
        in_specs=[_row_spec(tr, w, cm) for (_, w, cm) in rows] + [_full_spec(p) for p in params],
        out_specs=[_row_spec(tr, w, cm) for (_, w, cm, _) in outs],
        out_shape=[jax.ShapeDtypeStruct((S, c), d) for (c, _, _, d) in outs],
        compiler_params=_cp(("parallel", "arbitrary")),
    )(*[r[0] for r in rows], *params)


def rowwise_bwd(name, fn, rows, params, cts, grads, tr, nh=1, n_diff=None, add=None):
    S = rows[0][0].shape[0]
    nr, npar, nct = len(rows), len(params), len(cts)
    n_diff = nr if n_diff is None else n_diff
    n_add = 1 if add is not None else 0

    def body(*refs):
        row_refs = refs[:nr]
        par_refs = refs[nr:nr + npar]
        ct_refs = refs[nr + npar:nr + npar + nct]
        add_refs = refs[nr + npar + nct:nr + npar + nct + n_add]
        o = nr + npar + nct + n_add
        g_refs = refs[o:o + n_diff]
        cp_refs = refs[o + n_diff:o + n_diff + n_add]
        pg_refs = refs[o + n_diff + n_add:]
        i, h = pl.program_id(0), pl.program_id(1)
        rv = [r[...].astype(F32) for r in row_refs]
        pv = [p[...] for p in par_refs]
        aux = rv[n_diff:]

        def f(*dp):
            return fn(*dp[:n_diff], *aux, *dp[n_diff:])

        _, vjp = jax.vjp(f, *rv[:n_diff], *pv)
        gs = vjp(tuple(c[...].astype(F32) for c in ct_refs))
        for n, (g_ref, (_, _, _, _, over)) in enumerate(zip(g_refs, grads)):
            g = gs[n]
            if n == 0 and add is not None:
                g = g + add_refs[0][...]
                cp_refs[0][...] = g.astype(BF16)
            if over:
                @pl.when(h == 0)
                def _(g_ref=g_ref):
                    g_ref[...] = jnp.zeros_like(g_ref)
                g_ref[...] += g.astype(g_ref.dtype)
            else:
                g_ref[...] = g.astype(g_ref.dtype)
        for pg_ref, g in zip(pg_refs, gs[n_diff:]):
            @pl.when((i == 0) & (h == 0))
            def _(pg_ref=pg_ref):
                pg_ref[...] = jnp.zeros_like(pg_ref)
            pg_ref[...] += g

    in_specs = [_row_spec(tr, w, cm) for (_, w, cm) in rows] + [_full_spec(p) for p in params]
    in_specs += [_row_spec(tr, w, cm) for (_, w, cm) in cts]
    operands = [r[0] for r in rows] + list(params) + [c[0] for c in cts]
    out_specs = [_row_spec(tr, w, cm) for (_, w, cm, _, _) in grads]
    out_shape = [jax.ShapeDtypeStruct((S, c), d) for (c, _, _, d, _) in grads]
    if add is not None:
        in_specs.append(_row_spec(tr, add[1], add[2]))
        operands.append(add[0])
        out_specs.append(_row_spec(tr, add[1], add[2]))
        out_shape.append(jax.ShapeDtypeStruct(add[0].shape, BF16))
    out_specs += [_full_spec(p) for p in params]
    out_shape += [jax.ShapeDtypeStruct(p.shape, F32) for p in params]
    return pl.pallas_call(
        body, name=name, grid=(S // tr, nh), in_specs=in_specs, out_specs=out_specs, out_shape=out_shape,
        compiler_params=_cp(("arbitrary", "arbitrary")),
    )(*operands)


def _c0(h):
    return 0


def _rms(x, g, n):
    return x * lax.rsqrt(jnp.sum(x * x, axis=-1, keepdims=True) * (1.0 / n) + EPS) * g


def _rmsnorm_fn(d):
    def fn(x, g):
        return (_rms(x, g, d),)
    return fn


def rmsnorm_fwd(name, x, g):
    S, D = x.shape
    return rowwise_fwd(name, _rmsnorm_fn(D), [(x, D, _c0)], [g], [(D, D, _c0, BF16)], _pick_rows(S, 256))[0]


def rmsnorm_bwd(name, x, g, dh, dres):
    S, D = x.shape
    return rowwise_bwd(name, _rmsnorm_fn(D), [(x, D, _c0)], [g], [(dh, D, _c0)], [(D, D, _c0, F32, False)],
                       _pick_rows(S, 256), add=(dres, D, _c0))


def loss_head(y, t):
    S, D = y.shape
    tr = _pick_rows(S, 256)

    def body(y_ref, t_ref, l_ref, d_ref, db_ref):
        @pl.when(pl.program_id(0) == 0)
        def _():
            l_ref[...] = jnp.zeros_like(l_ref)
        e = y_ref[...] - t_ref[...]
        l_ref[...] += 0.5 * jnp.sum(jnp.sum(e * e, axis=1, keepdims=True), axis=0, keepdims=True) * (1.0 / D)
        d = e * (1.0 / D)
        d_ref[...] = d
        db_ref[...] = d.astype(BF16)

    row = pl.BlockSpec((tr, D), lambda i: (i, 0))
    return pl.pallas_call(
        body, name="loss_head", grid=(S // tr,), in_specs=[row, row],
        out_specs=[pl.BlockSpec((1, 1), lambda i: (0, 0)), row, row],
        out_shape=[jax.ShapeDtypeStruct((1, 1), F32), jax.ShapeDtypeStruct((S, D), F32), jax.ShapeDtypeStruct((S, D), BF16)],
        compiler_params=_cp(("arbitrary",)),
    )(y, t)


def _dot_nt(a, b):
    return lax.dot_general(a, b, (((1,), (1,)), ((), ())), preferred_element_type=F32)


def _dot_tn(a, b):
    return lax.dot_general(a, b, (((0,), (0,)), ((), ())), preferred_element_type=F32)


def attn_fwd(name, q, k, v, nh, dq, qoff, koff, voff, vstep, scale, cf_col=None, cf_row=None):
    S = q.shape[0]
    tq = _pick_rows(S, 256)
    tk = tq
    nq = S // tq
    bias = cf_col is not None

    def body(*refs):
        if bias:
            q_ref, k_ref, v_ref, cfc_ref, cfr_ref, o_ref, of_ref, lse_ref = refs
        else:
            q_ref, k_ref, v_ref, o_ref, of_ref, lse_ref = refs
        i = pl.program_id(1)
        qv = q_ref[...].astype(BF16)
        row = i * tq + lax.broadcasted_iota(jnp.int32, (tq, tk), 0)
        coli = lax.broadcasted_iota(jnp.int32, (tq, tk), 1)

        def step(j, carry):
            m, l, acc, acc_lo = carry
            off = pl.multiple_of(j * tk, tk)
            ks = k_ref[pl.ds(off, tk), :].astype(BF16)
            vs = v_ref[pl.ds(off, tk), :].astype(BF16)
            s = _dot_nt(qv, ks) * scale
            if bias:
                s = s + cfc_ref[0] - cfr_ref[0, pl.ds(j, 1), :]
            s = jnp.where(j * tk + coli <= row, s, NEG)
            m_new = jnp.maximum(m, jnp.max(s, axis=1, keepdims=True))
            alpha = jnp.exp(m - m_new)
            p = jnp.exp(s - m_new)
            pb = p.astype(BF16)
            l = alpha * l + jnp.sum(p, axis=1, keepdims=True)
            acc = alpha * acc + jnp.dot(pb, vs, preferred_element_type=F32)
            acc_lo = alpha * acc_lo + jnp.dot((p - pb.astype(F32)).astype(BF16), vs, preferred_element_type=F32)
            return m_new, l, acc, acc_lo

        z = jnp.zeros((tq, HEAD_DIM), F32)
        m, l, acc, acc_lo = lax.fori_loop(0, i + 1, step, (jnp.full((tq, 1), NEG, F32), jnp.zeros((tq, 1), F32), z, z))
        o_ref[...] = (acc / l).astype(o_ref.dtype)
        of_ref[...] = (acc + acc_lo) / l
        lse_ref[0] = m + jnp.log(l)

    in_specs = [pl.BlockSpec((tq, dq), lambda h, i: (i, qoff + h)),
                pl.BlockSpec((S, dq), lambda h, i: (0, koff + h)),
                pl.BlockSpec((S, HEAD_DIM), lambda h, i: (0, voff + vstep * h))]
    operands = [q, k, v]
    if bias:
        in_specs += [pl.BlockSpec((1, tq, 1), lambda h, i: (h, i, 0)), pl.BlockSpec((1, S // tk, tk), lambda h, i: (h, 0, 0))]
        operands += [cf_col, cf_row]
    return pl.pallas_call(
        body, name=name, grid=(nh, nq), in_specs=in_specs,
        out_specs=[pl.BlockSpec((tq, HEAD_DIM), lambda h, i: (i, h)), pl.BlockSpec((tq, HEAD_DIM), lambda h, i: (i, h)),
                   pl.BlockSpec((1, tq, 1), lambda h, i: (h, i, 0))],
        out_shape=[jax.ShapeDtypeStruct((S, nh * HEAD_DIM), BF16), jax.ShapeDtypeStruct((S, nh * HEAD_DIM), F32),
                   jax.ShapeDtypeStruct((nh, S, 1), F32)],
        compiler_params=_cp(("parallel", "arbitrary")),
    )(*operands)


def attn_bwd(name, q, k, v, o, do, lse, nh, dq, qoff, koff, voff, vstep, scale, cf_col=None, cf_row=None):
    S = q.shape[0]
    tq = _pick_rows(S, 256)
    tk = tq
    nq = S // tq
    bias = cf_col is not None

    def body(*refs):
        if bias:
            q_ref, k_ref, v_ref, o_ref, do_ref, lse_ref, cfc_ref, cfr_ref, dq_ref, dk_ref, dv_ref, dcf_ref = refs
        else:
            q_ref, k_ref, v_ref, o_ref, do_ref, lse_ref, dq_ref, dk_ref, dv_ref = refs
        dk_ref[...] = jnp.zeros_like(dk_ref)
        dv_ref[...] = jnp.zeros_like(dv_ref)
        if bias:
            dcf_ref[...] = jnp.zeros_like(dcf_ref)
        rowi = lax.broadcasted_iota(jnp.int32, (tq, tk), 0)
        coli = lax.broadcasted_iota(jnp.int32, (tq, tk), 1)

        def outer(i, _):
            roff = pl.multiple_of(i * tq, tq)
            qv = q_ref[pl.ds(roff, tq), :].astype(BF16)
            dov = do_ref[pl.ds(roff, tq), :]
            delta = jnp.sum(dov.astype(F32) * o_ref[pl.ds(roff, tq), :].astype(F32), axis=1, keepdims=True)
            lse = lse_ref[0, pl.ds(roff, tq), :]
            if bias:
                cq = cfc_ref[0, pl.ds(roff, tq), :]

            def inner(j, dq_acc):
                off = pl.multiple_of(j * tk, tk)
                ks = k_ref[pl.ds(off, tk), :].astype(BF16)
                vs = v_ref[pl.ds(off, tk), :].astype(BF16)
                s = _dot_nt(qv, ks) * scale
                if bias:
                    s = s + cq - cfr_ref[0, pl.ds(j, 1), :]
                p = jnp.where(j * tk + coli <= i * tq + rowi, jnp.exp(s - lse), 0.0)
                dp = _dot_nt(dov, vs)
                ds = p * (dp - delta)
                if bias:
                    dcf_ref[0, pl.ds(j, 1), :] -= jnp.sum(ds, axis=0, keepdims=True)
                dsb = (ds * scale).astype(BF16)
                dv_ref[pl.ds(off, tk), :] += _dot_tn(p.astype(BF16), dov)
                dk_ref[pl.ds(off, tk), :] += _dot_tn(dsb, qv)
                return dq_acc + jnp.dot(dsb, ks, preferred_element_type=F32)

            dq_ref[pl.ds(roff, tq), :] = lax.fori_loop(0, i + 1, inner, jnp.zeros((tq, dq), F32))
            return 0

        lax.fori_loop(0, nq, outer, 0)

    in_specs = [pl.BlockSpec((S, dq), lambda h: (0, qoff + h)),
                pl.BlockSpec((S, dq), lambda h: (0, koff + h)),
                pl.BlockSpec((S, HEAD_DIM), lambda h: (0, voff + vstep * h)),
                pl.BlockSpec((S, HEAD_DIM), lambda h: (0, h)),
                pl.BlockSpec((S, HEAD_DIM), lambda h: (0, h)),
                pl.BlockSpec((1, S, 1), lambda h: (h, 0, 0))]
    operands = [q, k, v, o, do, lse]
    out_specs = [pl.BlockSpec((S, dq), lambda h: (0, h)), pl.BlockSpec((S, dq), lambda h: (0, h)),
                 pl.BlockSpec((S, HEAD_DIM), lambda h: (0, h))]
    out_shape = [jax.ShapeDtypeStruct((S, nh * dq), F32), jax.ShapeDtypeStruct((S, nh * dq), F32),
                 jax.ShapeDtypeStruct((S, nh * HEAD_DIM), F32)]
    if bias:
        in_specs += [pl.BlockSpec((1, S, 1), lambda h: (h, 0, 0)), pl.BlockSpec((1, S // tk, tk), lambda h: (h, 0, 0))]
        operands += [cf_col, cf_row]
        out_specs.append(pl.BlockSpec((1, S // tk, tk), lambda h: (h, 0, 0)))
        out_shape.append(jax.ShapeDtypeStruct((nh, S // tk, tk), F32))
    return pl.pallas_call(
        body, name=name, grid=(nh,), in_specs=in_specs, out_specs=out_specs, out_shape=out_shape,
        compiler_params=_cp(("parallel",)),
    )(*operands)


SB_ROW_SPLIT = 1


def _sb_block(qv, kb, row, col, scale, later_c, tri_after):
    z = _dot_nt(qv, kb) * scale
    strict = col < row
    lsn = jnp.minimum(-z, 0.0) - jnp.log(1.0 + jnp.exp(-jnp.abs(z)))
    lsp = lsn + z
    L = jnp.where(strict, lsn, 0.0)
    later = _exact_dot(L, tri_after, 2) + later_c
    a = jnp.where(strict, jnp.exp(lsp + later), 0.0)
    return strict, lsp, lsn, L, a


def _tri(n, after_strict):
    r = lax.broadcasted_iota(jnp.int32, (n, n), 0)
    c = lax.broadcasted_iota(jnp.int32, (n, n), 1)
    return jnp.where(r > c if after_strict else r >= c, 1.0, 0.0).astype(BF16)


def sb_fwd(name, qkv, nh):
    S = qkv.shape[0]
    tq = _pick_rows(S, 256)
    tk = tq
    nq = S // tq
    scale = 1.0 / math.sqrt(HEAD_DIM)

    ns = SB_ROW_SPLIT if tq % (8 * SB_ROW_SPLIT) == 0 else 1
    ts = tq // ns

    def body(q_ref, k_ref, v_ref, o_ref, of_ref):
        i = pl.program_id(1)
        tri = _tri(tk, True)
        coli = lax.broadcasted_iota(jnp.int32, (ts, tk), 1)
        qs = [q_ref[s * ts:(s + 1) * ts, :] for s in range(ns)]
        rows = [i * tq + s * ts + lax.broadcasted_iota(jnp.int32, (ts, tk), 0) for s in range(ns)]

        def step(jj, carry):
            j = i - jj
            off = pl.multiple_of(j * tk, tk)
            kb = k_ref[pl.ds(off, tk), :]
            vb = v_ref[pl.ds(off, tk), :]
            out = []
            for s in range(ns):
                later_c, acc, acc_lo = carry[s]
                _, _, _, L, a = _sb_block(qs[s], kb, rows[s], j * tk + coli, scale, later_c, tri)
                ab = a.astype(BF16)
                acc = acc + jnp.dot(ab, vb, preferred_element_type=F32)
                acc_lo = acc_lo + jnp.dot((a - ab.astype(F32)).astype(BF16), vb, preferred_element_type=F32)
                out.append((later_c + jnp.sum(L, axis=1, keepdims=True), acc, acc_lo))
            return tuple(out)

        z = jnp.zeros((ts, HEAD_DIM), F32)
        res = lax.fori_loop(0, i + 1, step, tuple((jnp.zeros((ts, 1), F32), z, z) for _ in range(ns)))
        for s, (_, acc, acc_lo) in enumerate(res):
            o_ref[s * ts:(s + 1) * ts, :] = acc.astype(o_ref.dtype)
            of_ref[s * ts:(s + 1) * ts, :] = acc + acc_lo

    blk = pl.BlockSpec((tq, HEAD_DIM), lambda h, i: (i, h))
    return pl.pallas_call(
        body, name=name, grid=(nh, nq),
        in_specs=[blk,
                  pl.BlockSpec((S, HEAD_DIM), lambda h, i: (0, nh + h)),
                  pl.BlockSpec((S, HEAD_DIM), lambda h, i: (0, 2 * nh + h))],
        out_specs=[blk, blk],
        out_shape=[jax.ShapeDtypeStruct((S, nh * HEAD_DIM), BF16), jax.ShapeDtypeStruct((S, nh * HEAD_DIM), F32)],
        compiler_params=_cp(("parallel", "arbitrary")),
    )(qkv, qkv, qkv)


def sb_bwd(name, qkv, o, do, nh):
    S = qkv.shape[0]
    tq = _pick_rows(S, 256)
    tk = tq
    nq = S // tq
    scale = 1.0 / math.sqrt(HEAD_DIM)

    ns = SB_ROW_SPLIT if tq % (8 * SB_ROW_SPLIT) == 0 else 1
    ts = tq // ns

    def body(q_ref, k_ref, v_ref, o_ref, do_ref, dq_ref, dk_ref, dv_ref, dk_acc, dv_acc):
        dk_acc[...] = jnp.zeros_like(dk_acc)
        dv_acc[...] = jnp.zeros_like(dv_acc)
        tri = _tri(tk, True)
        tri_inc = _tri(tk, False)
        rowi = lax.broadcasted_iota(jnp.int32, (ts, tk), 0)
        coli = lax.broadcasted_iota(jnp.int32, (ts, tk), 1)

        def outer(i, _):
            qs, dos, dtots, rows = [], [], [], []
            for s in range(ns):
                roff = pl.multiple_of(i * tq + s * ts, ts)
                qs.append(q_ref[pl.ds(roff, ts), :])
                dos.append(do_ref[pl.ds(roff, ts), :])
                dtots.append(jnp.sum(dos[s].astype(F32) * o_ref[pl.ds(roff, ts), :].astype(F32), axis=1, keepdims=True))
                rows.append(i * tq + s * ts + rowi)

            def inner(jj, carry):
                j = i - jj
                off = pl.multiple_of(j * tk, tk)
                kb = k_ref[pl.ds(off, tk), :]
                vb = v_ref[pl.ds(off, tk), :]
                out, dv_add, dk_add = [], None, None
                for s in range(ns):
                    later_c, suf_c, dq_acc = carry[s]
                    strict, lsp, lsn, L, a = _sb_block(qs[s], kb, rows[s], j * tk + coli, scale, later_c, tri)
                    dl = _dot_nt(dos[s], vb) * a
                    before = dtots[s] - (suf_c + _exact_dot(dl, tri_inc, 3))
                    dz = jnp.where(strict, dl * jnp.exp(lsn) - jnp.exp(lsp) * before, 0.0) * scale
                    dzb = dz.astype(BF16)
                    dv_s = _dot_tn(a.astype(BF16), dos[s])
                    dk_s = _dot_tn(dzb, qs[s])
                    dv_add = dv_s if dv_add is None else dv_add + dv_s
                    dk_add = dk_s if dk_add is None else dk_add + dk_s
                    out.append((later_c + jnp.sum(L, axis=1, keepdims=True), suf_c + jnp.sum(dl, axis=1, keepdims=True),
                                dq_acc + jnp.dot(dzb, kb, preferred_element_type=F32)))
                dv_acc[pl.ds(off, tk), :] += dv_add
                dk_acc[pl.ds(off, tk), :] += dk_add
                return tuple(out)

            z1 = jnp.zeros((ts, 1), F32)
            res = lax.fori_loop(0, i + 1, inner, tuple((z1, z1, jnp.zeros((ts, HEAD_DIM), F32)) for _ in range(ns)))
            for s, (_, _, dq_acc) in enumerate(res):
                dq_ref[pl.ds(pl.multiple_of(i * tq + s * ts, ts), ts), :] = dq_acc.astype(dq_ref.dtype)
            return 0

        lax.fori_loop(0, nq, outer, 0)
        dk_ref[...] = dk_acc[...].astype(dk_ref.dtype)
        dv_ref[...] = dv_acc[...].astype(dv_ref.dtype)

    def col(off):
        return pl.BlockSpec((S, HEAD_DIM), lambda h: (0, off + h))

    dq, dk, dv = pl.pallas_call(
        body, name=name, grid=(nh,),
        in_specs=[col(0), col(nh), col(2 * nh), col(0), col(0)],
        out_specs=[col(0), col(0), col(0)],
        out_shape=[jax.ShapeDtypeStruct((S, nh * HEAD_DIM), BF16)] * 3,
        scratch_shapes=[pltpu.VMEM((S, HEAD_DIM), F32), pltpu.VMEM((S, HEAD_DIM), F32)],
        compiler_params=_cp(("parallel",)),
    )(qkv, qkv, qkv, o, do)
    return jnp.concatenate([dq, dk, dv], axis=1)


def seq_cumsum(name, x, reverse):
    S, W = x.shape

    tb = _pick_rows(S, 256)

    def body(x_ref, o_ref):
        parts = _split3(x_ref[...])
        c = lax.broadcasted_iota(jnp.int32, (tb, S), 1)
        for b in range(S // tb):
            r = b * tb + lax.broadcasted_iota(jnp.int32, (tb, S), 0)
            t = jnp.where(r <= c if reverse else r >= c, 1.0, 0.0).astype(BF16)
            out = None
            for p in parts:
                d = jnp.dot(t, p, preferred_element_type=F32)
                out = d if out is None else out + d
            o_ref[b * tb:(b + 1) * tb, :] = out

    return pl.pallas_call(body, name=name, out_shape=jax.ShapeDtypeStruct((S, W), F32),
                          compiler_params=pltpu.CompilerParams(vmem_limit_bytes=VMEM_LIMIT))(x)


def _adamw_math(w, g, m, v):
    c1 = 1.0 - ADAM_B1 ** ADAM_STEP
    c2 = 1.0 - ADAM_B2 ** ADAM_STEP
    nm = ADAM_B1 * m + (1.0 - ADAM_B1) * g
    nv = ADAM_B2 * v + (1.0 - ADAM_B2) * (g * g)
    return -ADAM_LR * ((nm / c1) / (jnp.sqrt(nv / c2) + ADAM_EPS) + ADAM_WD * w), nm, nv


def adamw(name, w, g, m, v):
    R, C = g.shape
    tr = _pick_rows(R, 256)

    def body(w_ref, g_ref, m_ref, v_ref, d_ref, nm_ref, nv_ref):
        d_ref[...], nm_ref[...], nv_ref[...] = _adamw_math(w_ref[...], g_ref[...], m_ref[...], v_ref[...])

    row = pl.BlockSpec((tr, C), lambda i: (i, 0))
    return pl.pallas_call(
        body, name=name, grid=(R // tr,), in_specs=[row] * 4, out_specs=[row] * 3,
        out_shape=[jax.ShapeDtypeStruct((R, C), F32)] * 3, compiler_params=_cp(("parallel",)),
    )(w, g, m, v)


def adamw_big(name, w, m, v, mine, other, c_arr, layer, prev):
    L, R, C = w.shape
    r2 = R // 2
    tr = _pick_rows(r2, 256)
    nb = r2 // tr

    def body(c_ref, w_ref, m_ref, v_ref, mine_ref, other_ref, *rest):
        g_ref, d_ref, nm_ref, nv_ref = rest[-4:]
        g = jnp.where(pl.program_id(0) == c_ref[0], mine_ref[...], other_ref[...])
        g_ref[...] = g
        d_ref[...], nm_ref[...], nv_ref[...] = _adamw_math(w_ref[...], g, m_ref[...], v_ref[...])

    sel = pl.BlockSpec((None, tr, C), lambda hf, i, c: (layer, hf * nb + i, 0))
    in_specs = [sel, sel, sel,
                pl.BlockSpec((tr, C), lambda hf, i, c: (jnp.where(hf == c[0], i, 0), 0)),
                pl.BlockSpec((tr, C), lambda hf, i, c: (jnp.where(hf == c[0], 0, i), 0))]
    operands = [c_arr, w, m, v, mine, other]
    aliases = {}
    if prev is not None:
        in_specs += [ANY] * 4
        aliases = {len(operands) + k: k for k in range(4)}
        operands += list(prev)
    return pl.pallas_call(
        body, name=name,
        grid_spec=pltpu.PrefetchScalarGridSpec(num_scalar_prefetch=1, grid=(2, nb), in_specs=in_specs, out_specs=[sel] * 4),
        out_shape=[jax.ShapeDtypeStruct((L, R, C), F32)] * 4, input_output_aliases=aliases,
        compiler_params=_cp(("arbitrary", "arbitrary")),
    )(*operands)


ANY = pl.BlockSpec(memory_space=pl.ANY)


def _place():
    x, y, c = lax.axis_index("x"), lax.axis_index("y"), lax.axis_index("c")
    others = [(1 - x, y), (x, 1 - y), (1 - x, 1 - y)]
    return x, y, c, others


def swap_halves(name, grads):
    n = len(grads)

    def body(*refs):
        src, dst = refs[:n], refs[n:2 * n]
        send, recv = refs[2 * n:]
        x, y, c, _ = _place()
        cps = []
        for t in range(n):
            r2 = grads[t].shape[1] // 2
            cps.append(pltpu.make_async_remote_copy(
                src_ref=src[t].at[:, pl.ds((1 - c) * r2, r2), :], dst_ref=dst[t],
                send_sem=send.at[t], recv_sem=recv.at[t], device_id=(x, y, 1 - c), device_id_type=MESH))
        for cp in cps:
            cp.start()
        for cp in cps:
            cp.wait()

    return pl.pallas_call(
        body, name=name, in_specs=[ANY] * n, out_specs=[ANY] * n,
        out_shape=[jax.ShapeDtypeStruct((N_CHIPS, g.shape[1] // 2, g.shape[2]), g.dtype) for g in grads],
        scratch_shapes=[pltpu.SemaphoreType.DMA((n,))] * 2,
    )(*grads)


def add_half(name, g, other, c_arr):
    _, R, C = g.shape
    r2 = R // 2
    tr = _pick_rows(r2, 512)
    nb = r2 // tr

    def body(c_ref, g_ref, o_ref, out_ref):
        out_ref[...] = (g_ref[...].astype(F32) + o_ref[...].astype(F32)).astype(out_ref.dtype)

    return pl.pallas_call(
        body, name=name,
        grid_spec=pltpu.PrefetchScalarGridSpec(
            num_scalar_prefetch=1, grid=(N_CHIPS, nb),
            in_specs=[pl.BlockSpec((None, tr, C), lambda s, i, c: (s, c[0] * nb + i, 0)),
                      pl.BlockSpec((None, tr, C), lambda s, i, c: (s, i, 0))],
            out_specs=pl.BlockSpec((None, tr, C), lambda s, i, c: (s, i, 0))),
        out_shape=jax.ShapeDtypeStruct((N_CHIPS, r2, C), BF16),
        compiler_params=_cp(("parallel", "parallel")),
    )(c_arr, g, other)


def sum_slabs(name, got, parts, chip_arr):
    _, r2, C = parts.shape
    tr = _pick_rows(r2, 512)

    def body(chip_ref, *refs):
        own_ref, out_ref = refs[N_CHIPS], refs[N_CHIPS + 1]
        acc = None
        for s in range(N_CHIPS):
            v = jnp.where(chip_ref[0] == s, own_ref[...], refs[s][...]).astype(F32)
            acc = v if acc is None else acc + v
        out_ref[...] = acc

    def slab(s):
        return pl.BlockSpec((None, tr, C), lambda i, ch: (jnp.where(ch[0] == s, (s + 1) % N_CHIPS, s), i, 0))

    return pl.pallas_call(
        body, name=name,
        grid_spec=pltpu.PrefetchScalarGridSpec(
            num_scalar_prefetch=1, grid=(r2 // tr,),
            in_specs=[slab(s) for s in range(N_CHIPS)] + [pl.BlockSpec((None, tr, C), lambda i, ch: (ch[0], i, 0))],
            out_specs=pl.BlockSpec((tr, C), lambda i, ch: (i, 0))),
        out_shape=jax.ShapeDtypeStruct((r2, C), F32), compiler_params=_cp(("parallel",)),
    )(chip_arr, got, got, got, got, parts)


def send_halves(name, halves):
    n = len(halves)

    def body(*refs):
        src, dst = refs[:n], refs[n:2 * n]
        send, recv = refs[2 * n:]
        x, y, c, _ = _place()
        cps = [pltpu.make_async_remote_copy(src_ref=src[t], dst_ref=dst[t], send_sem=send.at[t], recv_sem=recv.at[t],
                                            device_id=(x, y, 1 - c), device_id_type=MESH) for t in range(n)]
        for cp in cps:
            cp.start()
        for cp in cps:
            cp.wait()

    return pl.pallas_call(
        body, name=name, in_specs=[ANY] * n, out_specs=[ANY] * n,
        out_shape=[jax.ShapeDtypeStruct(h.shape, h.dtype) for h in halves],
        scratch_shapes=[pltpu.SemaphoreType.DMA((n,))] * 2,
    )(*halves)


HBM = pl.BlockSpec(memory_space=pltpu.HBM)
SEM = pl.BlockSpec(memory_space=pltpu.SEMAPHORE)
EFFECT = pltpu.SideEffectType.DATAFLOW_SIDE_EFFECTING


def _in_hbm(a):
    return pltpu.with_memory_space_constraint(a, pltpu.HBM)


def _chip_copies(kind, shapes, src, land, send, recv, mine):
    x, y, c, others = _place()
    me = 2 * x + y
    cps = []
    for t, shape in enumerate(shapes):
        for j, (px, py) in enumerate(others):
            slot = me if mine else 2 * px + py
            if kind == "gather":
                r2 = shape[0] // 2
                rows = pl.ds(c * r2, r2)
                s_ref, d_ref = src[t].at[rows, :], land[t].at[slot, rows, :]
            else:
                s_ref, d_ref = src[t].at[2 * px + py], land[t].at[slot]
            cps.append(pltpu.make_async_remote_copy(src_ref=s_ref, dst_ref=d_ref, send_sem=send.at[3 * t + j],
                                                    recv_sem=recv.at[3 * t + j], device_id=(px, py, c), device_id_type=MESH))
    return cps


def chips_start(name, kind, srcs, land_shapes, after=None):
    n = len(srcs)
    shapes = [s.shape for s in srcs]
    n_in = 2 * n + (0 if after is None else 1)

    def body(*refs):
        src, land = refs[:n], refs[n:2 * n]
        send, recv = refs[n_in], refs[n_in + 1]
        token = refs[-1]
        for cp in _chip_copies(kind, shapes, src, land, send, recv, True):
            cp.start()
        token[...] = jnp.zeros_like(token)

    lands = [lax.empty(s, srcs[0].dtype) for s in land_shapes]
    out = pl.pallas_call(
        body, name=name,
        out_shape=(pltpu.SemaphoreType.DMA((3 * n,)), pltpu.SemaphoreType.DMA((3 * n,)))
        + tuple(pltpu.HBM(s.shape, s.dtype) for s in srcs) + tuple(pltpu.HBM(l.shape, l.dtype) for l in lands)
        + (jax.ShapeDtypeStruct((8, LANE), F32),),
        in_specs=(HBM,) * (2 * n) + (ANY,) * (n_in - 2 * n),
        out_specs=(SEM, SEM) + (HBM,) * (2 * n) + (pl.BlockSpec(memory_space=pltpu.VMEM),),
        input_output_aliases={k: 2 + k for k in range(2 * n)},
        compiler_params=pltpu.CompilerParams(has_side_effects=EFFECT),
    )(*[_in_hbm(s) for s in srcs], *[_in_hbm(l) for l in lands], *([] if after is None else [after]))
    return (kind, shapes, out[0], out[1], out[2:2 + n], out[2 + n:2 + 2 * n]), out[-1][:1, :1]


def chips_wait(name, handle, after):
    kind, shapes, send, recv, srcs, lands = handle
    n = len(srcs)

    def body(*refs):
        src, land = refs[:n], refs[n:2 * n]
        for cp in _chip_copies(kind, shapes, src, land, refs[2 * n], refs[2 * n + 1], True):
            cp.wait_send()
        for cp in _chip_copies(kind, shapes, src, land, refs[2 * n], refs[2 * n + 1], False):
            cp.wait_recv()

    out = pl.pallas_call(
        body, name=name,
        out_shape=tuple(pltpu.HBM(s.shape, s.dtype) for s in srcs) + tuple(pltpu.HBM(l.shape, l.dtype) for l in lands),
        in_specs=(HBM,) * (2 * n) + (SEM, SEM, ANY), out_specs=(HBM,) * (2 * n),
        input_output_aliases={k: k for k in range(2 * n)},
        compiler_params=pltpu.CompilerParams(has_side_effects=EFFECT),
    )(*srcs, *lands, send, recv, after)
    return list(out[n:]), list(out[:n])


def pass_to_sibling(name, lands, shards):
    n = len(lands)

    def body(*refs):
        buf = refs[n:2 * n]
        send, recv = refs[2 * n:]
        x, y, c, others = _place()

        def cp(t, j, core):
            r2 = lands[t].shape[1] // 2
            px, py = others[j]
            rows = buf[t].at[2 * px + py, pl.ds(core * r2, r2), :]
            return pltpu.make_async_remote_copy(src_ref=rows, dst_ref=rows, send_sem=send.at[t, j], recv_sem=recv.at[t, j],
                                                device_id=(x, y, 1 - c), device_id_type=MESH)

        for t in range(n):
            for j in range(3):
                cp(t, j, c).start()
        for t in range(n):
            for j in range(3):
                cp(t, j, 1 - c).wait_recv()
        for t in range(n):
            for j in range(3):
                cp(t, j, c).wait_send()

    got = pl.pallas_call(
        body, name=name, in_specs=[ANY] * n, out_specs=[ANY] * n,
        out_shape=[jax.ShapeDtypeStruct(l.shape, l.dtype) for l in lands],
        input_output_aliases={k: k for k in range(n)},
        scratch_shapes=[pltpu.SemaphoreType.DMA((n, 3))] * 2,
    )(*lands)
    chip = 2 * lax.axis_index("x") + lax.axis_index("y")
    return [lax.dynamic_update_slice(g, s[None], (chip, 0, 0)) for g, s in zip(got, shards)]


def all_sum_small(name, v):
    R = v.shape[0]

    def body(v_ref, out_ref, slots, send, recv):
        x, y, c, _ = _place()
        me = 4 * x + 2 * y + c
        slots[me] = v_ref[...]
        cps = []
        for k in range(1, 8):
            dx, dy, dc = (k >> 2) & 1, (k >> 1) & 1, k & 1
            to = (x ^ dx, y ^ dy, c ^ dc)
            cps.append(pltpu.make_async_remote_copy(
                src_ref=v_ref, dst_ref=slots.at[me], send_sem=send.at[k - 1], recv_sem=recv.at[k - 1],
                device_id=to, device_id_type=MESH))
        for cp in cps:
            cp.start()
        for k in range(1, 8):
            dx, dy, dc = (k >> 2) & 1, (k >> 1) & 1, k & 1
            frm = 4 * (x ^ dx) + 2 * (y ^ dy) + (c ^ dc)
            pltpu.make_async_remote_copy(
                src_ref=v_ref, dst_ref=slots.at[frm], send_sem=send.at[k - 1], recv_sem=recv.at[k - 1],
                device_id=(x, y, c), device_id_type=MESH).wait_recv()
        for cp in cps:
            cp.wait_send()
        acc = slots[0]
        for d in range(1, 8):
            acc = acc + slots[d]
        out_ref[...] = acc

    vm = pl.BlockSpec(memory_space=pltpu.VMEM)
    return pl.pallas_call(
        body, name=name, in_specs=[vm], out_specs=vm, out_shape=jax.ShapeDtypeStruct((R, LANE), F32),
        scratch_shapes=[pltpu.VMEM((8, R, LANE), F32), pltpu.SemaphoreType.DMA((7,)), pltpu.SemaphoreType.DMA((7,))],
    )(v)


def _rope_mat(n, lo):
    half = MLA_ROPE // 2
    r = lax.broadcasted_iota(jnp.int32, (n, n), 0)
    c = lax.broadcasted_iota(jnp.int32, (n, n), 1)
    plus = (c >= lo + half) & (c < lo + 2 * half) & (r == c - half)
    minus = (c >= lo) & (c < lo + half) & (r == c + half)
    return (jnp.where(plus, 1.0, 0.0) - jnp.where(minus, 1.0, 0.0)).astype(BF16)


def _rope(v, cos, sin, lo):
    return v * cos + _exact_dot(v, _rope_mat(v.shape[-1], lo), 3) * sin


def rope_tables(pos):
    S = pos.shape[0]
    half = MLA_ROPE // 2
    inv = (np.float32(ROPE_THETA) ** (-np.arange(0, half, dtype=np.float32) * np.float32(2.0 / MLA_ROPE))).astype(np.float32)
    f1 = np.zeros((1, LANE), np.float32)
    f1[0, :MLA_ROPE] = np.tile(inv, 2)
    f2 = np.zeros((1, MLA_QK_PAD), np.float32)
    f2[0, MLA_NOPE:MLA_QK] = np.tile(inv, 2)
    tr = _pick_rows(S, 256)

    def body(p_ref, f1_ref, f2_ref, c1, s1, c2, s2):
        p = p_ref[...].astype(F32)
        a1 = p * f1_ref[...]
        a2 = p * f2_ref[...]
        c1[...] = jnp.cos(a1)
        s1[...] = jnp.sin(a1)
        c2[...] = jnp.cos(a2)
        s2[...] = jnp.sin(a2)

    def row(w):
        return pl.BlockSpec((tr, w), lambda i: (i, 0))

    def full(w):
        return pl.BlockSpec((1, w), lambda i: (0, 0))

    return pl.pallas_call(
        body, name="rope_tables", grid=(S // tr,), in_specs=[row(1), full(LANE), full(MLA_QK_PAD)],
        out_specs=[row(LANE), row(LANE), row(MLA_QK_PAD), row(MLA_QK_PAD)],
        out_shape=[jax.ShapeDtypeStruct((S, LANE), F32)] * 2 + [jax.ShapeDtypeStruct((S, MLA_QK_PAD), F32)] * 2,
        compiler_params=_cp(("parallel",)),
    )(pos, jnp.asarray(f1), jnp.asarray(f2))


def _log_sigmoid(z):
    return jnp.minimum(z, 0.0) - jnp.log(1.0 + jnp.exp(-jnp.abs(z)))


def _fox_qk_fn(q, k, gq, gk):
    return _rms(q, gq, HEAD_DIM), _rms(k, gk, HEAD_DIM)


def _fox_gate_fn(f, b):
    return (_log_sigmoid(f + b),)


def _mla_pre1_fn(q_rank, kv_rank):
    def fn(cq, ckv, kr, cos, sin, qn, kvn):
        return _rms(cq, qn, q_rank), _rms(ckv, kvn, kv_rank), _rope(kr, cos, sin, 0)
    return fn


def _mla_pre2_fn(qb, kn, kr, cos, sin, gq, gk):
    qh = _rms(_rope(qb, cos, sin, MLA_NOPE), gq, MLA_QK)
    kh = _rms(jnp.concatenate([kn, kr], axis=1), gk, MLA_QK)
    return qh, kh


def _pad_cols(a, n):
    return jnp.pad(a, ((0, 0), (0, n - a.shape[1])))


def _relu2(acc):
    r = jnp.maximum(acc, 0.0)
    return r * r, r


def _add(acc, res):
    return (acc + res,)


def _times_2r(acc, r):
    return (acc * (2.0 * r.astype(F32)),)


def kernel(x, positions, mix_norm, mlp_norm, sb_w_in, sb_w_out, fox_w_in, fox_b_f, fox_q_gain, fox_k_gain, fox_w_out, mla_w_in, mla_q_norm, mla_kv_norm, mla_w_uq, mla_w_ukv, mla_q_gain, mla_k_gain, mla_w_out, mlp_w1, mlp_w2, loss_target, m_mix_norm, m_mlp_norm, m_sb_w_in, m_sb_w_out, m_fox_w_in, m_fox_b_f, m_fox_q_gain, m_fox_k_gain, m_fox_w_out, m_mla_w_in, m_mla_q_norm, m_mla_kv_norm, m_mla_w_uq, m_mla_w_ukv, m_mla_q_gain, m_mla_k_gain, m_mla_w_out, m_mlp_w1, m_mlp_w2, v_mix_norm, v_mlp_norm, v_sb_w_in, v_sb_w_out, v_fox_w_in, v_fox_b_f, v_fox_q_gain, v_fox_k_gain, v_fox_w_out, v_mla_w_in, v_mla_q_norm, v_mla_kv_norm, v_mla_w_uq, v_mla_w_ukv, v_mla_q_gain, v_mla_k_gain, v_mla_w_out, v_mlp_w1, v_mlp_w2):
    S, D = x.shape[1], x.shape[2]
    nh = D // HEAD_DIM
    W = nh * HEAD_DIM
    depth = mix_norm.shape[0]
    q_rank, kv_rank = mla_w_uq.shape[1], mla_w_ukv.shape[1]
    n_fox_in = 3 * W + nh
    fox_pad = -(-n_fox_in // LANE) * LANE
    n_down = q_rank + kv_rank + MLA_ROPE
    down_pad = q_rank + kv_rank + LANE
    tr = _pick_rows(S, 256)
    tk = _pick_rows(S, 256)

    ax, ay, ac = lax.axis_index("x"), lax.axis_index("y"), lax.axis_index("c")
    chip = 2 * ax + ay
    c_arr = jnp.reshape(ac, (1,)).astype(jnp.int32)
    chip_arr = jnp.reshape(chip, (1,)).astype(jnp.int32)

    def bf(a):
        return a.astype(BF16)

    n_small_in = q_rank + kv_rank
    rows_in = -(-n_small_in // (8 * LANE)) * 8
    placed = jnp.zeros((rows_in * LANE,), F32)
    placed = lax.dynamic_update_slice(placed, mla_q_norm[0], (chip * mla_q_norm.shape[1],))
    placed = lax.dynamic_update_slice(placed, mla_kv_norm[0], (q_rank + chip * mla_kv_norm.shape[1],))
    placed = placed * (ac == 0).astype(F32)
    norms = all_sum_small("gather_norms", placed.reshape(rows_in, LANE)).reshape(-1)
    q_norm_full = norms[:q_rank].reshape(1, q_rank)
    kv_norm_full = norms[q_rank:q_rank + kv_rank].reshape(1, kv_rank)

    def gather_start(i, part, after):
        kind, j = i % N_MIXERS, i // N_MIXERS
        if part == "mlp":
            names, shards = ["w1", "w2"], [bf(mlp_w1[i]), bf(mlp_w2[i])]
        elif kind == 0:
            names, shards = ["w_in", "w_out"], [bf(sb_w_in[j]), bf(sb_w_out[j])]
        elif kind == 1:
            names, shards = ["w_in", "w_out"], [bf(fox_w_in[j]), bf(fox_w_out[j])]
        else:
            names = ["w_in", "w_uq", "w_ukv", "w_out"]
            shards = [bf(mla_w_in[j]), bf(mla_w_uq[j]), bf(mla_w_ukv[j]), bf(mla_w_out[j])]
        handle, token = chips_start(f"gather_start_{part}_{i}", "gather", shards, [(N_CHIPS,) + s.shape for s in shards], after)
        return f"{part}_{i}", names, handle, token

    def gather_finish(pend, after):
        tag, names, handle, _ = pend
        lands, shards = chips_wait(f"gather_wait_{tag}", handle, after)
        return dict(zip(names, pass_to_sibling(f"gather_pass_{tag}", lands, shards)))

    def rows_stacked(w):
        return w.reshape(w.shape[0] * w.shape[1], w.shape[2])

    xc = x[0]
    saved = []
    layers = []
    tables = None
    pending = gather_start(0, "mix", None)
    for i in range(depth):
        kind, j = i % N_MIXERS, i // N_MIXERS
        L = gather_finish(pending, xc)
        pending = gather_start(i, "mlp", L["w_out"])
        g1 = mix_norm[i:i + 1] + pending[3]
        hb = rmsnorm_fwd(f"norm1_{i}", xc, g1)
        st = dict(x=xc, hb=hb)
        if kind == 0:
            qkv = mm_nn(f"sb_proj_{i}", hb, L["w_in"], BF16)[0]
            o, o_f32 = sb_fwd(f"sb_attn_{i}", qkv, nh)
            st.update(qkv=qkv, o_f32=o_f32)
        elif kind == 1:
            w_in = _pad_cols(jnp.concatenate([L["w_in"][s] for s in range(N_CHIPS)], axis=1), fox_pad)
            proj = mm_nn(f"fox_proj_{i}", hb, w_in, F32)[0]
            gq, gk = fox_q_gain[j:j + 1], fox_k_gain[j:j + 1]
            qk_rows = [(proj, HEAD_DIM, lambda h: h), (proj, HEAD_DIM, lambda h: nh + h)]
            qh, kh = rowwise_fwd(f"fox_qk_{i}", _fox_qk_fn, qk_rows, [gq, gk],
                                 [(W, HEAD_DIM, lambda h: h, BF16)] * 2, tr, nh)
            b_pad = _pad_cols(fox_b_f[j:j + 1], LANE)
            gate_rows = [(proj, LANE, lambda h: 3 * nh)]
            logf = rowwise_fwd(f"fox_gate_{i}", _fox_gate_fn, gate_rows, [b_pad], [(LANE, LANE, _c0, F32)], tr)[0]
            cf = seq_cumsum(f"fox_cf_{i}", logf, False)[:, :nh].T
            cf_col, cf_row = cf.reshape(nh, S, 1), cf.reshape(nh, S // tk, tk)
            scale = 1.0 / math.sqrt(HEAD_DIM)
            o, o_f32, lse = attn_fwd(f"fox_attn_{i}", qh, kh, proj, nh, HEAD_DIM, 0, 0, 2 * nh, 1, scale, cf_col, cf_row)
            st.update(w_in=w_in, proj=proj, qk_rows=qk_rows, gq=gq, gk=gk, qh=qh, kh=kh, b_pad=b_pad, gate_rows=gate_rows,
                      cf_col=cf_col, cf_row=cf_row, lse=lse, scale=scale, o_f32=o_f32)
        else:
            w_in = _pad_cols(rows_stacked(L["w_in"]), down_pad)
            down = mm_nn(f"mla_down_{i}", hb, w_in, F32)[0]
            if tables is None:
                tables = rope_tables(positions.reshape(S, 1))
            cos1, sin1, cos2, sin2 = tables
            pre1_rows = [(down[:, :q_rank], q_rank, _c0), (down[:, q_rank:q_rank + kv_rank], kv_rank, _c0),
                         (down[:, q_rank + kv_rank:], LANE, _c0), (cos1, LANE, _c0), (sin1, LANE, _c0)]
            pre1_fn = _mla_pre1_fn(q_rank, kv_rank)
            c_q, c_kv, k_rope = rowwise_fwd(
                f"mla_pre1_{i}", pre1_fn, pre1_rows, [q_norm_full, kv_norm_full],
                [(q_rank, q_rank, _c0, BF16), (kv_rank, kv_rank, _c0, BF16), (LANE, LANE, _c0, F32)], tr)
            qfull = mm_nn(f"mla_uq_{i}", c_q, L["w_uq"], F32)[0]
            kv = mm_nn(f"mla_ukv_{i}", c_kv, L["w_ukv"], F32)[0]
            qpad = jnp.pad(qfull.reshape(S, nh, MLA_QK), ((0, 0), (0, 0), (0, MLA_QK_PAD - MLA_QK))).reshape(S, nh * MLA_QK_PAD)
            gq, gk = _pad_cols(mla_q_gain[j:j + 1], MLA_QK_PAD), _pad_cols(mla_k_gain[j:j + 1], MLA_QK_PAD)
            pre2_rows = [(qpad, MLA_QK_PAD, lambda h: h), (kv, MLA_NOPE, lambda h: 2 * h), (k_rope, LANE, _c0),
                         (cos2, MLA_QK_PAD, _c0), (sin2, MLA_QK_PAD, _c0)]
            qh, kh = rowwise_fwd(f"mla_pre2_{i}", _mla_pre2_fn, pre2_rows, [gq, gk],
                                 [(nh * MLA_QK_PAD, MLA_QK_PAD, lambda h: h, BF16)] * 2, tr, nh)
            scale = 1.0 / math.sqrt(MLA_QK)
            o, o_f32, lse = attn_fwd(f"mla_attn_{i}", qh, kh, kv, nh, MLA_QK_PAD, 0, 0, 1, 2, scale)
            st.update(w_in=w_in, pre1_rows=pre1_rows, pre1_fn=pre1_fn, c_q=c_q, c_kv=c_kv, pre2_rows=pre2_rows, gq=gq, gk=gk,
                      qh=qh, kh=kh, kv=kv, lse=lse, scale=scale, o_f32=o_f32)
        x1 = mm_nn(f"mix_out_{i}", o, rows_stacked(L["w_out"]), F32, epilogue=_add, extras=(xc,))[0]
        L.update(gather_finish(pending, x1))
        layers.append(L)
        g2 = mlp_norm[i:i + 1]
        if i + 1 < depth:
            pending = gather_start(i + 1, "mix", L["w2"])
            g2 = g2 + pending[3]
        h2 = rmsnorm_fwd(f"norm2_{i}", x1, g2)
        a, r = mm_nn(f"mlp_up_{i}", h2, L["w1"], None, epilogue=_relu2, out_dtypes=[BF16, BF16])
        xc = mm_nn(f"mlp_down_{i}", a, rows_stacked(L["w2"]), F32, epilogue=_add, extras=(x1,))[0]
        st.update(o=o, x1=x1, h2=h2, a=a, r=r, g1=g1, g2=g2)
        saved.append(st)

    loss_local, dx, dxb = loss_head(xc, loss_target[0])
    loss = lax.psum(loss_local[0, 0], ("x", "y", "c"))

    small = {}

    def stack_rows(g):
        return g.reshape(N_CHIPS, g.shape[0] // N_CHIPS, g.shape[1])

    full = [dict() for _ in range(depth)]
    flying = None

    def exchange_start(i, part, gr):
        tag = f"{part}_{i}"
        names = list(gr)
        grads = [gr[n] for n in names]
        theirs = swap_halves(f"swap_{tag}", grads)
        parts = [add_half(f"addh_{tag}_{t}", g, o, c_arr) for t, (g, o) in enumerate(zip(grads, theirs))]
        handle, token = chips_start(f"xchg_start_{tag}", "scatter", parts, [p.shape for p in parts])
        return i, tag, names, handle, token

    def exchange_finish(fly, after):
        i, tag, names, handle, _ = fly
        got, parts = chips_wait(f"xchg_wait_{tag}", handle, after)
        halves = [sum_slabs(f"sum4_{tag}_{t}", g, p, chip_arr) for t, (g, p) in enumerate(zip(got, parts))]
        full[i].update(zip(names, zip(halves, send_halves(f"sendh_{tag}", halves))))

    for i in reversed(range(depth)):
        kind, j = i % N_MIXERS, i // N_MIXERS
        L, st = layers[i], saved[i]
        gr = {}
        if flying is not None:
            st["g2"] = st["g2"] + flying[4]
        gr["w2"] = stack_rows(mm_tn(f"mlp_dw2_{i}", st["a"], dxb, BF16, False))
        du = mm_nt(f"mlp_du_{i}", dxb, rows_stacked(L["w2"]), BF16, epilogue=_times_2r, extras=(st["r"],))
        gr["w1"] = mm_tn(f"mlp_dw1_{i}", st["h2"], du, BF16, True)
        dh2 = mm_nt(f"mlp_dh_{i}", du, L["w1"], F32)
        dx1, dx1b, dg2 = rmsnorm_bwd(f"norm2_bwd_{i}", st["x1"], st["g2"], dh2, dx)
        small[("mlp_norm", i)] = dg2
        if flying is not None:
            exchange_finish(flying, dx1)
        flying = exchange_start(i, "mlp", gr)
        st["g1"] = st["g1"] + flying[4]
        gr = {}
        gr["w_out"] = stack_rows(mm_tn(f"mix_dwout_{i}", st["o"], dx1b, BF16, False))
        do = mm_nt(f"mix_do_{i}", dx1b, rows_stacked(L["w_out"]), BF16)
        if kind == 0:
            dqkv = sb_bwd(f"sb_attn_bwd_{i}", st["qkv"], st["o_f32"], do, nh)
            gr["w_in"] = mm_tn(f"sb_dwin_{i}", st["hb"], dqkv, BF16, True)
            dh = mm_nt(f"sb_dh_{i}", dqkv, L["w_in"], F32)
        elif kind == 1:
            dqh, dkh, dv, dcf = attn_bwd(f"fox_attn_bwd_{i}", st["qh"], st["kh"], st["proj"], st["o_f32"], do, st["lse"], nh,
                                         HEAD_DIM, 0, 0, 2 * nh, 1, st["scale"], st["cf_col"], st["cf_row"])
            dcf_s = _pad_cols(dcf.reshape(nh, S).T, LANE)
            dlogf = seq_cumsum(f"fox_dcf_{i}", dcf_s, True)
            dgate, db = rowwise_bwd(f"fox_gate_bwd_{i}", _fox_gate_fn, st["gate_rows"], [st["b_pad"]], [(dlogf, LANE, _c0)],
                                    [(LANE, LANE, _c0, BF16, False)], tr)
            dq, dk, dgq, dgk = rowwise_bwd(
                f"fox_qk_bwd_{i}", _fox_qk_fn, st["qk_rows"], [st["gq"], st["gk"]],
                [(dqh, HEAD_DIM, lambda h: h), (dkh, HEAD_DIM, lambda h: h)], [(W, HEAD_DIM, lambda h: h, BF16, False)] * 2, tr, nh)
            small[("fox_b_f", j)] = db[:, :nh]
            small[("fox_q_gain", j)] = dgq
            small[("fox_k_gain", j)] = dgk
            dproj = jnp.concatenate([dq, dk, bf(dv), dgate], axis=1)
            dw = mm_tn(f"fox_dwin_{i}", st["hb"], dproj, BF16, False)
            n4 = n_fox_in // N_CHIPS
            gr["w_in"] = jnp.stack([dw[:, s * n4:(s + 1) * n4] for s in range(N_CHIPS)])
            dh = mm_nt(f"fox_dh_{i}", dproj, st["w_in"], F32)
        else:
            dqh, dkh, dv = attn_bwd(f"mla_attn_bwd_{i}", st["qh"], st["kh"], st["kv"], st["o_f32"], do, st["lse"], nh,
                                    MLA_QK_PAD, 0, 0, 1, 2, st["scale"])
            dqpad, dkn, dkr, dgq, dgk = rowwise_bwd(
                f"mla_pre2_bwd_{i}", _mla_pre2_fn, st["pre2_rows"], [st["gq"], st["gk"]],
                [(dqh, MLA_QK_PAD, lambda h: h), (dkh, MLA_QK_PAD, lambda h: h)],
                [(nh * MLA_QK_PAD, MLA_QK_PAD, lambda h: h, BF16, False), (W, MLA_NOPE, lambda h: h, BF16, False),
                 (LANE, LANE, _c0, F32, True)], tr, nh, n_diff=3)
            small[("mla_q_gain", j)] = dgq[:, :MLA_QK]
            small[("mla_k_gain", j)] = dgk[:, :MLA_QK]
            dqfull = dqpad.reshape(S, nh, MLA_QK_PAD)[:, :, :MLA_QK].reshape(S, nh * MLA_QK)
            dkv = jnp.stack([dkn.reshape(S, nh, MLA_NOPE), bf(dv).reshape(S, nh, MLA_V)], axis=2).reshape(S, nh * (MLA_NOPE + MLA_V))
            gr["w_uq"] = mm_tn(f"mla_dwuq_{i}", st["c_q"], dqfull, BF16, True)
            dc_q = mm_nt(f"mla_dcq_{i}", dqfull, L["w_uq"], F32)
            gr["w_ukv"] = mm_tn(f"mla_dwukv_{i}", st["c_kv"], dkv, BF16, True)
            dc_kv = mm_nt(f"mla_dckv_{i}", dkv, L["w_ukv"], F32)
            d1, d2, d3, dqn, dkvn = rowwise_bwd(
                f"mla_pre1_bwd_{i}", st["pre1_fn"], st["pre1_rows"], [q_norm_full, kv_norm_full],
                [(dc_q, q_rank, _c0), (dc_kv, kv_rank, _c0), (dkr, LANE, _c0)],
                [(q_rank, q_rank, _c0, BF16, False), (kv_rank, kv_rank, _c0, BF16, False), (LANE, LANE, _c0, BF16, False)],
                tr, n_diff=3)
            small[("mla_q_norm", j)] = dqn
            small[("mla_kv_norm", j)] = dkvn
            ddown = jnp.concatenate([d1, d2, d3], axis=1)
            dw = mm_tn(f"mla_dwin_{i}", st["hb"], ddown, BF16, False)
            gr["w_in"] = stack_rows(dw[:, :n_down])
            dh = mm_nt(f"mla_dh_{i}", ddown, st["w_in"], F32)
        dx, dxb, dg1 = rmsnorm_bwd(f"norm1_bwd_{i}", st["x"], st["g1"], dh, dx1)
        small[("mix_norm", i)] = dg1
        if flying is not None:
            exchange_finish(flying, dx)
        flying = exchange_start(i, "mix", gr)

    keys = list(small)
    flat = jnp.concatenate([small[k].reshape(-1) for k in keys])
    rows_g = -(-flat.shape[0] // (8 * LANE)) * 8
    flat = jnp.pad(flat, (0, rows_g * LANE - flat.shape[0]))
    summed = all_sum_small("sum_small", flat.reshape(rows_g, LANE)).reshape(-1)
    sg, off = {}, 0
    for k in keys:
        n = small[k].size
        sg[k] = summed[off:off + n].reshape(small[k].shape)
        off += n

    def holders(kind_of):
        return [i for i in range(depth) if kind_of is None or i % N_MIXERS == kind_of]

    tensors = {
        "sb_w_in": (sb_w_in, m_sb_w_in, v_sb_w_in, "w_in", holders(0)),
        "sb_w_out": (sb_w_out, m_sb_w_out, v_sb_w_out, "w_out", holders(0)),
        "fox_w_in": (fox_w_in, m_fox_w_in, v_fox_w_in, "w_in", holders(1)),
        "fox_w_out": (fox_w_out, m_fox_w_out, v_fox_w_out, "w_out", holders(1)),
        "mla_w_in": (mla_w_in, m_mla_w_in, v_mla_w_in, "w_in", holders(2)),
        "mla_w_uq": (mla_w_uq, m_mla_w_uq, v_mla_w_uq, "w_uq", holders(2)),
        "mla_w_ukv": (mla_w_ukv, m_mla_w_ukv, v_mla_w_ukv, "w_ukv", holders(2)),
        "mla_w_out": (mla_w_out, m_mla_w_out, v_mla_w_out, "w_out", holders(2)),
        "mlp_w1": (mlp_w1, m_mlp_w1, v_mlp_w1, "w1", holders(None)),
        "mlp_w2": (mlp_w2, m_mlp_w2, v_mlp_w2, "w2", holders(None)),
    }
    updated = {n: None for n in tensors}

    def update_layers(first):
        for n, (w, m, v, key, held) in tensors.items():
            for l in reversed(range(len(held))):
                if (held[l] == 0) == first:
                    mine, other = full[held[l]][key]
                    updated[n] = adamw_big(f"adamw_{n}_{l}", w, m, v, mine, other, c_arr, l, updated[n])

    update_layers(False)
    exchange_finish(flying, updated["mlp_w2"][0])
    update_layers(True)

    def update_big(name):
        return list(updated[name])

    def update_small(name, w, m, v, g):
        return [g] + list(adamw(f"adamw_{name}", w, g, m, v))

    def small_rows(name, count):
        return jnp.concatenate([sg[(name, l)] for l in range(count)], axis=0)

    def my_part(g, n):
        return lax.dynamic_slice(g, (0, chip * n), (g.shape[0], n))

    res = {
        "mix_norm": update_small("mix_norm", mix_norm, m_mix_norm, v_mix_norm, small_rows("mix_norm", depth)),
        "mlp_norm": update_small("mlp_norm", mlp_norm, m_mlp_norm, v_mlp_norm, small_rows("mlp_norm", depth)),
        "sb_w_in": update_big("sb_w_in"),
        "sb_w_out": update_big("sb_w_out"),
        "fox_w_in": update_big("fox_w_in"),
        "fox_b_f": update_small("fox_b_f", fox_b_f, m_fox_b_f, v_fox_b_f, small_rows("fox_b_f", fox_b_f.shape[0])),
        "fox_q_gain": update_small("fox_q_gain", fox_q_gain, m_fox_q_gain, v_fox_q_gain, small_rows("fox_q_gain", fox_q_gain.shape[0])),
        "fox_k_gain": update_small("fox_k_gain", fox_k_gain, m_fox_k_gain, v_fox_k_gain, small_rows("fox_k_gain", fox_k_gain.shape[0])),
        "fox_w_out": update_big("fox_w_out"),
        "mla_w_in": update_big("mla_w_in"),
        "mla_q_norm": update_small("mla_q_norm", mla_q_norm, m_mla_q_norm, v_mla_q_norm,
                                   my_part(small_rows("mla_q_norm", mla_q_norm.shape[0]), mla_q_norm.shape[1])),
        "mla_kv_norm": update_small("mla_kv_norm", mla_kv_norm, m_mla_kv_norm, v_mla_kv_norm,
                                    my_part(small_rows("mla_kv_norm", mla_kv_norm.shape[0]), mla_kv_norm.shape[1])),
        "mla_w_uq": update_big("mla_w_uq"),
        "mla_w_ukv": update_big("mla_w_ukv"),
        "mla_q_gain": update_small("mla_q_gain", mla_q_gain, m_mla_q_gain, v_mla_q_gain, small_rows("mla_q_gain", mla_q_gain.shape[0])),
        "mla_k_gain": update_small("mla_k_gain", mla_k_gain, m_mla_k_gain, v_mla_k_gain, small_rows("mla_k_gain", mla_k_gain.shape[0])),
        "mla_w_out": update_big("mla_w_out"),
        "mlp_w1": update_big("mlp_w1"),
        "mlp_w2": update_big("mlp_w2"),
    }
    order = ["mix_norm", "mlp_norm", "sb_w_in", "sb_w_out", "fox_w_in", "fox_b_f", "fox_q_gain", "fox_k_gain", "fox_w_out",
             "mla_w_in", "mla_q_norm", "mla_kv_norm", "mla_w_uq", "mla_w_ukv", "mla_q_gain", "mla_k_gain", "mla_w_out",
             "mlp_w1", "mlp_w2"]
    outs = [loss, dx.reshape(x.shape)]
    for k in range(4):
        outs += [res[n][k] for n in order]
    return tuple(outs)
```

```python
import functools
import math

import numpy as np
import jax
import jax.numpy as jnp
from jax import lax
from jax.experimental import pallas as pl
from jax.experimental.pallas import tpu as pltpu

F32 = jnp.float32
BF16 = jnp.bfloat16
MESH = pl.DeviceIdType.MESH

EPS = 1e-6
HEAD_DIM = 128
MLA_NOPE = 128
MLA_ROPE = 64
MLA_V = 128
MLA_QK = MLA_NOPE + MLA_ROPE
MLA_QK_PAD = 256
ROPE_THETA = 10000.0
N_MIXERS = 3
ADAM_LR = 0.001
ADAM_B1 = 0.9
ADAM_B2 = 0.999
ADAM_EPS = 1e-08
ADAM_WD = 0.01
ADAM_STEP = 10

LANE = 128
N_CHIPS = 4
VMEM_LIMIT = 48 * 1024 * 1024
NEG = -1e30


def _cp(sem):
    return pltpu.CompilerParams(dimension_semantics=sem, vmem_limit_bytes=VMEM_LIMIT)


def _pick(dim, cap):
    best = None
    b = LANE
    while b <= min(dim, cap):
        if dim % b == 0:
            best = b
        b += LANE
    return best if best is not None else dim


def _pick_rows(dim, cap):
    b = min(dim, cap)
    while dim % b:
        b -= 8
    return b


def _matmul(name, a, b, a_blk, a_map, b_blk, b_map, dn, grid, acc_shape, outs, extras=(), epilogue=None, dep=None):
    nk = grid[2]
    n_ex, n_out = len(extras), len(outs)
    n_dep = 0 if dep is None else 1

    def body(*refs):
        a_ref, b_ref = refs[0], refs[1]
        ex_refs = refs[2:2 + n_ex]
        out_refs = refs[2 + n_ex + n_dep:2 + n_ex + n_dep + n_out]
        acc = refs[-1]
        k = pl.program_id(2)

        @pl.when(k == 0)
        def _():
            acc[...] = jnp.zeros_like(acc)

        acc[...] += lax.dot_general(a_ref[...], b_ref[...], dn, preferred_element_type=F32)

        @pl.when(k == nk - 1)
        def _():
            res = acc[...]
            vals = epilogue(res, *[e[...] for e in ex_refs]) if epilogue is not None else (res,)
            for o, v in zip(out_refs, vals):
                o[...] = v.astype(o.dtype)

    in_specs = [pl.BlockSpec(a_blk, a_map), pl.BlockSpec(b_blk, b_map)]
    in_specs += [pl.BlockSpec(blk, lambda i, j, k, m=m: m(i, j)) for (_, blk, m) in extras]
    in_specs += [pl.BlockSpec(memory_space=pl.ANY)] * n_dep
    out_specs = [pl.BlockSpec(blk, lambda i, j, k, m=m: m(i, j)) for (_, _, blk, m) in outs]
    out_shape = [jax.ShapeDtypeStruct(s, d) for (s, d, _, _) in outs]
    res = pl.pallas_call(
        body, name=name, grid=grid, in_specs=in_specs, out_specs=out_specs, out_shape=out_shape,
        scratch_shapes=[pltpu.VMEM(acc_shape, F32)],
        compiler_params=_cp(("parallel", "parallel", "arbitrary")),
    )(a, b, *[e[0] for e in extras], *([] if dep is None else [dep]))
    return res


BM, BN, BK = 1024, 1024, 2048


def mm_nn(name, a, w, out_dtype, epilogue=None, extras=(), n_out=1, out_dtypes=None):
    M, K = a.shape
    stacked = w.ndim == 3
    n4 = w.shape[-1]
    N = n4 * (N_CHIPS if stacked else 1)
    bm, bn, bk = _pick_rows(M, BM), _pick(n4, BN), _pick(K, BK)
    nb = n4 // bn
    if stacked:
        b_blk, b_map = (None, bk, bn), (lambda i, j, k: (j // nb, k, j % nb))
    else:
        b_blk, b_map = (bk, bn), (lambda i, j, k: (k, j))
    dts = out_dtypes if out_dtypes is not None else [out_dtype] * n_out
    outs = [((M, N), d, (bm, bn), lambda i, j: (i, j)) for d in dts]
    exs = [(e, (bm, bn), lambda i, j: (i, j)) for e in extras]
    return _matmul(name, a, w, (bm, bk), lambda i, j, k: (i, k), b_blk, b_map,
                   (((1,), (0,)), ((), ())), (M // bm, N // bn, K // bk), (bm, bn), outs, exs, epilogue)


def mm_nt(name, dy, w, out_dtype, epilogue=None, extras=(), dep=None):
    M, N = dy.shape
    stacked = w.ndim == 3
    K, n4 = w.shape[-2], w.shape[-1]
    bm, bn, bk = _pick_rows(M, BM), _pick(K, BN), _pick(n4, BK)
    nb = n4 // bk
    if stacked:
        b_blk, b_map = (None, bn, bk), (lambda i, j, k: (k // nb, j, k % nb))
    else:
        b_blk, b_map = (bn, bk), (lambda i, j, k: (j, k))
    outs = [((M, K), out_dtype, (bm, bn), lambda i, j: (i, j))]
    exs = [(e, (bm, bn), lambda i, j: (i, j)) for e in extras]
    return _matmul(name, dy, w, (bm, bk), lambda i, j, k: (i, k), b_blk, b_map,
                   (((1,), (1,)), ((), ())), (M // bm, K // bn, N // bk), (bm, bn), outs, exs, epilogue, dep)[0]


def mm_tn(name, a, dy, out_dtype, stacked, dep=None):
    M, K = a.shape
    N = dy.shape[1]
    n4 = N // N_CHIPS if stacked else N
    bm, bn, bk = _pick(K, BM), _pick(n4, BN), _pick_rows(M, BK)
    nb = n4 // bn
    if stacked:
        outs = [((N_CHIPS, K, n4), out_dtype, (None, bm, bn), lambda i, j: (j // nb, i, j % nb))]
    else:
        outs = [((K, N), out_dtype, (bm, bn), lambda i, j: (i, j))]
    return _matmul(name, a, dy, (bk, bm), lambda i, j, k: (k, i), (bk, bn), lambda i, j, k: (k, j),
                   (((0,), (0,)), ((), ())), (K // bm, N // bn, M // bk), (bm, bn), outs, dep=dep)[0]


def _split3(x):
    hi = x.astype(BF16)
    r = x - hi.astype(F32)
    mid = r.astype(BF16)
    lo = (r - mid.astype(F32)).astype(BF16)
    return hi, mid, lo


def _exact_dot(x, m, n):
    out = None
    for p in _split3(x)[:n]:
        d = jnp.dot(p, m, preferred_element_type=F32)
        out = d if out is None else out + d
    return out


def _row_spec(tr, width, cmap):
    return pl.BlockSpec((tr, width), lambda i, h: (i, cmap(h)))


def _full_spec(p):
    return pl.BlockSpec(p.shape, lambda i, h: (0,) * p.ndim)


def rowwise_fwd(name, fn, rows, params, outs, tr, nh=1):
    S = rows[0][0].shape[0]
    nr, npar = len(rows), len(params)

    def body(*refs):
        vals = fn(*[r[...].astype(F32) for r in refs[:nr]], *[p[...] for p in refs[nr:nr + npar]])
        for o, v in zip(refs[nr + npar:], vals):
            o[...] = v.astype(o.dtype)

    return pl.pallas_call(
        body, name=name, grid=(S // tr, nh),
        in_specs=[_row_spec(tr, w, cm) for (_, w, cm) in rows] + [_full_spec(p) for p in params],
        out_specs=[_row_spec(tr, w, cm) for (_, w, cm, _) in outs],
        out_shape=[jax.ShapeDtypeStruct((S, c), d) for (c, _, _, d) in outs],
        compiler_params=_cp(("parallel", "arbitrary")),
    )(*[r[0] for r in rows], *params)


def rowwise_bwd(name, fn, rows, params, cts, grads, tr, nh=1, n_diff=None, add=None):
    S = rows[0][0].shape[0]
    nr, npar, nct = len(rows), len(params), len(cts)
    n_diff = nr if n_diff is None else n_diff
    n_add = 1 if add is not None else 0

    def body(*refs):
        row_refs = refs[:nr]
        par_refs = refs[nr:nr + npar]
        ct_refs = refs[nr + npar:nr + npar + nct]
        add_refs = refs[nr + npar + nct:nr + npar + nct + n_add]
        o = nr + npar + nct + n_add
        g_refs = refs[o:o + n_diff]
        cp_refs = refs[o + n_diff:o + n_diff + n_add]
        pg_refs = refs[o + n_diff + n_add:]
        i, h = pl.program_id(0), pl.program_id(1)
        rv = [r[...].astype(F32) for r in row_refs]
        pv = [p[...] for p in par_refs]
        aux = rv[n_diff:]

        def f(*dp):
            return fn(*dp[:n_diff], *aux, *dp[n_diff:])

        _, vjp = jax.vjp(f, *rv[:n_diff], *pv)
        gs = vjp(tuple(c[...].astype(F32) for c in ct_refs))
        for n, (g_ref, (_, _, _, _, over)) in enumerate(zip(g_refs, grads)):
            g = gs[n]
            if n == 0 and add is not None:
                g = g + add_refs[0][...]
                cp_refs[0][...] = g.astype(BF16)
            if over:
                @pl.when(h == 0)
                def _(g_ref=g_ref):
                    g_ref[...] = jnp.zeros_like(g_ref)
                g_ref[...] += g.astype(g_ref.dtype)
            else:
                g_ref[...] = g.astype(g_ref.dtype)
        for pg_ref, g in zip(pg_refs, gs[n_diff:]):
            @pl.when((i == 0) & (h == 0))
            def _(pg_ref=pg_ref):
                pg_ref[...] = jnp.zeros_like(pg_ref)
            pg_ref[...] += g

    in_specs = [_row_spec(tr, w, cm) for (_, w, cm) in rows] + [_full_spec(p) for p in params]
    in_specs += [_row_spec(tr, w, cm) for (_, w, cm) in cts]
    operands = [r[0] for r in rows] + list(params) + [c[0] for c in cts]
    out_specs = [_row_spec(tr, w, cm) for (_, w, cm, _, _) in grads]
    out_shape = [jax.ShapeDtypeStruct((S, c), d) for (c, _, _, d, _) in grads]
    if add is not None:
        in_specs.append(_row_spec(tr, add[1], add[2]))
        operands.append(add[0])
        out_specs.append(_row_spec(tr, add[1], add[2]))
        out_shape.append(jax.ShapeDtypeStruct(add[0].shape, BF16))
    out_specs += [_full_spec(p) for p in params]
    out_shape += [jax.ShapeDtypeStruct(p.shape, F32) for p in params]
    return pl.pallas_call(
        body, name=name, grid=(S // tr, nh), in_specs=in_specs, out_specs=out_specs, out_shape=out_shape,
        compiler_params=_cp(("arbitrary", "arbitrary")),
    )(*operands)


def _c0(h):
    return 0


def _rms(x, g, n):
    return x * lax.rsqrt(jnp.sum(x * x, axis=-1, keepdims=True) * (1.0 / n) + EPS) * g


def _rmsnorm_fn(d):
    def fn(x, g):
        return (_rms(x, g, d),)
    return fn


def rmsnorm_fwd(name, x, g):
    S, D = x.shape
    return rowwise_fwd(name, _rmsnorm_fn(D), [(x, D, _c0)], [g], [(D, D, _c0, BF16)], _pick_rows(S, 256))[0]


def rmsnorm_bwd(name, x, g, dh, dres):
    S, D = x.shape
    return rowwise_bwd(name, _rmsnorm_fn(D), [(x, D, _c0)], [g], [(dh, D, _c0)], [(D, D, _c0, F32, False)],
                       _pick_rows(S, 256), add=(dres, D, _c0))


def loss_head(y, t):
    S, D = y.shape
    tr = _pick_rows(S, 256)

    def body(y_ref, t_ref, l_ref, d_ref, db_ref):
        @pl.when(pl.program_id(0) == 0)
        def _():
            l_ref[...] = jnp.zeros_like(l_ref)
        e = y_ref[...] - t_ref[...]
        l_ref[...] += 0.5 * jnp.sum(jnp.sum(e * e, axis=1, keepdims=True), axis=0, keepdims=True) * (1.0 / D)
        d = e * (1.0 / D)
        d_ref[...] = d
        db_ref[...] = d.astype(BF16)

    row = pl.BlockSpec((tr, D), lambda i: (i, 0))
    return pl.pallas_call(
        body, name="loss_head", grid=(S // tr,), in_specs=[row, row],
        out_specs=[pl.BlockSpec((1, 1), lambda i: (0, 0)), row, row],
        out_shape=[jax.ShapeDtypeStruct((1, 1), F32), jax.ShapeDtypeStruct((S, D), F32), jax.ShapeDtypeStruct((S, D), BF16)],
        compiler_params=_cp(("arbitrary",)),
    )(y, t)


def _dot_nt(a, b):
    return lax.dot_general(a, b, (((1,), (1,)), ((), ())), preferred_element_type=F32)


def _dot_tn(a, b):
    return lax.dot_general(a, b, (((0,), (0,)), ((), ())), preferred_element_type=F32)


def attn_fwd(name, q, k, v, nh, dq, qoff, koff, voff, vstep, scale, cf_col=None, cf_row=None):
    S = q.shape[0]
    tq = _pick_rows(S, 256)
    tk = tq
    nq = S // tq
    bias = cf_col is not None

    def body(*refs):
        if bias:
            q_ref, k_ref, v_ref, cfc_ref, cfr_ref, o_ref, of_ref, lse_ref = refs
        else:
            q_ref, k_ref, v_ref, o_ref, of_ref, lse_ref = refs
        i = pl.program_id(1)
        qv = q_ref[...].astype(BF16)
        row = i * tq + lax.broadcasted_iota(jnp.int32, (tq, tk), 0)
        coli = lax.broadcasted_iota(jnp.int32, (tq, tk), 1)

        def step(j, carry):
            m, l, acc, acc_lo = carry
            off = pl.multiple_of(j * tk, tk)
            ks = k_ref[pl.ds(off, tk), :].astype(BF16)
            vs = v_ref[pl.ds(off, tk), :].astype(BF16)
            s = _dot_nt(qv, ks) * scale
            if bias:
                s = s + cfc_ref[0] - cfr_ref[0, pl.ds(j, 1), :]
            s = jnp.where(j * tk + coli <= row, s, NEG)
            m_new = jnp.maximum(m, jnp.max(s, axis=1, keepdims=True))
            alpha = jnp.exp(m - m_new)
            p = jnp.exp(s - m_new)
            pb = p.astype(BF16)
            l = alpha * l + jnp.sum(p, axis=1, keepdims=True)
            acc = alpha * acc + jnp.dot(pb, vs, preferred_element_type=F32)
            acc_lo = alpha * acc_lo + jnp.dot((p - pb.astype(F32)).astype(BF16), vs, preferred_element_type=F32)
            return m_new, l, acc, acc_lo

        z = jnp.zeros((tq, HEAD_DIM), F32)
        m, l, acc, acc_lo = lax.fori_loop(0, i + 1, step, (jnp.full((tq, 1), NEG, F32), jnp.zeros((tq, 1), F32), z, z))
        o_ref[...] = (acc / l).astype(o_ref.dtype)
        of_ref[...] = (acc + acc_lo) / l
        lse_ref[0] = m + jnp.log(l)

    in_specs = [pl.BlockSpec((tq, dq), lambda h, i: (i, qoff + h)),
                pl.BlockSpec((S, dq), lambda h, i: (0, koff + h)),
                pl.BlockSpec((S, HEAD_DIM), lambda h, i: (0, voff + vstep * h))]
    operands = [q, k, v]
    if bias:
        in_specs += [pl.BlockSpec((1, tq, 1), lambda h, i: (h, i, 0)), pl.BlockSpec((1, S // tk, tk), lambda h, i: (h, 0, 0))]
        operands += [cf_col, cf_row]
    return pl.pallas_call(
        body, name=name, grid=(nh, nq), in_specs=in_specs,
        out_specs=[pl.BlockSpec((tq, HEAD_DIM), lambda h, i: (i, h)), pl.BlockSpec((tq, HEAD_DIM), lambda h, i: (i, h)),
                   pl.BlockSpec((1, tq, 1), lambda h, i: (h, i, 0))],
        out_shape=[jax.ShapeDtypeStruct((S, nh * HEAD_DIM), BF16), jax.ShapeDtypeStruct((S, nh * HEAD_DIM), F32),
                   jax.ShapeDtypeStruct((nh, S, 1), F32)],
        compiler_params=_cp(("parallel", "arbitrary")),
    )(*operands)


def attn_bwd(name, q, k, v, o, do, lse, nh, dq, qoff, koff, voff, vstep, scale, cf_col=None, cf_row=None):
    S = q.shape[0]
    tq = _pick_rows(S, 256)
    tk = tq
    nq = S // tq
    bias = cf_col is not None

    def body(*refs):
        if bias:
            q_ref, k_ref, v_ref, o_ref, do_ref, lse_ref, cfc_ref, cfr_ref, dq_ref, dk_ref, dv_ref, dcf_ref = refs
        else:
            q_ref, k_ref, v_ref, o_ref, do_ref, lse_ref, dq_ref, dk_ref, dv_ref = refs
        dk_ref[...] = jnp.zeros_like(dk_ref)
        dv_ref[...] = jnp.zeros_like(dv_ref)
        if bias:
            dcf_ref[...] = jnp.zeros_like(dcf_ref)
        rowi = lax.broadcasted_iota(jnp.int32, (tq, tk), 0)
        coli = lax.broadcasted_iota(jnp.int32, (tq, tk), 1)

        def outer(i, _):
            roff = pl.multiple_of(i * tq, tq)
            qv = q_ref[pl.ds(roff, tq), :].astype(BF16)
            dov = do_ref[pl.ds(roff, tq), :]
            delta = jnp.sum(dov.astype(F32) * o_ref[pl.ds(roff, tq), :].astype(F32), axis=1, keepdims=True)
            lse = lse_ref[0, pl.ds(roff, tq), :]
            if bias:
                cq = cfc_ref[0, pl.ds(roff, tq), :]

            def inner(j, dq_acc):
                off = pl.multiple_of(j * tk, tk)
                ks = k_ref[pl.ds(off, tk), :].astype(BF16)
                vs = v_ref[pl.ds(off, tk), :].astype(BF16)
                s = _dot_nt(qv, ks) * scale
                if bias:
                    s = s + cq - cfr_ref[0, pl.ds(j, 1), :]
                p = jnp.where(j * tk + coli <= i * tq + rowi, jnp.exp(s - lse), 0.0)
                dp = _dot_nt(dov, vs)
                ds = p * (dp - delta)
                if bias:
                    dcf_ref[0, pl.ds(j, 1), :] -= jnp.sum(ds, axis=0, keepdims=True)
                dsb = (ds * scale).astype(BF16)
                dv_ref[pl.ds(off, tk), :] += _dot_tn(p.astype(BF16), dov)
                dk_ref[pl.ds(off, tk), :] += _dot_tn(dsb, qv)
                return dq_acc + jnp.dot(dsb, ks, preferred_element_type=F32)

            dq_ref[pl.ds(roff, tq), :] = lax.fori_loop(0, i + 1, inner, jnp.zeros((tq, dq), F32))
            return 0

        lax.fori_loop(0, nq, outer, 0)

    in_specs = [pl.BlockSpec((S, dq), lambda h: (0, qoff + h)),
                pl.BlockSpec((S, dq), lambda h: (0, koff + h)),
                pl.BlockSpec((S, HEAD_DIM), lambda h: (0, voff + vstep * h)),
                pl.BlockSpec((S, HEAD_DIM), lambda h: (0, h)),
                pl.BlockSpec((S, HEAD_DIM), lambda h: (0, h)),
                pl.BlockSpec((1, S, 1), lambda h: (h, 0, 0))]
    operands = [q, k, v, o, do, lse]
    out_specs = [pl.BlockSpec((S, dq), lambda h: (0, h)), pl.BlockSpec((S, dq), lambda h: (0, h)),
                 pl.BlockSpec((S, HEAD_DIM), lambda h: (0, h))]
    out_shape = [jax.ShapeDtypeStruct((S, nh * dq), F32), jax.ShapeDtypeStruct((S, nh * dq), F32),
                 jax.ShapeDtypeStruct((S, nh * HEAD_DIM), F32)]
    if bias:
        in_specs += [pl.BlockSpec((1, S, 1), lambda h: (h, 0, 0)), pl.BlockSpec((1, S // tk, tk), lambda h: (h, 0, 0))]
        operands += [cf_col, cf_row]
        out_specs.append(pl.BlockSpec((1, S // tk, tk), lambda h: (h, 0, 0)))
        out_shape.append(jax.ShapeDtypeStruct((nh, S // tk, tk), F32))
    return pl.pallas_call(
        body, name=name, grid=(nh,), in_specs=in_specs, out_specs=out_specs, out_shape=out_shape,
        compiler_params=_cp(("parallel",)),
    )(*operands)


SB_ROW_SPLIT = 1


def _sb_block(qv, kb, row, col, scale, later_c, tri_after):
    z = _dot_nt(qv, kb) * scale
    strict = col < row
    lsn = jnp.minimum(-z, 0.0) - jnp.log(1.0 + jnp.exp(-jnp.abs(z)))
    lsp = lsn + z
    L = jnp.where(strict, lsn, 0.0)
    later = _exact_dot(L, tri_after, 2) + later_c
    a = jnp.where(strict, jnp.exp(lsp + later), 0.0)
    return strict, lsp, lsn, L, a


def _tri(n, after_strict):
    r = lax.broadcasted_iota(jnp.int32, (n, n), 0)
    c = lax.broadcasted_iota(jnp.int32, (n, n), 1)
    return jnp.where(r > c if after_strict else r >= c, 1.0, 0.0).astype(BF16)


def sb_fwd(name, qkv, nh):
    S = qkv.shape[0]
    tq = _pick_rows(S, 256)
    tk = tq
    nq = S // tq
    scale = 1.0 / math.sqrt(HEAD_DIM)

    ns = SB_ROW_SPLIT if tq % (8 * SB_ROW_SPLIT) == 0 else 1
    ts = tq // ns

    def body(q_ref, k_ref, v_ref, o_ref, of_ref):
        i = pl.program_id(1)
        tri = _tri(tk, True)
        coli = lax.broadcasted_iota(jnp.int32, (ts, tk), 1)
        qs = [q_ref[s * ts:(s + 1) * ts, :] for s in range(ns)]
        rows = [i * tq + s * ts + lax.broadcasted_iota(jnp.int32, (ts, tk), 0) for s in range(ns)]

        def step(jj, carry):
            j = i - jj
            off = pl.multiple_of(j * tk, tk)
            kb = k_ref[pl.ds(off, tk), :]
            vb = v_ref[pl.ds(off, tk), :]
            out = []
            for s in range(ns):
                later_c, acc, acc_lo = carry[s]
                _, _, _, L, a = _sb_block(qs[s], kb, rows[s], j * tk + coli, scale, later_c, tri)
                ab = a.astype(BF16)
                acc = acc + jnp.dot(ab, vb, preferred_element_type=F32)
                acc_lo = acc_lo + jnp.dot((a - ab.astype(F32)).astype(BF16), vb, preferred_element_type=F32)
                out.append((later_c + jnp.sum(L, axis=1, keepdims=True), acc, acc_lo))
            return tuple(out)

        z = jnp.zeros((ts, HEAD_DIM), F32)
        res = lax.fori_loop(0, i + 1, step, tuple((jnp.zeros((ts, 1), F32), z, z) for _ in range(ns)))
        for s, (_, acc, acc_lo) in enumerate(res):
            o_ref[s * ts:(s + 1) * ts, :] = acc.astype(o_ref.dtype)
            of_ref[s * ts:(s + 1) * ts, :] = acc + acc_lo

    blk = pl.BlockSpec((tq, HEAD_DIM), lambda h, i: (i, h))
    return pl.pallas_call(
        body, name=name, grid=(nh, nq),
        in_specs=[blk,
                  pl.BlockSpec((S, HEAD_DIM), lambda h, i: (0, nh + h)),
                  pl.BlockSpec((S, HEAD_DIM), lambda h, i: (0, 2 * nh + h))],
        out_specs=[blk, blk],
        out_shape=[jax.ShapeDtypeStruct((S, nh * HEAD_DIM), BF16), jax.ShapeDtypeStruct((S, nh * HEAD_DIM), F32)],
        compiler_params=_cp(("parallel", "arbitrary")),
    )(qkv, qkv, qkv)


def sb_bwd(name, qkv, o, do, nh):
    S = qkv.shape[0]
    tq = _pick_rows(S, 256)
    tk = tq
    nq = S // tq
    scale = 1.0 / math.sqrt(HEAD_DIM)

    ns = SB_ROW_SPLIT if tq % (8 * SB_ROW_SPLIT) == 0 else 1
    ts = tq // ns

    def body(q_ref, k_ref, v_ref, o_ref, do_ref, dq_ref, dk_ref, dv_ref, dk_acc, dv_acc):
        dk_acc[...] = jnp.zeros_like(dk_acc)
        dv_acc[...] = jnp.zeros_like(dv_acc)
        tri = _tri(tk, True)
        tri_inc = _tri(tk, False)
        rowi = lax.broadcasted_iota(jnp.int32, (ts, tk), 0)
        coli = lax.broadcasted_iota(jnp.int32, (ts, tk), 1)

        def outer(i, _):
            qs, dos, dtots, rows = [], [], [], []
            for s in range(ns):
                roff = pl.multiple_of(i * tq + s * ts, ts)
                qs.append(q_ref[pl.ds(roff, ts), :])
                dos.append(do_ref[pl.ds(roff, ts), :])
                dtots.append(jnp.sum(dos[s].astype(F32) * o_ref[pl.ds(roff, ts), :].astype(F32), axis=1, keepdims=True))
                rows.append(i * tq + s * ts + rowi)

            def inner(jj, carry):
                j = i - jj
                off = pl.multiple_of(j * tk, tk)
                kb = k_ref[pl.ds(off, tk), :]
                vb = v_ref[pl.ds(off, tk), :]
                out, dv_add, dk_add = [], None, None
                for s in range(ns):
                    later_c, suf_c, dq_acc = carry[s]
                    strict, lsp, lsn, L, a = _sb_block(qs[s], kb, rows[s], j * tk + coli, scale, later_c, tri)
                    dl = _dot_nt(dos[s], vb) * a
                    before = dtots[s] - (suf_c + _exact_dot(dl, tri_inc, 3))
                    dz = jnp.where(strict, dl * jnp.exp(lsn) - jnp.exp(lsp) * before, 0.0) * scale
                    dzb = dz.astype(BF16)
                    dv_s = _dot_tn(a.astype(BF16), dos[s])
                    dk_s = _dot_tn(dzb, qs[s])
                    dv_add = dv_s if dv_add is None else dv_add + dv_s
                    dk_add = dk_s if dk_add is None else dk_add + dk_s
                    out.append((later_c + jnp.sum(L, axis=1, keepdims=True), suf_c + jnp.sum(dl, axis=1, keepdims=True),
                                dq_acc + jnp.dot(dzb, kb, preferred_element_type=F32)))
                dv_acc[pl.ds(off, tk), :] += dv_add
                dk_acc[pl.ds(off, tk), :] += dk_add
                return tuple(out)

            z1 = jnp.zeros((ts, 1), F32)
            res = lax.fori_loop(0, i + 1, inner, tuple((z1, z1, jnp.zeros((ts, HEAD_DIM), F32)) for _ in range(ns)))
            for s, (_, _, dq_acc) in enumerate(res):
                dq_ref[pl.ds(pl.multiple_of(i * tq + s * ts, ts), ts), :] = dq_acc.astype(dq_ref.dtype)
            return 0

        lax.fori_loop(0, nq, outer, 0)
        dk_ref[...] = dk_acc[...].astype(dk_ref.dtype)
        dv_ref[...] = dv_acc[...].astype(dv_ref.dtype)

    def col(off):
        return pl.BlockSpec((S, HEAD_DIM), lambda h: (0, off + h))

    dq, dk, dv = pl.pallas_call(
        body, name=name, grid=(nh,),
        in_specs=[col(0), col(nh), col(2 * nh), col(0), col(0)],
        out_specs=[col(0), col(0), col(0)],
        out_shape=[jax.ShapeDtypeStruct((S, nh * HEAD_DIM), BF16)] * 3,
        scratch_shapes=[pltpu.VMEM((S, HEAD_DIM), F32), pltpu.VMEM((S, HEAD_DIM), F32)],
        compiler_params=_cp(("parallel",)),
    )(qkv, qkv, qkv, o, do)
    return jnp.concatenate([dq, dk, dv], axis=1)


def seq_cumsum(name, x, reverse):
    S, W = x.shape

    tb = _pick_rows(S, 256)

    def body(x_ref, o_ref):
        parts = _split3(x_ref[...])
        c = lax.broadcasted_iota(jnp.int32, (tb, S), 1)
        for b in range(S // tb):
            r = b * tb + lax.broadcasted_iota(jnp.int32, (tb, S), 0)
            t = jnp.where(r <= c if reverse else r >= c, 1.0, 0.0).astype(BF16)
            out = None
            for p in parts:
                d = jnp.dot(t, p, preferred_element_type=F32)
                out = d if out is None else out + d
            o_ref[b * tb:(b + 1) * tb, :] = out

    return pl.pallas_call(body, name=name, out_shape=jax.ShapeDtypeStruct((S, W), F32),
                          compiler_params=pltpu.CompilerParams(vmem_limit_bytes=VMEM_LIMIT))(x)


def _adamw_math(w, g, m, v):
    c1 = 1.0 - ADAM_B1 ** ADAM_STEP
    c2 = 1.0 - ADAM_B2 ** ADAM_STEP
    nm = ADAM_B1 * m + (1.0 - ADAM_B1) * g
    nv = ADAM_B2 * v + (1.0 - ADAM_B2) * (g * g)
    return -ADAM_LR * ((nm / c1) / (jnp.sqrt(nv / c2) + ADAM_EPS) + ADAM_WD * w), nm, nv


def adamw(name, w, g, m, v):
    R, C = g.shape
    tr = _pick_rows(R, 256)

    def body(w_ref, g_ref, m_ref, v_ref, d_ref, nm_ref, nv_ref):
        d_ref[...], nm_ref[...], nv_ref[...] = _adamw_math(w_ref[...], g_ref[...], m_ref[...], v_ref[...])

    row = pl.BlockSpec((tr, C), lambda i: (i, 0))
    return pl.pallas_call(
        body, name=name, grid=(R // tr,), in_specs=[row] * 4, out_specs=[row] * 3,
        out_shape=[jax.ShapeDtypeStruct((R, C), F32)] * 3, compiler_params=_cp(("parallel",)),
    )(w, g, m, v)


def adamw_big(name, w, m, v, mine, other, c_arr, layer, prev):
    L, R, C = w.shape
    r2 = R // 2
    tr = _pick_rows(r2, 256)
    nb = r2 // tr

    def body(c_ref, w_ref, m_ref, v_ref, mine_ref, other_ref, *rest):
        g_ref, d_ref, nm_ref, nv_ref = rest[-4:]
        g = jnp.where(pl.program_id(0) == c_ref[0], mine_ref[...], other_ref[...])
        g_ref[...] = g
        d_ref[...], nm_ref[...], nv_ref[...] = _adamw_math(w_ref[...], g, m_ref[...], v_ref[...])

    sel = pl.BlockSpec((None, tr, C), lambda hf, i, c: (layer, hf * nb + i, 0))
    in_specs = [sel, sel, sel,
                pl.BlockSpec((tr, C), lambda hf, i, c: (jnp.where(hf == c[0], i, 0), 0)),
                pl.BlockSpec((tr, C), lambda hf, i, c: (jnp.where(hf == c[0], 0, i), 0))]
    operands = [c_arr, w, m, v, mine, other]
    aliases = {}
    if prev is not None:
        in_specs += [ANY] * 4
        aliases = {len(operands) + k: k for k in range(4)}
        operands += list(prev)
    return pl.pallas_call(
        body, name=name,
        grid_spec=pltpu.PrefetchScalarGridSpec(num_scalar_prefetch=1, grid=(2, nb), in_specs=in_specs, out_specs=[sel] * 4),
        out_shape=[jax.ShapeDtypeStruct((L, R, C), F32)] * 4, input_output_aliases=aliases,
        compiler_params=_cp(("arbitrary", "arbitrary")),
    )(*operands)


ANY = pl.BlockSpec(memory_space=pl.ANY)


def _place():
    x, y, c = lax.axis_index("x"), lax.axis_index("y"), lax.axis_index("c")
    others = [(1 - x, y), (x, 1 - y), (1 - x, 1 - y)]
    return x, y, c, others


def swap_halves(name, grads):
    n = len(grads)

    def body(*refs):
        src, dst = refs[:n], refs[n:2 * n]
        send, recv = refs[2 * n:]
        x, y, c, _ = _place()
        cps = []
        for t in range(n):
            r2 = grads[t].shape[1] // 2
            cps.append(pltpu.make_async_remote_copy(
                src_ref=src[t].at[:, pl.ds((1 - c) * r2, r2), :], dst_ref=dst[t],
                send_sem=send.at[t], recv_sem=recv.at[t], device_id=(x, y, 1 - c), device_id_type=MESH))
        for cp in cps:
            cp.start()
        for cp in cps:
            cp.wait()

    return pl.pallas_call(
        body, name=name, in_specs=[ANY] * n, out_specs=[ANY] * n,
        out_shape=[jax.ShapeDtypeStruct((N_CHIPS, g.shape[1] // 2, g.shape[2]), g.dtype) for g in grads],
        scratch_shapes=[pltpu.SemaphoreType.DMA((n,))] * 2,
    )(*grads)


def add_half(name, g, other, c_arr):
    _, R, C = g.shape
    r2 = R // 2
    tr = _pick_rows(r2, 512)
    nb = r2 // tr

    def body(c_ref, g_ref, o_ref, out_ref):
        out_ref[...] = (g_ref[...].astype(F32) + o_ref[...].astype(F32)).astype(out_ref.dtype)

    return pl.pallas_call(
        body, name=name,
        grid_spec=pltpu.PrefetchScalarGridSpec(
            num_scalar_prefetch=1, grid=(N_CHIPS, nb),
            in_specs=[pl.BlockSpec((None, tr, C), lambda s, i, c: (s, c[0] * nb + i, 0)),
                      pl.BlockSpec((None, tr, C), lambda s, i, c: (s, i, 0))],
            out_specs=pl.BlockSpec((None, tr, C), lambda s, i, c: (s, i, 0))),
        out_shape=jax.ShapeDtypeStruct((N_CHIPS, r2, C), BF16),
        compiler_params=_cp(("parallel", "parallel")),
    )(c_arr, g, other)


def sum_slabs(name, got, parts, chip_arr):
    _, r2, C = parts.shape
    tr = _pick_rows(r2, 512)

    def body(chip_ref, *refs):
        own_ref, out_ref = refs[N_CHIPS], refs[N_CHIPS + 1]
        acc = None
        for s in range(N_CHIPS):
            v = jnp.where(chip_ref[0] == s, own_ref[...], refs[s][...]).astype(F32)
            acc = v if acc is None else acc + v
        out_ref[...] = acc

    def slab(s):
        return pl.BlockSpec((None, tr, C), lambda i, ch: (jnp.where(ch[0] == s, (s + 1) % N_CHIPS, s), i, 0))

    return pl.pallas_call(
        body, name=name,
        grid_spec=pltpu.PrefetchScalarGridSpec(
            num_scalar_prefetch=1, grid=(r2 // tr,),
            in_specs=[slab(s) for s in range(N_CHIPS)] + [pl.BlockSpec((None, tr, C), lambda i, ch: (ch[0], i, 0))],
            out_specs=pl.BlockSpec((tr, C), lambda i, ch: (i, 0))),
        out_shape=jax.ShapeDtypeStruct((r2, C), F32), compiler_params=_cp(("parallel",)),
    )(chip_arr, got, got, got, got, parts)


def send_halves(name, halves):
    n = len(halves)

    def body(*refs):
        src, dst = refs[:n], refs[n:2 * n]
        send, recv = refs[2 * n:]
        x, y, c, _ = _place()
        cps = [pltpu.make_async_remote_copy(src_ref=src[t], dst_ref=dst[t], send_sem=send.at[t], recv_sem=recv.at[t],
                                            device_id=(x, y, 1 - c), device_id_type=MESH) for t in range(n)]
        for cp in cps:
            cp.start()
        for cp in cps:
            cp.wait()

    return pl.pallas_call(
        body, name=name, in_specs=[ANY] * n, out_specs=[ANY] * n,
        out_shape=[jax.ShapeDtypeStruct(h.shape, h.dtype) for h in halves],
        scratch_shapes=[pltpu.SemaphoreType.DMA((n,))] * 2,
    )(*halves)


HBM = pl.BlockSpec(memory_space=pltpu.HBM)
SEM = pl.BlockSpec(memory_space=pltpu.SEMAPHORE)
EFFECT = pltpu.SideEffectType.DATAFLOW_SIDE_EFFECTING


def _in_hbm(a):
    return pltpu.with_memory_space_constraint(a, pltpu.HBM)


def _chip_copies(kind, shapes, src, land, send, recv, mine):
    x, y, c, others = _place()
    me = 2 * x + y
    cps = []
    for t, shape in enumerate(shapes):
        for j, (px, py) in enumerate(others):
            slot = me if mine else 2 * px + py
            if kind == "gather":
                r2 = shape[0] // 2
                rows = pl.ds(c * r2, r2)
                s_ref, d_ref = src[t].at[rows, :], land[t].at[slot, rows, :]
            else:
                s_ref, d_ref = src[t].at[2 * px + py], land[t].at[slot]
            cps.append(pltpu.make_async_remote_copy(src_ref=s_ref, dst_ref=d_ref, send_sem=send.at[3 * t + j],
                                                    recv_sem=recv.at[3 * t + j], device_id=(px, py, c), device_id_type=MESH))
    return cps


def chips_start(name, kind, srcs, land_shapes, after=None):
    n = len(srcs)
    shapes = [s.shape for s in srcs]
    n_in = 2 * n + (0 if after is None else 1)

    def body(*refs):
        src, land = refs[:n], refs[n:2 * n]
        send, recv = refs[n_in], refs[n_in + 1]
        token = refs[-1]
        for cp in _chip_copies(kind, shapes, src, land, send, recv, True):
            cp.start()
        token[...] = jnp.zeros_like(token)

    lands = [lax.empty(s, srcs[0].dtype) for s in land_shapes]
    out = pl.pallas_call(
        body, name=name,
        out_shape=(pltpu.SemaphoreType.DMA((3 * n,)), pltpu.SemaphoreType.DMA((3 * n,)))
        + tuple(pltpu.HBM(s.shape, s.dtype) for s in srcs) + tuple(pltpu.HBM(l.shape, l.dtype) for l in lands)
        + (jax.ShapeDtypeStruct((8, LANE), F32),),
        in_specs=(HBM,) * (2 * n) + (ANY,) * (n_in - 2 * n),
        out_specs=(SEM, SEM) + (HBM,) * (2 * n) + (pl.BlockSpec(memory_space=pltpu.VMEM),),
        input_output_aliases={k: 2 + k for k in range(2 * n)},
        compiler_params=pltpu.CompilerParams(has_side_effects=EFFECT),
    )(*[_in_hbm(s) for s in srcs], *[_in_hbm(l) for l in lands], *([] if after is None else [after]))
    return (kind, shapes, out[0], out[1], out[2:2 + n], out[2 + n:2 + 2 * n]), out[-1][:1, :1]


def chips_wait(name, handle, after):
    kind, shapes, send, recv, srcs, lands = handle
    n = len(srcs)
    after = list(after) if isinstance(after, (list, tuple)) else [after]

    def body(*refs):
        src, land = refs[:n], refs[n:2 * n]
        for cp in _chip_copies(kind, shapes, src, land, refs[2 * n], refs[2 * n + 1], True):
            cp.wait_send()
        for cp in _chip_copies(kind, shapes, src, land, refs[2 * n], refs[2 * n + 1], False):
            cp.wait_recv()

    out = pl.pallas_call(
        body, name=name,
        out_shape=tuple(pltpu.HBM(s.shape, s.dtype) for s in srcs) + tuple(pltpu.HBM(l.shape, l.dtype) for l in lands),
        in_specs=(HBM,) * (2 * n) + (SEM, SEM) + (ANY,) * len(after), out_specs=(HBM,) * (2 * n),
        input_output_aliases={k: k for k in range(2 * n)},
        compiler_params=pltpu.CompilerParams(has_side_effects=EFFECT),
    )(*srcs, *lands, send, recv, *after)
    return list(out[n:]), list(out[:n])


def pass_to_sibling(name, lands, shards):
    n = len(lands)

    def body(*refs):
        buf = refs[n:2 * n]
        send, recv = refs[2 * n:]
        x, y, c, others = _place()

        def cp(t, j, core):
            r2 = lands[t].shape[1] // 2
            px, py = others[j]
            rows = buf[t].at[2 * px + py, pl.ds(core * r2, r2), :]
            return pltpu.make_async_remote_copy(src_ref=rows, dst_ref=rows, send_sem=send.at[t, j], recv_sem=recv.at[t, j],
                                                device_id=(x, y, 1 - c), device_id_type=MESH)

        for t in range(n):
            for j in range(3):
                cp(t, j, c).start()
        for t in range(n):
            for j in range(3):
                cp(t, j, 1 - c).wait_recv()
        for t in range(n):
            for j in range(3):
                cp(t, j, c).wait_send()

    got = pl.pallas_call(
        body, name=name, in_specs=[ANY] * n, out_specs=[ANY] * n,
        out_shape=[jax.ShapeDtypeStruct(l.shape, l.dtype) for l in lands],
        input_output_aliases={k: k for k in range(n)},
        scratch_shapes=[pltpu.SemaphoreType.DMA((n, 3))] * 2,
    )(*lands)
    chip = 2 * lax.axis_index("x") + lax.axis_index("y")
    return [lax.dynamic_update_slice(g, s[None], (chip, 0, 0)) for g, s in zip(got, shards)]


def all_sum_small(name, v):
    R = v.shape[0]

    def body(v_ref, out_ref, slots, send, recv):
        x, y, c, _ = _place()
        me = 4 * x + 2 * y + c
        slots[me] = v_ref[...]
        cps = []
        for k in range(1, 8):
            dx, dy, dc = (k >> 2) & 1, (k >> 1) & 1, k & 1
            to = (x ^ dx, y ^ dy, c ^ dc)
            cps.append(pltpu.make_async_remote_copy(
                src_ref=v_ref, dst_ref=slots.at[me], send_sem=send.at[k - 1], recv_sem=recv.at[k - 1],
                device_id=to, device_id_type=MESH))
        for cp in cps:
            cp.start()
        for k in range(1, 8):
            dx, dy, dc = (k >> 2) & 1, (k >> 1) & 1, k & 1
            frm = 4 * (x ^ dx) + 2 * (y ^ dy) + (c ^ dc)
            pltpu.make_async_remote_copy(
                src_ref=v_ref, dst_ref=slots.at[frm], send_sem=send.at[k - 1], recv_sem=recv.at[k - 1],
                device_id=(x, y, c), device_id_type=MESH).wait_recv()
        for cp in cps:
            cp.wait_send()
        acc = slots[0]
        for d in range(1, 8):
            acc = acc + slots[d]
        out_ref[...] = acc

    vm = pl.BlockSpec(memory_space=pltpu.VMEM)
    return pl.pallas_call(
        body, name=name, in_specs=[vm], out_specs=vm, out_shape=jax.ShapeDtypeStruct((R, LANE), F32),
        scratch_shapes=[pltpu.VMEM((8, R, LANE), F32), pltpu.SemaphoreType.DMA((7,)), pltpu.SemaphoreType.DMA((7,))],
    )(v)


def _rope_mat(n, lo):
    half = MLA_ROPE // 2
    r = lax.broadcasted_iota(jnp.int32, (n, n), 0)
    c = lax.broadcasted_iota(jnp.int32, (n, n), 1)
    plus = (c >= lo + half) & (c < lo + 2 * half) & (r == c - half)
    minus = (c >= lo) & (c < lo + half) & (r == c + half)
    return (jnp.where(plus, 1.0, 0.0) - jnp.where(minus, 1.0, 0.0)).astype(BF16)


def _rope(v, cos, sin, lo):
    return v * cos + _exact_dot(v, _rope_mat(v.shape[-1], lo), 3) * sin


def rope_tables(pos):
    S = pos.shape[0]
    half = MLA_ROPE // 2
    inv = (np.float32(ROPE_THETA) ** (-np.arange(0, half, dtype=np.float32) * np.float32(2.0 / MLA_ROPE))).astype(np.float32)
    f1 = np.zeros((1, LANE), np.float32)
    f1[0, :MLA_ROPE] = np.tile(inv, 2)
    f2 = np.zeros((1, MLA_QK_PAD), np.float32)
    f2[0, MLA_NOPE:MLA_QK] = np.tile(inv, 2)
    tr = _pick_rows(S, 256)

    def body(p_ref, f1_ref, f2_ref, c1, s1, c2, s2):
        p = p_ref[...].astype(F32)
        a1 = p * f1_ref[...]
        a2 = p * f2_ref[...]
        c1[...] = jnp.cos(a1)
        s1[...] = jnp.sin(a1)
        c2[...] = jnp.cos(a2)
        s2[...] = jnp.sin(a2)

    def row(w):
        return pl.BlockSpec((tr, w), lambda i: (i, 0))

    def full(w):
        return pl.BlockSpec((1, w), lambda i: (0, 0))

    return pl.pallas_call(
        body, name="rope_tables", grid=(S // tr,), in_specs=[row(1), full(LANE), full(MLA_QK_PAD)],
        out_specs=[row(LANE), row(LANE), row(MLA_QK_PAD), row(MLA_QK_PAD)],
        out_shape=[jax.ShapeDtypeStruct((S, LANE), F32)] * 2 + [jax.ShapeDtypeStruct((S, MLA_QK_PAD), F32)] * 2,
        compiler_params=_cp(("parallel",)),
    )(pos, jnp.asarray(f1), jnp.asarray(f2))


def _log_sigmoid(z):
    return jnp.minimum(z, 0.0) - jnp.log(1.0 + jnp.exp(-jnp.abs(z)))


def _fox_qk_fn(q, k, gq, gk):
    return _rms(q, gq, HEAD_DIM), _rms(k, gk, HEAD_DIM)


def _fox_gate_fn(f, b):
    return (_log_sigmoid(f + b),)


def _mla_pre1_fn(q_rank, kv_rank):
    def fn(cq, ckv, kr, cos, sin, qn, kvn):
        return _rms(cq, qn, q_rank), _rms(ckv, kvn, kv_rank), _rope(kr, cos, sin, 0)
    return fn


def _mla_pre2_fn(qb, kn, kr, cos, sin, gq, gk):
    qh = _rms(_rope(qb, cos, sin, MLA_NOPE), gq, MLA_QK)
    kh = _rms(jnp.concatenate([kn, kr], axis=1), gk, MLA_QK)
    return qh, kh


def _pad_cols(a, n):
    return jnp.pad(a, ((0, 0), (0, n - a.shape[1])))


def _relu2(acc):
    r = jnp.maximum(acc, 0.0)
    return r * r, r


def _add(acc, res):
    return (acc + res,)


def _times_2r(acc, r):
    return (acc * (2.0 * r.astype(F32)),)


def kernel(x, positions, mix_norm, mlp_norm, sb_w_in, sb_w_out, fox_w_in, fox_b_f, fox_q_gain, fox_k_gain, fox_w_out, mla_w_in, mla_q_norm, mla_kv_norm, mla_w_uq, mla_w_ukv, mla_q_gain, mla_k_gain, mla_w_out, mlp_w1, mlp_w2, loss_target, m_mix_norm, m_mlp_norm, m_sb_w_in, m_sb_w_out, m_fox_w_in, m_fox_b_f, m_fox_q_gain, m_fox_k_gain, m_fox_w_out, m_mla_w_in, m_mla_q_norm, m_mla_kv_norm, m_mla_w_uq, m_mla_w_ukv, m_mla_q_gain, m_mla_k_gain, m_mla_w_out, m_mlp_w1, m_mlp_w2, v_mix_norm, v_mlp_norm, v_sb_w_in, v_sb_w_out, v_fox_w_in, v_fox_b_f, v_fox_q_gain, v_fox_k_gain, v_fox_w_out, v_mla_w_in, v_mla_q_norm, v_mla_kv_norm, v_mla_w_uq, v_mla_w_ukv, v_mla_q_gain, v_mla_k_gain, v_mla_w_out, v_mlp_w1, v_mlp_w2):
    S, D = x.shape[1], x.shape[2]
    nh = D // HEAD_DIM
    W = nh * HEAD_DIM
    depth = mix_norm.shape[0]
    q_rank, kv_rank = mla_w_uq.shape[1], mla_w_ukv.shape[1]
    n_fox_in = 3 * W + nh
    fox_pad = -(-n_fox_in // LANE) * LANE
    n_down = q_rank + kv_rank + MLA_ROPE
    down_pad = q_rank + kv_rank + LANE
    tr = _pick_rows(S, 256)
    tk = _pick_rows(S, 256)

    ax, ay, ac = lax.axis_index("x"), lax.axis_index("y"), lax.axis_index("c")
    chip = 2 * ax + ay
    c_arr = jnp.reshape(ac, (1,)).astype(jnp.int32)
    chip_arr = jnp.reshape(chip, (1,)).astype(jnp.int32)

    def bf(a):
        return a.astype(BF16)

    n_small_in = q_rank + kv_rank
    rows_in = -(-n_small_in // (8 * LANE)) * 8
    placed = jnp.zeros((rows_in * LANE,), F32)
    placed = lax.dynamic_update_slice(placed, mla_q_norm[0], (chip * mla_q_norm.shape[1],))
    placed = lax.dynamic_update_slice(placed, mla_kv_norm[0], (q_rank + chip * mla_kv_norm.shape[1],))
    placed = placed * (ac == 0).astype(F32)
    norms = all_sum_small("gather_norms", placed.reshape(rows_in, LANE)).reshape(-1)
    q_norm_full = norms[:q_rank].reshape(1, q_rank)
    kv_norm_full = norms[q_rank:q_rank + kv_rank].reshape(1, kv_rank)

    def gather_start(i, part, after):
        kind, j = i % N_MIXERS, i // N_MIXERS
        if part == "mlp":
            names, shards = ["w1", "w2"], [bf(mlp_w1[i]), bf(mlp_w2[i])]
        elif kind == 0:
            names, shards = ["w_in", "w_out"], [bf(sb_w_in[j]), bf(sb_w_out[j])]
        elif kind == 1:
            names, shards = ["w_in", "w_out"], [bf(fox_w_in[j]), bf(fox_w_out[j])]
        else:
            names = ["w_in", "w_uq", "w_ukv", "w_out"]
            shards = [bf(mla_w_in[j]), bf(mla_w_uq[j]), bf(mla_w_ukv[j]), bf(mla_w_out[j])]
        handle, token = chips_start(f"gather_start_{part}_{i}", "gather", shards, [(N_CHIPS,) + s.shape for s in shards], after)
        return f"{part}_{i}", names, handle, token

    def gather_finish(pend, after):
        tag, names, handle, _ = pend
        lands, shards = chips_wait(f"gather_wait_{tag}", handle, after)
        return dict(zip(names, pass_to_sibling(f"gather_pass_{tag}", lands, shards)))

    def rows_stacked(w):
        return w.reshape(w.shape[0] * w.shape[1], w.shape[2])

    xc = x[0]
    saved = []
    layers = []
    tables = None
    pending = gather_start(0, "mix", norms)
    for i in range(depth):
        kind, j = i % N_MIXERS, i // N_MIXERS
        L = gather_finish(pending, xc)
        pending = gather_start(i, "mlp", L["w_out"])
        g1 = mix_norm[i:i + 1] + pending[3]
        hb = rmsnorm_fwd(f"norm1_{i}", xc, g1)
        st = dict(x=xc, hb=hb)
        if kind == 0:
            qkv = mm_nn(f"sb_proj_{i}", hb, L["w_in"], BF16)[0]
            o, o_f32 = sb_fwd(f"sb_attn_{i}", qkv, nh)
            st.update(qkv=qkv, o_f32=o_f32)
        elif kind == 1:
            w_in = _pad_cols(jnp.concatenate([L["w_in"][s] for s in range(N_CHIPS)], axis=1), fox_pad)
            proj = mm_nn(f"fox_proj_{i}", hb, w_in, F32)[0]
            gq, gk = fox_q_gain[j:j + 1], fox_k_gain[j:j + 1]
            qk_rows = [(proj, HEAD_DIM, lambda h: h), (proj, HEAD_DIM, lambda h: nh + h)]
            qh, kh = rowwise_fwd(f"fox_qk_{i}", _fox_qk_fn, qk_rows, [gq, gk],
                                 [(W, HEAD_DIM, lambda h: h, BF16)] * 2, tr, nh)
            b_pad = _pad_cols(fox_b_f[j:j + 1], LANE)
            gate_rows = [(proj, LANE, lambda h: 3 * nh)]
            logf = rowwise_fwd(f"fox_gate_{i}", _fox_gate_fn, gate_rows, [b_pad], [(LANE, LANE, _c0, F32)], tr)[0]
            cf = seq_cumsum(f"fox_cf_{i}", logf, False)[:, :nh].T
            cf_col, cf_row = cf.reshape(nh, S, 1), cf.reshape(nh, S // tk, tk)
            scale = 1.0 / math.sqrt(HEAD_DIM)
            o, o_f32, lse = attn_fwd(f"fox_attn_{i}", qh, kh, proj, nh, HEAD_DIM, 0, 0, 2 * nh, 1, scale, cf_col, cf_row)
            st.update(w_in=w_in, proj=proj, qk_rows=qk_rows, gq=gq, gk=gk, qh=qh, kh=kh, b_pad=b_pad, gate_rows=gate_rows,
                      cf_col=cf_col, cf_row=cf_row, lse=lse, scale=scale, o_f32=o_f32)
        else:
            w_in = _pad_cols(rows_stacked(L["w_in"]), down_pad)
            down = mm_nn(f"mla_down_{i}", hb, w_in, F32)[0]
            if tables is None:
                tables = rope_tables(positions.reshape(S, 1))
            cos1, sin1, cos2, sin2 = tables
            pre1_rows = [(down[:, :q_rank], q_rank, _c0), (down[:, q_rank:q_rank + kv_rank], kv_rank, _c0),
                         (down[:, q_rank + kv_rank:], LANE, _c0), (cos1, LANE, _c0), (sin1, LANE, _c0)]
            pre1_fn = _mla_pre1_fn(q_rank, kv_rank)
            c_q, c_kv, k_rope = rowwise_fwd(
                f"mla_pre1_{i}", pre1_fn, pre1_rows, [q_norm_full, kv_norm_full],
                [(q_rank, q_rank, _c0, BF16), (kv_rank, kv_rank, _c0, BF16), (LANE, LANE, _c0, F32)], tr)
            qfull = mm_nn(f"mla_uq_{i}", c_q, L["w_uq"], F32)[0]
            kv = mm_nn(f"mla_ukv_{i}", c_kv, L["w_ukv"], F32)[0]
            qpad = jnp.pad(qfull.reshape(S, nh, MLA_QK), ((0, 0), (0, 0), (0, MLA_QK_PAD - MLA_QK))).reshape(S, nh * MLA_QK_PAD)
            gq, gk = _pad_cols(mla_q_gain[j:j + 1], MLA_QK_PAD), _pad_cols(mla_k_gain[j:j + 1], MLA_QK_PAD)
            pre2_rows = [(qpad, MLA_QK_PAD, lambda h: h), (kv, MLA_NOPE, lambda h: 2 * h), (k_rope, LANE, _c0),
                         (cos2, MLA_QK_PAD, _c0), (sin2, MLA_QK_PAD, _c0)]
            qh, kh = rowwise_fwd(f"mla_pre2_{i}", _mla_pre2_fn, pre2_rows, [gq, gk],
                                 [(nh * MLA_QK_PAD, MLA_QK_PAD, lambda h: h, BF16)] * 2, tr, nh)
            scale = 1.0 / math.sqrt(MLA_QK)
            o, o_f32, lse = attn_fwd(f"mla_attn_{i}", qh, kh, kv, nh, MLA_QK_PAD, 0, 0, 1, 2, scale)
            st.update(w_in=w_in, pre1_rows=pre1_rows, pre1_fn=pre1_fn, c_q=c_q, c_kv=c_kv, pre2_rows=pre2_rows, gq=gq, gk=gk,
                      qh=qh, kh=kh, kv=kv, lse=lse, scale=scale, o_f32=o_f32)
        x1 = mm_nn(f"mix_out_{i}", o, rows_stacked(L["w_out"]), F32, epilogue=_add, extras=(xc,))[0]
        L.update(gather_finish(pending, x1))
        layers.append(L)
        g2 = mlp_norm[i:i + 1]
        if i + 1 < depth:
            pending = gather_start(i + 1, "mix", L["w2"])
            g2 = g2 + pending[3]
        h2 = rmsnorm_fwd(f"norm2_{i}", x1, g2)
        a, r = mm_nn(f"mlp_up_{i}", h2, L["w1"], None, epilogue=_relu2, out_dtypes=[BF16, BF16])
        xc = mm_nn(f"mlp_down_{i}", a, rows_stacked(L["w2"]), F32, epilogue=_add, extras=(x1,))[0]
        st.update(o=o, x1=x1, h2=h2, a=a, r=r, g1=g1, g2=g2)
        saved.append(st)

    loss_local, dx, dxb = loss_head(xc, loss_target[0])
    loss = lax.psum(loss_local[0, 0], ("x", "y", "c"))

    small = {}

    def stack_rows(g):
        return g.reshape(N_CHIPS, g.shape[0] // N_CHIPS, g.shape[1])

    full = [dict() for _ in range(depth)]
    flying = None

    def exchange_start(i, part, gr):
        tag = f"{part}_{i}"
        names = list(gr)
        grads = [gr[n] for n in names]
        theirs = swap_halves(f"swap_{tag}", grads)
        parts = [add_half(f"addh_{tag}_{t}", g, o, c_arr) for t, (g, o) in enumerate(zip(grads, theirs))]
        handle, token = chips_start(f"xchg_start_{tag}", "scatter", parts, [p.shape for p in parts])
        return i, tag, names, handle, token

    def exchange_finish(fly, after):
        i, tag, names, handle, _ = fly
        got, parts = chips_wait(f"xchg_wait_{tag}", handle, after)
        halves = [sum_slabs(f"sum4_{tag}_{t}", g, p, chip_arr) for t, (g, p) in enumerate(zip(got, parts))]
        full[i].update(zip(names, zip(halves, send_halves(f"sendh_{tag}", halves))))

    for i in reversed(range(depth)):
        kind, j = i % N_MIXERS, i // N_MIXERS
        L, st = layers[i], saved[i]
        gr = {}
        dep = None if flying is None else flying[4]
        gr["w2"] = stack_rows(mm_tn(f"mlp_dw2_{i}", st["a"], dxb, BF16, False, dep=dep))
        du = mm_nt(f"mlp_du_{i}", dxb, rows_stacked(L["w2"]), BF16, epilogue=_times_2r, extras=(st["r"],), dep=dep)
        gr["w1"] = mm_tn(f"mlp_dw1_{i}", st["h2"], du, BF16, True)
        dh2 = mm_nt(f"mlp_dh_{i}", du, L["w1"], F32)
        dx1, dx1b, dg2 = rmsnorm_bwd(f"norm2_bwd_{i}", st["x1"], st["g2"], dh2, dx)
        small[("mlp_norm", i)] = dg2
        if flying is not None:
            exchange_finish(flying, dx1)
        flying = exchange_start(i, "mlp", gr)
        gr = {}
        gr["w_out"] = stack_rows(mm_tn(f"mix_dwout_{i}", st["o"], dx1b, BF16, False, dep=flying[4]))
        do = mm_nt(f"mix_do_{i}", dx1b, rows_stacked(L["w_out"]), BF16, dep=flying[4])
        if kind == 0:
            dqkv = sb_bwd(f"sb_attn_bwd_{i}", st["qkv"], st["o_f32"], do, nh)
            gr["w_in"] = mm_tn(f"sb_dwin_{i}", st["hb"], dqkv, BF16, True)
            dh = mm_nt(f"sb_dh_{i}", dqkv, L["w_in"], F32)
        elif kind == 1:
            dqh, dkh, dv, dcf = attn_bwd(f"fox_attn_bwd_{i}", st["qh"], st["kh"], st["proj"], st["o_f32"], do, st["lse"], nh,
                                         HEAD_DIM, 0, 0, 2 * nh, 1, st["scale"], st["cf_col"], st["cf_row"])
            dcf_s = _pad_cols(dcf.reshape(nh, S).T, LANE)
            dlogf = seq_cumsum(f"fox_dcf_{i}", dcf_s, True)
            dgate, db = rowwise_bwd(f"fox_gate_bwd_{i}", _fox_gate_fn, st["gate_rows"], [st["b_pad"]], [(dlogf, LANE, _c0)],
                                    [(LANE, LANE, _c0, BF16, False)], tr)
            dq, dk, dgq, dgk = rowwise_bwd(
                f"fox_qk_bwd_{i}", _fox_qk_fn, st["qk_rows"], [st["gq"], st["gk"]],
                [(dqh, HEAD_DIM, lambda h: h), (dkh, HEAD_DIM, lambda h: h)], [(W, HEAD_DIM, lambda h: h, BF16, False)] * 2, tr, nh)
            small[("fox_b_f", j)] = db[:, :nh]
            small[("fox_q_gain", j)] = dgq
            small[("fox_k_gain", j)] = dgk
            dproj = jnp.concatenate([dq, dk, bf(dv), dgate], axis=1)
            dw = mm_tn(f"fox_dwin_{i}", st["hb"], dproj, BF16, False)
            n4 = n_fox_in // N_CHIPS
            gr["w_in"] = jnp.stack([dw[:, s * n4:(s + 1) * n4] for s in range(N_CHIPS)])
            dh = mm_nt(f"fox_dh_{i}", dproj, st["w_in"], F32)
        else:
            dqh, dkh, dv = attn_bwd(f"mla_attn_bwd_{i}", st["qh"], st["kh"], st["kv"], st["o_f32"], do, st["lse"], nh,
                                    MLA_QK_PAD, 0, 0, 1, 2, st["scale"])
            dqpad, dkn, dkr, dgq, dgk = rowwise_bwd(
                f"mla_pre2_bwd_{i}", _mla_pre2_fn, st["pre2_rows"], [st["gq"], st["gk"]],
                [(dqh, MLA_QK_PAD, lambda h: h), (dkh, MLA_QK_PAD, lambda h: h)],
                [(nh * MLA_QK_PAD, MLA_QK_PAD, lambda h: h, BF16, False), (W, MLA_NOPE, lambda h: h, BF16, False),
                 (LANE, LANE, _c0, F32, True)], tr, nh, n_diff=3)
            small[("mla_q_gain", j)] = dgq[:, :MLA_QK]
            small[("mla_k_gain", j)] = dgk[:, :MLA_QK]
            dqfull = dqpad.reshape(S, nh, MLA_QK_PAD)[:, :, :MLA_QK].reshape(S, nh * MLA_QK)
            dkv = jnp.stack([dkn.reshape(S, nh, MLA_NOPE), bf(dv).reshape(S, nh, MLA_V)], axis=2).reshape(S, nh * (MLA_NOPE + MLA_V))
            gr["w_uq"] = mm_tn(f"mla_dwuq_{i}", st["c_q"], dqfull, BF16, True)
            dc_q = mm_nt(f"mla_dcq_{i}", dqfull, L["w_uq"], F32)
            gr["w_ukv"] = mm_tn(f"mla_dwukv_{i}", st["c_kv"], dkv, BF16, True)
            dc_kv = mm_nt(f"mla_dckv_{i}", dkv, L["w_ukv"], F32)
            d1, d2, d3, dqn, dkvn = rowwise_bwd(
                f"mla_pre1_bwd_{i}", st["pre1_fn"], st["pre1_rows"], [q_norm_full, kv_norm_full],
                [(dc_q, q_rank, _c0), (dc_kv, kv_rank, _c0), (dkr, LANE, _c0)],
                [(q_rank, q_rank, _c0, BF16, False), (kv_rank, kv_rank, _c0, BF16, False), (LANE, LANE, _c0, BF16, False)],
                tr, n_diff=3)
            small[("mla_q_norm", j)] = dqn
            small[("mla_kv_norm", j)] = dkvn
            ddown = jnp.concatenate([d1, d2, d3], axis=1)
            dw = mm_tn(f"mla_dwin_{i}", st["hb"], ddown, BF16, False)
            gr["w_in"] = stack_rows(dw[:, :n_down])
            dh = mm_nt(f"mla_dh_{i}", ddown, st["w_in"], F32)
        dx, dxb, dg1 = rmsnorm_bwd(f"norm1_bwd_{i}", st["x"], st["g1"], dh, dx1)
        small[("mix_norm", i)] = dg1
        if flying is not None:
            exchange_finish(flying, dx)
        flying = exchange_start(i, "mix", gr)

    keys = list(small)
    flat = jnp.concatenate([small[k].reshape(-1) for k in keys])
    rows_g = -(-flat.shape[0] // (8 * LANE)) * 8
    flat = jnp.pad(flat, (0, rows_g * LANE - flat.shape[0]))
    summed = all_sum_small("sum_small", flat.reshape(rows_g, LANE)).reshape(-1)
    sg, off = {}, 0
    for k in keys:
        n = small[k].size
        sg[k] = summed[off:off + n].reshape(small[k].shape)
        off += n

    def holders(kind_of):
        return [i for i in range(depth) if kind_of is None or i % N_MIXERS == kind_of]

    tensors = {
        "sb_w_in": (sb_w_in, m_sb_w_in, v_sb_w_in, "w_in", holders(0)),
        "sb_w_out": (sb_w_out, m_sb_w_out, v_sb_w_out, "w_out", holders(0)),
        "fox_w_in": (fox_w_in, m_fox_w_in, v_fox_w_in, "w_in", holders(1)),
        "fox_w_out": (fox_w_out, m_fox_w_out, v_fox_w_out, "w_out", holders(1)),
        "mla_w_in": (mla_w_in, m_mla_w_in, v_mla_w_in, "w_in", holders(2)),
        "mla_w_uq": (mla_w_uq, m_mla_w_uq, v_mla_w_uq, "w_uq", holders(2)),
        "mla_w_ukv": (mla_w_ukv, m_mla_w_ukv, v_mla_w_ukv, "w_ukv", holders(2)),
        "mla_w_out": (mla_w_out, m_mla_w_out, v_mla_w_out, "w_out", holders(2)),
        "mlp_w1": (mlp_w1, m_mlp_w1, v_mlp_w1, "w1", holders(None)),
        "mlp_w2": (mlp_w2, m_mlp_w2, v_mlp_w2, "w2", holders(None)),
    }
    updated = {n: None for n in tensors}

    def update_layers(last):
        for n, (w, m, v, key, held) in tensors.items():
            for l in reversed(range(len(held))):
                if (held[l] == 0 and key not in ("w1", "w2")) == last:
                    mine, other = full[held[l]][key]
                    updated[n] = adamw_big(f"adamw_{n}_{l}", w, m, v, mine, other, c_arr, l, updated[n])

    update_layers(False)
    exchange_finish(flying, [u[0] for u in updated.values() if u is not None])
    update_layers(True)

    def update_big(name):
        return list(updated[name])

    def update_small(name, w, m, v, g):
        return [g] + list(adamw(f"adamw_{name}", w, g, m, v))

    def small_rows(name, count):
        return jnp.concatenate([sg[(name, l)] for l in range(count)], axis=0)

    def my_part(g, n):
        return lax.dynamic_slice(g, (0, chip * n), (g.shape[0], n))

    res = {
        "mix_norm": update_small("mix_norm", mix_norm, m_mix_norm, v_mix_norm, small_rows("mix_norm", depth)),
        "mlp_norm": update_small("mlp_norm", mlp_norm, m_mlp_norm, v_mlp_norm, small_rows("mlp_norm", depth)),
        "sb_w_in": update_big("sb_w_in"),
        "sb_w_out": update_big("sb_w_out"),
        "fox_w_in": update_big("fox_w_in"),
        "fox_b_f": update_small("fox_b_f", fox_b_f, m_fox_b_f, v_fox_b_f, small_rows("fox_b_f", fox_b_f.shape[0])),
        "fox_q_gain": update_small("fox_q_gain", fox_q_gain, m_fox_q_gain, v_fox_q_gain, small_rows("fox_q_gain", fox_q_gain.shape[0])),
        "fox_k_gain": update_small("fox_k_gain", fox_k_gain, m_fox_k_gain, v_fox_k_gain, small_rows("fox_k_gain", fox_k_gain.shape[0])),
        "fox_w_out": update_big("fox_w_out"),
        "mla_w_in": update_big("mla_w_in"),
        "mla_q_norm": update_small("mla_q_norm", mla_q_norm, m_mla_q_norm, v_mla_q_norm,
                                   my_part(small_rows("mla_q_norm", mla_q_norm.shape[0]), mla_q_norm.shape[1])),
        "mla_kv_norm": update_small("mla_kv_norm", mla_kv_norm, m_mla_kv_norm, v_mla_kv_norm,
                                    my_part(small_rows("mla_kv_norm", mla_kv_norm.shape[0]), mla_kv_norm.shape[1])),
        "mla_w_uq": update_big("mla_w_uq"),
        "mla_w_ukv": update_big("mla_w_ukv"),
        "mla_q_gain": update_small("mla_q_gain", mla_q_gain, m_mla_q_gain, v_mla_q_gain, small_rows("mla_q_gain", mla_q_gain.shape[0])),
        "mla_k_gain": update_small("mla_k_gain", mla_k_gain, m_mla_k_gain, v_mla_k_gain, small_rows("mla_k_gain", mla_k_gain.shape[0])),
        "mla_w_out": update_big("mla_w_out"),
        "mlp_w1": update_big("mlp_w1"),
        "mlp_w2": update_big("mlp_w2"),
    }
    order = ["mix_norm", "mlp_norm", "sb_w_in", "sb_w_out", "fox_w_in", "fox_b_f", "fox_q_gain", "fox_k_gain", "fox_w_out",
             "mla_w_in", "mla_q_norm", "mla_kv_norm", "mla_w_uq", "mla_w_ukv", "mla_q_gain", "mla_k_gain", "mla_w_out",
             "mlp_w1", "mlp_w2"]
    outs = [loss, dx.reshape(x.shape)]
    for k in range(4):
        outs += [res[n][k] for n in order]
    return tuple(outs)
```

```python
import functools
import math

import numpy as np
import jax
import jax.numpy as jnp
from jax import lax
from jax.experimental import pallas as pl
from jax.experimental.pallas import tpu as pltpu

F32 = jnp.float32
BF16 = jnp.bfloat16
MESH = pl.DeviceIdType.MESH

EPS = 1e-6
HEAD_DIM = 128
MLA_NOPE = 128
MLA_ROPE = 64
MLA_V = 128
MLA_QK = MLA_NOPE + MLA_ROPE
MLA_QK_PAD = 256
ROPE_THETA = 10000.0
N_MIXERS = 3
ADAM_LR = 0.001
ADAM_B1 = 0.9
ADAM_B2 = 0.999
ADAM_EPS = 1e-08
ADAM_WD = 0.01
ADAM_STEP = 10

LANE = 128
N_CHIPS = 4
VMEM_LIMIT = 48 * 1024 * 1024
NEG = -1e30


def _cp(sem):
    return pltpu.CompilerParams(dimension_semantics=sem, vmem_limit_bytes=VMEM_LIMIT)


def _pick(dim, cap):
    best = None
    b = LANE
    while b <= min(dim, cap):
        if dim % b == 0:
            best = b
        b += LANE
    return best if best is not None else dim


def _pick_rows(dim, cap):
    b = min(dim, cap)
    while dim % b:
        b -= 8
    return b


def _matmul(name, a, b, a_blk, a_map, b_blk, b_map, dn, grid, acc_shape, outs, extras=(), epilogue=None, dep=None):
    nk = grid[2]
    n_ex, n_out = len(extras), len(outs)
    n_dep = 0 if dep is None else 1

    def body(*refs):
        a_ref, b_ref = refs[0], refs[1]
        ex_refs = refs[2:2 + n_ex]
        out_refs = refs[2 + n_ex + n_dep:2 + n_ex + n_dep + n_out]
        acc = refs[-1]
        k = pl.program_id(2)

        @pl.when(k == 0)
        def _():
            acc[...] = jnp.zeros_like(acc)

        acc[...] += lax.dot_general(a_ref[...], b_ref[...], dn, preferred_element_type=F32)

        @pl.when(k == nk - 1)
        def _():
            res = acc[...]
            vals = epilogue(res, *[e[...] for e in ex_refs]) if epilogue is not None else (res,)
            for o, v in zip(out_refs, vals):
                o[...] = v.astype(o.dtype)

    in_specs = [pl.BlockSpec(a_blk, a_map), pl.BlockSpec(b_blk, b_map)]
    in_specs += [pl.BlockSpec(blk, lambda i, j, k, m=m: m(i, j)) for (_, blk, m) in extras]
    in_specs += [pl.BlockSpec(memory_space=pl.ANY)] * n_dep
    out_specs = [pl.BlockSpec(blk, lambda i, j, k, m=m: m(i, j)) for (_, _, blk, m) in outs]
    out_shape = [jax.ShapeDtypeStruct(s, d) for (s, d, _, _) in outs]
    res = pl.pallas_call(
        body, name=name, grid=grid, in_specs=in_specs, out_specs=out_specs, out_shape=out_shape,
        scratch_shapes=[pltpu.VMEM(acc_shape, F32)],
        compiler_params=_cp(("parallel", "parallel", "arbitrary")),
    )(a, b, *[e[0] for e in extras], *([] if dep is None else [dep]))
    return res


BM, BN, BK = 1024, 1024, 2048


def mm_nn(name, a, w, out_dtype, epilogue=None, extras=(), n_out=1, out_dtypes=None):
    M, K = a.shape
    stacked = w.ndim == 3
    n4 = w.shape[-1]
    N = n4 * (N_CHIPS if stacked else 1)
    bm, bn, bk = _pick_rows(M, BM), _pick(n4, BN), _pick(K, BK)
    nb = n4 // bn
    if stacked:
        b_blk, b_map = (None, bk, bn), (lambda i, j, k: (j // nb, k, j % nb))
    else:
        b_blk, b_map = (bk, bn), (lambda i, j, k: (k, j))
    dts = out_dtypes if out_dtypes is not None else [out_dtype] * n_out
    outs = [((M, N), d, (bm, bn), lambda i, j: (i, j)) for d in dts]
    exs = [(e, (bm, bn), lambda i, j: (i, j)) for e in extras]
    return _matmul(name, a, w, (bm, bk), lambda i, j, k: (i, k), b_blk, b_map,
                   (((1,), (0,)), ((), ())), (M // bm, N // bn, K // bk), (bm, bn), outs, exs, epilogue)


def mm_nt(name, dy, w, out_dtype, epilogue=None, extras=(), dep=None):
    M, N = dy.shape
    stacked = w.ndim == 3
    K, n4 = w.shape[-2], w.shape[-1]
    bm, bn, bk = _pick_rows(M, BM), _pick(K, BN), _pick(n4, BK)
    nb = n4 // bk
    if stacked:
        b_blk, b_map = (None, bn, bk), (lambda i, j, k: (k // nb, j, k % nb))
    else:
        b_blk, b_map = (bn, bk), (lambda i, j, k: (j, k))
    outs = [((M, K), out_dtype, (bm, bn), lambda i, j: (i, j))]
    exs = [(e, (bm, bn), lambda i, j: (i, j)) for e in extras]
    return _matmul(name, dy, w, (bm, bk), lambda i, j, k: (i, k), b_blk, b_map,
                   (((1,), (1,)), ((), ())), (M // bm, K // bn, N // bk), (bm, bn), outs, exs, epilogue, dep)[0]


def mm_tn(name, a, dy, out_dtype, stacked, dep=None):
    M, K = a.shape
    N = dy.shape[1]
    n4 = N // N_CHIPS if stacked else N
    bm, bn, bk = _pick(K, BM), _pick(n4, BN), _pick_rows(M, BK)
    nb = n4 // bn
    if stacked:
        outs = [((N_CHIPS, K, n4), out_dtype, (None, bm, bn), lambda i, j: (j // nb, i, j % nb))]
    else:
        outs = [((K, N), out_dtype, (bm, bn), lambda i, j: (i, j))]
    return _matmul(name, a, dy, (bk, bm), lambda i, j, k: (k, i), (bk, bn), lambda i, j, k: (k, j),
                   (((0,), (0,)), ((), ())), (K // bm, N // bn, M // bk), (bm, bn), outs, dep=dep)[0]


def _split3(x):
    hi = x.astype(BF16)
    r = x - hi.astype(F32)
    mid = r.astype(BF16)
    lo = (r - mid.astype(F32)).astype(BF16)
    return hi, mid, lo


def _exact_dot(x, m, n):
    out = None
    for p in _split3(x)[:n]:
        d = jnp.dot(p, m, preferred_element_type=F32)
        out = d if out is None else out + d
    return out


def _row_spec(tr, width, cmap):
    return pl.BlockSpec((tr, width), lambda i, h: (i, cmap(h)))


def _full_spec(p):
    return pl.BlockSpec(p.shape, lambda i, h: (0,) * p.ndim)


def rowwise_fwd(name, fn, rows, params, outs, tr, nh=1):
    S = rows[0][0].shape[0]
    nr, npar = len(rows), len(params)

    def body(*refs):
        vals = fn(*[r[...].astype(F32) for r in refs[:nr]], *[p[...] for p in refs[nr:nr + npar]])
        for o, v in zip(refs[nr + npar:], vals):
            o[...] = v.astype(o.dtype)

    return pl.pallas_call(
        body, name=name, grid=(S // tr, nh),
        in_specs=[_row_spec(tr, w, cm) for (_, w, cm) in rows] + [_full_spec(p) for p in params],
        out_specs=[_row_spec(tr, w, cm) for (_, w, cm, _) in outs],
        out_shape=[jax.ShapeDtypeStruct((S, c), d) for (c, _, _, d) in outs],
        compiler_params=_cp(("parallel", "arbitrary")),
    )(*[r[0] for r in rows], *params)


def rowwise_bwd(name, fn, rows, params, cts, grads, tr, nh=1, n_diff=None, add=None):
    S = rows[0][0].shape[0]
    nr, npar, nct = len(rows), len(params), len(cts)
    n_diff = nr if n_diff is None else n_diff
    n_add = 1 if add is not None else 0

    def body(*refs):
        row_refs = refs[:nr]
        par_refs = refs[nr:nr + npar]
        ct_refs = refs[nr + npar:nr + npar + nct]
        add_refs = refs[nr + npar + nct:nr + npar + nct + n_add]
        o = nr + npar + nct + n_add
        g_refs = refs[o:o + n_diff]
        cp_refs = refs[o + n_diff:o + n_diff + n_add]
        pg_refs = refs[o + n_diff + n_add:]
        i, h = pl.program_id(0), pl.program_id(1)
        rv = [r[...].astype(F32) for r in row_refs]
        pv = [p[...] for p in par_refs]
        aux = rv[n_diff:]

        def f(*dp):
            return fn(*dp[:n_diff], *aux, *dp[n_diff:])

        _, vjp = jax.vjp(f, *rv[:n_diff], *pv)
        gs = vjp(tuple(c[...].astype(F32) for c in ct_refs))
        for n, (g_ref, (_, _, _, _, over)) in enumerate(zip(g_refs, grads)):
            g = gs[n]
            if n == 0 and add is not None:
                g = g + add_refs[0][...]
                cp_refs[0][...] = g.astype(BF16)
            if over:
                @pl.when(h == 0)
                def _(g_ref=g_ref):
                    g_ref[...] = jnp.zeros_like(g_ref)
                g_ref[...] += g.astype(g_ref.dtype)
            else:
                g_ref[...] = g.astype(g_ref.dtype)
        for pg_ref, g in zip(pg_refs, gs[n_diff:]):
            @pl.when((i == 0) & (h == 0))
            def _(pg_ref=pg_ref):
                pg_ref[...] = jnp.zeros_like(pg_ref)
            pg_ref[...] += g

    in_specs = [_row_spec(tr, w, cm) for (_, w, cm) in rows] + [_full_spec(p) for p in params]
    in_specs += [_row_spec(tr, w, cm) for (_, w, cm) in cts]
    operands = [r[0] for r in rows] + list(params) + [c[0] for c in cts]
    out_specs = [_row_spec(tr, w, cm) for (_, w, cm, _, _) in grads]
    out_shape = [jax.ShapeDtypeStruct((S, c), d) for (c, _, _, d, _) in grads]
    if add is not None:
        in_specs.append(_row_spec(tr, add[1], add[2]))
        operands.append(add[0])
        out_specs.append(_row_spec(tr, add[1], add[2]))
        out_shape.append(jax.ShapeDtypeStruct(add[0].shape, BF16))
    out_specs += [_full_spec(p) for p in params]
    out_shape += [jax.ShapeDtypeStruct(p.shape, F32) for p in params]
    return pl.pallas_call(
        body, name=name, grid=(S // tr, nh), in_specs=in_specs, out_specs=out_specs, out_shape=out_shape,
        compiler_params=_cp(("arbitrary", "arbitrary")),
    )(*operands)


def _c0(h):
    return 0


def _rms(x, g, n):
    return x * lax.rsqrt(jnp.sum(x * x, axis=-1, keepdims=True) * (1.0 / n) + EPS) * g


def _rmsnorm_fn(d):
    def fn(x, g):
        return (_rms(x, g, d),)
    return fn


def rmsnorm_fwd(name, x, g):
    S, D = x.shape
    return rowwise_fwd(name, _rmsnorm_fn(D), [(x, D, _c0)], [g], [(D, D, _c0, BF16)], _pick_rows(S, 256))[0]


def rmsnorm_bwd(name, x, g, dh, dres):
    S, D = x.shape
    return rowwise_bwd(name, _rmsnorm_fn(D), [(x, D, _c0)], [g], [(dh, D, _c0)], [(D, D, _c0, F32, False)],
                       _pick_rows(S, 256), add=(dres, D, _c0))


def loss_head(y, t):
    S, D = y.shape
    tr = _pick_rows(S, 256)

    def body(y_ref, t_ref, l_ref, d_ref, db_ref):
        @pl.when(pl.program_id(0) == 0)
        def _():
            l_ref[...] = jnp.zeros_like(l_ref)
        e = y_ref[...] - t_ref[...]
        l_ref[...] += 0.5 * jnp.sum(jnp.sum(e * e, axis=1, keepdims=True), axis=0, keepdims=True) * (1.0 / D)
        d = e * (1.0 / D)
        d_ref[...] = d
        db_ref[...] = d.astype(BF16)

    row = pl.BlockSpec((tr, D), lambda i: (i, 0))
    return pl.pallas_call(
        body, name="loss_head", grid=(S // tr,), in_specs=[row, row],
        out_specs=[pl.BlockSpec((1, 1), lambda i: (0, 0)), row, row],
        out_shape=[jax.ShapeDtypeStruct((1, 1), F32), jax.ShapeDtypeStruct((S, D), F32), jax.ShapeDtypeStruct((S, D), BF16)],
        compiler_params=_cp(("arbitrary",)),
    )(y, t)


def _dot_nt(a, b):
    return lax.dot_general(a, b, (((1,), (1,)), ((), ())), preferred_element_type=F32)


def _dot_tn(a, b):
    return lax.dot_general(a, b, (((0,), (0,)), ((), ())), preferred_element_type=F32)


AT_TQ, AT_TK = 512, 256


def attn_fwd(name, q, k, v, nh, dq, qoff, koff, voff, vstep, scale, cf_col=None, cf_row=None):
    S = q.shape[0]
    tq = _pick_rows(S, AT_TQ)
    tk = _pick_rows(S, AT_TK)
    nq = S // tq
    bias = cf_col is not None

    def body(*refs):
        if bias:
            q_ref, k_ref, v_ref, cfc_ref, cfr_ref, o_ref, of_ref, lse_ref = refs
        else:
            q_ref, k_ref, v_ref, o_ref, of_ref, lse_ref = refs
        i = pl.program_id(1)
        qv = q_ref[...].astype(BF16)
        row = i * tq + lax.broadcasted_iota(jnp.int32, (tq, tk), 0)
        coli = lax.broadcasted_iota(jnp.int32, (tq, tk), 1)
        nkb = ((i + 1) * tq + tk - 1) // tk

        def logits(j):
            return _dot_nt(qv, k_ref[pl.ds(pl.multiple_of(j * tk, tk), tk), :].astype(BF16)) * scale

        def step(j, carry):
            s, m, l, acc, acc_lo = carry
            s_next = logits(jnp.minimum(j + 1, nkb - 1))
            vs = v_ref[pl.ds(pl.multiple_of(j * tk, tk), tk), :].astype(BF16)
            if bias:
                s = s + cfc_ref[0] - cfr_ref[0, pl.ds(j, 1), :]
            s = jnp.where(j * tk + coli <= row, s, NEG)
            m_new = jnp.maximum(m, jnp.max(s, axis=1, keepdims=True))
            alpha = jnp.exp(m - m_new)
            p = jnp.exp(s - m_new)
            pb = p.astype(BF16)
            l = alpha * l + jnp.sum(p, axis=1, keepdims=True)
            acc = alpha * acc + jnp.dot(pb, vs, preferred_element_type=F32)
            acc_lo = alpha * acc_lo + jnp.dot((p - pb.astype(F32)).astype(BF16), vs, preferred_element_type=F32)
            return s_next, m_new, l, acc, acc_lo

        z = jnp.zeros((tq, HEAD_DIM), F32)
        _, m, l, acc, acc_lo = lax.fori_loop(
            0, nkb, step, (logits(0), jnp.full((tq, 1), NEG, F32), jnp.zeros((tq, 1), F32), z, z))
        o_ref[...] = (acc / l).astype(o_ref.dtype)
        of_ref[...] = (acc + acc_lo) / l
        lse_ref[0] = m + jnp.log(l)

    in_specs = [pl.BlockSpec((tq, dq), lambda h, i: (i, qoff + h)),
                pl.BlockSpec((S, dq), lambda h, i: (0, koff + h)),
                pl.BlockSpec((S, HEAD_DIM), lambda h, i: (0, voff + vstep * h))]
    operands = [q, k, v]
    if bias:
        in_specs += [pl.BlockSpec((1, tq, 1), lambda h, i: (h, i, 0)), pl.BlockSpec((1, S // tk, tk), lambda h, i: (h, 0, 0))]
        operands += [cf_col, cf_row]
    return pl.pallas_call(
        body, name=name, grid=(nh, nq), in_specs=in_specs,
        out_specs=[pl.BlockSpec((tq, HEAD_DIM), lambda h, i: (i, h)), pl.BlockSpec((tq, HEAD_DIM), lambda h, i: (i, h)),
                   pl.BlockSpec((1, tq, 1), lambda h, i: (h, i, 0))],
        out_shape=[jax.ShapeDtypeStruct((S, nh * HEAD_DIM), BF16), jax.ShapeDtypeStruct((S, nh * HEAD_DIM), F32),
                   jax.ShapeDtypeStruct((nh, S, 1), F32)],
        compiler_params=_cp(("parallel", "arbitrary")),
    )(*operands)


def attn_bwd(name, q, k, v, o, do, lse, nh, dq, qoff, koff, voff, vstep, scale, cf_col=None, cf_row=None):
    S = q.shape[0]
    tq = _pick_rows(S, AT_TQ)
    tk = _pick_rows(S, AT_TK)
    nq = S // tq
    bias = cf_col is not None

    def body(*refs):
        if bias:
            q_ref, k_ref, v_ref, o_ref, do_ref, lse_ref, cfc_ref, cfr_ref, dq_ref, dk_ref, dv_ref, dcf_ref = refs
        else:
            q_ref, k_ref, v_ref, o_ref, do_ref, lse_ref, dq_ref, dk_ref, dv_ref = refs
        dk_ref[...] = jnp.zeros_like(dk_ref)
        dv_ref[...] = jnp.zeros_like(dv_ref)
        if bias:
            dcf_ref[...] = jnp.zeros_like(dcf_ref)
        rowi = lax.broadcasted_iota(jnp.int32, (tq, tk), 0)
        coli = lax.broadcasted_iota(jnp.int32, (tq, tk), 1)

        def outer(i, _):
            roff = pl.multiple_of(i * tq, tq)
            qv = q_ref[pl.ds(roff, tq), :].astype(BF16)
            dov = do_ref[pl.ds(roff, tq), :]
            delta = jnp.sum(dov.astype(F32) * o_ref[pl.ds(roff, tq), :].astype(F32), axis=1, keepdims=True)
            lse = lse_ref[0, pl.ds(roff, tq), :]
            if bias:
                cq = cfc_ref[0, pl.ds(roff, tq), :]

            nkb = ((i + 1) * tq + tk - 1) // tk

            def products(j):
                off = pl.multiple_of(j * tk, tk)
                return (_dot_nt(qv, k_ref[pl.ds(off, tk), :].astype(BF16)) * scale,
                        _dot_nt(dov, v_ref[pl.ds(off, tk), :].astype(BF16)))

            def inner(j, carry):
                s, dp, dq_acc = carry
                s_next, dp_next = products(jnp.minimum(j + 1, nkb - 1))
                off = pl.multiple_of(j * tk, tk)
                ks = k_ref[pl.ds(off, tk), :].astype(BF16)
                if bias:
                    s = s + cq - cfr_ref[0, pl.ds(j, 1), :]
                p = jnp.where(j * tk + coli <= i * tq + rowi, jnp.exp(s - lse), 0.0)
                ds = p * (dp - delta)
                if bias:
                    dcf_ref[0, pl.ds(j, 1), :] -= jnp.sum(ds, axis=0, keepdims=True)
                dsb = (ds * scale).astype(BF16)
                dv_ref[pl.ds(off, tk), :] += _dot_tn(p.astype(BF16), dov)
                dk_ref[pl.ds(off, tk), :] += _dot_tn(dsb, qv)
                return s_next, dp_next, dq_acc + jnp.dot(dsb, ks, preferred_element_type=F32)

            dq_ref[pl.ds(roff, tq), :] = lax.fori_loop(0, nkb, inner, products(0) + (jnp.zeros((tq, dq), F32),))[2]
            return 0

        lax.fori_loop(0, nq, outer, 0)

    in_specs = [pl.BlockSpec((S, dq), lambda h: (0, qoff + h)),
                pl.BlockSpec((S, dq), lambda h: (0, koff + h)),
                pl.BlockSpec((S, HEAD_DIM), lambda h: (0, voff + vstep * h)),
                pl.BlockSpec((S, HEAD_DIM), lambda h: (0, h)),
                pl.BlockSpec((S, HEAD_DIM), lambda h: (0, h)),
                pl.BlockSpec((1, S, 1), lambda h: (h, 0, 0))]
    operands = [q, k, v, o, do, lse]
    out_specs = [pl.BlockSpec((S, dq), lambda h: (0, h)), pl.BlockSpec((S, dq), lambda h: (0, h)),
                 pl.BlockSpec((S, HEAD_DIM), lambda h: (0, h))]
    out_shape = [jax.ShapeDtypeStruct((S, nh * dq), F32), jax.ShapeDtypeStruct((S, nh * dq), F32),
                 jax.ShapeDtypeStruct((S, nh * HEAD_DIM), F32)]
    if bias:
        in_specs += [pl.BlockSpec((1, S, 1), lambda h: (h, 0, 0)), pl.BlockSpec((1, S // tk, tk), lambda h: (h, 0, 0))]
        operands += [cf_col, cf_row]
        out_specs.append(pl.BlockSpec((1, S // tk, tk), lambda h: (h, 0, 0)))
        out_shape.append(jax.ShapeDtypeStruct((nh, S // tk, tk), F32))
    return pl.pallas_call(
        body, name=name, grid=(nh,), in_specs=in_specs, out_specs=out_specs, out_shape=out_shape,
        compiler_params=_cp(("parallel",)),
    )(*operands)


SB_TQ, SB_TK = 512, 256


def _sb_block(z, row, col, later_c, tri_after):
    strict = col < row
    lsn = jnp.minimum(-z, 0.0) - jnp.log(1.0 + jnp.exp(-jnp.abs(z)))
    lsp = lsn + z
    L = jnp.where(strict, lsn, 0.0)
    later = _exact_dot(L, tri_after, 2) + later_c
    a = jnp.where(strict, jnp.exp(lsp + later), 0.0)
    return strict, lsp, lsn, L, a


def _tri(n, after_strict):
    r = lax.broadcasted_iota(jnp.int32, (n, n), 0)
    c = lax.broadcasted_iota(jnp.int32, (n, n), 1)
    return jnp.where(r > c if after_strict else r >= c, 1.0, 0.0).astype(BF16)


def sb_fwd(name, qkv, nh):
    S = qkv.shape[0]
    tq = _pick_rows(S, SB_TQ)
    tk = _pick_rows(S, SB_TK)
    nq = S // tq
    scale = 1.0 / math.sqrt(HEAD_DIM)

    def body(q_ref, k_ref, v_ref, o_ref, of_ref):
        i = pl.program_id(1)
        qv = q_ref[...]
        tri = _tri(tk, True)
        row = i * tq + lax.broadcasted_iota(jnp.int32, (tq, tk), 0)
        coli = lax.broadcasted_iota(jnp.int32, (tq, tk), 1)

        def logits(j):
            return _dot_nt(qv, k_ref[pl.ds(pl.multiple_of(j * tk, tk), tk), :]) * scale

        nkb = ((i + 1) * tq + tk - 1) // tk

        def step(jj, carry):
            z, later_c, acc, acc_lo = carry
            j = nkb - 1 - jj
            z_next = logits(jnp.maximum(j - 1, 0))
            vb = v_ref[pl.ds(pl.multiple_of(j * tk, tk), tk), :]
            _, _, _, L, a = _sb_block(z, row, j * tk + coli, later_c, tri)
            ab = a.astype(BF16)
            acc = acc + jnp.dot(ab, vb, preferred_element_type=F32)
            acc_lo = acc_lo + jnp.dot((a - ab.astype(F32)).astype(BF16), vb, preferred_element_type=F32)
            return z_next, later_c + jnp.sum(L, axis=1, keepdims=True), acc, acc_lo

        zero = jnp.zeros((tq, HEAD_DIM), F32)
        _, _, acc, acc_lo = lax.fori_loop(0, nkb, step, (logits(nkb - 1), jnp.zeros((tq, 1), F32), zero, zero))
        o_ref[...] = acc.astype(o_ref.dtype)
        of_ref[...] = acc + acc_lo

    blk = pl.BlockSpec((tq, HEAD_DIM), lambda h, i: (i, h))
    return pl.pallas_call(
        body, name=name, grid=(nh, nq),
        in_specs=[blk,
                  pl.BlockSpec((S, HEAD_DIM), lambda h, i: (0, nh + h)),
                  pl.BlockSpec((S, HEAD_DIM), lambda h, i: (0, 2 * nh + h))],
        out_specs=[blk, blk],
        out_shape=[jax.ShapeDtypeStruct((S, nh * HEAD_DIM), BF16), jax.ShapeDtypeStruct((S, nh * HEAD_DIM), F32)],
        compiler_params=_cp(("parallel", "arbitrary")),
    )(qkv, qkv, qkv)


def sb_bwd(name, qkv, o, do, nh):
    S = qkv.shape[0]
    tq = _pick_rows(S, SB_TQ)
    tk = _pick_rows(S, SB_TK)
    nq = S // tq
    scale = 1.0 / math.sqrt(HEAD_DIM)

    def body(q_ref, k_ref, v_ref, o_ref, do_ref, dq_ref, dk_ref, dv_ref, dk_acc, dv_acc):
        dk_acc[...] = jnp.zeros_like(dk_acc)
        dv_acc[...] = jnp.zeros_like(dv_acc)
        tri = _tri(tk, True)
        tri_inc = _tri(tk, False)
        rowi = lax.broadcasted_iota(jnp.int32, (tq, tk), 0)
        coli = lax.broadcasted_iota(jnp.int32, (tq, tk), 1)

        def outer(i, _):
            roff = pl.multiple_of(i * tq, tq)
            qv = q_ref[pl.ds(roff, tq), :]
            dov = do_ref[pl.ds(roff, tq), :]
            dtot = jnp.sum(dov.astype(F32) * o_ref[pl.ds(roff, tq), :].astype(F32), axis=1, keepdims=True)
            row = i * tq + rowi

            def products(j):
                off = pl.multiple_of(j * tk, tk)
                return _dot_nt(qv, k_ref[pl.ds(off, tk), :]) * scale, _dot_nt(dov, v_ref[pl.ds(off, tk), :])

            nkb = ((i + 1) * tq + tk - 1) // tk

            def inner(jj, carry):
                z, da, later_c, suf_c, dq_acc = carry
                j = nkb - 1 - jj
                z_next, da_next = products(jnp.maximum(j - 1, 0))
                off = pl.multiple_of(j * tk, tk)
                kb = k_ref[pl.ds(off, tk), :]
                strict, lsp, lsn, L, a = _sb_block(z, row, j * tk + coli, later_c, tri)
                dl = da * a
                before = dtot - (suf_c + _exact_dot(dl, tri_inc, 3))
                dz = jnp.where(strict, dl * jnp.exp(lsn) - jnp.exp(lsp) * before, 0.0) * scale
                dzb = dz.astype(BF16)
                dv_acc[pl.ds(off, tk), :] += _dot_tn(a.astype(BF16), dov)
                dk_acc[pl.ds(off, tk), :] += _dot_tn(dzb, qv)
                return (z_next, da_next, later_c + jnp.sum(L, axis=1, keepdims=True), suf_c + jnp.sum(dl, axis=1, keepdims=True),
                        dq_acc + jnp.dot(dzb, kb, preferred_element_type=F32))

            z1 = jnp.zeros((tq, 1), F32)
            res = lax.fori_loop(0, nkb, inner, products(nkb - 1) + (z1, z1, jnp.zeros((tq, HEAD_DIM), F32)))
            dq_ref[pl.ds(roff, tq), :] = res[4].astype(dq_ref.dtype)
            return 0

        lax.fori_loop(0, nq, outer, 0)
        dk_ref[...] = dk_acc[...].astype(dk_ref.dtype)
        dv_ref[...] = dv_acc[...].astype(dv_ref.dtype)

    def col(off):
        return pl.BlockSpec((S, HEAD_DIM), lambda h: (0, off + h))

    dq, dk, dv = pl.pallas_call(
        body, name=name, grid=(nh,),
        in_specs=[col(0), col(nh), col(2 * nh), col(0), col(0)],
        out_specs=[col(0), col(0), col(0)],
        out_shape=[jax.ShapeDtypeStruct((S, nh * HEAD_DIM), BF16)] * 3,
        scratch_shapes=[pltpu.VMEM((S, HEAD_DIM), F32), pltpu.VMEM((S, HEAD_DIM), F32)],
        compiler_params=_cp(("parallel",)),
    )(qkv, qkv, qkv, o, do)
    return jnp.concatenate([dq, dk, dv], axis=1)


def seq_cumsum(name, x, reverse):
    S, W = x.shape

    tb = _pick_rows(S, 256)

    def body(x_ref, o_ref):
        parts = _split3(x_ref[...])
        c = lax.broadcasted_iota(jnp.int32, (tb, S), 1)
        for b in range(S // tb):
            r = b * tb + lax.broadcasted_iota(jnp.int32, (tb, S), 0)
            t = jnp.where(r <= c if reverse else r >= c, 1.0, 0.0).astype(BF16)
            out = None
            for p in parts:
                d = jnp.dot(t, p, preferred_element_type=F32)
                out = d if out is None else out + d
            o_ref[b * tb:(b + 1) * tb, :] = out

    return pl.pallas_call(body, name=name, out_shape=jax.ShapeDtypeStruct((S, W), F32),
                          compiler_params=pltpu.CompilerParams(vmem_limit_bytes=VMEM_LIMIT))(x)


def _adamw_math(w, g, m, v):
    c1 = 1.0 - ADAM_B1 ** ADAM_STEP
    c2 = 1.0 - ADAM_B2 ** ADAM_STEP
    nm = ADAM_B1 * m + (1.0 - ADAM_B1) * g
    nv = ADAM_B2 * v + (1.0 - ADAM_B2) * (g * g)
    return -ADAM_LR * ((nm / c1) / (jnp.sqrt(nv / c2) + ADAM_EPS) + ADAM_WD * w), nm, nv


def adamw(name, w, g, m, v):
    R, C = g.shape
    tr = _pick_rows(R, 256)

    def body(w_ref, g_ref, m_ref, v_ref, d_ref, nm_ref, nv_ref):
        d_ref[...], nm_ref[...], nv_ref[...] = _adamw_math(w_ref[...], g_ref[...], m_ref[...], v_ref[...])

    row = pl.BlockSpec((tr, C), lambda i: (i, 0))
    return pl.pallas_call(
        body, name=name, grid=(R // tr,), in_specs=[row] * 4, out_specs=[row] * 3,
        out_shape=[jax.ShapeDtypeStruct((R, C), F32)] * 3, compiler_params=_cp(("parallel",)),
    )(w, g, m, v)


def adamw_big(name, w, m, v, mine, other, c_arr, layer, prev):
    L, R, C = w.shape
    r2 = R // 2
    tr = _pick_rows(r2, 256)
    nb = r2 // tr

    def body(c_ref, w_ref, m_ref, v_ref, mine_ref, other_ref, *rest):
        g_ref, d_ref, nm_ref, nv_ref = rest[-4:]
        g = jnp.where(pl.program_id(0) == c_ref[0], mine_ref[...], other_ref[...])
        g_ref[...] = g
        d_ref[...], nm_ref[...], nv_ref[...] = _adamw_math(w_ref[...], g, m_ref[...], v_ref[...])

    sel = pl.BlockSpec((None, tr, C), lambda hf, i, c: (layer, hf * nb + i, 0))
    in_specs = [sel, sel, sel,
                pl.BlockSpec((tr, C), lambda hf, i, c: (jnp.where(hf == c[0], i, 0), 0)),
                pl.BlockSpec((tr, C), lambda hf, i, c: (jnp.where(hf == c[0], 0, i), 0))]
    operands = [c_arr, w, m, v, mine, other]
    aliases = {}
    if prev is not None:
        in_specs += [ANY] * 4
        aliases = {len(operands) + k: k for k in range(4)}
        operands += list(prev)
    return pl.pallas_call(
        body, name=name,
        grid_spec=pltpu.PrefetchScalarGridSpec(num_scalar_prefetch=1, grid=(2, nb), in_specs=in_specs, out_specs=[sel] * 4),
        out_shape=[jax.ShapeDtypeStruct((L, R, C), F32)] * 4, input_output_aliases=aliases,
        compiler_params=_cp(("arbitrary", "arbitrary")),
    )(*operands)


ANY = pl.BlockSpec(memory_space=pl.ANY)


def _place():
    x, y, c = lax.axis_index("x"), lax.axis_index("y"), lax.axis_index("c")
    others = [(1 - x, y), (x, 1 - y), (1 - x, 1 - y)]
    return x, y, c, others


def swap_halves(name, grads):
    n = len(grads)

    def body(*refs):
        src, dst = refs[:n], refs[n:2 * n]
        send, recv = refs[2 * n:]
        x, y, c, _ = _place()
        cps = []
        for t in range(n):
            r2 = grads[t].shape[1] // 2
            cps.append(pltpu.make_async_remote_copy(
                src_ref=src[t].at[:, pl.ds((1 - c) * r2, r2), :], dst_ref=dst[t],
                send_sem=send.at[t], recv_sem=recv.at[t], device_id=(x, y, 1 - c), device_id_type=MESH))
        for cp in cps:
            cp.start()
        for cp in cps:
            cp.wait()

    return pl.pallas_call(
        body, name=name, in_specs=[ANY] * n, out_specs=[ANY] * n,
        out_shape=[jax.ShapeDtypeStruct((N_CHIPS, g.shape[1] // 2, g.shape[2]), g.dtype) for g in grads],
        scratch_shapes=[pltpu.SemaphoreType.DMA((n,))] * 2,
    )(*grads)


def add_half(name, g, other, c_arr):
    _, R, C = g.shape
    r2 = R // 2
    tr = _pick_rows(r2, 512)
    nb = r2 // tr

    def body(c_ref, g_ref, o_ref, out_ref):
        out_ref[...] = (g_ref[...].astype(F32) + o_ref[...].astype(F32)).astype(out_ref.dtype)

    return pl.pallas_call(
        body, name=name,
        grid_spec=pltpu.PrefetchScalarGridSpec(
            num_scalar_prefetch=1, grid=(N_CHIPS, nb),
            in_specs=[pl.BlockSpec((None, tr, C), lambda s, i, c: (s, c[0] * nb + i, 0)),
                      pl.BlockSpec((None, tr, C), lambda s, i, c: (s, i, 0))],
            out_specs=pl.BlockSpec((None, tr, C), lambda s, i, c: (s, i, 0))),
        out_shape=jax.ShapeDtypeStruct((N_CHIPS, r2, C), BF16),
        compiler_params=_cp(("parallel", "parallel")),
    )(c_arr, g, other)


def sum_slabs(name, got, parts, chip_arr):
    _, r2, C = parts.shape
    tr = _pick_rows(r2, 512)

    def body(chip_ref, *refs):
        own_ref, out_ref = refs[N_CHIPS], refs[N_CHIPS + 1]
        acc = None
        for s in range(N_CHIPS):
            v = jnp.where(chip_ref[0] == s, own_ref[...], refs[s][...]).astype(F32)
            acc = v if acc is None else acc + v
        out_ref[...] = acc

    def slab(s):
        return pl.BlockSpec((None, tr, C), lambda i, ch: (jnp.where(ch[0] == s, (s + 1) % N_CHIPS, s), i, 0))

    return pl.pallas_call(
        body, name=name,
        grid_spec=pltpu.PrefetchScalarGridSpec(
            num_scalar_prefetch=1, grid=(r2 // tr,),
            in_specs=[slab(s) for s in range(N_CHIPS)] + [pl.BlockSpec((None, tr, C), lambda i, ch: (ch[0], i, 0))],
            out_specs=pl.BlockSpec((tr, C), lambda i, ch: (i, 0))),
        out_shape=jax.ShapeDtypeStruct((r2, C), F32), compiler_params=_cp(("parallel",)),
    )(chip_arr, got, got, got, got, parts)


def send_halves(name, halves):
    n = len(halves)

    def body(*refs):
        src, dst = refs[:n], refs[n:2 * n]
        send, recv = refs[2 * n:]
        x, y, c, _ = _place()
        cps = [pltpu.make_async_remote_copy(src_ref=src[t], dst_ref=dst[t], send_sem=send.at[t], recv_sem=recv.at[t],
                                            device_id=(x, y, 1 - c), device_id_type=MESH) for t in range(n)]
        for cp in cps:
            cp.start()
        for cp in cps:
            cp.wait()

    return pl.pallas_call(
        body, name=name, in_specs=[ANY] * n, out_specs=[ANY] * n,
        out_shape=[jax.ShapeDtypeStruct(h.shape, h.dtype) for h in halves],
        scratch_shapes=[pltpu.SemaphoreType.DMA((n,))] * 2,
    )(*halves)


HBM = pl.BlockSpec(memory_space=pltpu.HBM)
SEM = pl.BlockSpec(memory_space=pltpu.SEMAPHORE)
EFFECT = pltpu.SideEffectType.DATAFLOW_SIDE_EFFECTING


def _in_hbm(a):
    return pltpu.with_memory_space_constraint(a, pltpu.HBM)


def _chip_copies(kind, shapes, src, land, send, recv, mine):
    x, y, c, others = _place()
    me = 2 * x + y
    cps = []
    for t, shape in enumerate(shapes):
        for j, (px, py) in enumerate(others):
            slot = me if mine else 2 * px + py
            if kind == "gather":
                r2 = shape[0] // 2
                rows = pl.ds(c * r2, r2)
                s_ref, d_ref = src[t].at[rows, :], land[t].at[slot, rows, :]
            else:
                s_ref, d_ref = src[t].at[2 * px + py], land[t].at[slot]
            cps.append(pltpu.make_async_remote_copy(src_ref=s_ref, dst_ref=d_ref, send_sem=send.at[3 * t + j],
                                                    recv_sem=recv.at[3 * t + j], device_id=(px, py, c), device_id_type=MESH))
    return cps


def chips_start(name, kind, srcs, land_shapes, after=None):
    n = len(srcs)
    shapes = [s.shape for s in srcs]
    n_in = 2 * n + (0 if after is None else 1)

    def body(*refs):
        src, land = refs[:n], refs[n:2 * n]
        send, recv = refs[n_in], refs[n_in + 1]
        token = refs[-1]
        for cp in _chip_copies(kind, shapes, src, land, send, recv, True):
            cp.start()
        token[...] = jnp.zeros_like(token)

    lands = [lax.empty(s, srcs[0].dtype) for s in land_shapes]
    out = pl.pallas_call(
        body, name=name,
        out_shape=(pltpu.SemaphoreType.DMA((3 * n,)), pltpu.SemaphoreType.DMA((3 * n,)))
        + tuple(pltpu.HBM(s.shape, s.dtype) for s in srcs) + tuple(pltpu.HBM(l.shape, l.dtype) for l in lands)
        + (jax.ShapeDtypeStruct((8, LANE), F32),),
        in_specs=(HBM,) * (2 * n) + (ANY,) * (n_in - 2 * n),
        out_specs=(SEM, SEM) + (HBM,) * (2 * n) + (pl.BlockSpec(memory_space=pltpu.VMEM),),
        input_output_aliases={k: 2 + k for k in range(2 * n)},
        compiler_params=pltpu.CompilerParams(has_side_effects=EFFECT),
    )(*[_in_hbm(s) for s in srcs], *[_in_hbm(l) for l in lands], *([] if after is None else [after]))
    return (kind, shapes, out[0], out[1], out[2:2 + n], out[2 + n:2 + 2 * n]), out[-1][:1, :1]


def chips_wait(name, handle, after):
    kind, shapes, send, recv, srcs, lands = handle
    n = len(srcs)
    after = list(after) if isinstance(after, (list, tuple)) else [after]

    def body(*refs):
        src, land = refs[:n], refs[n:2 * n]
        for cp in _chip_copies(kind, shapes, src, land, refs[2 * n], refs[2 * n + 1], True):
            cp.wait_send()
        for cp in _chip_copies(kind, shapes, src, land, refs[2 * n], refs[2 * n + 1], False):
            cp.wait_recv()

    out = pl.pallas_call(
        body, name=name,
        out_shape=tuple(pltpu.HBM(s.shape, s.dtype) for s in srcs) + tuple(pltpu.HBM(l.shape, l.dtype) for l in lands),
        in_specs=(HBM,) * (2 * n) + (SEM, SEM) + (ANY,) * len(after), out_specs=(HBM,) * (2 * n),
        input_output_aliases={k: k for k in range(2 * n)},
        compiler_params=pltpu.CompilerParams(has_side_effects=EFFECT),
    )(*srcs, *lands, send, recv, *after)
    return list(out[n:]), list(out[:n])


def pass_to_sibling(name, lands, shards):
    n = len(lands)

    def body(*refs):
        buf = refs[n:2 * n]
        send, recv = refs[2 * n:]
        x, y, c, others = _place()

        def cp(t, j, core):
            r2 = lands[t].shape[1] // 2
            px, py = others[j]
            rows = buf[t].at[2 * px + py, pl.ds(core * r2, r2), :]
            return pltpu.make_async_remote_copy(src_ref=rows, dst_ref=rows, send_sem=send.at[t, j], recv_sem=recv.at[t, j],
                                                device_id=(x, y, 1 - c), device_id_type=MESH)

        for t in range(n):
            for j in range(3):
                cp(t, j, c).start()
        for t in range(n):
            for j in range(3):
                cp(t, j, 1 - c).wait_recv()
        for t in range(n):
            for j in range(3):
                cp(t, j, c).wait_send()

    got = pl.pallas_call(
        body, name=name, in_specs=[ANY] * n, out_specs=[ANY] * n,
        out_shape=[jax.ShapeDtypeStruct(l.shape, l.dtype) for l in lands],
        input_output_aliases={k: k for k in range(n)},
        scratch_shapes=[pltpu.SemaphoreType.DMA((n, 3))] * 2,
    )(*lands)
    chip = 2 * lax.axis_index("x") + lax.axis_index("y")
    return [lax.dynamic_update_slice(g, s[None], (chip, 0, 0)) for g, s in zip(got, shards)]


def all_sum_small(name, v):
    R = v.shape[0]

    def body(v_ref, out_ref, slots, send, recv):
        x, y, c, _ = _place()
        me = 4 * x + 2 * y + c
        slots[me] = v_ref[...]
        cps = []
        for k in range(1, 8):
            dx, dy, dc = (k >> 2) & 1, (k >> 1) & 1, k & 1
            to = (x ^ dx, y ^ dy, c ^ dc)
            cps.append(pltpu.make_async_remote_copy(
                src_ref=v_ref, dst_ref=slots.at[me], send_sem=send.at[k - 1], recv_sem=recv.at[k - 1],
                device_id=to, device_id_type=MESH))
        for cp in cps:
            cp.start()
        for k in range(1, 8):
            dx, dy, dc = (k >> 2) & 1, (k >> 1) & 1, k & 1
            frm = 4 * (x ^ dx) + 2 * (y ^ dy) + (c ^ dc)
            pltpu.make_async_remote_copy(
                src_ref=v_ref, dst_ref=slots.at[frm], send_sem=send.at[k - 1], recv_sem=recv.at[k - 1],
                device_id=(x, y, c), device_id_type=MESH).wait_recv()
        for cp in cps:
            cp.wait_send()
        acc = slots[0]
        for d in range(1, 8):
            acc = acc + slots[d]
        out_ref[...] = acc

    vm = pl.BlockSpec(memory_space=pltpu.VMEM)
    return pl.pallas_call(
        body, name=name, in_specs=[vm], out_specs=vm, out_shape=jax.ShapeDtypeStruct((R, LANE), F32),
        scratch_shapes=[pltpu.VMEM((8, R, LANE), F32), pltpu.SemaphoreType.DMA((7,)), pltpu.SemaphoreType.DMA((7,))],
    )(v)


def _rope_mat(n, lo):
    half = MLA_ROPE // 2
    r = lax.broadcasted_iota(jnp.int32, (n, n), 0)
    c = lax.broadcasted_iota(jnp.int32, (n, n), 1)
    plus = (c >= lo + half) & (c < lo + 2 * half) & (r == c - half)
    minus = (c >= lo) & (c < lo + half) & (r == c + half)
    return (jnp.where(plus, 1.0, 0.0) - jnp.where(minus, 1.0, 0.0)).astype(BF16)


def _rope(v, cos, sin, lo):
    return v * cos + _exact_dot(v, _rope_mat(v.shape[-1], lo), 3) * sin


def rope_tables(pos):
    S = pos.shape[0]
    half = MLA_ROPE // 2
    inv = (np.float32(ROPE_THETA) ** (-np.arange(0, half, dtype=np.float32) * np.float32(2.0 / MLA_ROPE))).astype(np.float32)
    f1 = np.zeros((1, LANE), np.float32)
    f1[0, :MLA_ROPE] = np.tile(inv, 2)
    f2 = np.zeros((1, MLA_QK_PAD), np.float32)
    f2[0, MLA_NOPE:MLA_QK] = np.tile(inv, 2)
    tr = _pick_rows(S, 256)

    def body(p_ref, f1_ref, f2_ref, c1, s1, c2, s2):
        p = p_ref[...].astype(F32)
        a1 = p * f1_ref[...]
        a2 = p * f2_ref[...]
        c1[...] = jnp.cos(a1)
        s1[...] = jnp.sin(a1)
        c2[...] = jnp.cos(a2)
        s2[...] = jnp.sin(a2)

    def row(w):
        return pl.BlockSpec((tr, w), lambda i: (i, 0))

    def full(w):
        return pl.BlockSpec((1, w), lambda i: (0, 0))

    return pl.pallas_call(
        body, name="rope_tables", grid=(S // tr,), in_specs=[row(1), full(LANE), full(MLA_QK_PAD)],
        out_specs=[row(LANE), row(LANE), row(MLA_QK_PAD), row(MLA_QK_PAD)],
        out_shape=[jax.ShapeDtypeStruct((S, LANE), F32)] * 2 + [jax.ShapeDtypeStruct((S, MLA_QK_PAD), F32)] * 2,
        compiler_params=_cp(("parallel",)),
    )(pos, jnp.asarray(f1), jnp.asarray(f2))


def _log_sigmoid(z):
    return jnp.minimum(z, 0.0) - jnp.log(1.0 + jnp.exp(-jnp.abs(z)))


def _fox_qk_fn(q, k, gq, gk):
    return _rms(q, gq, HEAD_DIM), _rms(k, gk, HEAD_DIM)


def _fox_gate_fn(f, b):
    return (_log_sigmoid(f + b),)


def _mla_pre1_fn(q_rank, kv_rank):
    def fn(cq, ckv, kr, cos, sin, qn, kvn):
        return _rms(cq, qn, q_rank), _rms(ckv, kvn, kv_rank), _rope(kr, cos, sin, 0)
    return fn


def _mla_pre2_fn(qb, kn, kr, cos, sin, gq, gk):
    qh = _rms(_rope(qb, cos, sin, MLA_NOPE), gq, MLA_QK)
    kh = _rms(jnp.concatenate([kn, kr], axis=1), gk, MLA_QK)
    return qh, kh


def _pad_cols(a, n):
    return jnp.pad(a, ((0, 0), (0, n - a.shape[1])))


def _relu2(acc):
    r = jnp.maximum(acc, 0.0)
    return r * r, r


def _add(acc, res):
    return (acc + res,)


def _times_2r(acc, r):
    return (acc * (2.0 * r.astype(F32)),)


def kernel(x, positions, mix_norm, mlp_norm, sb_w_in, sb_w_out, fox_w_in, fox_b_f, fox_q_gain, fox_k_gain, fox_w_out, mla_w_in, mla_q_norm, mla_kv_norm, mla_w_uq, mla_w_ukv, mla_q_gain, mla_k_gain, mla_w_out, mlp_w1, mlp_w2, loss_target, m_mix_norm, m_mlp_norm, m_sb_w_in, m_sb_w_out, m_fox_w_in, m_fox_b_f, m_fox_q_gain, m_fox_k_gain, m_fox_w_out, m_mla_w_in, m_mla_q_norm, m_mla_kv_norm, m_mla_w_uq, m_mla_w_ukv, m_mla_q_gain, m_mla_k_gain, m_mla_w_out, m_mlp_w1, m_mlp_w2, v_mix_norm, v_mlp_norm, v_sb_w_in, v_sb_w_out, v_fox_w_in, v_fox_b_f, v_fox_q_gain, v_fox_k_gain, v_fox_w_out, v_mla_w_in, v_mla_q_norm, v_mla_kv_norm, v_mla_w_uq, v_mla_w_ukv, v_mla_q_gain, v_mla_k_gain, v_mla_w_out, v_mlp_w1, v_mlp_w2):
    S, D = x.shape[1], x.shape[2]
    nh = D // HEAD_DIM
    W = nh * HEAD_DIM
    depth = mix_norm.shape[0]
    q_rank, kv_rank = mla_w_uq.shape[1], mla_w_ukv.shape[1]
    n_fox_in = 3 * W + nh
    fox_pad = -(-n_fox_in // LANE) * LANE
    n_down = q_rank + kv_rank + MLA_ROPE
    down_pad = q_rank + kv_rank + LANE
    tr = _pick_rows(S, 256)
    tk = _pick_rows(S, 256)

    ax, ay, ac = lax.axis_index("x"), lax.axis_index("y"), lax.axis_index("c")
    chip = 2 * ax + ay
    c_arr = jnp.reshape(ac, (1,)).astype(jnp.int32)
    chip_arr = jnp.reshape(chip, (1,)).astype(jnp.int32)

    def bf(a):
        return a.astype(BF16)

    n_small_in = q_rank + kv_rank
    rows_in = -(-n_small_in // (8 * LANE)) * 8
    placed = jnp.zeros((rows_in * LANE,), F32)
    placed = lax.dynamic_update_slice(placed, mla_q_norm[0], (chip * mla_q_norm.shape[1],))
    placed = lax.dynamic_update_slice(placed, mla_kv_norm[0], (q_rank + chip * mla_kv_norm.shape[1],))
    placed = placed * (ac == 0).astype(F32)
    norms = all_sum_small("gather_norms", placed.reshape(rows_in, LANE)).reshape(-1)
    q_norm_full = norms[:q_rank].reshape(1, q_rank)
    kv_norm_full = norms[q_rank:q_rank + kv_rank].reshape(1, kv_rank)

    def gather_start(i, part, after):
        kind, j = i % N_MIXERS, i // N_MIXERS
        if part == "mlp":
            names, shards = ["w1", "w2"], [bf(mlp_w1[i]), bf(mlp_w2[i])]
        elif kind == 0:
            names, shards = ["w_in", "w_out"], [bf(sb_w_in[j]), bf(sb_w_out[j])]
        elif kind == 1:
            names, shards = ["w_in", "w_out"], [bf(fox_w_in[j]), bf(fox_w_out[j])]
        else:
            names = ["w_in", "w_uq", "w_ukv", "w_out"]
            shards = [bf(mla_w_in[j]), bf(mla_w_uq[j]), bf(mla_w_ukv[j]), bf(mla_w_out[j])]
        handle, token = chips_start(f"gather_start_{part}_{i}", "gather", shards, [(N_CHIPS,) + s.shape for s in shards], after)
        return f"{part}_{i}", names, handle, token

    def gather_finish(pend, after):
        tag, names, handle, _ = pend
        lands, shards = chips_wait(f"gather_wait_{tag}", handle, after)
        return dict(zip(names, pass_to_sibling(f"gather_pass_{tag}", lands, shards)))

    def rows_stacked(w):
        return w.reshape(w.shape[0] * w.shape[1], w.shape[2])

    xc = x[0]
    saved = []
    layers = []
    tables = None
    pending = gather_start(0, "mix", norms)
    for i in range(depth):
        kind, j = i % N_MIXERS, i // N_MIXERS
        L = gather_finish(pending, xc)
        pending = gather_start(i, "mlp", L["w_out"])
        g1 = mix_norm[i:i + 1] + pending[3]
        hb = rmsnorm_fwd(f"norm1_{i}", xc, g1)
        st = dict(x=xc, hb=hb)
        if kind == 0:
            qkv = mm_nn(f"sb_proj_{i}", hb, L["w_in"], BF16)[0]
            o, o_f32 = sb_fwd(f"sb_attn_{i}", qkv, nh)
            st.update(qkv=qkv, o_f32=o_f32)
        elif kind == 1:
            w_in = _pad_cols(jnp.concatenate([L["w_in"][s] for s in range(N_CHIPS)], axis=1), fox_pad)
            proj = mm_nn(f"fox_proj_{i}", hb, w_in, F32)[0]
            gq, gk = fox_q_gain[j:j + 1], fox_k_gain[j:j + 1]
            qk_rows = [(proj, HEAD_DIM, lambda h: h), (proj, HEAD_DIM, lambda h: nh + h)]
            qh, kh = rowwise_fwd(f"fox_qk_{i}", _fox_qk_fn, qk_rows, [gq, gk],
                                 [(W, HEAD_DIM, lambda h: h, BF16)] * 2, tr, nh)
            b_pad = _pad_cols(fox_b_f[j:j + 1], LANE)
            gate_rows = [(proj, LANE, lambda h: 3 * nh)]
            logf = rowwise_fwd(f"fox_gate_{i}", _fox_gate_fn, gate_rows, [b_pad], [(LANE, LANE, _c0, F32)], tr)[0]
            cf = seq_cumsum(f"fox_cf_{i}", logf, False)[:, :nh].T
            cf_col, cf_row = cf.reshape(nh, S, 1), cf.reshape(nh, S // tk, tk)
            scale = 1.0 / math.sqrt(HEAD_DIM)
            o, o_f32, lse = attn_fwd(f"fox_attn_{i}", qh, kh, proj, nh, HEAD_DIM, 0, 0, 2 * nh, 1, scale, cf_col, cf_row)
            st.update(w_in=w_in, proj=proj, qk_rows=qk_rows, gq=gq, gk=gk, qh=qh, kh=kh, b_pad=b_pad, gate_rows=gate_rows,
                      cf_col=cf_col, cf_row=cf_row, lse=lse, scale=scale, o_f32=o_f32)
        else:
            w_in = _pad_cols(rows_stacked(L["w_in"]), down_pad)
            down = mm_nn(f"mla_down_{i}", hb, w_in, F32)[0]
            if tables is None:
                tables = rope_tables(positions.reshape(S, 1))
            cos1, sin1, cos2, sin2 = tables
            pre1_rows = [(down[:, :q_rank], q_rank, _c0), (down[:, q_rank:q_rank + kv_rank], kv_rank, _c0),
                         (down[:, q_rank + kv_rank:], LANE, _c0), (cos1, LANE, _c0), (sin1, LANE, _c0)]
            pre1_fn = _mla_pre1_fn(q_rank, kv_rank)
            c_q, c_kv, k_rope = rowwise_fwd(
                f"mla_pre1_{i}", pre1_fn, pre1_rows, [q_norm_full, kv_norm_full],
                [(q_rank, q_rank, _c0, BF16), (kv_rank, kv_rank, _c0, BF16), (LANE, LANE, _c0, F32)], tr)
            qfull = mm_nn(f"mla_uq_{i}", c_q, L["w_uq"], F32)[0]
            kv = mm_nn(f"mla_ukv_{i}", c_kv, L["w_ukv"], F32)[0]
            qpad = jnp.pad(qfull.reshape(S, nh, MLA_QK), ((0, 0), (0, 0), (0, MLA_QK_PAD - MLA_QK))).reshape(S, nh * MLA_QK_PAD)
            gq, gk = _pad_cols(mla_q_gain[j:j + 1], MLA_QK_PAD), _pad_cols(mla_k_gain[j:j + 1], MLA_QK_PAD)
            pre2_rows = [(qpad, MLA_QK_PAD, lambda h: h), (kv, MLA_NOPE, lambda h: 2 * h), (k_rope, LANE, _c0),
                         (cos2, MLA_QK_PAD, _c0), (sin2, MLA_QK_PAD, _c0)]
            qh, kh = rowwise_fwd(f"mla_pre2_{i}", _mla_pre2_fn, pre2_rows, [gq, gk],
                                 [(nh * MLA_QK_PAD, MLA_QK_PAD, lambda h: h, BF16)] * 2, tr, nh)
            scale = 1.0 / math.sqrt(MLA_QK)
            o, o_f32, lse = attn_fwd(f"mla_attn_{i}", qh, kh, kv, nh, MLA_QK_PAD, 0, 0, 1, 2, scale)
            st.update(w_in=w_in, pre1_rows=pre1_rows, pre1_fn=pre1_fn, c_q=c_q, c_kv=c_kv, pre2_rows=pre2_rows, gq=gq, gk=gk,
                      qh=qh, kh=kh, kv=kv, lse=lse, scale=scale, o_f32=o_f32)
        x1 = mm_nn(f"mix_out_{i}", o, rows_stacked(L["w_out"]), F32, epilogue=_add, extras=(xc,))[0]
        L.update(gather_finish(pending, x1))
        layers.append(L)
        g2 = mlp_norm[i:i + 1]
        if i + 1 < depth:
            pending = gather_start(i + 1, "mix", L["w2"])
            g2 = g2 + pending[3]
        h2 = rmsnorm_fwd(f"norm2_{i}", x1, g2)
        a, r = mm_nn(f"mlp_up_{i}", h2, L["w1"], None, epilogue=_relu2, out_dtypes=[BF16, BF16])
        xc = mm_nn(f"mlp_down_{i}", a, rows_stacked(L["w2"]), F32, epilogue=_add, extras=(x1,))[0]
        st.update(o=o, x1=x1, h2=h2, a=a, r=r, g1=g1, g2=g2)
        saved.append(st)

    loss_local, dx, dxb = loss_head(xc, loss_target[0])
    loss = lax.psum(loss_local[0, 0], ("x", "y", "c"))

    small = {}

    def stack_rows(g):
        return g.reshape(N_CHIPS, g.shape[0] // N_CHIPS, g.shape[1])

    full = [dict() for _ in range(depth)]
    flying = None

    def exchange_start(i, part, gr):
        tag = f"{part}_{i}"
        names = list(gr)
        grads = [gr[n] for n in names]
        theirs = swap_halves(f"swap_{tag}", grads)
        parts = [add_half(f"addh_{tag}_{t}", g, o, c_arr) for t, (g, o) in enumerate(zip(grads, theirs))]
        handle, token = chips_start(f"xchg_start_{tag}", "scatter", parts, [p.shape for p in parts])
        return i, tag, names, handle, token

    def exchange_finish(fly, after):
        i, tag, names, handle, _ = fly
        got, parts = chips_wait(f"xchg_wait_{tag}", handle, after)
        halves = [sum_slabs(f"sum4_{tag}_{t}", g, p, chip_arr) for t, (g, p) in enumerate(zip(got, parts))]
        full[i].update(zip(names, zip(halves, send_halves(f"sendh_{tag}", halves))))

    for i in reversed(range(depth)):
        kind, j = i % N_MIXERS, i // N_MIXERS
        L, st = layers[i], saved[i]
        gr = {}
        dep = None if flying is None else flying[4]
        gr["w2"] = stack_rows(mm_tn(f"mlp_dw2_{i}", st["a"], dxb, BF16, False, dep=dep))
        du = mm_nt(f"mlp_du_{i}", dxb, rows_stacked(L["w2"]), BF16, epilogue=_times_2r, extras=(st["r"],), dep=dep)
        gr["w1"] = mm_tn(f"mlp_dw1_{i}", st["h2"], du, BF16, True)
        dh2 = mm_nt(f"mlp_dh_{i}", du, L["w1"], F32)
        dx1, dx1b, dg2 = rmsnorm_bwd(f"norm2_bwd_{i}", st["x1"], st["g2"], dh2, dx)
        small[("mlp_norm", i)] = dg2
        if flying is not None:
            exchange_finish(flying, dx1)
        flying = exchange_start(i, "mlp", gr)
        gr = {}
        gr["w_out"] = stack_rows(mm_tn(f"mix_dwout_{i}", st["o"], dx1b, BF16, False, dep=flying[4]))
        do = mm_nt(f"mix_do_{i}", dx1b, rows_stacked(L["w_out"]), BF16, dep=flying[4])
        if kind == 0:
            dqkv = sb_bwd(f"sb_attn_bwd_{i}", st["qkv"], st["o_f32"], do, nh)
            gr["w_in"] = mm_tn(f"sb_dwin_{i}", st["hb"], dqkv, BF16, True)
            dh = mm_nt(f"sb_dh_{i}", dqkv, L["w_in"], F32)
        elif kind == 1:
            dqh, dkh, dv, dcf = attn_bwd(f"fox_attn_bwd_{i}", st["qh"], st["kh"], st["proj"], st["o_f32"], do, st["lse"], nh,
                                         HEAD_DIM, 0, 0, 2 * nh, 1, st["scale"], st["cf_col"], st["cf_row"])
            dcf_s = _pad_cols(dcf.reshape(nh, S).T, LANE)
            dlogf = seq_cumsum(f"fox_dcf_{i}", dcf_s, True)
            dgate, db = rowwise_bwd(f"fox_gate_bwd_{i}", _fox_gate_fn, st["gate_rows"], [st["b_pad"]], [(dlogf, LANE, _c0)],
                                    [(LANE, LANE, _c0, BF16, False)], tr)
            dq, dk, dgq, dgk = rowwise_bwd(
                f"fox_qk_bwd_{i}", _fox_qk_fn, st["qk_rows"], [st["gq"], st["gk"]],
                [(dqh, HEAD_DIM, lambda h: h), (dkh, HEAD_DIM, lambda h: h)], [(W, HEAD_DIM, lambda h: h, BF16, False)] * 2, tr, nh)
            small[("fox_b_f", j)] = db[:, :nh]
            small[("fox_q_gain", j)] = dgq
            small[("fox_k_gain", j)] = dgk
            dproj = jnp.concatenate([dq, dk, bf(dv), dgate], axis=1)
            dw = mm_tn(f"fox_dwin_{i}", st["hb"], dproj, BF16, False)
            n4 = n_fox_in // N_CHIPS
            gr["w_in"] = jnp.stack([dw[:, s * n4:(s + 1) * n4] for s in range(N_CHIPS)])
            dh = mm_nt(f"fox_dh_{i}", dproj, st["w_in"], F32)
        else:
            dqh, dkh, dv = attn_bwd(f"mla_attn_bwd_{i}", st["qh"], st["kh"], st["kv"], st["o_f32"], do, st["lse"], nh,
                                    MLA_QK_PAD, 0, 0, 1, 2, st["scale"])
            dqpad, dkn, dkr, dgq, dgk = rowwise_bwd(
                f"mla_pre2_bwd_{i}", _mla_pre2_fn, st["pre2_rows"], [st["gq"], st["gk"]],
                [(dqh, MLA_QK_PAD, lambda h: h), (dkh, MLA_QK_PAD, lambda h: h)],
                [(nh * MLA_QK_PAD, MLA_QK_PAD, lambda h: h, BF16, False), (W, MLA_NOPE, lambda h: h, BF16, False),
                 (LANE, LANE, _c0, F32, True)], tr, nh, n_diff=3)
            small[("mla_q_gain", j)] = dgq[:, :MLA_QK]
            small[("mla_k_gain", j)] = dgk[:, :MLA_QK]
            dqfull = dqpad.reshape(S, nh, MLA_QK_PAD)[:, :, :MLA_QK].reshape(S, nh * MLA_QK)
            dkv = jnp.stack([dkn.reshape(S, nh, MLA_NOPE), bf(dv).reshape(S, nh, MLA_V)], axis=2).reshape(S, nh * (MLA_NOPE + MLA_V))
            gr["w_uq"] = mm_tn(f"mla_dwuq_{i}", st["c_q"], dqfull, BF16, True)
            dc_q = mm_nt(f"mla_dcq_{i}", dqfull, L["w_uq"], F32)
            gr["w_ukv"] = mm_tn(f"mla_dwukv_{i}", st["c_kv"], dkv, BF16, True)
            dc_kv = mm_nt(f"mla_dckv_{i}", dkv, L["w_ukv"], F32)
            d1, d2, d3, dqn, dkvn = rowwise_bwd(
                f"mla_pre1_bwd_{i}", st["pre1_fn"], st["pre1_rows"], [q_norm_full, kv_norm_full],
                [(dc_q, q_rank, _c0), (dc_kv, kv_rank, _c0), (dkr, LANE, _c0)],
                [(q_rank, q_rank, _c0, BF16, False), (kv_rank, kv_rank, _c0, BF16, False), (LANE, LANE, _c0, BF16, False)],
                tr, n_diff=3)
            small[("mla_q_norm", j)] = dqn
            small[("mla_kv_norm", j)] = dkvn
            ddown = jnp.concatenate([d1, d2, d3], axis=1)
            dw = mm_tn(f"mla_dwin_{i}", st["hb"], ddown, BF16, False)
            gr["w_in"] = stack_rows(dw[:, :n_down])
            dh = mm_nt(f"mla_dh_{i}", ddown, st["w_in"], F32)
        dx, dxb, dg1 = rmsnorm_bwd(f"norm1_bwd_{i}", st["x"], st["g1"], dh, dx1)
        small[("mix_norm", i)] = dg1
        if flying is not None:
            exchange_finish(flying, dx)
        flying = exchange_start(i, "mix", gr)

    keys = list(small)
    flat = jnp.concatenate([small[k].reshape(-1) for k in keys])
    rows_g = -(-flat.shape[0] // (8 * LANE)) * 8
    flat = jnp.pad(flat, (0, rows_g * LANE - flat.shape[0]))
    summed = all_sum_small("sum_small", flat.reshape(rows_g, LANE)).reshape(-1)
    sg, off = {}, 0
    for k in keys:
        n = small[k].size
        sg[k] = summed[off:off + n].reshape(small[k].shape)
        off += n

    def holders(kind_of):
        return [i for i in range(depth) if kind_of is None or i % N_MIXERS == kind_of]

    tensors = {
        "sb_w_in": (sb_w_in, m_sb_w_in, v_sb_w_in, "w_in", holders(0)),
        "sb_w_out": (sb_w_out, m_sb_w_out, v_sb_w_out, "w_out", holders(0)),
        "fox_w_in": (fox_w_in, m_fox_w_in, v_fox_w_in, "w_in", holders(1)),
        "fox_w_out": (fox_w_out, m_fox_w_out, v_fox_w_out, "w_out", holders(1)),
        "mla_w_in": (mla_w_in, m_mla_w_in, v_mla_w_in, "w_in", holders(2)),
        "mla_w_uq": (mla_w_uq, m_mla_w_uq, v_mla_w_uq, "w_uq", holders(2)),
        "mla_w_ukv": (mla_w_ukv, m_mla_w_ukv, v_mla_w_ukv, "w_ukv", holders(2)),
        "mla_w_out": (mla_w_out, m_mla_w_out, v_mla_w_out, "w_out", holders(2)),
        "mlp_w1": (mlp_w1, m_mlp_w1, v_mlp_w1, "w1", holders(None)),
        "mlp_w2": (mlp_w2, m_mlp_w2, v_mlp_w2, "w2", holders(None)),
    }
    updated = {n: None for n in tensors}

    def update_layers(last):
        for n, (w, m, v, key, held) in tensors.items():
            for l in reversed(range(len(held))):
                if (held[l] == 0 and key not in ("w1", "w2")) == last:
                    mine, other = full[held[l]][key]
                    updated[n] = adamw_big(f"adamw_{n}_{l}", w, m, v, mine, other, c_arr, l, updated[n])

    update_layers(False)
    exchange_finish(flying, [u[0] for u in updated.values() if u is not None])
    update_layers(True)

    def update_big(name):
        return list(updated[name])

    def update_small(name, w, m, v, g):
        return [g] + list(adamw(f"adamw_{name}", w, g, m, v))

    def small_rows(name, count):
        return jnp.concatenate([sg[(name, l)] for l in range(count)], axis=0)

    def my_part(g, n):
        return lax.dynamic_slice(g, (0, chip * n), (g.shape[0], n))

    res = {
        "mix_norm": update_small("mix_norm", mix_norm, m_mix_norm, v_mix_norm, small_rows("mix_norm", depth)),
        "mlp_norm": update_small("mlp_norm", mlp_norm, m_mlp_norm, v_mlp_norm, small_rows("mlp_norm", depth)),
        "sb_w_in": update_big("sb_w_in"),
        "sb_w_out": update_big("sb_w_out"),
        "fox_w_in": update_big("fox_w_in"),
        "fox_b_f": update_small("fox_b_f", fox_b_f, m_fox_b_f, v_fox_b_f, small_rows("fox_b_f", fox_b_f.shape[0])),
        "fox_q_gain": update_small("fox_q_gain", fox_q_gain, m_fox_q_gain, v_fox_q_gain, small_rows("fox_q_gain", fox_q_gain.shape[0])),
        "fox_k_gain": update_small("fox_k_gain", fox_k_gain, m_fox_k_gain, v_fox_k_gain, small_rows("fox_k_gain", fox_k_gain.shape[0])),
        "fox_w_out": update_big("fox_w_out"),
        "mla_w_in": update_big("mla_w_in"),
        "mla_q_norm": update_small("mla_q_norm", mla_q_norm, m_mla_q_norm, v_mla_q_norm,
                                   my_part(small_rows("mla_q_norm", mla_q_norm.shape[0]), mla_q_norm.shape[1])),
        "mla_kv_norm": update_small("mla_kv_norm", mla_kv_norm, m_mla_kv_norm, v_mla_kv_norm,
                                    my_part(small_rows("mla_kv_norm", mla_kv_norm.shape[0]), mla_kv_norm.shape[1])),
        "mla_w_uq": update_big("mla_w_uq"),
        "mla_w_ukv": update_big("mla_w_ukv"),
        "mla_q_gain": update_small("mla_q_gain", mla_q_gain, m_mla_q_gain, v_mla_q_gain, small_rows("mla_q_gain", mla_q_gain.shape[0])),
        "mla_k_gain": update_small("mla_k_gain", mla_k_gain, m_mla_k_gain, v_mla_k_gain, small_rows("mla_k_gain", mla_k_gain.shape[0])),
        "mla_w_out": update_big("mla_w_out"),
        "mlp_w1": update_big("mlp_w1"),
        "mlp_w2": update_big("mlp_w2"),
    }
    order = ["mix_norm", "mlp_norm", "sb_w_in", "sb_w_out", "fox_w_in", "fox_b_f", "fox_q_gain", "fox_k_gain", "fox_w_out",
             "mla_w_in", "mla_q_norm", "mla_kv_norm", "mla_w_uq", "mla_w_ukv", "mla_q_gain", "mla_k_gain", "mla_w_out",
             "mlp_w1", "mlp_w2"]
    outs = [loss, dx.reshape(x.shape)]
    for k in range(4):
        outs += [res[n][k] for n in order]
    return tuple(outs)
```

```python
import functools
import math

import numpy as np
import jax
import jax.numpy as jnp
from jax import lax
from jax.experimental import pallas as pl
from jax.experimental.pallas import tpu as pltpu

F32 = jnp.float32
BF16 = jnp.bfloat16
MESH = pl.DeviceIdType.MESH

EPS = 1e-6
HEAD_DIM = 128
MLA_NOPE = 128
MLA_ROPE = 64
MLA_V = 128
MLA_QK = MLA_NOPE + MLA_ROPE
MLA_QK_PAD = 256
ROPE_THETA = 10000.0
N_MIXERS = 3
ADAM_LR = 0.001
ADAM_B1 = 0.9
ADAM_B2 = 0.999
ADAM_EPS = 1e-08
ADAM_WD = 0.01
ADAM_STEP = 10

LANE = 128
N_CHIPS = 4
VMEM_LIMIT = 48 * 1024 * 1024
NEG = -1e30


def _cp(sem):
    return pltpu.CompilerParams(dimension_semantics=sem, vmem_limit_bytes=VMEM_LIMIT)


def _pick(dim, cap):
    best = None
    b = LANE
    while b <= min(dim, cap):
        if dim % b == 0:
            best = b
        b += LANE
    return best if best is not None else dim


def _pick_rows(dim, cap):
    b = min(dim, cap)
    while dim % b:
        b -= 8
    return b


def _matmul(name, a, b, a_blk, a_map, b_blk, b_map, dn, grid, acc_shape, outs, extras=(), epilogue=None, dep=None):
    nk = grid[2]
    n_ex, n_out = len(extras), len(outs)
    n_dep = 0 if dep is None else 1

    def body(*refs):
        a_ref, b_ref = refs[0], refs[1]
        ex_refs = refs[2:2 + n_ex]
        out_refs = refs[2 + n_ex + n_dep:2 + n_ex + n_dep + n_out]
        acc = refs[-1]
        k = pl.program_id(2)

        @pl.when(k == 0)
        def _():
            acc[...] = jnp.zeros_like(acc)

        acc[...] += lax.dot_general(a_ref[...], b_ref[...], dn, preferred_element_type=F32)

        @pl.when(k == nk - 1)
        def _():
            res = acc[...]
            vals = epilogue(res, *[e[...] for e in ex_refs]) if epilogue is not None else (res,)
            for o, v in zip(out_refs, vals):
                o[...] = v.astype(o.dtype)

    in_specs = [pl.BlockSpec(a_blk, a_map), pl.BlockSpec(b_blk, b_map)]
    in_specs += [pl.BlockSpec(blk, lambda i, j, k, m=m: m(i, j)) for (_, blk, m) in extras]
    in_specs += [pl.BlockSpec(memory_space=pl.ANY)] * n_dep
    out_specs = [pl.BlockSpec(blk, lambda i, j, k, m=m: m(i, j)) for (_, _, blk, m) in outs]
    out_shape = [jax.ShapeDtypeStruct(s, d) for (s, d, _, _) in outs]
    res = pl.pallas_call(
        body, name=name, grid=grid, in_specs=in_specs, out_specs=out_specs, out_shape=out_shape,
        scratch_shapes=[pltpu.VMEM(acc_shape, F32)],
        compiler_params=_cp(("parallel", "parallel", "arbitrary")),
    )(a, b, *[e[0] for e in extras], *([] if dep is None else [dep]))
    return res


BM, BN, BK = 1024, 1024, 2048


def mm_nn(name, a, w, out_dtype, epilogue=None, extras=(), n_out=1, out_dtypes=None):
    M, K = a.shape
    stacked = w.ndim == 3
    n4 = w.shape[-1]
    N = n4 * (N_CHIPS if stacked else 1)
    bm, bn, bk = _pick_rows(M, BM), _pick(n4, BN), _pick(K, BK)
    nb = n4 // bn
    if stacked:
        b_blk, b_map = (None, bk, bn), (lambda i, j, k: (j // nb, k, j % nb))
    else:
        b_blk, b_map = (bk, bn), (lambda i, j, k: (k, j))
    dts = out_dtypes if out_dtypes is not None else [out_dtype] * n_out
    outs = [((M, N), d, (bm, bn), lambda i, j: (i, j)) for d in dts]
    exs = [(e, (bm, bn), lambda i, j: (i, j)) for e in extras]
    return _matmul(name, a, w, (bm, bk), lambda i, j, k: (i, k), b_blk, b_map,
                   (((1,), (0,)), ((), ())), (M // bm, N // bn, K // bk), (bm, bn), outs, exs, epilogue)


def mm_nt(name, dy, w, out_dtype, epilogue=None, extras=(), dep=None):
    M, N = dy.shape
    stacked = w.ndim == 3
    K, n4 = w.shape[-2], w.shape[-1]
    bm, bn, bk = _pick_rows(M, BM), _pick(K, BN), _pick(n4, BK)
    nb = n4 // bk
    if stacked:
        b_blk, b_map = (None, bn, bk), (lambda i, j, k: (k // nb, j, k % nb))
    else:
        b_blk, b_map = (bn, bk), (lambda i, j, k: (j, k))
    outs = [((M, K), out_dtype, (bm, bn), lambda i, j: (i, j))]
    exs = [(e, (bm, bn), lambda i, j: (i, j)) for e in extras]
    return _matmul(name, dy, w, (bm, bk), lambda i, j, k: (i, k), b_blk, b_map,
                   (((1,), (1,)), ((), ())), (M // bm, K // bn, N // bk), (bm, bn), outs, exs, epilogue, dep)[0]


def mm_tn(name, a, dy, out_dtype, stacked, dep=None):
    M, K = a.shape
    N = dy.shape[1]
    n4 = N // N_CHIPS if stacked else N
    bm, bn, bk = _pick(K, BM), _pick(n4, BN), _pick_rows(M, BK)
    nb = n4 // bn
    if stacked:
        outs = [((N_CHIPS, K, n4), out_dtype, (None, bm, bn), lambda i, j: (j // nb, i, j % nb))]
    else:
        outs = [((K, N), out_dtype, (bm, bn), lambda i, j: (i, j))]
    return _matmul(name, a, dy, (bk, bm), lambda i, j, k: (k, i), (bk, bn), lambda i, j, k: (k, j),
                   (((0,), (0,)), ((), ())), (K // bm, N // bn, M // bk), (bm, bn), outs, dep=dep)[0]


def _split3(x):
    hi = x.astype(BF16)
    r = x - hi.astype(F32)
    mid = r.astype(BF16)
    lo = (r - mid.astype(F32)).astype(BF16)
    return hi, mid, lo


def _exact_dot(x, m, n):
    out = None
    for p in _split3(x)[:n]:
        d = jnp.dot(p, m, preferred_element_type=F32)
        out = d if out is None else out + d
    return out


def _row_spec(tr, width, cmap):
    return pl.BlockSpec((tr, width), lambda i, h: (i, cmap(h)))


def _full_spec(p):
    return pl.BlockSpec(p.shape, lambda i, h: (0,) * p.ndim)


def rowwise_fwd(name, fn, rows, params, outs, tr, nh=1):
    S = rows[0][0].shape[0]
    nr, npar = len(rows), len(params)

    def body(*refs):
        vals = fn(*[r[...].astype(F32) for r in refs[:nr]], *[p[...] for p in refs[nr:nr + npar]])
        for o, v in zip(refs[nr + npar:], vals):
            o[...] = v.astype(o.dtype)

    return pl.pallas_call(
        body, name=name, grid=(S // tr, nh),
        in_specs=[_row_spec(tr, w, cm) for (_, w, cm) in rows] + [_full_spec(p) for p in params],
        out_specs=[_row_spec(tr, w, cm) for (_, w, cm, _) in outs],
        out_shape=[jax.ShapeDtypeStruct((S, c), d) for (c, _, _, d) in outs],
        compiler_params=_cp(("parallel", "arbitrary")),
    )(*[r[0] for r in rows], *params)


def rowwise_bwd(name, fn, rows, params, cts, grads, tr, nh=1, n_diff=None, add=None):
    S = rows[0][0].shape[0]
    nr, npar, nct = len(rows), len(params), len(cts)
    n_diff = nr if n_diff is None else n_diff
    n_add = 1 if add is not None else 0

    def body(*refs):
        row_refs = refs[:nr]
        par_refs = refs[nr:nr + npar]
        ct_refs = refs[nr + npar:nr + npar + nct]
        add_refs = refs[nr + npar + nct:nr + npar + nct + n_add]
        o = nr + npar + nct + n_add
        g_refs = refs[o:o + n_diff]
        cp_refs = refs[o + n_diff:o + n_diff + n_add]
        pg_refs = refs[o + n_diff + n_add:]
        i, h = pl.program_id(0), pl.program_id(1)
        rv = [r[...].astype(F32) for r in row_refs]
        pv = [p[...] for p in par_refs]
        aux = rv[n_diff:]

        def f(*dp):
            return fn(*dp[:n_diff], *aux, *dp[n_diff:])

        _, vjp = jax.vjp(f, *rv[:n_diff], *pv)
        gs = vjp(tuple(c[...].astype(F32) for c in ct_refs))
        for n, (g_ref, (_, _, _, _, over)) in enumerate(zip(g_refs, grads)):
            g = gs[n]
            if n == 0 and add is not None:
                g = g + add_refs[0][...]
                cp_refs[0][...] = g.astype(BF16)
            if over:
                @pl.when(h == 0)
                def _(g_ref=g_ref):
                    g_ref[...] = jnp.zeros_like(g_ref)
                g_ref[...] += g.astype(g_ref.dtype)
            else:
                g_ref[...] = g.astype(g_ref.dtype)
        for pg_ref, g in zip(pg_refs, gs[n_diff:]):
            @pl.when((i == 0) & (h == 0))
            def _(pg_ref=pg_ref):
                pg_ref[...] = jnp.zeros_like(pg_ref)
            pg_ref[...] += g

    in_specs = [_row_spec(tr, w, cm) for (_, w, cm) in rows] + [_full_spec(p) for p in params]
    in_specs += [_row_spec(tr, w, cm) for (_, w, cm) in cts]
    operands = [r[0] for r in rows] + list(params) + [c[0] for c in cts]
    out_specs = [_row_spec(tr, w, cm) for (_, w, cm, _, _) in grads]
    out_shape = [jax.ShapeDtypeStruct((S, c), d) for (c, _, _, d, _) in grads]
    if add is not None:
        in_specs.append(_row_spec(tr, add[1], add[2]))
        operands.append(add[0])
        out_specs.append(_row_spec(tr, add[1], add[2]))
        out_shape.append(jax.ShapeDtypeStruct(add[0].shape, BF16))
    out_specs += [_full_spec(p) for p in params]
    out_shape += [jax.ShapeDtypeStruct(p.shape, F32) for p in params]
    return pl.pallas_call(
        body, name=name, grid=(S // tr, nh), in_specs=in_specs, out_specs=out_specs, out_shape=out_shape,
        compiler_params=_cp(("arbitrary", "arbitrary")),
    )(*operands)


def _c0(h):
    return 0


def _rms(x, g, n):
    return x * lax.rsqrt(jnp.sum(x * x, axis=-1, keepdims=True) * (1.0 / n) + EPS) * g


def _rmsnorm_fn(d):
    def fn(x, g):
        return (_rms(x, g, d),)
    return fn


def rmsnorm_fwd(name, x, g):
    S, D = x.shape
    return rowwise_fwd(name, _rmsnorm_fn(D), [(x, D, _c0)], [g], [(D, D, _c0, BF16)], _pick_rows(S, 256))[0]


def rmsnorm_bwd(name, x, g, dh, dres):
    S, D = x.shape
    return rowwise_bwd(name, _rmsnorm_fn(D), [(x, D, _c0)], [g], [(dh, D, _c0)], [(D, D, _c0, F32, False)],
                       _pick_rows(S, 256), add=(dres, D, _c0))


def loss_head(y, t):
    S, D = y.shape
    tr = _pick_rows(S, 256)

    def body(y_ref, t_ref, l_ref, d_ref, db_ref):
        @pl.when(pl.program_id(0) == 0)
        def _():
            l_ref[...] = jnp.zeros_like(l_ref)
        e = y_ref[...] - t_ref[...]
        l_ref[...] += 0.5 * jnp.sum(jnp.sum(e * e, axis=1, keepdims=True), axis=0, keepdims=True) * (1.0 / D)
        d = e * (1.0 / D)
        d_ref[...] = d
        db_ref[...] = d.astype(BF16)

    row = pl.BlockSpec((tr, D), lambda i: (i, 0))
    return pl.pallas_call(
        body, name="loss_head", grid=(S // tr,), in_specs=[row, row],
        out_specs=[pl.BlockSpec((1, 1), lambda i: (0, 0)), row, row],
        out_shape=[jax.ShapeDtypeStruct((1, 1), F32), jax.ShapeDtypeStruct((S, D), F32), jax.ShapeDtypeStruct((S, D), BF16)],
        compiler_params=_cp(("arbitrary",)),
    )(y, t)


def _dot_nt(a, b):
    return lax.dot_general(a, b, (((1,), (1,)), ((), ())), preferred_element_type=F32)


def _dot_tn(a, b):
    return lax.dot_general(a, b, (((0,), (0,)), ((), ())), preferred_element_type=F32)


AT_TQ, AT_TK = 512, 256


def attn_fwd(name, q, k, v, nh, dq, qoff, koff, voff, vstep, scale, cf_col=None, cf_row=None):
    S = q.shape[0]
    tq = _pick_rows(S, AT_TQ)
    tk = _pick_rows(S, AT_TK)
    nq = S // tq
    bias = cf_col is not None

    def body(*refs):
        if bias:
            q_ref, k_ref, v_ref, cfc_ref, cfr_ref, o_ref, of_ref, lse_ref = refs
        else:
            q_ref, k_ref, v_ref, o_ref, of_ref, lse_ref = refs
        i = pl.program_id(1)
        qv = q_ref[...].astype(BF16)
        row = i * tq + lax.broadcasted_iota(jnp.int32, (tq, tk), 0)
        coli = lax.broadcasted_iota(jnp.int32, (tq, tk), 1)
        nkb = ((i + 1) * tq + tk - 1) // tk

        def logits(j):
            return _dot_nt(qv, k_ref[pl.ds(pl.multiple_of(j * tk, tk), tk), :].astype(BF16)) * scale

        def step(j, carry):
            s, m, l, acc, acc_lo = carry
            s_next = logits(jnp.minimum(j + 1, nkb - 1))
            vs = v_ref[pl.ds(pl.multiple_of(j * tk, tk), tk), :].astype(BF16)
            if bias:
                s = s + cfc_ref[0] - cfr_ref[0, pl.ds(j, 1), :]
            s = jnp.where(j * tk + coli <= row, s, NEG)
            m_new = jnp.maximum(m, jnp.max(s, axis=1, keepdims=True))
            alpha = jnp.exp(m - m_new)
            p = jnp.exp(s - m_new)
            pb = p.astype(BF16)
            l = alpha * l + jnp.sum(p, axis=1, keepdims=True)
            acc = alpha * acc + jnp.dot(pb, vs, preferred_element_type=F32)
            acc_lo = alpha * acc_lo + jnp.dot((p - pb.astype(F32)).astype(BF16), vs, preferred_element_type=F32)
            return s_next, m_new, l, acc, acc_lo

        z = jnp.zeros((tq, HEAD_DIM), F32)
        _, m, l, acc, acc_lo = lax.fori_loop(
            0, nkb, step, (logits(0), jnp.full((tq, 1), NEG, F32), jnp.zeros((tq, 1), F32), z, z))
        o_ref[...] = (acc / l).astype(o_ref.dtype)
        of_ref[...] = (acc + acc_lo) / l
        lse_ref[0] = m + jnp.log(l)

    in_specs = [pl.BlockSpec((tq, dq), lambda h, i: (i, qoff + h)),
                pl.BlockSpec((S, dq), lambda h, i: (0, koff + h)),
                pl.BlockSpec((S, HEAD_DIM), lambda h, i: (0, voff + vstep * h))]
    operands = [q, k, v]
    if bias:
        in_specs += [pl.BlockSpec((1, tq, 1), lambda h, i: (h, i, 0)), pl.BlockSpec((1, S // tk, tk), lambda h, i: (h, 0, 0))]
        operands += [cf_col, cf_row]
    return pl.pallas_call(
        body, name=name, grid=(nh, nq), in_specs=in_specs,
        out_specs=[pl.BlockSpec((tq, HEAD_DIM), lambda h, i: (i, h)), pl.BlockSpec((tq, HEAD_DIM), lambda h, i: (i, h)),
                   pl.BlockSpec((1, tq, 1), lambda h, i: (h, i, 0))],
        out_shape=[jax.ShapeDtypeStruct((S, nh * HEAD_DIM), BF16), jax.ShapeDtypeStruct((S, nh * HEAD_DIM), F32),
                   jax.ShapeDtypeStruct((nh, S, 1), F32)],
        compiler_params=_cp(("parallel", "arbitrary")),
    )(*operands)


def attn_bwd(name, q, k, v, o, do, lse, nh, dq, qoff, koff, voff, vstep, scale, cf_col=None, cf_row=None):
    S = q.shape[0]
    tq = _pick_rows(S, AT_TQ)
    tk = _pick_rows(S, AT_TK)
    nq = S // tq
    bias = cf_col is not None

    def body(*refs):
        if bias:
            q_ref, k_ref, v_ref, o_ref, do_ref, lse_ref, cfc_ref, cfr_ref, dq_ref, dk_ref, dv_ref, dcf_ref = refs
        else:
            q_ref, k_ref, v_ref, o_ref, do_ref, lse_ref, dq_ref, dk_ref, dv_ref = refs
        dk_ref[...] = jnp.zeros_like(dk_ref)
        dv_ref[...] = jnp.zeros_like(dv_ref)
        if bias:
            dcf_ref[...] = jnp.zeros_like(dcf_ref)
        rowi = lax.broadcasted_iota(jnp.int32, (tq, tk), 0)
        coli = lax.broadcasted_iota(jnp.int32, (tq, tk), 1)

        def outer(i, _):
            roff = pl.multiple_of(i * tq, tq)
            qv = q_ref[pl.ds(roff, tq), :].astype(BF16)
            dov = do_ref[pl.ds(roff, tq), :]
            delta = jnp.sum(dov.astype(F32) * o_ref[pl.ds(roff, tq), :].astype(F32), axis=1, keepdims=True)
            lse = lse_ref[0, pl.ds(roff, tq), :]
            if bias:
                cq = cfc_ref[0, pl.ds(roff, tq), :]

            nkb = ((i + 1) * tq + tk - 1) // tk

            def products(j):
                off = pl.multiple_of(j * tk, tk)
                return (_dot_nt(qv, k_ref[pl.ds(off, tk), :].astype(BF16)) * scale,
                        _dot_nt(dov, v_ref[pl.ds(off, tk), :].astype(BF16)))

            def inner(j, carry):
                s, dp, dq_acc = carry
                s_next, dp_next = products(jnp.minimum(j + 1, nkb - 1))
                off = pl.multiple_of(j * tk, tk)
                ks = k_ref[pl.ds(off, tk), :].astype(BF16)
                if bias:
                    s = s + cq - cfr_ref[0, pl.ds(j, 1), :]
                p = jnp.where(j * tk + coli <= i * tq + rowi, jnp.exp(s - lse), 0.0)
                ds = p * (dp - delta)
                if bias:
                    dcf_ref[0, pl.ds(j, 1), :] -= jnp.sum(ds, axis=0, keepdims=True)
                dsb = (ds * scale).astype(BF16)
                dv_ref[pl.ds(off, tk), :] += _dot_tn(p.astype(BF16), dov)
                dk_ref[pl.ds(off, tk), :] += _dot_tn(dsb, qv)
                return s_next, dp_next, dq_acc + jnp.dot(dsb, ks, preferred_element_type=F32)

            dq_ref[pl.ds(roff, tq), :] = lax.fori_loop(0, nkb, inner, products(0) + (jnp.zeros((tq, dq), F32),))[2]
            return 0

        lax.fori_loop(0, nq, outer, 0)

    in_specs = [pl.BlockSpec((S, dq), lambda h: (0, qoff + h)),
                pl.BlockSpec((S, dq), lambda h: (0, koff + h)),
                pl.BlockSpec((S, HEAD_DIM), lambda h: (0, voff + vstep * h)),
                pl.BlockSpec((S, HEAD_DIM), lambda h: (0, h)),
                pl.BlockSpec((S, HEAD_DIM), lambda h: (0, h)),
                pl.BlockSpec((1, S, 1), lambda h: (h, 0, 0))]
    operands = [q, k, v, o, do, lse]
    out_specs = [pl.BlockSpec((S, dq), lambda h: (0, h)), pl.BlockSpec((S, dq), lambda h: (0, h)),
                 pl.BlockSpec((S, HEAD_DIM), lambda h: (0, h))]
    out_shape = [jax.ShapeDtypeStruct((S, nh * dq), F32), jax.ShapeDtypeStruct((S, nh * dq), F32),
                 jax.ShapeDtypeStruct((S, nh * HEAD_DIM), F32)]
    if bias:
        in_specs += [pl.BlockSpec((1, S, 1), lambda h: (h, 0, 0)), pl.BlockSpec((1, S // tk, tk), lambda h: (h, 0, 0))]
        operands += [cf_col, cf_row]
        out_specs.append(pl.BlockSpec((1, S // tk, tk), lambda h: (h, 0, 0)))
        out_shape.append(jax.ShapeDtypeStruct((nh, S // tk, tk), F32))
    return pl.pallas_call(
        body, name=name, grid=(nh,), in_specs=in_specs, out_specs=out_specs, out_shape=out_shape,
        compiler_params=_cp(("parallel",)),
    )(*operands)


SB_TQ, SB_TK = 512, 256


def _sb_block(z, row, col, later_c, tri_after):
    strict = col < row
    lsn = jnp.minimum(-z, 0.0) - jnp.log(1.0 + jnp.exp(-jnp.abs(z)))
    lsp = lsn + z
    L = jnp.where(strict, lsn, 0.0)
    later = _exact_dot(L, tri_after, 2) + later_c
    a = jnp.where(strict, jnp.exp(lsp + later), 0.0)
    return strict, lsp, lsn, L, a


def _tri(n, after_strict):
    r = lax.broadcasted_iota(jnp.int32, (n, n), 0)
    c = lax.broadcasted_iota(jnp.int32, (n, n), 1)
    return jnp.where(r > c if after_strict else r >= c, 1.0, 0.0).astype(BF16)


def sb_fwd(name, qkv, nh):
    S = qkv.shape[0]
    tq = _pick_rows(S, SB_TQ)
    tk = _pick_rows(S, SB_TK)
    nq = S // tq
    scale = 1.0 / math.sqrt(HEAD_DIM)

    def body(q_ref, k_ref, v_ref, o_ref, of_ref):
        i = pl.program_id(1)
        qv = q_ref[...]
        tri = _tri(tk, True)
        row = i * tq + lax.broadcasted_iota(jnp.int32, (tq, tk), 0)
        coli = lax.broadcasted_iota(jnp.int32, (tq, tk), 1)

        def logits(j):
            return _dot_nt(qv, k_ref[pl.ds(pl.multiple_of(j * tk, tk), tk), :]) * scale

        nkb = ((i + 1) * tq + tk - 1) // tk

        def step(jj, carry):
            z, later_c, acc, acc_lo = carry
            j = nkb - 1 - jj
            z_next = logits(jnp.maximum(j - 1, 0))
            vb = v_ref[pl.ds(pl.multiple_of(j * tk, tk), tk), :]
            _, _, _, L, a = _sb_block(z, row, j * tk + coli, later_c, tri)
            ab = a.astype(BF16)
            acc = acc + jnp.dot(ab, vb, preferred_element_type=F32)
            acc_lo = acc_lo + jnp.dot((a - ab.astype(F32)).astype(BF16), vb, preferred_element_type=F32)
            return z_next, later_c + jnp.sum(L, axis=1, keepdims=True), acc, acc_lo

        zero = jnp.zeros((tq, HEAD_DIM), F32)
        _, _, acc, acc_lo = lax.fori_loop(0, nkb, step, (logits(nkb - 1), jnp.zeros((tq, 1), F32), zero, zero))
        o_ref[...] = acc.astype(o_ref.dtype)
        of_ref[...] = acc + acc_lo

    blk = pl.BlockSpec((tq, HEAD_DIM), lambda h, i: (i, h))
    return pl.pallas_call(
        body, name=name, grid=(nh, nq),
        in_specs=[blk,
                  pl.BlockSpec((S, HEAD_DIM), lambda h, i: (0, nh + h)),
                  pl.BlockSpec((S, HEAD_DIM), lambda h, i: (0, 2 * nh + h))],
        out_specs=[blk, blk],
        out_shape=[jax.ShapeDtypeStruct((S, nh * HEAD_DIM), BF16), jax.ShapeDtypeStruct((S, nh * HEAD_DIM), F32)],
        compiler_params=_cp(("parallel", "arbitrary")),
    )(qkv, qkv, qkv)


def sb_bwd(name, qkv, o, do, nh):
    S = qkv.shape[0]
    tq = _pick_rows(S, SB_TQ)
    tk = _pick_rows(S, SB_TK)
    nq = S // tq
    scale = 1.0 / math.sqrt(HEAD_DIM)

    def body(q_ref, k_ref, v_ref, o_ref, do_ref, dq_ref, dk_ref, dv_ref, dk_acc, dv_acc):
        dk_acc[...] = jnp.zeros_like(dk_acc)
        dv_acc[...] = jnp.zeros_like(dv_acc)
        tri = _tri(tk, True)
        tri_inc = _tri(tk, False)
        rowi = lax.broadcasted_iota(jnp.int32, (tq, tk), 0)
        coli = lax.broadcasted_iota(jnp.int32, (tq, tk), 1)

        def outer(i, _):
            roff = pl.multiple_of(i * tq, tq)
            qv = q_ref[pl.ds(roff, tq), :]
            dov = do_ref[pl.ds(roff, tq), :]
            dtot = jnp.sum(dov.astype(F32) * o_ref[pl.ds(roff, tq), :].astype(F32), axis=1, keepdims=True)
            row = i * tq + rowi

            def products(j):
                off = pl.multiple_of(j * tk, tk)
                return _dot_nt(qv, k_ref[pl.ds(off, tk), :]) * scale, _dot_nt(dov, v_ref[pl.ds(off, tk), :])

            nkb = ((i + 1) * tq + tk - 1) // tk

            def inner(jj, carry):
                z, da, later_c, suf_c, dq_acc = carry
                j = nkb - 1 - jj
                z_next, da_next = products(jnp.maximum(j - 1, 0))
                off = pl.multiple_of(j * tk, tk)
                kb = k_ref[pl.ds(off, tk), :]
                strict, lsp, lsn, L, a = _sb_block(z, row, j * tk + coli, later_c, tri)
                dl = da * a
                before = dtot - (suf_c + _exact_dot(dl, tri_inc, 3))
                dz = jnp.where(strict, dl * jnp.exp(lsn) - jnp.exp(lsp) * before, 0.0) * scale
                dzb = dz.astype(BF16)
                dv_acc[pl.ds(off, tk), :] += _dot_tn(a.astype(BF16), dov)
                dk_acc[pl.ds(off, tk), :] += _dot_tn(dzb, qv)
                return (z_next, da_next, later_c + jnp.sum(L, axis=1, keepdims=True), suf_c + jnp.sum(dl, axis=1, keepdims=True),
                        dq_acc + jnp.dot(dzb, kb, preferred_element_type=F32))

            z1 = jnp.zeros((tq, 1), F32)
            res = lax.fori_loop(0, nkb, inner, products(nkb - 1) + (z1, z1, jnp.zeros((tq, HEAD_DIM), F32)))
            dq_ref[pl.ds(roff, tq), :] = res[4].astype(dq_ref.dtype)
            return 0

        lax.fori_loop(0, nq, outer, 0)
        dk_ref[...] = dk_acc[...].astype(dk_ref.dtype)
        dv_ref[...] = dv_acc[...].astype(dv_ref.dtype)

    def col(off):
        return pl.BlockSpec((S, HEAD_DIM), lambda h: (0, off + h))

    dq, dk, dv = pl.pallas_call(
        body, name=name, grid=(nh,),
        in_specs=[col(0), col(nh), col(2 * nh), col(0), col(0)],
        out_specs=[col(0), col(0), col(0)],
        out_shape=[jax.ShapeDtypeStruct((S, nh * HEAD_DIM), BF16)] * 3,
        scratch_shapes=[pltpu.VMEM((S, HEAD_DIM), F32), pltpu.VMEM((S, HEAD_DIM), F32)],
        compiler_params=_cp(("parallel",)),
    )(qkv, qkv, qkv, o, do)
    return jnp.concatenate([dq, dk, dv], axis=1)


def seq_cumsum(name, x, reverse):
    S, W = x.shape

    tb = _pick_rows(S, 256)

    def body(x_ref, o_ref):
        parts = _split3(x_ref[...])
        c = lax.broadcasted_iota(jnp.int32, (tb, S), 1)
        for b in range(S // tb):
            r = b * tb + lax.broadcasted_iota(jnp.int32, (tb, S), 0)
            t = jnp.where(r <= c if reverse else r >= c, 1.0, 0.0).astype(BF16)
            out = None
            for p in parts:
                d = jnp.dot(t, p, preferred_element_type=F32)
                out = d if out is None else out + d
            o_ref[b * tb:(b + 1) * tb, :] = out

    return pl.pallas_call(body, name=name, out_shape=jax.ShapeDtypeStruct((S, W), F32),
                          compiler_params=pltpu.CompilerParams(vmem_limit_bytes=VMEM_LIMIT))(x)


def _adamw_math(w, g, m, v):
    c1 = 1.0 - ADAM_B1 ** ADAM_STEP
    c2 = 1.0 - ADAM_B2 ** ADAM_STEP
    nm = ADAM_B1 * m + (1.0 - ADAM_B1) * g
    nv = ADAM_B2 * v + (1.0 - ADAM_B2) * (g * g)
    return -ADAM_LR * ((nm / c1) / (jnp.sqrt(nv / c2) + ADAM_EPS) + ADAM_WD * w), nm, nv


def adamw(name, w, g, m, v):
    R, C = g.shape
    tr = _pick_rows(R, 256)

    def body(w_ref, g_ref, m_ref, v_ref, d_ref, nm_ref, nv_ref):
        d_ref[...], nm_ref[...], nv_ref[...] = _adamw_math(w_ref[...], g_ref[...], m_ref[...], v_ref[...])

    row = pl.BlockSpec((tr, C), lambda i: (i, 0))
    return pl.pallas_call(
        body, name=name, grid=(R // tr,), in_specs=[row] * 4, out_specs=[row] * 3,
        out_shape=[jax.ShapeDtypeStruct((R, C), F32)] * 3, compiler_params=_cp(("parallel",)),
    )(w, g, m, v)


def adamw_big(name, w, m, v, mine, other, c_arr, layer, prev, dep=None):
    L, R, C = w.shape
    r2 = R // 2
    tr = _pick_rows(r2, 256)
    nb = r2 // tr

    def body(c_ref, w_ref, m_ref, v_ref, mine_ref, other_ref, *rest):
        g_ref, d_ref, nm_ref, nv_ref = rest[-4:]
        g = jnp.where(pl.program_id(0) == c_ref[0], mine_ref[...], other_ref[...])
        g_ref[...] = g
        d_ref[...], nm_ref[...], nv_ref[...] = _adamw_math(w_ref[...], g, m_ref[...], v_ref[...])

    sel = pl.BlockSpec((None, tr, C), lambda hf, i, c: (layer, hf * nb + i, 0))
    in_specs = [sel, sel, sel,
                pl.BlockSpec((tr, C), lambda hf, i, c: (jnp.where(hf == c[0], i, 0), 0)),
                pl.BlockSpec((tr, C), lambda hf, i, c: (jnp.where(hf == c[0], 0, i), 0))]
    operands = [c_arr, w, m, v, mine, other]
    aliases = {}
    if prev is not None:
        in_specs += [ANY] * 4
        aliases = {len(operands) + k: k for k in range(4)}
        operands += list(prev)
    if dep is not None:
        in_specs += [ANY]
        operands += [dep]
    return pl.pallas_call(
        body, name=name,
        grid_spec=pltpu.PrefetchScalarGridSpec(num_scalar_prefetch=1, grid=(2, nb), in_specs=in_specs, out_specs=[sel] * 4),
        out_shape=[jax.ShapeDtypeStruct((L, R, C), F32)] * 4, input_output_aliases=aliases,
        compiler_params=_cp(("arbitrary", "arbitrary")),
    )(*operands)


ANY = pl.BlockSpec(memory_space=pl.ANY)


def _place():
    x, y, c = lax.axis_index("x"), lax.axis_index("y"), lax.axis_index("c")
    others = [(1 - x, y), (x, 1 - y), (1 - x, 1 - y)]
    return x, y, c, others


def swap_halves(name, grads):
    n = len(grads)

    def body(*refs):
        src, dst = refs[:n], refs[n:2 * n]
        send, recv = refs[2 * n:]
        x, y, c, _ = _place()
        cps = []
        for t in range(n):
            r2 = grads[t].shape[1] // 2
            cps.append(pltpu.make_async_remote_copy(
                src_ref=src[t].at[:, pl.ds((1 - c) * r2, r2), :], dst_ref=dst[t],
                send_sem=send.at[t], recv_sem=recv.at[t], device_id=(x, y, 1 - c), device_id_type=MESH))
        for cp in cps:
            cp.start()
        for cp in cps:
            cp.wait()

    return pl.pallas_call(
        body, name=name, in_specs=[ANY] * n, out_specs=[ANY] * n,
        out_shape=[jax.ShapeDtypeStruct((N_CHIPS, g.shape[1] // 2, g.shape[2]), g.dtype) for g in grads],
        scratch_shapes=[pltpu.SemaphoreType.DMA((n,))] * 2,
    )(*grads)


def add_half(name, g, other, c_arr):
    _, R, C = g.shape
    r2 = R // 2
    tr = _pick_rows(r2, 512)
    nb = r2 // tr

    def body(c_ref, g_ref, o_ref, out_ref):
        out_ref[...] = (g_ref[...].astype(F32) + o_ref[...].astype(F32)).astype(out_ref.dtype)

    return pl.pallas_call(
        body, name=name,
        grid_spec=pltpu.PrefetchScalarGridSpec(
            num_scalar_prefetch=1, grid=(N_CHIPS, nb),
            in_specs=[pl.BlockSpec((None, tr, C), lambda s, i, c: (s, c[0] * nb + i, 0)),
                      pl.BlockSpec((None, tr, C), lambda s, i, c: (s, i, 0))],
            out_specs=pl.BlockSpec((None, tr, C), lambda s, i, c: (s, i, 0))),
        out_shape=jax.ShapeDtypeStruct((N_CHIPS, r2, C), BF16),
        compiler_params=_cp(("parallel", "parallel")),
    )(c_arr, g, other)


def sum_slabs(name, got, parts, chip_arr):
    _, r2, C = parts.shape
    tr = _pick_rows(r2, 512)

    def body(chip_ref, *refs):
        own_ref, out_ref = refs[N_CHIPS], refs[N_CHIPS + 1]
        acc = None
        for s in range(N_CHIPS):
            v = jnp.where(chip_ref[0] == s, own_ref[...], refs[s][...]).astype(F32)
            acc = v if acc is None else acc + v
        out_ref[...] = acc

    def slab(s):
        return pl.BlockSpec((None, tr, C), lambda i, ch: (jnp.where(ch[0] == s, (s + 1) % N_CHIPS, s), i, 0))

    return pl.pallas_call(
        body, name=name,
        grid_spec=pltpu.PrefetchScalarGridSpec(
            num_scalar_prefetch=1, grid=(r2 // tr,),
            in_specs=[slab(s) for s in range(N_CHIPS)] + [pl.BlockSpec((None, tr, C), lambda i, ch: (ch[0], i, 0))],
            out_specs=pl.BlockSpec((tr, C), lambda i, ch: (i, 0))),
        out_shape=jax.ShapeDtypeStruct((r2, C), F32), compiler_params=_cp(("parallel",)),
    )(chip_arr, got, got, got, got, parts)


def send_halves(name, halves):
    n = len(halves)

    def body(*refs):
        src, dst = refs[:n], refs[n:2 * n]
        send, recv = refs[2 * n:]
        x, y, c, _ = _place()
        cps = [pltpu.make_async_remote_copy(src_ref=src[t], dst_ref=dst[t], send_sem=send.at[t], recv_sem=recv.at[t],
                                            device_id=(x, y, 1 - c), device_id_type=MESH) for t in range(n)]
        for cp in cps:
            cp.start()
        for cp in cps:
            cp.wait()

    return pl.pallas_call(
        body, name=name, in_specs=[ANY] * n, out_specs=[ANY] * n,
        out_shape=[jax.ShapeDtypeStruct(h.shape, h.dtype) for h in halves],
        scratch_shapes=[pltpu.SemaphoreType.DMA((n,))] * 2,
    )(*halves)


HBM = pl.BlockSpec(memory_space=pltpu.HBM)
SEM = pl.BlockSpec(memory_space=pltpu.SEMAPHORE)
EFFECT = pltpu.SideEffectType.DATAFLOW_SIDE_EFFECTING


def _in_hbm(a):
    return pltpu.with_memory_space_constraint(a, pltpu.HBM)


def _chip_copies(kind, shapes, src, land, send, recv, mine):
    x, y, c, others = _place()
    me = 2 * x + y
    cps = []

    def remote(s_ref, d_ref, k, to):
        cps.append(pltpu.make_async_remote_copy(src_ref=s_ref, dst_ref=d_ref, send_sem=send.at[k], recv_sem=recv.at[k],
                                                device_id=to, device_id_type=MESH))

    for t, shape in enumerate(shapes):
        if kind == "swap":
            r2 = shape[1] // 2
            remote(src[t].at[:, pl.ds((1 - c) * r2, r2), :], land[t], t, (x, y, 1 - c))
        elif kind == "whole":
            remote(src[t], land[t], t, (x, y, 1 - c))
        for j, (px, py) in enumerate(others if kind in ("gather", "scatter", "pass") else []):
            slot = me if mine else 2 * px + py
            if kind == "gather":
                r2 = shape[0] // 2
                rows = pl.ds(c * r2, r2)
                remote(src[t].at[rows, :], land[t].at[slot, rows, :], 3 * t + j, (px, py, c))
            elif kind == "scatter":
                remote(src[t].at[2 * px + py], land[t].at[slot], 3 * t + j, (px, py, c))
            else:
                r2 = shape[1] // 2
                rows = src[t].at[2 * px + py, pl.ds((c if mine else 1 - c) * r2, r2), :]
                remote(rows, rows, 3 * t + j, (x, y, 1 - c))
    return cps


def _n_copies(kind, n):
    return n if kind in ("swap", "whole") else 3 * n


def chips_start(name, kind, srcs, land_shapes, after=None):
    n, nl = len(srcs), len(land_shapes)
    shapes = [s.shape for s in srcs]
    n_buf = n + nl
    n_in = n_buf + (0 if after is None else 1)
    n_sem = _n_copies(kind, n)

    def body(*refs):
        src, land = refs[:n], refs[n:n_buf]
        send, recv = refs[n_in], refs[n_in + 1]
        token = refs[-1]
        for cp in _chip_copies(kind, shapes, src, land, send, recv, True):
            cp.start()
        token[...] = jnp.zeros_like(token)

    lands = [lax.empty(s, srcs[0].dtype) for s in land_shapes]
    out = pl.pallas_call(
        body, name=name,
        out_shape=(pltpu.SemaphoreType.DMA((n_sem,)), pltpu.SemaphoreType.DMA((n_sem,)))
        + tuple(pltpu.HBM(s.shape, s.dtype) for s in srcs) + tuple(pltpu.HBM(l.shape, l.dtype) for l in lands)
        + (jax.ShapeDtypeStruct((8, LANE), F32),),
        in_specs=(HBM,) * n_buf + (ANY,) * (n_in - n_buf),
        out_specs=(SEM, SEM) + (HBM,) * n_buf + (pl.BlockSpec(memory_space=pltpu.VMEM),),
        input_output_aliases={k: 2 + k for k in range(n_buf)},
        compiler_params=pltpu.CompilerParams(has_side_effects=EFFECT),
    )(*[_in_hbm(s) for s in srcs], *[_in_hbm(l) for l in lands], *([] if after is None else [after]))
    return (kind, shapes, out[0], out[1], out[2:2 + n], out[2 + n:2 + n_buf]), out[-1][:1, :1]


def chips_wait(name, handle, after):
    kind, shapes, send, recv, srcs, lands = handle
    n, n_buf = len(srcs), len(srcs) + len(lands)
    after = [a for a in (list(after) if isinstance(after, (list, tuple)) else [after]) if a is not None]

    def body(*refs):
        src, land = refs[:n], refs[n:n_buf]
        for cp in _chip_copies(kind, shapes, src, land, refs[n_buf], refs[n_buf + 1], True):
            cp.wait_send()
        for cp in _chip_copies(kind, shapes, src, land, refs[n_buf], refs[n_buf + 1], False):
            cp.wait_recv()

    out = pl.pallas_call(
        body, name=name,
        out_shape=tuple(pltpu.HBM(s.shape, s.dtype) for s in srcs) + tuple(pltpu.HBM(l.shape, l.dtype) for l in lands),
        in_specs=(HBM,) * n_buf + (SEM, SEM) + (ANY,) * len(after), out_specs=(HBM,) * n_buf,
        input_output_aliases={k: k for k in range(n_buf)},
        compiler_params=pltpu.CompilerParams(has_side_effects=EFFECT),
    )(*srcs, *lands, send, recv, *after)
    return list(out[n:]), list(out[:n])


def pass_to_sibling(name, lands, shards):
    n = len(lands)

    def body(*refs):
        buf = refs[n:2 * n]
        send, recv = refs[2 * n:]
        x, y, c, others = _place()

        def cp(t, j, core):
            r2 = lands[t].shape[1] // 2
            px, py = others[j]
            rows = buf[t].at[2 * px + py, pl.ds(core * r2, r2), :]
            return pltpu.make_async_remote_copy(src_ref=rows, dst_ref=rows, send_sem=send.at[t, j], recv_sem=recv.at[t, j],
                                                device_id=(x, y, 1 - c), device_id_type=MESH)

        for t in range(n):
            for j in range(3):
                cp(t, j, c).start()
        for t in range(n):
            for j in range(3):
                cp(t, j, 1 - c).wait_recv()
        for t in range(n):
            for j in range(3):
                cp(t, j, c).wait_send()

    got = pl.pallas_call(
        body, name=name, in_specs=[ANY] * n, out_specs=[ANY] * n,
        out_shape=[jax.ShapeDtypeStruct(l.shape, l.dtype) for l in lands],
        input_output_aliases={k: k for k in range(n)},
        scratch_shapes=[pltpu.SemaphoreType.DMA((n, 3))] * 2,
    )(*lands)
    chip = 2 * lax.axis_index("x") + lax.axis_index("y")
    return [lax.dynamic_update_slice(g, s[None], (chip, 0, 0)) for g, s in zip(got, shards)]


def all_sum_small(name, v, dep=None):
    R = v.shape[0]
    n_dep = 0 if dep is None else 1

    def body(v_ref, *refs):
        out_ref, slots, send, recv = refs[n_dep:]
        x, y, c, _ = _place()
        me = 4 * x + 2 * y + c
        slots[me] = v_ref[...]
        cps = []
        for k in range(1, 8):
            dx, dy, dc = (k >> 2) & 1, (k >> 1) & 1, k & 1
            to = (x ^ dx, y ^ dy, c ^ dc)
            cps.append(pltpu.make_async_remote_copy(
                src_ref=v_ref, dst_ref=slots.at[me], send_sem=send.at[k - 1], recv_sem=recv.at[k - 1],
                device_id=to, device_id_type=MESH))
        for cp in cps:
            cp.start()
        for k in range(1, 8):
            dx, dy, dc = (k >> 2) & 1, (k >> 1) & 1, k & 1
            frm = 4 * (x ^ dx) + 2 * (y ^ dy) + (c ^ dc)
            pltpu.make_async_remote_copy(
                src_ref=v_ref, dst_ref=slots.at[frm], send_sem=send.at[k - 1], recv_sem=recv.at[k - 1],
                device_id=(x, y, c), device_id_type=MESH).wait_recv()
        for cp in cps:
            cp.wait_send()
        acc = slots[0]
        for d in range(1, 8):
            acc = acc + slots[d]
        out_ref[...] = acc

    vm = pl.BlockSpec(memory_space=pltpu.VMEM)
    return pl.pallas_call(
        body, name=name, in_specs=[vm] + [ANY] * n_dep, out_specs=vm, out_shape=jax.ShapeDtypeStruct((R, LANE), F32),
        scratch_shapes=[pltpu.VMEM((8, R, LANE), F32), pltpu.SemaphoreType.DMA((7,)), pltpu.SemaphoreType.DMA((7,))],
    )(v, *([] if dep is None else [dep]))


def _rope_mat(n, lo):
    half = MLA_ROPE // 2
    r = lax.broadcasted_iota(jnp.int32, (n, n), 0)
    c = lax.broadcasted_iota(jnp.int32, (n, n), 1)
    plus = (c >= lo + half) & (c < lo + 2 * half) & (r == c - half)
    minus = (c >= lo) & (c < lo + half) & (r == c + half)
    return (jnp.where(plus, 1.0, 0.0) - jnp.where(minus, 1.0, 0.0)).astype(BF16)


def _rope(v, cos, sin, lo):
    return v * cos + _exact_dot(v, _rope_mat(v.shape[-1], lo), 3) * sin


def rope_tables(pos):
    S = pos.shape[0]
    half = MLA_ROPE // 2
    inv = (np.float32(ROPE_THETA) ** (-np.arange(0, half, dtype=np.float32) * np.float32(2.0 / MLA_ROPE))).astype(np.float32)
    f1 = np.zeros((1, LANE), np.float32)
    f1[0, :MLA_ROPE] = np.tile(inv, 2)
    f2 = np.zeros((1, MLA_QK_PAD), np.float32)
    f2[0, MLA_NOPE:MLA_QK] = np.tile(inv, 2)
    tr = _pick_rows(S, 256)

    def body(p_ref, f1_ref, f2_ref, c1, s1, c2, s2):
        p = p_ref[...].astype(F32)
        a1 = p * f1_ref[...]
        a2 = p * f2_ref[...]
        c1[...] = jnp.cos(a1)
        s1[...] = jnp.sin(a1)
        c2[...] = jnp.cos(a2)
        s2[...] = jnp.sin(a2)

    def row(w):
        return pl.BlockSpec((tr, w), lambda i: (i, 0))

    def full(w):
        return pl.BlockSpec((1, w), lambda i: (0, 0))

    return pl.pallas_call(
        body, name="rope_tables", grid=(S // tr,), in_specs=[row(1), full(LANE), full(MLA_QK_PAD)],
        out_specs=[row(LANE), row(LANE), row(MLA_QK_PAD), row(MLA_QK_PAD)],
        out_shape=[jax.ShapeDtypeStruct((S, LANE), F32)] * 2 + [jax.ShapeDtypeStruct((S, MLA_QK_PAD), F32)] * 2,
        compiler_params=_cp(("parallel",)),
    )(pos, jnp.asarray(f1), jnp.asarray(f2))


def _log_sigmoid(z):
    return jnp.minimum(z, 0.0) - jnp.log(1.0 + jnp.exp(-jnp.abs(z)))


def _fox_qk_fn(q, k, gq, gk):
    return _rms(q, gq, HEAD_DIM), _rms(k, gk, HEAD_DIM)


def _fox_gate_fn(f, b):
    return (_log_sigmoid(f + b),)


def _mla_pre1_fn(q_rank, kv_rank):
    def fn(cq, ckv, kr, cos, sin, qn, kvn):
        return _rms(cq, qn, q_rank), _rms(ckv, kvn, kv_rank), _rope(kr, cos, sin, 0)
    return fn


def _mla_pre2_fn(qb, kn, kr, cos, sin, gq, gk):
    qh = _rms(_rope(qb, cos, sin, MLA_NOPE), gq, MLA_QK)
    kh = _rms(jnp.concatenate([kn, kr], axis=1), gk, MLA_QK)
    return qh, kh


def _pad_cols(a, n):
    return jnp.pad(a, ((0, 0), (0, n - a.shape[1])))


def _relu2(acc):
    r = jnp.maximum(acc, 0.0)
    return r * r, r


def _add(acc, res):
    return (acc + res,)


def _times_2r(acc, r):
    return (acc * (2.0 * r.astype(F32)),)


def kernel(x, positions, mix_norm, mlp_norm, sb_w_in, sb_w_out, fox_w_in, fox_b_f, fox_q_gain, fox_k_gain, fox_w_out, mla_w_in, mla_q_norm, mla_kv_norm, mla_w_uq, mla_w_ukv, mla_q_gain, mla_k_gain, mla_w_out, mlp_w1, mlp_w2, loss_target, m_mix_norm, m_mlp_norm, m_sb_w_in, m_sb_w_out, m_fox_w_in, m_fox_b_f, m_fox_q_gain, m_fox_k_gain, m_fox_w_out, m_mla_w_in, m_mla_q_norm, m_mla_kv_norm, m_mla_w_uq, m_mla_w_ukv, m_mla_q_gain, m_mla_k_gain, m_mla_w_out, m_mlp_w1, m_mlp_w2, v_mix_norm, v_mlp_norm, v_sb_w_in, v_sb_w_out, v_fox_w_in, v_fox_b_f, v_fox_q_gain, v_fox_k_gain, v_fox_w_out, v_mla_w_in, v_mla_q_norm, v_mla_kv_norm, v_mla_w_uq, v_mla_w_ukv, v_mla_q_gain, v_mla_k_gain, v_mla_w_out, v_mlp_w1, v_mlp_w2):
    S, D = x.shape[1], x.shape[2]
    nh = D // HEAD_DIM
    W = nh * HEAD_DIM
    depth = mix_norm.shape[0]
    q_rank, kv_rank = mla_w_uq.shape[1], mla_w_ukv.shape[1]
    n_fox_in = 3 * W + nh
    fox_pad = -(-n_fox_in // LANE) * LANE
    n_down = q_rank + kv_rank + MLA_ROPE
    down_pad = q_rank + kv_rank + LANE
    tr = _pick_rows(S, 256)
    tk = _pick_rows(S, 256)

    ax, ay, ac = lax.axis_index("x"), lax.axis_index("y"), lax.axis_index("c")
    chip = 2 * ax + ay
    c_arr = jnp.reshape(ac, (1,)).astype(jnp.int32)
    chip_arr = jnp.reshape(chip, (1,)).astype(jnp.int32)

    def bf(a):
        return a.astype(BF16)

    n_small_in = q_rank + kv_rank
    rows_in = -(-n_small_in // (8 * LANE)) * 8
    placed = jnp.zeros((rows_in * LANE,), F32)
    placed = lax.dynamic_update_slice(placed, mla_q_norm[0], (chip * mla_q_norm.shape[1],))
    placed = lax.dynamic_update_slice(placed, mla_kv_norm[0], (q_rank + chip * mla_kv_norm.shape[1],))
    placed = placed * (ac == 0).astype(F32)
    norms = all_sum_small("gather_norms", placed.reshape(rows_in, LANE)).reshape(-1)
    q_norm_full = norms[:q_rank].reshape(1, q_rank)
    kv_norm_full = norms[q_rank:q_rank + kv_rank].reshape(1, kv_rank)

    groups = [(i, part) for i in range(depth) for part in ("mix", "mlp")]
    G = {}

    def group_shards(i, part):
        kind, j = i % N_MIXERS, i // N_MIXERS
        if part == "mlp":
            return ["w1", "w2"], [bf(mlp_w1[i]), bf(mlp_w2[i])]
        if kind == 0:
            return ["w_in", "w_out"], [bf(sb_w_in[j]), bf(sb_w_out[j])]
        if kind == 1:
            return ["w_in", "w_out"], [bf(fox_w_in[j]), bf(fox_w_out[j])]
        return ["w_in", "w_uq", "w_ukv", "w_out"], [bf(mla_w_in[j]), bf(mla_w_uq[j]), bf(mla_w_ukv[j]), bf(mla_w_out[j])]

    def cross_chips_start(key, after):
        names, shards = group_shards(*key)
        handle, token = chips_start(f"gather_start_{key[1]}_{key[0]}", "gather", shards,
                                    [(N_CHIPS,) + s.shape for s in shards], after)
        G[key] = dict(names=names, ici=handle, token=token)
        return token

    def cross_cores_start(key, after):
        g = G[key]
        g["lands"], g["shards"] = chips_wait(f"gather_wait_{key[1]}_{key[0]}", g["ici"], after)
        g["d2d"], token = chips_start(f"pass_start_{key[1]}_{key[0]}", "pass", g["lands"], [])
        return token

    def group_ready(key, after):
        g = G[key]
        _, bufs = chips_wait(f"pass_wait_{key[1]}_{key[0]}", g["d2d"], after)
        return dict(zip(g["names"], [lax.dynamic_update_slice(b, s[None], (chip, 0, 0)) for b, s in zip(bufs, g["shards"])]))

    def before_part(s, after):
        key = groups[s]
        if key not in G:
            cross_chips_start(key, norms)
        if "d2d" not in G[key]:
            cross_cores_start(key, after)
        L = group_ready(key, after)
        token = jnp.zeros((1, 1), F32)
        nxt = groups[s + 1] if s + 1 < len(groups) else None
        if nxt is not None:
            if nxt in G:
                token = token + cross_cores_start(nxt, after)
            else:
                token = token + cross_chips_start(nxt, L[G[key]["names"][0]])
        if s + 2 < len(groups):
            dep = G[nxt]["lands"][0] if "lands" in G[nxt] else G[nxt]["token"]
            token = token + cross_chips_start(groups[s + 2], dep)
        return L, token

    def rows_stacked(w):
        return w.reshape(w.shape[0] * w.shape[1], w.shape[2])

    xc = x[0]
    saved = []
    layers = []
    tables = None
    for i in range(depth):
        kind, j = i % N_MIXERS, i // N_MIXERS
        L, token = before_part(2 * i, xc if i else None)
        g1 = mix_norm[i:i + 1] + token
        hb = rmsnorm_fwd(f"norm1_{i}", xc, g1)
        st = dict(x=xc, hb=hb)
        if kind == 0:
            qkv = mm_nn(f"sb_proj_{i}", hb, L["w_in"], BF16)[0]
            o, o_f32 = sb_fwd(f"sb_attn_{i}", qkv, nh)
            st.update(qkv=qkv, o_f32=o_f32)
        elif kind == 1:
            w_in = _pad_cols(jnp.concatenate([L["w_in"][s] for s in range(N_CHIPS)], axis=1), fox_pad)
            proj = mm_nn(f"fox_proj_{i}", hb, w_in, F32)[0]
            gq, gk = fox_q_gain[j:j + 1], fox_k_gain[j:j + 1]
            qk_rows = [(proj, HEAD_DIM, lambda h: h), (proj, HEAD_DIM, lambda h: nh + h)]
            qh, kh = rowwise_fwd(f"fox_qk_{i}", _fox_qk_fn, qk_rows, [gq, gk],
                                 [(W, HEAD_DIM, lambda h: h, BF16)] * 2, tr, nh)
            b_pad = _pad_cols(fox_b_f[j:j + 1], LANE)
            gate_rows = [(proj, LANE, lambda h: 3 * nh)]
            logf = rowwise_fwd(f"fox_gate_{i}", _fox_gate_fn, gate_rows, [b_pad], [(LANE, LANE, _c0, F32)], tr)[0]
            cf = seq_cumsum(f"fox_cf_{i}", logf, False)[:, :nh].T
            cf_col, cf_row = cf.reshape(nh, S, 1), cf.reshape(nh, S // tk, tk)
            scale = 1.0 / math.sqrt(HEAD_DIM)
            o, o_f32, lse = attn_fwd(f"fox_attn_{i}", qh, kh, proj, nh, HEAD_DIM, 0, 0, 2 * nh, 1, scale, cf_col, cf_row)
            st.update(w_in=w_in, proj=proj, qk_rows=qk_rows, gq=gq, gk=gk, qh=qh, kh=kh, b_pad=b_pad, gate_rows=gate_rows,
                      cf_col=cf_col, cf_row=cf_row, lse=lse, scale=scale, o_f32=o_f32)
        else:
            w_in = _pad_cols(rows_stacked(L["w_in"]), down_pad)
            down = mm_nn(f"mla_down_{i}", hb, w_in, F32)[0]
            if tables is None:
                tables = rope_tables(positions.reshape(S, 1))
            cos1, sin1, cos2, sin2 = tables
            pre1_rows = [(down[:, :q_rank], q_rank, _c0), (down[:, q_rank:q_rank + kv_rank], kv_rank, _c0),
                         (down[:, q_rank + kv_rank:], LANE, _c0), (cos1, LANE, _c0), (sin1, LANE, _c0)]
            pre1_fn = _mla_pre1_fn(q_rank, kv_rank)
            c_q, c_kv, k_rope = rowwise_fwd(
                f"mla_pre1_{i}", pre1_fn, pre1_rows, [q_norm_full, kv_norm_full],
                [(q_rank, q_rank, _c0, BF16), (kv_rank, kv_rank, _c0, BF16), (LANE, LANE, _c0, F32)], tr)
            qfull = mm_nn(f"mla_uq_{i}", c_q, L["w_uq"], F32)[0]
            kv = mm_nn(f"mla_ukv_{i}", c_kv, L["w_ukv"], F32)[0]
            qpad = jnp.pad(qfull.reshape(S, nh, MLA_QK), ((0, 0), (0, 0), (0, MLA_QK_PAD - MLA_QK))).reshape(S, nh * MLA_QK_PAD)
            gq, gk = _pad_cols(mla_q_gain[j:j + 1], MLA_QK_PAD), _pad_cols(mla_k_gain[j:j + 1], MLA_QK_PAD)
            pre2_rows = [(qpad, MLA_QK_PAD, lambda h: h), (kv, MLA_NOPE, lambda h: 2 * h), (k_rope, LANE, _c0),
                         (cos2, MLA_QK_PAD, _c0), (sin2, MLA_QK_PAD, _c0)]
            qh, kh = rowwise_fwd(f"mla_pre2_{i}", _mla_pre2_fn, pre2_rows, [gq, gk],
                                 [(nh * MLA_QK_PAD, MLA_QK_PAD, lambda h: h, BF16)] * 2, tr, nh)
            scale = 1.0 / math.sqrt(MLA_QK)
            o, o_f32, lse = attn_fwd(f"mla_attn_{i}", qh, kh, kv, nh, MLA_QK_PAD, 0, 0, 1, 2, scale)
            st.update(w_in=w_in, pre1_rows=pre1_rows, pre1_fn=pre1_fn, c_q=c_q, c_kv=c_kv, pre2_rows=pre2_rows, gq=gq, gk=gk,
                      qh=qh, kh=kh, kv=kv, lse=lse, scale=scale, o_f32=o_f32)
        x1 = mm_nn(f"mix_out_{i}", o, rows_stacked(L["w_out"]), F32, epilogue=_add, extras=(xc,))[0]
        L_mlp, token = before_part(2 * i + 1, x1)
        L.update(L_mlp)
        layers.append(L)
        g2 = mlp_norm[i:i + 1] + token
        h2 = rmsnorm_fwd(f"norm2_{i}", x1, g2)
        a, r = mm_nn(f"mlp_up_{i}", h2, L["w1"], None, epilogue=_relu2, out_dtypes=[BF16, BF16])
        xc = mm_nn(f"mlp_down_{i}", a, rows_stacked(L["w2"]), F32, epilogue=_add, extras=(x1,))[0]
        st.update(o=o, x1=x1, h2=h2, a=a, r=r, g1=g1, g2=g2)
        saved.append(st)

    loss_local, dx, dxb = loss_head(xc, loss_target[0])
    loss = lax.psum(loss_local[0, 0], ("x", "y", "c"))

    small = {}

    def stack_rows(g):
        return g.reshape(N_CHIPS, g.shape[0] // N_CHIPS, g.shape[1])

    full = [dict() for _ in range(depth)]
    sums = []

    def sums_advance(after):
        token = jnp.zeros((1, 1), F32)
        for e in sums:
            tag = e["tag"]
            if "back" in e:
                others, halves = chips_wait(f"sendh_wait_{tag}", e.pop("back"), after)
                full[e["layer"]].update(zip(e["names"], zip(halves, others)))
                e["done"] = True
            elif "chips" in e:
                got, parts = chips_wait(f"xchg_wait_{tag}", e.pop("chips"), after)
                halves = [sum_slabs(f"sum4_{tag}_{t}", g, p, chip_arr) for t, (g, p) in enumerate(zip(got, parts))]
                e["back"], tok = chips_start(f"sendh_start_{tag}", "whole", halves, [h.shape for h in halves])
                token = token + tok
            elif "pair" in e:
                theirs, grads = chips_wait(f"swap_wait_{tag}", e.pop("pair"), after)
                parts = [add_half(f"addh_{tag}_{t}", g, o, c_arr) for t, (g, o) in enumerate(zip(grads, theirs))]
                e["chips"], tok = chips_start(f"xchg_start_{tag}", "scatter", parts, [p.shape for p in parts])
                token = token + tok
            elif not e.get("done"):
                grads = e.pop("grads")
                e["pair"], tok = chips_start(f"swap_start_{tag}", "swap", grads,
                                             [(N_CHIPS, g.shape[1] // 2, g.shape[2]) for g in grads])
                token = token + tok
        sums[:] = [e for e in sums if not e.get("done")]
        return token

    def sums_add(i, part, gr, after):
        sums.append(dict(layer=i, tag=f"{part}_{i}", names=list(gr), grads=list(gr.values())))
        return sums_advance(after)

    dep = None
    for i in reversed(range(depth)):
        kind, j = i % N_MIXERS, i // N_MIXERS
        L, st = layers[i], saved[i]
        gr = {}
        gr["w2"] = stack_rows(mm_tn(f"mlp_dw2_{i}", st["a"], dxb, BF16, False, dep=dep))
        du = mm_nt(f"mlp_du_{i}", dxb, rows_stacked(L["w2"]), BF16, epilogue=_times_2r, extras=(st["r"],), dep=dep)
        gr["w1"] = mm_tn(f"mlp_dw1_{i}", st["h2"], du, BF16, True)
        dh2 = mm_nt(f"mlp_dh_{i}", du, L["w1"], F32)
        dx1, dx1b, dg2 = rmsnorm_bwd(f"norm2_bwd_{i}", st["x1"], st["g2"], dh2, dx)
        small[("mlp_norm", i)] = dg2
        dep = sums_add(i, "mlp", gr, dx1)
        gr = {}
        gr["w_out"] = stack_rows(mm_tn(f"mix_dwout_{i}", st["o"], dx1b, BF16, False, dep=dep))
        do = mm_nt(f"mix_do_{i}", dx1b, rows_stacked(L["w_out"]), BF16, dep=dep)
        if kind == 0:
            dqkv = sb_bwd(f"sb_attn_bwd_{i}", st["qkv"], st["o_f32"], do, nh)
            gr["w_in"] = mm_tn(f"sb_dwin_{i}", st["hb"], dqkv, BF16, True)
            dh = mm_nt(f"sb_dh_{i}", dqkv, L["w_in"], F32)
        elif kind == 1:
            dqh, dkh, dv, dcf = attn_bwd(f"fox_attn_bwd_{i}", st["qh"], st["kh"], st["proj"], st["o_f32"], do, st["lse"], nh,
                                         HEAD_DIM, 0, 0, 2 * nh, 1, st["scale"], st["cf_col"], st["cf_row"])
            dcf_s = _pad_cols(dcf.reshape(nh, S).T, LANE)
            dlogf = seq_cumsum(f"fox_dcf_{i}", dcf_s, True)
            dgate, db = rowwise_bwd(f"fox_gate_bwd_{i}", _fox_gate_fn, st["gate_rows"], [st["b_pad"]], [(dlogf, LANE, _c0)],
                                    [(LANE, LANE, _c0, BF16, False)], tr)
            dq, dk, dgq, dgk = rowwise_bwd(
                f"fox_qk_bwd_{i}", _fox_qk_fn, st["qk_rows"], [st["gq"], st["gk"]],
                [(dqh, HEAD_DIM, lambda h: h), (dkh, HEAD_DIM, lambda h: h)], [(W, HEAD_DIM, lambda h: h, BF16, False)] * 2, tr, nh)
            small[("fox_b_f", j)] = db[:, :nh]
            small[("fox_q_gain", j)] = dgq
            small[("fox_k_gain", j)] = dgk
            dproj = jnp.concatenate([dq, dk, bf(dv), dgate], axis=1)
            dw = mm_tn(f"fox_dwin_{i}", st["hb"], dproj, BF16, False)
            n4 = n_fox_in // N_CHIPS
            gr["w_in"] = jnp.stack([dw[:, s * n4:(s + 1) * n4] for s in range(N_CHIPS)])
            dh = mm_nt(f"fox_dh_{i}", dproj, st["w_in"], F32)
        else:
            dqh, dkh, dv = attn_bwd(f"mla_attn_bwd_{i}", st["qh"], st["kh"], st["kv"], st["o_f32"], do, st["lse"], nh,
                                    MLA_QK_PAD, 0, 0, 1, 2, st["scale"])
            dqpad, dkn, dkr, dgq, dgk = rowwise_bwd(
                f"mla_pre2_bwd_{i}", _mla_pre2_fn, st["pre2_rows"], [st["gq"], st["gk"]],
                [(dqh, MLA_QK_PAD, lambda h: h), (dkh, MLA_QK_PAD, lambda h: h)],
                [(nh * MLA_QK_PAD, MLA_QK_PAD, lambda h: h, BF16, False), (W, MLA_NOPE, lambda h: h, BF16, False),
                 (LANE, LANE, _c0, F32, True)], tr, nh, n_diff=3)
            small[("mla_q_gain", j)] = dgq[:, :MLA_QK]
            small[("mla_k_gain", j)] = dgk[:, :MLA_QK]
            dqfull = dqpad.reshape(S, nh, MLA_QK_PAD)[:, :, :MLA_QK].reshape(S, nh * MLA_QK)
            dkv = jnp.stack([dkn.reshape(S, nh, MLA_NOPE), bf(dv).reshape(S, nh, MLA_V)], axis=2).reshape(S, nh * (MLA_NOPE + MLA_V))
            gr["w_uq"] = mm_tn(f"mla_dwuq_{i}", st["c_q"], dqfull, BF16, True)
            dc_q = mm_nt(f"mla_dcq_{i}", dqfull, L["w_uq"], F32)
            gr["w_ukv"] = mm_tn(f"mla_dwukv_{i}", st["c_kv"], dkv, BF16, True)
            dc_kv = mm_nt(f"mla_dckv_{i}", dkv, L["w_ukv"], F32)
            d1, d2, d3, dqn, dkvn = rowwise_bwd(
                f"mla_pre1_bwd_{i}", st["pre1_fn"], st["pre1_rows"], [q_norm_full, kv_norm_full],
                [(dc_q, q_rank, _c0), (dc_kv, kv_rank, _c0), (dkr, LANE, _c0)],
                [(q_rank, q_rank, _c0, BF16, False), (kv_rank, kv_rank, _c0, BF16, False), (LANE, LANE, _c0, BF16, False)],
                tr, n_diff=3)
            small[("mla_q_norm", j)] = dqn
            small[("mla_kv_norm", j)] = dkvn
            ddown = jnp.concatenate([d1, d2, d3], axis=1)
            dw = mm_tn(f"mla_dwin_{i}", st["hb"], ddown, BF16, False)
            gr["w_in"] = stack_rows(dw[:, :n_down])
            dh = mm_nt(f"mla_dh_{i}", ddown, st["w_in"], F32)
        dx, dxb, dg1 = rmsnorm_bwd(f"norm1_bwd_{i}", st["x"], st["g1"], dh, dx1)
        small[("mix_norm", i)] = dg1
        dep = sums_add(i, "mix", gr, dx)

    def holders(kind_of):
        return [i for i in range(depth) if kind_of is None or i % N_MIXERS == kind_of]

    tensors = {
        "sb_w_in": (sb_w_in, m_sb_w_in, v_sb_w_in, "w_in", holders(0)),
        "sb_w_out": (sb_w_out, m_sb_w_out, v_sb_w_out, "w_out", holders(0)),
        "fox_w_in": (fox_w_in, m_fox_w_in, v_fox_w_in, "w_in", holders(1)),
        "fox_w_out": (fox_w_out, m_fox_w_out, v_fox_w_out, "w_out", holders(1)),
        "mla_w_in": (mla_w_in, m_mla_w_in, v_mla_w_in, "w_in", holders(2)),
        "mla_w_uq": (mla_w_uq, m_mla_w_uq, v_mla_w_uq, "w_uq", holders(2)),
        "mla_w_ukv": (mla_w_ukv, m_mla_w_ukv, v_mla_w_ukv, "w_ukv", holders(2)),
        "mla_w_out": (mla_w_out, m_mla_w_out, v_mla_w_out, "w_out", holders(2)),
        "mlp_w1": (mlp_w1, m_mlp_w1, v_mlp_w1, "w1", holders(None)),
        "mlp_w2": (mlp_w2, m_mlp_w2, v_mlp_w2, "w2", holders(None)),
    }
    updated = {n: None for n in tensors}

    def update_ready(dep):
        outs = []
        for n, (w, m, v, key, held) in tensors.items():
            for l in reversed(range(len(held))):
                if (n, l) not in applied and key in full[held[l]]:
                    mine, other = full[held[l]][key]
                    updated[n] = adamw_big(f"adamw_{n}_{l}", w, m, v, mine, other, c_arr, l, updated[n], dep)
                    applied.add((n, l))
                    outs.append(updated[n][0])
        return outs

    applied = set()
    after = dx
    while sums:
        outs = update_ready(dep)
        after = outs if outs else after
        dep = sums_advance(after)

    last = update_ready(dep)

    keys = list(small)
    flat = jnp.concatenate([small[k].reshape(-1) for k in keys])
    rows_g = -(-flat.shape[0] // (8 * LANE)) * 8
    flat = jnp.pad(flat, (0, rows_g * LANE - flat.shape[0]))
    tail = last or (after if isinstance(after, list) else [after])
    summed = all_sum_small("sum_small", flat.reshape(rows_g, LANE), dep=tail[0]).reshape(-1)
    sg, off = {}, 0
    for k in keys:
        n = small[k].size
        sg[k] = summed[off:off + n].reshape(small[k].shape)
        off += n

    def update_big(name):
        return list(updated[name])

    def update_small(name, w, m, v, g):
        return [g] + list(adamw(f"adamw_{name}", w, g, m, v))

    def small_rows(name, count):
        return jnp.concatenate([sg[(name, l)] for l in range(count)], axis=0)

    def my_part(g, n):
        return lax.dynamic_slice(g, (0, chip * n), (g.shape[0], n))

    res = {
        "mix_norm": update_small("mix_norm", mix_norm, m_mix_norm, v_mix_norm, small_rows("mix_norm", depth)),
        "mlp_norm": update_small("mlp_norm", mlp_norm, m_mlp_norm, v_mlp_norm, small_rows("mlp_norm", depth)),
        "sb_w_in": update_big("sb_w_in"),
        "sb_w_out": update_big("sb_w_out"),
        "fox_w_in": update_big("fox_w_in"),
        "fox_b_f": update_small("fox_b_f", fox_b_f, m_fox_b_f, v_fox_b_f, small_rows("fox_b_f", fox_b_f.shape[0])),
        "fox_q_gain": update_small("fox_q_gain", fox_q_gain, m_fox_q_gain, v_fox_q_gain, small_rows("fox_q_gain", fox_q_gain.shape[0])),
        "fox_k_gain": update_small("fox_k_gain", fox_k_gain, m_fox_k_gain, v_fox_k_gain, small_rows("fox_k_gain", fox_k_gain.shape[0])),
        "fox_w_out": update_big("fox_w_out"),
        "mla_w_in": update_big("mla_w_in"),
        "mla_q_norm": update_small("mla_q_norm", mla_q_norm, m_mla_q_norm, v_mla_q_norm,
                                   my_part(small_rows("mla_q_norm", mla_q_norm.shape[0]), mla_q_norm.shape[1])),
        "mla_kv_norm": update_small("mla_kv_norm", mla_kv_norm, m_mla_kv_norm, v_mla_kv_norm,
                                    my_part(small_rows("mla_kv_norm", mla_kv_norm.shape[0]), mla_kv_norm.shape[1])),
        "mla_w_uq": update_big("mla_w_uq"),
        "mla_w_ukv": update_big("mla_w_ukv"),
        "mla_q_gain": update_small("mla_q_gain", mla_q_gain, m_mla_q_gain, v_mla_q_gain, small_rows("mla_q_gain", mla_q_gain.shape[0])),
        "mla_k_gain": update_small("mla_k_gain", mla_k_gain, m_mla_k_gain, v_mla_k_gain, small_rows("mla_k_gain", mla_k_gain.shape[0])),
        "mla_w_out": update_big("mla_w_out"),
        "mlp_w1": update_big("mlp_w1"),
        "mlp_w2": update_big("mlp_w2"),
    }
    order = ["mix_norm", "mlp_norm", "sb_w_in", "sb_w_out", "fox_w_in", "fox_b_f", "fox_q_gain", "fox_k_gain", "fox_w_out",
             "mla_w_in", "mla_q_norm", "mla_kv_norm", "mla_w_uq", "mla_w_ukv", "mla_q_gain", "mla_k_gain", "mla_w_out",
             "mlp_w1", "mlp_w2"]
    outs = [loss, dx.reshape(x.shape)]
    for k in range(4):
        outs += [res[n][k] for n in order]
    return tuple(outs)
```

```python
import functools
import math

import numpy as np
import jax
import jax.numpy as jnp
from jax import lax
from jax.experimental import pallas as pl
from jax.experimental.pallas import tpu as pltpu

F32 = jnp.float32
BF16 = jnp.bfloat16
MESH = pl.DeviceIdType.MESH

EPS = 1e-6
HEAD_DIM = 128
MLA_NOPE = 128
MLA_ROPE = 64
MLA_V = 128
MLA_QK = MLA_NOPE + MLA_ROPE
MLA_QK_PAD = 256
ROPE_THETA = 10000.0
N_MIXERS = 3
ADAM_LR = 0.001
ADAM_B1 = 0.9
ADAM_B2 = 0.999
ADAM_EPS = 1e-08
ADAM_WD = 0.01
ADAM_STEP = 10

LANE = 128
N_CHIPS = 4
GATHER_AHEAD = 3
VMEM_LIMIT = 48 * 1024 * 1024
NEG = -1e30


def _cp(sem):
    return pltpu.CompilerParams(dimension_semantics=sem, vmem_limit_bytes=VMEM_LIMIT)


def _pick(dim, cap):
    best = None
    b = LANE
    while b <= min(dim, cap):
        if dim % b == 0:
            best = b
        b += LANE
    return best if best is not None else dim


def _pick_rows(dim, cap):
    b = min(dim, cap)
    while dim % b:
        b -= 8
    return b


def _matmul(name, a, b, a_blk, a_map, b_blk, b_map, dn, grid, acc_shape, outs, extras=(), epilogue=None, dep=None):
    nk = grid[2]
    n_ex, n_out = len(extras), len(outs)
    n_dep = 0 if dep is None else 1

    def body(*refs):
        a_ref, b_ref = refs[0], refs[1]
        ex_refs = refs[2:2 + n_ex]
        out_refs = refs[2 + n_ex + n_dep:2 + n_ex + n_dep + n_out]
        acc = refs[-1]
        k = pl.program_id(2)

        @pl.when(k == 0)
        def _():
            acc[...] = jnp.zeros_like(acc)

        acc[...] += lax.dot_general(a_ref[...], b_ref[...], dn, preferred_element_type=F32)

        @pl.when(k == nk - 1)
        def _():
            res = acc[...]
            vals = epilogue(res, *[e[...] for e in ex_refs]) if epilogue is not None else (res,)
            for o, v in zip(out_refs, vals):
                o[...] = v.astype(o.dtype)

    in_specs = [pl.BlockSpec(a_blk, a_map), pl.BlockSpec(b_blk, b_map)]
    in_specs += [pl.BlockSpec(blk, lambda i, j, k, m=m: m(i, j)) for (_, blk, m) in extras]
    in_specs += [pl.BlockSpec(memory_space=pl.ANY)] * n_dep
    out_specs = [pl.BlockSpec(blk, lambda i, j, k, m=m: m(i, j)) for (_, _, blk, m) in outs]
    out_shape = [jax.ShapeDtypeStruct(s, d) for (s, d, _, _) in outs]
    res = pl.pallas_call(
        body, name=name, grid=grid, in_specs=in_specs, out_specs=out_specs, out_shape=out_shape,
        scratch_shapes=[pltpu.VMEM(acc_shape, F32)],
        compiler_params=_cp(("parallel", "parallel", "arbitrary")),
    )(a, b, *[e[0] for e in extras], *([] if dep is None else [dep]))
    return res


BM, BN, BK = 1024, 1024, 2048


def mm_nn(name, a, w, out_dtype, epilogue=None, extras=(), n_out=1, out_dtypes=None):
    M, K = a.shape
    stacked = w.ndim == 3
    n4 = w.shape[-1]
    N = n4 * (N_CHIPS if stacked else 1)
    bm, bn, bk = _pick_rows(M, BM), _pick(n4, BN), _pick(K, BK)
    nb = n4 // bn
    if stacked:
        b_blk, b_map = (None, bk, bn), (lambda i, j, k: (j // nb, k, j % nb))
    else:
        b_blk, b_map = (bk, bn), (lambda i, j, k: (k, j))
    dts = out_dtypes if out_dtypes is not None else [out_dtype] * n_out
    outs = [((M, N), d, (bm, bn), lambda i, j: (i, j)) for d in dts]
    exs = [(e, (bm, bn), lambda i, j: (i, j)) for e in extras]
    return _matmul(name, a, w, (bm, bk), lambda i, j, k: (i, k), b_blk, b_map,
                   (((1,), (0,)), ((), ())), (M // bm, N // bn, K // bk), (bm, bn), outs, exs, epilogue)


def mm_nt(name, dy, w, out_dtype, epilogue=None, extras=(), dep=None):
    M, N = dy.shape
    stacked = w.ndim == 3
    K, n4 = w.shape[-2], w.shape[-1]
    bm, bn, bk = _pick_rows(M, BM), _pick(K, BN), _pick(n4, BK)
    nb = n4 // bk
    if stacked:
        b_blk, b_map = (None, bn, bk), (lambda i, j, k: (k // nb, j, k % nb))
    else:
        b_blk, b_map = (bn, bk), (lambda i, j, k: (j, k))
    outs = [((M, K), out_dtype, (bm, bn), lambda i, j: (i, j))]
    exs = [(e, (bm, bn), lambda i, j: (i, j)) for e in extras]
    return _matmul(name, dy, w, (bm, bk), lambda i, j, k: (i, k), b_blk, b_map,
                   (((1,), (1,)), ((), ())), (M // bm, K // bn, N // bk), (bm, bn), outs, exs, epilogue, dep)[0]


def mm_tn(name, a, dy, out_dtype, stacked, dep=None):
    M, K = a.shape
    N = dy.shape[1]
    n4 = N // N_CHIPS if stacked else N
    bm, bn, bk = _pick(K, BM), _pick(n4, BN), _pick_rows(M, BK)
    nb = n4 // bn
    if stacked:
        outs = [((N_CHIPS, K, n4), out_dtype, (None, bm, bn), lambda i, j: (j // nb, i, j % nb))]
    else:
        outs = [((K, N), out_dtype, (bm, bn), lambda i, j: (i, j))]
    return _matmul(name, a, dy, (bk, bm), lambda i, j, k: (k, i), (bk, bn), lambda i, j, k: (k, j),
                   (((0,), (0,)), ((), ())), (K // bm, N // bn, M // bk), (bm, bn), outs, dep=dep)[0]


def _split3(x):
    hi = x.astype(BF16)
    r = x - hi.astype(F32)
    mid = r.astype(BF16)
    lo = (r - mid.astype(F32)).astype(BF16)
    return hi, mid, lo


def _exact_dot(x, m, n):
    out = None
    for p in _split3(x)[:n]:
        d = jnp.dot(p, m, preferred_element_type=F32)
        out = d if out is None else out + d
    return out


def _row_spec(tr, width, cmap):
    return pl.BlockSpec((tr, width), lambda i, h: (i, cmap(h)))


def _full_spec(p):
    return pl.BlockSpec(p.shape, lambda i, h: (0,) * p.ndim)


def rowwise_fwd(name, fn, rows, params, outs, tr, nh=1):
    S = rows[0][0].shape[0]
    nr, npar = len(rows), len(params)

    def body(*refs):
        vals = fn(*[r[...].astype(F32) for r in refs[:nr]], *[p[...] for p in refs[nr:nr + npar]])
        for o, v in zip(refs[nr + npar:], vals):
            o[...] = v.astype(o.dtype)

    return pl.pallas_call(
        body, name=name, grid=(S // tr, nh),
        in_specs=[_row_spec(tr, w, cm) for (_, w, cm) in rows] + [_full_spec(p) for p in params],
        out_specs=[_row_spec(tr, w, cm) for (_, w, cm, _) in outs],
        out_shape=[jax.ShapeDtypeStruct((S, c), d) for (c, _, _, d) in outs],
        compiler_params=_cp(("parallel", "arbitrary")),
    )(*[r[0] for r in rows], *params)


def rowwise_bwd(name, fn, rows, params, cts, grads, tr, nh=1, n_diff=None, add=None):
    S = rows[0][0].shape[0]
    nr, npar, nct = len(rows), len(params), len(cts)
    n_diff = nr if n_diff is None else n_diff
    n_add = 1 if add is not None else 0

    def body(*refs):
        row_refs = refs[:nr]
        par_refs = refs[nr:nr + npar]
        ct_refs = refs[nr + npar:nr + npar + nct]
        add_refs = refs[nr + npar + nct:nr + npar + nct + n_add]
        o = nr + npar + nct + n_add
        g_refs = refs[o:o + n_diff]
        cp_refs = refs[o + n_diff:o + n_diff + n_add]
        pg_refs = refs[o + n_diff + n_add:]
        i, h = pl.program_id(0), pl.program_id(1)
        rv = [r[...].astype(F32) for r in row_refs]
        pv = [p[...] for p in par_refs]
        aux = rv[n_diff:]

        def f(*dp):
            return fn(*dp[:n_diff], *aux, *dp[n_diff:])

        _, vjp = jax.vjp(f, *rv[:n_diff], *pv)
        gs = vjp(tuple(c[...].astype(F32) for c in ct_refs))
        for n, (g_ref, (_, _, _, _, over)) in enumerate(zip(g_refs, grads)):
            g = gs[n]
            if n == 0 and add is not None:
                g = g + add_refs[0][...]
                cp_refs[0][...] = g.astype(BF16)
            if over:
                @pl.when(h == 0)
                def _(g_ref=g_ref):
                    g_ref[...] = jnp.zeros_like(g_ref)
                g_ref[...] += g.astype(g_ref.dtype)
            else:
                g_ref[...] = g.astype(g_ref.dtype)
        for pg_ref, g in zip(pg_refs, gs[n_diff:]):
            @pl.when((i == 0) & (h == 0))
            def _(pg_ref=pg_ref):
                pg_ref[...] = jnp.zeros_like(pg_ref)
            pg_ref[...] += g

    in_specs = [_row_spec(tr, w, cm) for (_, w, cm) in rows] + [_full_spec(p) for p in params]
    in_specs += [_row_spec(tr, w, cm) for (_, w, cm) in cts]
    operands = [r[0] for r in rows] + list(params) + [c[0] for c in cts]
    out_specs = [_row_spec(tr, w, cm) for (_, w, cm, _, _) in grads]
    out_shape = [jax.ShapeDtypeStruct((S, c), d) for (c, _, _, d, _) in grads]
    if add is not None:
        in_specs.append(_row_spec(tr, add[1], add[2]))
        operands.append(add[0])
        out_specs.append(_row_spec(tr, add[1], add[2]))
        out_shape.append(jax.ShapeDtypeStruct(add[0].shape, BF16))
    out_specs += [_full_spec(p) for p in params]
    out_shape += [jax.ShapeDtypeStruct(p.shape, F32) for p in params]
    return pl.pallas_call(
        body, name=name, grid=(S // tr, nh), in_specs=in_specs, out_specs=out_specs, out_shape=out_shape,
        compiler_params=_cp(("arbitrary", "arbitrary")),
    )(*operands)


def _c0(h):
    return 0


def _rms(x, g, n):
    return x * lax.rsqrt(jnp.sum(x * x, axis=-1, keepdims=True) * (1.0 / n) + EPS) * g


def _rmsnorm_fn(d):
    def fn(x, g):
        return (_rms(x, g, d),)
    return fn


def rmsnorm_fwd(name, x, g):
    S, D = x.shape
    return rowwise_fwd(name, _rmsnorm_fn(D), [(x, D, _c0)], [g], [(D, D, _c0, BF16)], _pick_rows(S, 256))[0]


def rmsnorm_bwd(name, x, g, dh, dres):
    S, D = x.shape
    return rowwise_bwd(name, _rmsnorm_fn(D), [(x, D, _c0)], [g], [(dh, D, _c0)], [(D, D, _c0, F32, False)],
                       _pick_rows(S, 256), add=(dres, D, _c0))


def loss_head(y, t):
    S, D = y.shape
    tr = _pick_rows(S, 256)

    def body(y_ref, t_ref, l_ref, d_ref, db_ref):
        @pl.when(pl.program_id(0) == 0)
        def _():
            l_ref[...] = jnp.zeros_like(l_ref)
        e = y_ref[...] - t_ref[...]
        l_ref[...] += 0.5 * jnp.sum(jnp.sum(e * e, axis=1, keepdims=True), axis=0, keepdims=True) * (1.0 / D)
        d = e * (1.0 / D)
        d_ref[...] = d
        db_ref[...] = d.astype(BF16)

    row = pl.BlockSpec((tr, D), lambda i: (i, 0))
    return pl.pallas_call(
        body, name="loss_head", grid=(S // tr,), in_specs=[row, row],
        out_specs=[pl.BlockSpec((1, 1), lambda i: (0, 0)), row, row],
        out_shape=[jax.ShapeDtypeStruct((1, 1), F32), jax.ShapeDtypeStruct((S, D), F32), jax.ShapeDtypeStruct((S, D), BF16)],
        compiler_params=_cp(("arbitrary",)),
    )(y, t)


def _dot_nt(a, b):
    return lax.dot_general(a, b, (((1,), (1,)), ((), ())), preferred_element_type=F32)


def _dot_tn(a, b):
    return lax.dot_general(a, b, (((0,), (0,)), ((), ())), preferred_element_type=F32)


AT_TQ, AT_TK = 512, 256


def attn_fwd(name, q, k, v, nh, dq, qoff, koff, voff, vstep, scale, cf_col=None, cf_row=None):
    S = q.shape[0]
    tq = _pick_rows(S, AT_TQ)
    tk = _pick_rows(S, AT_TK)
    nq = S // tq
    bias = cf_col is not None

    def body(*refs):
        if bias:
            q_ref, k_ref, v_ref, cfc_ref, cfr_ref, o_ref, of_ref, lse_ref = refs
        else:
            q_ref, k_ref, v_ref, o_ref, of_ref, lse_ref = refs
        i = pl.program_id(1)
        qv = q_ref[...].astype(BF16)
        row = i * tq + lax.broadcasted_iota(jnp.int32, (tq, tk), 0)
        coli = lax.broadcasted_iota(jnp.int32, (tq, tk), 1)
        nkb = ((i + 1) * tq + tk - 1) // tk

        def logits(j):
            return _dot_nt(qv, k_ref[pl.ds(pl.multiple_of(j * tk, tk), tk), :].astype(BF16)) * scale

        def step(j, carry):
            s, m, l, acc, acc_lo = carry
            s_next = logits(jnp.minimum(j + 1, nkb - 1))
            vs = v_ref[pl.ds(pl.multiple_of(j * tk, tk), tk), :].astype(BF16)
            if bias:
                s = s + cfc_ref[0] - cfr_ref[0, pl.ds(j, 1), :]
            s = jnp.where(j * tk + coli <= row, s, NEG)
            m_new = jnp.maximum(m, jnp.max(s, axis=1, keepdims=True))
            alpha = jnp.exp(m - m_new)
            p = jnp.exp(s - m_new)
            pb = p.astype(BF16)
            l = alpha * l + jnp.sum(p, axis=1, keepdims=True)
            acc = alpha * acc + jnp.dot(pb, vs, preferred_element_type=F32)
            acc_lo = alpha * acc_lo + jnp.dot((p - pb.astype(F32)).astype(BF16), vs, preferred_element_type=F32)
            return s_next, m_new, l, acc, acc_lo

        z = jnp.zeros((tq, HEAD_DIM), F32)
        _, m, l, acc, acc_lo = lax.fori_loop(
            0, nkb, step, (logits(0), jnp.full((tq, 1), NEG, F32), jnp.zeros((tq, 1), F32), z, z))
        o_ref[...] = (acc / l).astype(o_ref.dtype)
        of_ref[...] = (acc + acc_lo) / l
        lse_ref[0] = m + jnp.log(l)

    in_specs = [pl.BlockSpec((tq, dq), lambda h, i: (i, qoff + h)),
                pl.BlockSpec((S, dq), lambda h, i: (0, koff + h)),
                pl.BlockSpec((S, HEAD_DIM), lambda h, i: (0, voff + vstep * h))]
    operands = [q, k, v]
    if bias:
        in_specs += [pl.BlockSpec((1, tq, 1), lambda h, i: (h, i, 0)), pl.BlockSpec((1, S // tk, tk), lambda h, i: (h, 0, 0))]
        operands += [cf_col, cf_row]
    return pl.pallas_call(
        body, name=name, grid=(nh, nq), in_specs=in_specs,
        out_specs=[pl.BlockSpec((tq, HEAD_DIM), lambda h, i: (i, h)), pl.BlockSpec((tq, HEAD_DIM), lambda h, i: (i, h)),
                   pl.BlockSpec((1, tq, 1), lambda h, i: (h, i, 0))],
        out_shape=[jax.ShapeDtypeStruct((S, nh * HEAD_DIM), BF16), jax.ShapeDtypeStruct((S, nh * HEAD_DIM), F32),
                   jax.ShapeDtypeStruct((nh, S, 1), F32)],
        compiler_params=_cp(("parallel", "arbitrary")),
    )(*operands)


def attn_bwd(name, q, k, v, o, do, lse, nh, dq, qoff, koff, voff, vstep, scale, cf_col=None, cf_row=None):
    S = q.shape[0]
    tq = _pick_rows(S, AT_TQ)
    tk = _pick_rows(S, AT_TK)
    nq = S // tq
    bias = cf_col is not None

    def body(*refs):
        if bias:
            q_ref, k_ref, v_ref, o_ref, do_ref, lse_ref, cfc_ref, cfr_ref, dq_ref, dk_ref, dv_ref, dcf_ref = refs
        else:
            q_ref, k_ref, v_ref, o_ref, do_ref, lse_ref, dq_ref, dk_ref, dv_ref = refs
        dk_ref[...] = jnp.zeros_like(dk_ref)
        dv_ref[...] = jnp.zeros_like(dv_ref)
        if bias:
            dcf_ref[...] = jnp.zeros_like(dcf_ref)
        rowi = lax.broadcasted_iota(jnp.int32, (tq, tk), 0)
        coli = lax.broadcasted_iota(jnp.int32, (tq, tk), 1)

        def outer(i, _):
            roff = pl.multiple_of(i * tq, tq)
            qv = q_ref[pl.ds(roff, tq), :].astype(BF16)
            dov = do_ref[pl.ds(roff, tq), :]
            delta = jnp.sum(dov.astype(F32) * o_ref[pl.ds(roff, tq), :].astype(F32), axis=1, keepdims=True)
            lse = lse_ref[0, pl.ds(roff, tq), :]
            if bias:
                cq = cfc_ref[0, pl.ds(roff, tq), :]

            nkb = ((i + 1) * tq + tk - 1) // tk

            def products(j):
                off = pl.multiple_of(j * tk, tk)
                return (_dot_nt(qv, k_ref[pl.ds(off, tk), :].astype(BF16)) * scale,
                        _dot_nt(dov, v_ref[pl.ds(off, tk), :].astype(BF16)))

            def inner(j, carry):
                s, dp, dq_acc = carry
                s_next, dp_next = products(jnp.minimum(j + 1, nkb - 1))
                off = pl.multiple_of(j * tk, tk)
                ks = k_ref[pl.ds(off, tk), :].astype(BF16)
                if bias:
                    s = s + cq - cfr_ref[0, pl.ds(j, 1), :]
                p = jnp.where(j * tk + coli <= i * tq + rowi, jnp.exp(s - lse), 0.0)
                ds = p * (dp - delta)
                if bias:
                    dcf_ref[0, pl.ds(j, 1), :] -= jnp.sum(ds, axis=0, keepdims=True)
                dsb = (ds * scale).astype(BF16)
                dv_ref[pl.ds(off, tk), :] += _dot_tn(p.astype(BF16), dov)
                dk_ref[pl.ds(off, tk), :] += _dot_tn(dsb, qv)
                return s_next, dp_next, dq_acc + jnp.dot(dsb, ks, preferred_element_type=F32)

            dq_ref[pl.ds(roff, tq), :] = lax.fori_loop(0, nkb, inner, products(0) + (jnp.zeros((tq, dq), F32),))[2]
            return 0

        lax.fori_loop(0, nq, outer, 0)

    in_specs = [pl.BlockSpec((S, dq), lambda h: (0, qoff + h)),
                pl.BlockSpec((S, dq), lambda h: (0, koff + h)),
                pl.BlockSpec((S, HEAD_DIM), lambda h: (0, voff + vstep * h)),
                pl.BlockSpec((S, HEAD_DIM), lambda h: (0, h)),
                pl.BlockSpec((S, HEAD_DIM), lambda h: (0, h)),
                pl.BlockSpec((1, S, 1), lambda h: (h, 0, 0))]
    operands = [q, k, v, o, do, lse]
    out_specs = [pl.BlockSpec((S, dq), lambda h: (0, h)), pl.BlockSpec((S, dq), lambda h: (0, h)),
                 pl.BlockSpec((S, HEAD_DIM), lambda h: (0, h))]
    out_shape = [jax.ShapeDtypeStruct((S, nh * dq), F32), jax.ShapeDtypeStruct((S, nh * dq), F32),
                 jax.ShapeDtypeStruct((S, nh * HEAD_DIM), F32)]
    if bias:
        in_specs += [pl.BlockSpec((1, S, 1), lambda h: (h, 0, 0)), pl.BlockSpec((1, S // tk, tk), lambda h: (h, 0, 0))]
        operands += [cf_col, cf_row]
        out_specs.append(pl.BlockSpec((1, S // tk, tk), lambda h: (h, 0, 0)))
        out_shape.append(jax.ShapeDtypeStruct((nh, S // tk, tk), F32))
    return pl.pallas_call(
        body, name=name, grid=(nh,), in_specs=in_specs, out_specs=out_specs, out_shape=out_shape,
        compiler_params=_cp(("parallel",)),
    )(*operands)


SB_TQ, SB_TK = 512, 256


def _sb_block(z, row, col, later_c, tri_after):
    strict = col < row
    lsn = jnp.minimum(-z, 0.0) - jnp.log(1.0 + jnp.exp(-jnp.abs(z)))
    lsp = lsn + z
    L = jnp.where(strict, lsn, 0.0)
    later = _exact_dot(L, tri_after, 2) + later_c
    a = jnp.where(strict, jnp.exp(lsp + later), 0.0)
    return strict, lsp, lsn, L, a


def _tri(n, after_strict):
    r = lax.broadcasted_iota(jnp.int32, (n, n), 0)
    c = lax.broadcasted_iota(jnp.int32, (n, n), 1)
    return jnp.where(r > c if after_strict else r >= c, 1.0, 0.0).astype(BF16)


def sb_fwd(name, qkv, nh):
    S = qkv.shape[0]
    tq = _pick_rows(S, SB_TQ)
    tk = _pick_rows(S, SB_TK)
    nq = S // tq
    scale = 1.0 / math.sqrt(HEAD_DIM)

    def body(q_ref, k_ref, v_ref, o_ref, of_ref):
        i = pl.program_id(1)
        qv = q_ref[...]
        tri = _tri(tk, True)
        row = i * tq + lax.broadcasted_iota(jnp.int32, (tq, tk), 0)
        coli = lax.broadcasted_iota(jnp.int32, (tq, tk), 1)

        def logits(j):
            return _dot_nt(qv, k_ref[pl.ds(pl.multiple_of(j * tk, tk), tk), :]) * scale

        nkb = ((i + 1) * tq + tk - 1) // tk

        def step(jj, carry):
            z, later_c, acc, acc_lo = carry
            j = nkb - 1 - jj
            z_next = logits(jnp.maximum(j - 1, 0))
            vb = v_ref[pl.ds(pl.multiple_of(j * tk, tk), tk), :]
            _, _, _, L, a = _sb_block(z, row, j * tk + coli, later_c, tri)
            ab = a.astype(BF16)
            acc = acc + jnp.dot(ab, vb, preferred_element_type=F32)
            acc_lo = acc_lo + jnp.dot((a - ab.astype(F32)).astype(BF16), vb, preferred_element_type=F32)
            return z_next, later_c + jnp.sum(L, axis=1, keepdims=True), acc, acc_lo

        zero = jnp.zeros((tq, HEAD_DIM), F32)
        _, _, acc, acc_lo = lax.fori_loop(0, nkb, step, (logits(nkb - 1), jnp.zeros((tq, 1), F32), zero, zero))
        o_ref[...] = acc.astype(o_ref.dtype)
        of_ref[...] = acc + acc_lo

    blk = pl.BlockSpec((tq, HEAD_DIM), lambda h, i: (i, h))
    return pl.pallas_call(
        body, name=name, grid=(nh, nq),
        in_specs=[blk,
                  pl.BlockSpec((S, HEAD_DIM), lambda h, i: (0, nh + h)),
                  pl.BlockSpec((S, HEAD_DIM), lambda h, i: (0, 2 * nh + h))],
        out_specs=[blk, blk],
        out_shape=[jax.ShapeDtypeStruct((S, nh * HEAD_DIM), BF16), jax.ShapeDtypeStruct((S, nh * HEAD_DIM), F32)],
        compiler_params=_cp(("parallel", "arbitrary")),
    )(qkv, qkv, qkv)


def sb_bwd(name, qkv, o, do, nh):
    S = qkv.shape[0]
    tq = _pick_rows(S, SB_TQ)
    tk = _pick_rows(S, SB_TK)
    nq = S // tq
    scale = 1.0 / math.sqrt(HEAD_DIM)

    def body(q_ref, k_ref, v_ref, o_ref, do_ref, dq_ref, dk_ref, dv_ref, dk_acc, dv_acc):
        dk_acc[...] = jnp.zeros_like(dk_acc)
        dv_acc[...] = jnp.zeros_like(dv_acc)
        tri = _tri(tk, True)
        tri_inc = _tri(tk, False)
        rowi = lax.broadcasted_iota(jnp.int32, (tq, tk), 0)
        coli = lax.broadcasted_iota(jnp.int32, (tq, tk), 1)

        def outer(i, _):
            roff = pl.multiple_of(i * tq, tq)
            qv = q_ref[pl.ds(roff, tq), :]
            dov = do_ref[pl.ds(roff, tq), :]
            dtot = jnp.sum(dov.astype(F32) * o_ref[pl.ds(roff, tq), :].astype(F32), axis=1, keepdims=True)
            row = i * tq + rowi

            def products(j):
                off = pl.multiple_of(j * tk, tk)
                return _dot_nt(qv, k_ref[pl.ds(off, tk), :]) * scale, _dot_nt(dov, v_ref[pl.ds(off, tk), :])

            nkb = ((i + 1) * tq + tk - 1) // tk

            def inner(jj, carry):
                z, da, later_c, suf_c, dq_acc = carry
                j = nkb - 1 - jj
                z_next, da_next = products(jnp.maximum(j - 1, 0))
                off = pl.multiple_of(j * tk, tk)
                kb = k_ref[pl.ds(off, tk), :]
                strict, lsp, lsn, L, a = _sb_block(z, row, j * tk + coli, later_c, tri)
                dl = da * a
                before = dtot - (suf_c + _exact_dot(dl, tri_inc, 3))
                dz = jnp.where(strict, dl * jnp.exp(lsn) - jnp.exp(lsp) * before, 0.0) * scale
                dzb = dz.astype(BF16)
                dv_acc[pl.ds(off, tk), :] += _dot_tn(a.astype(BF16), dov)
                dk_acc[pl.ds(off, tk), :] += _dot_tn(dzb, qv)
                return (z_next, da_next, later_c + jnp.sum(L, axis=1, keepdims=True), suf_c + jnp.sum(dl, axis=1, keepdims=True),
                        dq_acc + jnp.dot(dzb, kb, preferred_element_type=F32))

            z1 = jnp.zeros((tq, 1), F32)
            res = lax.fori_loop(0, nkb, inner, products(nkb - 1) + (z1, z1, jnp.zeros((tq, HEAD_DIM), F32)))
            dq_ref[pl.ds(roff, tq), :] = res[4].astype(dq_ref.dtype)
            return 0

        lax.fori_loop(0, nq, outer, 0)
        dk_ref[...] = dk_acc[...].astype(dk_ref.dtype)
        dv_ref[...] = dv_acc[...].astype(dv_ref.dtype)

    def col(off):
        return pl.BlockSpec((S, HEAD_DIM), lambda h: (0, off + h))

    dq, dk, dv = pl.pallas_call(
        body, name=name, grid=(nh,),
        in_specs=[col(0), col(nh), col(2 * nh), col(0), col(0)],
        out_specs=[col(0), col(0), col(0)],
        out_shape=[jax.ShapeDtypeStruct((S, nh * HEAD_DIM), BF16)] * 3,
        scratch_shapes=[pltpu.VMEM((S, HEAD_DIM), F32), pltpu.VMEM((S, HEAD_DIM), F32)],
        compiler_params=_cp(("parallel",)),
    )(qkv, qkv, qkv, o, do)
    return jnp.concatenate([dq, dk, dv], axis=1)


def seq_cumsum(name, x, reverse):
    S, W = x.shape

    tb = _pick_rows(S, 256)

    def body(x_ref, o_ref):
        parts = _split3(x_ref[...])
        c = lax.broadcasted_iota(jnp.int32, (tb, S), 1)
        for b in range(S // tb):
            r = b * tb + lax.broadcasted_iota(jnp.int32, (tb, S), 0)
            t = jnp.where(r <= c if reverse else r >= c, 1.0, 0.0).astype(BF16)
            out = None
            for p in parts:
                d = jnp.dot(t, p, preferred_element_type=F32)
                out = d if out is None else out + d
            o_ref[b * tb:(b + 1) * tb, :] = out

    return pl.pallas_call(body, name=name, out_shape=jax.ShapeDtypeStruct((S, W), F32),
                          compiler_params=pltpu.CompilerParams(vmem_limit_bytes=VMEM_LIMIT))(x)


def _adamw_math(w, g, m, v):
    c1 = 1.0 - ADAM_B1 ** ADAM_STEP
    c2 = 1.0 - ADAM_B2 ** ADAM_STEP
    nm = ADAM_B1 * m + (1.0 - ADAM_B1) * g
    nv = ADAM_B2 * v + (1.0 - ADAM_B2) * (g * g)
    return -ADAM_LR * ((nm / c1) / (jnp.sqrt(nv / c2) + ADAM_EPS) + ADAM_WD * w), nm, nv


def adamw(name, w, g, m, v):
    R, C = g.shape
    tr = _pick_rows(R, 256)

    def body(w_ref, g_ref, m_ref, v_ref, d_ref, nm_ref, nv_ref):
        d_ref[...], nm_ref[...], nv_ref[...] = _adamw_math(w_ref[...], g_ref[...], m_ref[...], v_ref[...])

    row = pl.BlockSpec((tr, C), lambda i: (i, 0))
    return pl.pallas_call(
        body, name=name, grid=(R // tr,), in_specs=[row] * 4, out_specs=[row] * 3,
        out_shape=[jax.ShapeDtypeStruct((R, C), F32)] * 3, compiler_params=_cp(("parallel",)),
    )(w, g, m, v)


def adamw_big(name, w, m, v, mine, other, c_arr, layer, prev, dep=None):
    L, R, C = w.shape
    r2 = R // 2
    tr = _pick_rows(r2, 256)
    nb = r2 // tr

    def body(c_ref, w_ref, m_ref, v_ref, mine_ref, other_ref, *rest):
        g_ref, d_ref, nm_ref, nv_ref = rest[-4:]
        g = jnp.where(pl.program_id(0) == c_ref[0], mine_ref[...], other_ref[...])
        g_ref[...] = g
        d_ref[...], nm_ref[...], nv_ref[...] = _adamw_math(w_ref[...], g, m_ref[...], v_ref[...])

    sel = pl.BlockSpec((None, tr, C), lambda hf, i, c: (layer, hf * nb + i, 0))
    in_specs = [sel, sel, sel,
                pl.BlockSpec((tr, C), lambda hf, i, c: (jnp.where(hf == c[0], i, 0), 0)),
                pl.BlockSpec((tr, C), lambda hf, i, c: (jnp.where(hf == c[0], 0, i), 0))]
    operands = [c_arr, w, m, v, mine, other]
    aliases = {}
    if prev is not None:
        in_specs += [ANY] * 4
        aliases = {len(operands) + k: k for k in range(4)}
        operands += list(prev)
    if dep is not None:
        in_specs += [ANY]
        operands += [dep]
    return pl.pallas_call(
        body, name=name,
        grid_spec=pltpu.PrefetchScalarGridSpec(num_scalar_prefetch=1, grid=(2, nb), in_specs=in_specs, out_specs=[sel] * 4),
        out_shape=[jax.ShapeDtypeStruct((L, R, C), F32)] * 4, input_output_aliases=aliases,
        compiler_params=_cp(("arbitrary", "arbitrary")),
    )(*operands)


ANY = pl.BlockSpec(memory_space=pl.ANY)


def _place():
    x, y, c = lax.axis_index("x"), lax.axis_index("y"), lax.axis_index("c")
    others = [(1 - x, y), (x, 1 - y), (1 - x, 1 - y)]
    return x, y, c, others


def swap_halves(name, grads):
    n = len(grads)

    def body(*refs):
        src, dst = refs[:n], refs[n:2 * n]
        send, recv = refs[2 * n:]
        x, y, c, _ = _place()
        cps = []
        for t in range(n):
            r2 = grads[t].shape[1] // 2
            cps.append(pltpu.make_async_remote_copy(
                src_ref=src[t].at[:, pl.ds((1 - c) * r2, r2), :], dst_ref=dst[t],
                send_sem=send.at[t], recv_sem=recv.at[t], device_id=(x, y, 1 - c), device_id_type=MESH))
        for cp in cps:
            cp.start()
        for cp in cps:
            cp.wait()

    return pl.pallas_call(
        body, name=name, in_specs=[ANY] * n, out_specs=[ANY] * n,
        out_shape=[jax.ShapeDtypeStruct((N_CHIPS, g.shape[1] // 2, g.shape[2]), g.dtype) for g in grads],
        scratch_shapes=[pltpu.SemaphoreType.DMA((n,))] * 2,
    )(*grads)


def add_half(name, g, other, c_arr):
    _, R, C = g.shape
    r2 = R // 2
    tr = _pick_rows(r2, 512)
    nb = r2 // tr

    def body(c_ref, g_ref, o_ref, out_ref):
        out_ref[...] = (g_ref[...].astype(F32) + o_ref[...].astype(F32)).astype(out_ref.dtype)

    return pl.pallas_call(
        body, name=name,
        grid_spec=pltpu.PrefetchScalarGridSpec(
            num_scalar_prefetch=1, grid=(N_CHIPS, nb),
            in_specs=[pl.BlockSpec((None, tr, C), lambda s, i, c: (s, c[0] * nb + i, 0)),
                      pl.BlockSpec((None, tr, C), lambda s, i, c: (s, i, 0))],
            out_specs=pl.BlockSpec((None, tr, C), lambda s, i, c: (s, i, 0))),
        out_shape=jax.ShapeDtypeStruct((N_CHIPS, r2, C), BF16),
        compiler_params=_cp(("parallel", "parallel")),
    )(c_arr, g, other)


def sum_slabs(name, got, parts, chip_arr):
    _, r2, C = parts.shape
    tr = _pick_rows(r2, 512)

    def body(chip_ref, *refs):
        own_ref, out_ref = refs[N_CHIPS], refs[N_CHIPS + 1]
        acc = None
        for s in range(N_CHIPS):
            v = jnp.where(chip_ref[0] == s, own_ref[...], refs[s][...]).astype(F32)
            acc = v if acc is None else acc + v
        out_ref[...] = acc

    def slab(s):
        return pl.BlockSpec((None, tr, C), lambda i, ch: (jnp.where(ch[0] == s, (s + 1) % N_CHIPS, s), i, 0))

    return pl.pallas_call(
        body, name=name,
        grid_spec=pltpu.PrefetchScalarGridSpec(
            num_scalar_prefetch=1, grid=(r2 // tr,),
            in_specs=[slab(s) for s in range(N_CHIPS)] + [pl.BlockSpec((None, tr, C), lambda i, ch: (ch[0], i, 0))],
            out_specs=pl.BlockSpec((tr, C), lambda i, ch: (i, 0))),
        out_shape=jax.ShapeDtypeStruct((r2, C), F32), compiler_params=_cp(("parallel",)),
    )(chip_arr, got, got, got, got, parts)


def send_halves(name, halves):
    n = len(halves)

    def body(*refs):
        src, dst = refs[:n], refs[n:2 * n]
        send, recv = refs[2 * n:]
        x, y, c, _ = _place()
        cps = [pltpu.make_async_remote_copy(src_ref=src[t], dst_ref=dst[t], send_sem=send.at[t], recv_sem=recv.at[t],
                                            device_id=(x, y, 1 - c), device_id_type=MESH) for t in range(n)]
        for cp in cps:
            cp.start()
        for cp in cps:
            cp.wait()

    return pl.pallas_call(
        body, name=name, in_specs=[ANY] * n, out_specs=[ANY] * n,
        out_shape=[jax.ShapeDtypeStruct(h.shape, h.dtype) for h in halves],
        scratch_shapes=[pltpu.SemaphoreType.DMA((n,))] * 2,
    )(*halves)


HBM = pl.BlockSpec(memory_space=pltpu.HBM)
SEM = pl.BlockSpec(memory_space=pltpu.SEMAPHORE)
EFFECT = pltpu.SideEffectType.DATAFLOW_SIDE_EFFECTING


def _in_hbm(a):
    return pltpu.with_memory_space_constraint(a, pltpu.HBM)


def _chip_copies(kind, shapes, src, land, send, recv, mine):
    x, y, c, others = _place()
    me = 2 * x + y
    cps = []

    def remote(s_ref, d_ref, k, to):
        cps.append(pltpu.make_async_remote_copy(src_ref=s_ref, dst_ref=d_ref, send_sem=send.at[k], recv_sem=recv.at[k],
                                                device_id=to, device_id_type=MESH))

    for t, shape in enumerate(shapes):
        if kind == "swap":
            r2 = shape[1] // 2
            remote(src[t].at[:, pl.ds((1 - c) * r2, r2), :], land[t], t, (x, y, 1 - c))
        elif kind == "whole":
            remote(src[t], land[t], t, (x, y, 1 - c))
        for j, (px, py) in enumerate(others if kind in ("gather", "scatter", "pass") else []):
            slot = me if mine else 2 * px + py
            if kind == "gather":
                r2 = shape[0] // 2
                rows = pl.ds(c * r2, r2)
                remote(src[t].at[rows, :], land[t].at[slot, rows, :], 3 * t + j, (px, py, c))
            elif kind == "scatter":
                remote(src[t].at[2 * px + py], land[t].at[slot], 3 * t + j, (px, py, c))
            else:
                r2 = shape[1] // 2
                rows = src[t].at[2 * px + py, pl.ds((c if mine else 1 - c) * r2, r2), :]
                remote(rows, rows, 3 * t + j, (x, y, 1 - c))
    return cps


def _n_copies(kind, n):
    return n if kind in ("swap", "whole") else 3 * n


def chips_start(name, kind, srcs, land_shapes, after=None):
    n, nl = len(srcs), len(land_shapes)
    shapes = [s.shape for s in srcs]
    n_buf = n + nl
    n_in = n_buf + (0 if after is None else 1)
    n_sem = _n_copies(kind, n)

    def body(*refs):
        src, land = refs[:n], refs[n:n_buf]
        send, recv = refs[n_in], refs[n_in + 1]
        token = refs[-1]
        for cp in _chip_copies(kind, shapes, src, land, send, recv, True):
            cp.start()
        token[...] = jnp.zeros_like(token)

    lands = [lax.empty(s, srcs[0].dtype) for s in land_shapes]
    out = pl.pallas_call(
        body, name=name,
        out_shape=(pltpu.SemaphoreType.DMA((n_sem,)), pltpu.SemaphoreType.DMA((n_sem,)))
        + tuple(pltpu.HBM(s.shape, s.dtype) for s in srcs) + tuple(pltpu.HBM(l.shape, l.dtype) for l in lands)
        + (jax.ShapeDtypeStruct((8, LANE), F32),),
        in_specs=(HBM,) * n_buf + (ANY,) * (n_in - n_buf),
        out_specs=(SEM, SEM) + (HBM,) * n_buf + (pl.BlockSpec(memory_space=pltpu.VMEM),),
        input_output_aliases={k: 2 + k for k in range(n_buf)},
        compiler_params=pltpu.CompilerParams(has_side_effects=EFFECT),
    )(*[_in_hbm(s) for s in srcs], *[_in_hbm(l) for l in lands], *([] if after is None else [after]))
    return (kind, shapes, out[0], out[1], out[2:2 + n], out[2 + n:2 + n_buf]), out[-1][:1, :1]


def chips_wait(name, handle, after):
    kind, shapes, send, recv, srcs, lands = handle
    n, n_buf = len(srcs), len(srcs) + len(lands)
    after = [a for a in (list(after) if isinstance(after, (list, tuple)) else [after]) if a is not None]

    def body(*refs):
        src, land = refs[:n], refs[n:n_buf]
        for cp in _chip_copies(kind, shapes, src, land, refs[n_buf], refs[n_buf + 1], True):
            cp.wait_send()
        for cp in _chip_copies(kind, shapes, src, land, refs[n_buf], refs[n_buf + 1], False):
            cp.wait_recv()

    out = pl.pallas_call(
        body, name=name,
        out_shape=tuple(pltpu.HBM(s.shape, s.dtype) for s in srcs) + tuple(pltpu.HBM(l.shape, l.dtype) for l in lands),
        in_specs=(HBM,) * n_buf + (SEM, SEM) + (ANY,) * len(after), out_specs=(HBM,) * n_buf,
        input_output_aliases={k: k for k in range(n_buf)},
        compiler_params=pltpu.CompilerParams(has_side_effects=EFFECT),
    )(*srcs, *lands, send, recv, *after)
    return list(out[n:]), list(out[:n])


def pass_to_sibling(name, lands, shards):
    n = len(lands)

    def body(*refs):
        buf = refs[n:2 * n]
        send, recv = refs[2 * n:]
        x, y, c, others = _place()

        def cp(t, j, core):
            r2 = lands[t].shape[1] // 2
            px, py = others[j]
            rows = buf[t].at[2 * px + py, pl.ds(core * r2, r2), :]
            return pltpu.make_async_remote_copy(src_ref=rows, dst_ref=rows, send_sem=send.at[t, j], recv_sem=recv.at[t, j],
                                                device_id=(x, y, 1 - c), device_id_type=MESH)

        for t in range(n):
            for j in range(3):
                cp(t, j, c).start()
        for t in range(n):
            for j in range(3):
                cp(t, j, 1 - c).wait_recv()
        for t in range(n):
            for j in range(3):
                cp(t, j, c).wait_send()

    got = pl.pallas_call(
        body, name=name, in_specs=[ANY] * n, out_specs=[ANY] * n,
        out_shape=[jax.ShapeDtypeStruct(l.shape, l.dtype) for l in lands],
        input_output_aliases={k: k for k in range(n)},
        scratch_shapes=[pltpu.SemaphoreType.DMA((n, 3))] * 2,
    )(*lands)
    chip = 2 * lax.axis_index("x") + lax.axis_index("y")
    return [lax.dynamic_update_slice(g, s[None], (chip, 0, 0)) for g, s in zip(got, shards)]


def all_sum_small(name, v, dep=None):
    R = v.shape[0]
    n_dep = 0 if dep is None else 1

    def body(v_ref, *refs):
        out_ref, slots, send, recv = refs[n_dep:]
        x, y, c, _ = _place()
        me = 4 * x + 2 * y + c
        slots[me] = v_ref[...]
        cps = []
        for k in range(1, 8):
            dx, dy, dc = (k >> 2) & 1, (k >> 1) & 1, k & 1
            to = (x ^ dx, y ^ dy, c ^ dc)
            cps.append(pltpu.make_async_remote_copy(
                src_ref=v_ref, dst_ref=slots.at[me], send_sem=send.at[k - 1], recv_sem=recv.at[k - 1],
                device_id=to, device_id_type=MESH))
        for cp in cps:
            cp.start()
        for k in range(1, 8):
            dx, dy, dc = (k >> 2) & 1, (k >> 1) & 1, k & 1
            frm = 4 * (x ^ dx) + 2 * (y ^ dy) + (c ^ dc)
            pltpu.make_async_remote_copy(
                src_ref=v_ref, dst_ref=slots.at[frm], send_sem=send.at[k - 1], recv_sem=recv.at[k - 1],
                device_id=(x, y, c), device_id_type=MESH).wait_recv()
        for cp in cps:
            cp.wait_send()
        acc = slots[0]
        for d in range(1, 8):
            acc = acc + slots[d]
        out_ref[...] = acc

    vm = pl.BlockSpec(memory_space=pltpu.VMEM)
    return pl.pallas_call(
        body, name=name, in_specs=[vm] + [ANY] * n_dep, out_specs=vm, out_shape=jax.ShapeDtypeStruct((R, LANE), F32),
        scratch_shapes=[pltpu.VMEM((8, R, LANE), F32), pltpu.SemaphoreType.DMA((7,)), pltpu.SemaphoreType.DMA((7,))],
    )(v, *([] if dep is None else [dep]))


def _rope_mat(n, lo):
    half = MLA_ROPE // 2
    r = lax.broadcasted_iota(jnp.int32, (n, n), 0)
    c = lax.broadcasted_iota(jnp.int32, (n, n), 1)
    plus = (c >= lo + half) & (c < lo + 2 * half) & (r == c - half)
    minus = (c >= lo) & (c < lo + half) & (r == c + half)
    return (jnp.where(plus, 1.0, 0.0) - jnp.where(minus, 1.0, 0.0)).astype(BF16)


def _rope(v, cos, sin, lo):
    return v * cos + _exact_dot(v, _rope_mat(v.shape[-1], lo), 3) * sin


def rope_tables(pos):
    S = pos.shape[0]
    half = MLA_ROPE // 2
    inv = (np.float32(ROPE_THETA) ** (-np.arange(0, half, dtype=np.float32) * np.float32(2.0 / MLA_ROPE))).astype(np.float32)
    f1 = np.zeros((1, LANE), np.float32)
    f1[0, :MLA_ROPE] = np.tile(inv, 2)
    f2 = np.zeros((1, MLA_QK_PAD), np.float32)
    f2[0, MLA_NOPE:MLA_QK] = np.tile(inv, 2)
    tr = _pick_rows(S, 256)

    def body(p_ref, f1_ref, f2_ref, c1, s1, c2, s2):
        p = p_ref[...].astype(F32)
        a1 = p * f1_ref[...]
        a2 = p * f2_ref[...]
        c1[...] = jnp.cos(a1)
        s1[...] = jnp.sin(a1)
        c2[...] = jnp.cos(a2)
        s2[...] = jnp.sin(a2)

    def row(w):
        return pl.BlockSpec((tr, w), lambda i: (i, 0))

    def full(w):
        return pl.BlockSpec((1, w), lambda i: (0, 0))

    return pl.pallas_call(
        body, name="rope_tables", grid=(S // tr,), in_specs=[row(1), full(LANE), full(MLA_QK_PAD)],
        out_specs=[row(LANE), row(LANE), row(MLA_QK_PAD), row(MLA_QK_PAD)],
        out_shape=[jax.ShapeDtypeStruct((S, LANE), F32)] * 2 + [jax.ShapeDtypeStruct((S, MLA_QK_PAD), F32)] * 2,
        compiler_params=_cp(("parallel",)),
    )(pos, jnp.asarray(f1), jnp.asarray(f2))


def _log_sigmoid(z):
    return jnp.minimum(z, 0.0) - jnp.log(1.0 + jnp.exp(-jnp.abs(z)))


def _fox_qk_fn(q, k, gq, gk):
    return _rms(q, gq, HEAD_DIM), _rms(k, gk, HEAD_DIM)


def _fox_gate_fn(f, b):
    return (_log_sigmoid(f + b),)


def _mla_pre1_fn(q_rank, kv_rank):
    def fn(cq, ckv, kr, cos, sin, qn, kvn):
        return _rms(cq, qn, q_rank), _rms(ckv, kvn, kv_rank), _rope(kr, cos, sin, 0)
    return fn


def _mla_pre2_fn(qb, kn, kr, cos, sin, gq, gk):
    qh = _rms(_rope(qb, cos, sin, MLA_NOPE), gq, MLA_QK)
    kh = _rms(jnp.concatenate([kn, kr], axis=1), gk, MLA_QK)
    return qh, kh


def _pad_cols(a, n):
    return jnp.pad(a, ((0, 0), (0, n - a.shape[1])))


def _relu2(acc):
    r = jnp.maximum(acc, 0.0)
    return r * r, r


def _add(acc, res):
    return (acc + res,)


def _times_2r(acc, r):
    return (acc * (2.0 * r.astype(F32)),)


def kernel(x, positions, mix_norm, mlp_norm, sb_w_in, sb_w_out, fox_w_in, fox_b_f, fox_q_gain, fox_k_gain, fox_w_out, mla_w_in, mla_q_norm, mla_kv_norm, mla_w_uq, mla_w_ukv, mla_q_gain, mla_k_gain, mla_w_out, mlp_w1, mlp_w2, loss_target, m_mix_norm, m_mlp_norm, m_sb_w_in, m_sb_w_out, m_fox_w_in, m_fox_b_f, m_fox_q_gain, m_fox_k_gain, m_fox_w_out, m_mla_w_in, m_mla_q_norm, m_mla_kv_norm, m_mla_w_uq, m_mla_w_ukv, m_mla_q_gain, m_mla_k_gain, m_mla_w_out, m_mlp_w1, m_mlp_w2, v_mix_norm, v_mlp_norm, v_sb_w_in, v_sb_w_out, v_fox_w_in, v_fox_b_f, v_fox_q_gain, v_fox_k_gain, v_fox_w_out, v_mla_w_in, v_mla_q_norm, v_mla_kv_norm, v_mla_w_uq, v_mla_w_ukv, v_mla_q_gain, v_mla_k_gain, v_mla_w_out, v_mlp_w1, v_mlp_w2):
    S, D = x.shape[1], x.shape[2]
    nh = D // HEAD_DIM
    W = nh * HEAD_DIM
    depth = mix_norm.shape[0]
    q_rank, kv_rank = mla_w_uq.shape[1], mla_w_ukv.shape[1]
    n_fox_in = 3 * W + nh
    fox_pad = -(-n_fox_in // LANE) * LANE
    n_down = q_rank + kv_rank + MLA_ROPE
    down_pad = q_rank + kv_rank + LANE
    tr = _pick_rows(S, 256)
    tk = _pick_rows(S, 256)

    ax, ay, ac = lax.axis_index("x"), lax.axis_index("y"), lax.axis_index("c")
    chip = 2 * ax + ay
    c_arr = jnp.reshape(ac, (1,)).astype(jnp.int32)
    chip_arr = jnp.reshape(chip, (1,)).astype(jnp.int32)

    def bf(a):
        return a.astype(BF16)

    n_small_in = q_rank + kv_rank
    rows_in = -(-n_small_in // (8 * LANE)) * 8
    placed = jnp.zeros((rows_in * LANE,), F32)
    placed = lax.dynamic_update_slice(placed, mla_q_norm[0], (chip * mla_q_norm.shape[1],))
    placed = lax.dynamic_update_slice(placed, mla_kv_norm[0], (q_rank + chip * mla_kv_norm.shape[1],))
    placed = placed * (ac == 0).astype(F32)
    norms = all_sum_small("gather_norms", placed.reshape(rows_in, LANE)).reshape(-1)
    q_norm_full = norms[:q_rank].reshape(1, q_rank)
    kv_norm_full = norms[q_rank:q_rank + kv_rank].reshape(1, kv_rank)

    groups = [(i, part) for i in range(depth) for part in ("mix", "mlp")]
    G = {}

    def group_shards(i, part):
        kind, j = i % N_MIXERS, i // N_MIXERS
        if part == "mlp":
            return ["w1", "w2"], [bf(mlp_w1[i]), bf(mlp_w2[i])]
        if kind == 0:
            return ["w_in", "w_out"], [bf(sb_w_in[j]), bf(sb_w_out[j])]
        if kind == 1:
            return ["w_in", "w_out"], [bf(fox_w_in[j]), bf(fox_w_out[j])]
        return ["w_in", "w_uq", "w_ukv", "w_out"], [bf(mla_w_in[j]), bf(mla_w_uq[j]), bf(mla_w_ukv[j]), bf(mla_w_out[j])]

    def cross_chips_start(key, after):
        names, shards = group_shards(*key)
        handle, token = chips_start(f"gather_start_{key[1]}_{key[0]}", "gather", shards,
                                    [(N_CHIPS,) + s.shape for s in shards], after)
        G[key] = dict(names=names, ici=handle, token=token)
        return token

    def cross_cores_start(key, after):
        g = G[key]
        g["lands"], g["shards"] = chips_wait(f"gather_wait_{key[1]}_{key[0]}", g["ici"], after)
        g["d2d"], token = chips_start(f"pass_start_{key[1]}_{key[0]}", "pass", g["lands"], [])
        return token

    def group_ready(key, after):
        g = G[key]
        _, bufs = chips_wait(f"pass_wait_{key[1]}_{key[0]}", g["d2d"], after)
        return dict(zip(g["names"], [lax.dynamic_update_slice(b, s[None], (chip, 0, 0)) for b, s in zip(bufs, g["shards"])]))

    def before_part(s, after):
        key = groups[s]
        if key not in G:
            cross_chips_start(key, norms)
        if "d2d" not in G[key]:
            cross_cores_start(key, after)
        L = group_ready(key, after)
        token = jnp.zeros((1, 1), F32)
        if s + 1 < len(groups) and groups[s + 1] in G:
            token = token + cross_cores_start(groups[s + 1], after)
        last = L[G[key]["names"][0]]
        for ahead in groups[s + 1:s + 1 + GATHER_AHEAD]:
            if ahead not in G:
                token = token + cross_chips_start(ahead, last)
            last = G[ahead]["token"]
        return L, token

    def rows_stacked(w):
        return w.reshape(w.shape[0] * w.shape[1], w.shape[2])

    xc = x[0]
    saved = []
    layers = []
    tables = None
    for i in range(depth):
        kind, j = i % N_MIXERS, i // N_MIXERS
        L, token = before_part(2 * i, xc if i else None)
        g1 = mix_norm[i:i + 1] + token
        hb = rmsnorm_fwd(f"norm1_{i}", xc, g1)
        st = dict(x=xc, hb=hb)
        if kind == 0:
            qkv = mm_nn(f"sb_proj_{i}", hb, L["w_in"], BF16)[0]
            o, o_f32 = sb_fwd(f"sb_attn_{i}", qkv, nh)
            st.update(qkv=qkv, o_f32=o_f32)
        elif kind == 1:
            w_in = _pad_cols(jnp.concatenate([L["w_in"][s] for s in range(N_CHIPS)], axis=1), fox_pad)
            proj = mm_nn(f"fox_proj_{i}", hb, w_in, F32)[0]
            gq, gk = fox_q_gain[j:j + 1], fox_k_gain[j:j + 1]
            qk_rows = [(proj, HEAD_DIM, lambda h: h), (proj, HEAD_DIM, lambda h: nh + h)]
            qh, kh = rowwise_fwd(f"fox_qk_{i}", _fox_qk_fn, qk_rows, [gq, gk],
                                 [(W, HEAD_DIM, lambda h: h, BF16)] * 2, tr, nh)
            b_pad = _pad_cols(fox_b_f[j:j + 1], LANE)
            gate_rows = [(proj, LANE, lambda h: 3 * nh)]
            logf = rowwise_fwd(f"fox_gate_{i}", _fox_gate_fn, gate_rows, [b_pad], [(LANE, LANE, _c0, F32)], tr)[0]
            cf = seq_cumsum(f"fox_cf_{i}", logf, False)[:, :nh].T
            cf_col, cf_row = cf.reshape(nh, S, 1), cf.reshape(nh, S // tk, tk)
            scale = 1.0 / math.sqrt(HEAD_DIM)
            o, o_f32, lse = attn_fwd(f"fox_attn_{i}", qh, kh, proj, nh, HEAD_DIM, 0, 0, 2 * nh, 1, scale, cf_col, cf_row)
            st.update(w_in=w_in, proj=proj, qk_rows=qk_rows, gq=gq, gk=gk, qh=qh, kh=kh, b_pad=b_pad, gate_rows=gate_rows,
                      cf_col=cf_col, cf_row=cf_row, lse=lse, scale=scale, o_f32=o_f32)
        else:
            w_in = _pad_cols(rows_stacked(L["w_in"]), down_pad)
            down = mm_nn(f"mla_down_{i}", hb, w_in, F32)[0]
            if tables is None:
                tables = rope_tables(positions.reshape(S, 1))
            cos1, sin1, cos2, sin2 = tables
            pre1_rows = [(down[:, :q_rank], q_rank, _c0), (down[:, q_rank:q_rank + kv_rank], kv_rank, _c0),
                         (down[:, q_rank + kv_rank:], LANE, _c0), (cos1, LANE, _c0), (sin1, LANE, _c0)]
            pre1_fn = _mla_pre1_fn(q_rank, kv_rank)
            c_q, c_kv, k_rope = rowwise_fwd(
                f"mla_pre1_{i}", pre1_fn, pre1_rows, [q_norm_full, kv_norm_full],
                [(q_rank, q_rank, _c0, BF16), (kv_rank, kv_rank, _c0, BF16), (LANE, LANE, _c0, F32)], tr)
            qfull = mm_nn(f"mla_uq_{i}", c_q, L["w_uq"], F32)[0]
            kv = mm_nn(f"mla_ukv_{i}", c_kv, L["w_ukv"], F32)[0]
            qpad = jnp.pad(qfull.reshape(S, nh, MLA_QK), ((0, 0), (0, 0), (0, MLA_QK_PAD - MLA_QK))).reshape(S, nh * MLA_QK_PAD)
            gq, gk = _pad_cols(mla_q_gain[j:j + 1], MLA_QK_PAD), _pad_cols(mla_k_gain[j:j + 1], MLA_QK_PAD)
            pre2_rows = [(qpad, MLA_QK_PAD, lambda h: h), (kv, MLA_NOPE, lambda h: 2 * h), (k_rope, LANE, _c0),
                         (cos2, MLA_QK_PAD, _c0), (sin2, MLA_QK_PAD, _c0)]
            qh, kh = rowwise_fwd(f"mla_pre2_{i}", _mla_pre2_fn, pre2_rows, [gq, gk],
                                 [(nh * MLA_QK_PAD, MLA_QK_PAD, lambda h: h, BF16)] * 2, tr, nh)
            scale = 1.0 / math.sqrt(MLA_QK)
            o, o_f32, lse = attn_fwd(f"mla_attn_{i}", qh, kh, kv, nh, MLA_QK_PAD, 0, 0, 1, 2, scale)
            st.update(w_in=w_in, pre1_rows=pre1_rows, pre1_fn=pre1_fn, c_q=c_q, c_kv=c_kv, pre2_rows=pre2_rows, gq=gq, gk=gk,
                      qh=qh, kh=kh, kv=kv, lse=lse, scale=scale, o_f32=o_f32)
        x1 = mm_nn(f"mix_out_{i}", o, rows_stacked(L["w_out"]), F32, epilogue=_add, extras=(xc,))[0]
        L_mlp, token = before_part(2 * i + 1, x1)
        L.update(L_mlp)
        layers.append(L)
        g2 = mlp_norm[i:i + 1] + token
        h2 = rmsnorm_fwd(f"norm2_{i}", x1, g2)
        a, r = mm_nn(f"mlp_up_{i}", h2, L["w1"], None, epilogue=_relu2, out_dtypes=[BF16, BF16])
        xc = mm_nn(f"mlp_down_{i}", a, rows_stacked(L["w2"]), F32, epilogue=_add, extras=(x1,))[0]
        st.update(o=o, x1=x1, h2=h2, a=a, r=r, g1=g1, g2=g2)
        saved.append(st)

    loss_local, dx, dxb = loss_head(xc, loss_target[0])
    loss = lax.psum(loss_local[0, 0], ("x", "y", "c"))

    small = {}

    def stack_rows(g):
        return g.reshape(N_CHIPS, g.shape[0] // N_CHIPS, g.shape[1])

    full = [dict() for _ in range(depth)]
    sums = []

    def sums_advance(after):
        token = jnp.zeros((1, 1), F32)
        for e in sums:
            tag = e["tag"]
            if "back" in e:
                others, halves = chips_wait(f"sendh_wait_{tag}", e.pop("back"), after)
                full[e["layer"]].update(zip(e["names"], zip(halves, others)))
                e["done"] = True
            elif "chips" in e:
                got, parts = chips_wait(f"xchg_wait_{tag}", e.pop("chips"), after)
                halves = [sum_slabs(f"sum4_{tag}_{t}", g, p, chip_arr) for t, (g, p) in enumerate(zip(got, parts))]
                e["back"], tok = chips_start(f"sendh_start_{tag}", "whole", halves, [h.shape for h in halves])
                token = token + tok
            elif "pair" in e:
                theirs, grads = chips_wait(f"swap_wait_{tag}", e.pop("pair"), after)
                parts = [add_half(f"addh_{tag}_{t}", g, o, c_arr) for t, (g, o) in enumerate(zip(grads, theirs))]
                e["chips"], tok = chips_start(f"xchg_start_{tag}", "scatter", parts, [p.shape for p in parts])
                token = token + tok
            elif not e.get("done"):
                grads = e.pop("grads")
                e["pair"], tok = chips_start(f"swap_start_{tag}", "swap", grads,
                                             [(N_CHIPS, g.shape[1] // 2, g.shape[2]) for g in grads])
                token = token + tok
        sums[:] = [e for e in sums if not e.get("done")]
        return token

    def sums_add(i, part, gr, after):
        sums.append(dict(layer=i, tag=f"{part}_{i}", names=list(gr), grads=list(gr.values())))
        return sums_advance(after)

    dep = None
    for i in reversed(range(depth)):
        kind, j = i % N_MIXERS, i // N_MIXERS
        L, st = layers[i], saved[i]
        gr = {}
        gr["w2"] = stack_rows(mm_tn(f"mlp_dw2_{i}", st["a"], dxb, BF16, False, dep=dep))
        du = mm_nt(f"mlp_du_{i}", dxb, rows_stacked(L["w2"]), BF16, epilogue=_times_2r, extras=(st["r"],), dep=dep)
        gr["w1"] = mm_tn(f"mlp_dw1_{i}", st["h2"], du, BF16, True)
        dh2 = mm_nt(f"mlp_dh_{i}", du, L["w1"], F32)
        dx1, dx1b, dg2 = rmsnorm_bwd(f"norm2_bwd_{i}", st["x1"], st["g2"], dh2, dx)
        small[("mlp_norm", i)] = dg2
        dep = sums_add(i, "mlp", gr, dx1)
        gr = {}
        gr["w_out"] = stack_rows(mm_tn(f"mix_dwout_{i}", st["o"], dx1b, BF16, False, dep=dep))
        do = mm_nt(f"mix_do_{i}", dx1b, rows_stacked(L["w_out"]), BF16, dep=dep)
        if kind == 0:
            dqkv = sb_bwd(f"sb_attn_bwd_{i}", st["qkv"], st["o_f32"], do, nh)
            gr["w_in"] = mm_tn(f"sb_dwin_{i}", st["hb"], dqkv, BF16, True)
            dh = mm_nt(f"sb_dh_{i}", dqkv, L["w_in"], F32)
        elif kind == 1:
            dqh, dkh, dv, dcf = attn_bwd(f"fox_attn_bwd_{i}", st["qh"], st["kh"], st["proj"], st["o_f32"], do, st["lse"], nh,
                                         HEAD_DIM, 0, 0, 2 * nh, 1, st["scale"], st["cf_col"], st["cf_row"])
            dcf_s = _pad_cols(dcf.reshape(nh, S).T, LANE)
            dlogf = seq_cumsum(f"fox_dcf_{i}", dcf_s, True)
            dgate, db = rowwise_bwd(f"fox_gate_bwd_{i}", _fox_gate_fn, st["gate_rows"], [st["b_pad"]], [(dlogf, LANE, _c0)],
                                    [(LANE, LANE, _c0, BF16, False)], tr)
            dq, dk, dgq, dgk = rowwise_bwd(
                f"fox_qk_bwd_{i}", _fox_qk_fn, st["qk_rows"], [st["gq"], st["gk"]],
                [(dqh, HEAD_DIM, lambda h: h), (dkh, HEAD_DIM, lambda h: h)], [(W, HEAD_DIM, lambda h: h, BF16, False)] * 2, tr, nh)
            small[("fox_b_f", j)] = db[:, :nh]
            small[("fox_q_gain", j)] = dgq
            small[("fox_k_gain", j)] = dgk
            dproj = jnp.concatenate([dq, dk, bf(dv), dgate], axis=1)
            dw = mm_tn(f"fox_dwin_{i}", st["hb"], dproj, BF16, False)
            n4 = n_fox_in // N_CHIPS
            gr["w_in"] = jnp.stack([dw[:, s * n4:(s + 1) * n4] for s in range(N_CHIPS)])
            dh = mm_nt(f"fox_dh_{i}", dproj, st["w_in"], F32)
        else:
            dqh, dkh, dv = attn_bwd(f"mla_attn_bwd_{i}", st["qh"], st["kh"], st["kv"], st["o_f32"], do, st["lse"], nh,
                                    MLA_QK_PAD, 0, 0, 1, 2, st["scale"])
            dqpad, dkn, dkr, dgq, dgk = rowwise_bwd(
                f"mla_pre2_bwd_{i}", _mla_pre2_fn, st["pre2_rows"], [st["gq"], st["gk"]],
                [(dqh, MLA_QK_PAD, lambda h: h), (dkh, MLA_QK_PAD, lambda h: h)],
                [(nh * MLA_QK_PAD, MLA_QK_PAD, lambda h: h, BF16, False), (W, MLA_NOPE, lambda h: h, BF16, False),
                 (LANE, LANE, _c0, F32, True)], tr, nh, n_diff=3)
            small[("mla_q_gain", j)] = dgq[:, :MLA_QK]
            small[("mla_k_gain", j)] = dgk[:, :MLA_QK]
            dqfull = dqpad.reshape(S, nh, MLA_QK_PAD)[:, :, :MLA_QK].reshape(S, nh * MLA_QK)
            dkv = jnp.stack([dkn.reshape(S, nh, MLA_NOPE), bf(dv).reshape(S, nh, MLA_V)], axis=2).reshape(S, nh * (MLA_NOPE + MLA_V))
            gr["w_uq"] = mm_tn(f"mla_dwuq_{i}", st["c_q"], dqfull, BF16, True)
            dc_q = mm_nt(f"mla_dcq_{i}", dqfull, L["w_uq"], F32)
            gr["w_ukv"] = mm_tn(f"mla_dwukv_{i}", st["c_kv"], dkv, BF16, True)
            dc_kv = mm_nt(f"mla_dckv_{i}", dkv, L["w_ukv"], F32)
            d1, d2, d3, dqn, dkvn = rowwise_bwd(
                f"mla_pre1_bwd_{i}", st["pre1_fn"], st["pre1_rows"], [q_norm_full, kv_norm_full],
                [(dc_q, q_rank, _c0), (dc_kv, kv_rank, _c0), (dkr, LANE, _c0)],
                [(q_rank, q_rank, _c0, BF16, False), (kv_rank, kv_rank, _c0, BF16, False), (LANE, LANE, _c0, BF16, False)],
                tr, n_diff=3)
            small[("mla_q_norm", j)] = dqn
            small[("mla_kv_norm", j)] = dkvn
            ddown = jnp.concatenate([d1, d2, d3], axis=1)
            dw = mm_tn(f"mla_dwin_{i}", st["hb"], ddown, BF16, False)
            gr["w_in"] = stack_rows(dw[:, :n_down])
            dh = mm_nt(f"mla_dh_{i}", ddown, st["w_in"], F32)
        dx, dxb, dg1 = rmsnorm_bwd(f"norm1_bwd_{i}", st["x"], st["g1"], dh, dx1)
        small[("mix_norm", i)] = dg1
        dep = sums_add(i, "mix", gr, dx)

    def holders(kind_of):
        return [i for i in range(depth) if kind_of is None or i % N_MIXERS == kind_of]

    tensors = {
        "sb_w_in": (sb_w_in, m_sb_w_in, v_sb_w_in, "w_in", holders(0)),
        "sb_w_out": (sb_w_out, m_sb_w_out, v_sb_w_out, "w_out", holders(0)),
        "fox_w_in": (fox_w_in, m_fox_w_in, v_fox_w_in, "w_in", holders(1)),
        "fox_w_out": (fox_w_out, m_fox_w_out, v_fox_w_out, "w_out", holders(1)),
        "mla_w_in": (mla_w_in, m_mla_w_in, v_mla_w_in, "w_in", holders(2)),
        "mla_w_uq": (mla_w_uq, m_mla_w_uq, v_mla_w_uq, "w_uq", holders(2)),
        "mla_w_ukv": (mla_w_ukv, m_mla_w_ukv, v_mla_w_ukv, "w_ukv", holders(2)),
        "mla_w_out": (mla_w_out, m_mla_w_out, v_mla_w_out, "w_out", holders(2)),
        "mlp_w1": (mlp_w1, m_mlp_w1, v_mlp_w1, "w1", holders(None)),
        "mlp_w2": (mlp_w2, m_mlp_w2, v_mlp_w2, "w2", holders(None)),
    }
    updated = {n: None for n in tensors}

    def update_ready(dep):
        outs = []
        for n, (w, m, v, key, held) in tensors.items():
            for l in reversed(range(len(held))):
                if (n, l) not in applied and key in full[held[l]]:
                    mine, other = full[held[l]][key]
                    updated[n] = adamw_big(f"adamw_{n}_{l}", w, m, v, mine, other, c_arr, l, updated[n], dep)
                    applied.add((n, l))
                    outs.append(updated[n][3][l, :8, :LANE])
        return outs

    applied = set()
    after = dx
    while sums:
        outs = update_ready(dep)
        after = outs if outs else after
        dep = sums_advance(after)

    last = update_ready(dep)

    keys = list(small)
    flat = jnp.concatenate([small[k].reshape(-1) for k in keys])
    rows_g = -(-flat.shape[0] // (8 * LANE)) * 8
    flat = jnp.pad(flat, (0, rows_g * LANE - flat.shape[0]))
    tail = last or (after if isinstance(after, list) else [after])
    summed = all_sum_small("sum_small", flat.reshape(rows_g, LANE), dep=tail[0]).reshape(-1)
    sg, off = {}, 0
    for k in keys:
        n = small[k].size
        sg[k] = summed[off:off + n].reshape(small[k].shape)
        off += n

    def update_big(name):
        return list(updated[name])

    def update_small(name, w, m, v, g):
        return [g] + list(adamw(f"adamw_{name}", w, g, m, v))

    def small_rows(name, count):
        return jnp.concatenate([sg[(name, l)] for l in range(count)], axis=0)

    def my_part(g, n):
        return lax.dynamic_slice(g, (0, chip * n), (g.shape[0], n))

    res = {
        "mix_norm": update_small("mix_norm", mix_norm, m_mix_norm, v_mix_norm, small_rows("mix_norm", depth)),
        "mlp_norm": update_small("mlp_norm", mlp_norm, m_mlp_norm, v_mlp_norm, small_rows("mlp_norm", depth)),
        "sb_w_in": update_big("sb_w_in"),
        "sb_w_out": update_big("sb_w_out"),
        "fox_w_in": update_big("fox_w_in"),
        "fox_b_f": update_small("fox_b_f", fox_b_f, m_fox_b_f, v_fox_b_f, small_rows("fox_b_f", fox_b_f.shape[0])),
        "fox_q_gain": update_small("fox_q_gain", fox_q_gain, m_fox_q_gain, v_fox_q_gain, small_rows("fox_q_gain", fox_q_gain.shape[0])),
        "fox_k_gain": update_small("fox_k_gain", fox_k_gain, m_fox_k_gain, v_fox_k_gain, small_rows("fox_k_gain", fox_k_gain.shape[0])),
        "fox_w_out": update_big("fox_w_out"),
        "mla_w_in": update_big("mla_w_in"),
        "mla_q_norm": update_small("mla_q_norm", mla_q_norm, m_mla_q_norm, v_mla_q_norm,
                                   my_part(small_rows("mla_q_norm", mla_q_norm.shape[0]), mla_q_norm.shape[1])),
        "mla_kv_norm": update_small("mla_kv_norm", mla_kv_norm, m_mla_kv_norm, v_mla_kv_norm,
                                    my_part(small_rows("mla_kv_norm", mla_kv_norm.shape[0]), mla_kv_norm.shape[1])),
        "mla_w_uq": update_big("mla_w_uq"),
        "mla_w_ukv": update_big("mla_w_ukv"),
        "mla_q_gain": update_small("mla_q_gain", mla_q_gain, m_mla_q_gain, v_mla_q_gain, small_rows("mla_q_gain", mla_q_gain.shape[0])),
        "mla_k_gain": update_small("mla_k_gain", mla_k_gain, m_mla_k_gain, v_mla_k_gain, small_rows("mla_k_gain", mla_k_gain.shape[0])),
        "mla_w_out": update_big("mla_w_out"),
        "mlp_w1": update_big("mlp_w1"),
        "mlp_w2": update_big("mlp_w2"),
    }
    order = ["mix_norm", "mlp_norm", "sb_w_in", "sb_w_out", "fox_w_in", "fox_b_f", "fox_q_gain", "fox_k_gain", "fox_w_out",
             "mla_w_in", "mla_q_norm", "mla_kv_norm", "mla_w_uq", "mla_w_ukv", "mla_q_gain", "mla_k_gain", "mla_w_out",
             "mlp_w1", "mlp_w2"]
    outs = [loss, dx.reshape(x.shape)]
    for k in range(4):
        outs += [res[n][k] for n in order]
    return tuple(outs)
```

```python
import functools
import math

import numpy as np
import jax
import jax.numpy as jnp
from jax import lax
from jax.experimental import pallas as pl
from jax.experimental.pallas import tpu as pltpu

F32 = jnp.float32
BF16 = jnp.bfloat16
MESH = pl.DeviceIdType.MESH

EPS = 1e-6
HEAD_DIM = 128
MLA_NOPE = 128
MLA_ROPE = 64
MLA_V = 128
MLA_QK = MLA_NOPE + MLA_ROPE
MLA_QK_PAD = 256
ROPE_THETA = 10000.0
N_MIXERS = 3
ADAM_LR = 0.001
ADAM_B1 = 0.9
ADAM_B2 = 0.999
ADAM_EPS = 1e-08
ADAM_WD = 0.01
ADAM_STEP = 10

LANE = 128
N_CHIPS = 4
GATHER_AHEAD = 3
VMEM_LIMIT = 48 * 1024 * 1024
NEG = -1e30


def _cp(sem):
    return pltpu.CompilerParams(dimension_semantics=sem, vmem_limit_bytes=VMEM_LIMIT)


def _pick(dim, cap):
    best = None
    b = LANE
    while b <= min(dim, cap):
        if dim % b == 0:
            best = b
        b += LANE
    return best if best is not None else dim


def _pick_rows(dim, cap):
    b = min(dim, cap)
    while dim % b:
        b -= 8
    return b


def _matmul(name, a, b, a_blk, a_map, b_blk, b_map, dn, grid, acc_shape, outs, extras=(), epilogue=None, dep=None):
    nk = grid[2]
    n_ex, n_out = len(extras), len(outs)
    n_dep = 0 if dep is None else 1

    def body(*refs):
        a_ref, b_ref = refs[0], refs[1]
        ex_refs = refs[2:2 + n_ex]
        out_refs = refs[2 + n_ex + n_dep:2 + n_ex + n_dep + n_out]
        acc = refs[-1]
        k = pl.program_id(2)

        @pl.when(k == 0)
        def _():
            acc[...] = jnp.zeros_like(acc)

        acc[...] += lax.dot_general(a_ref[...], b_ref[...], dn, preferred_element_type=F32)

        @pl.when(k == nk - 1)
        def _():
            res = acc[...]
            vals = epilogue(res, *[e[...] for e in ex_refs]) if epilogue is not None else (res,)
            for o, v in zip(out_refs, vals):
                o[...] = v.astype(o.dtype)

    in_specs = [pl.BlockSpec(a_blk, a_map), pl.BlockSpec(b_blk, b_map)]
    in_specs += [pl.BlockSpec(blk, lambda i, j, k, m=m: m(i, j)) for (_, blk, m) in extras]
    in_specs += [pl.BlockSpec(memory_space=pl.ANY)] * n_dep
    out_specs = [pl.BlockSpec(blk, lambda i, j, k, m=m: m(i, j)) for (_, _, blk, m) in outs]
    out_shape = [jax.ShapeDtypeStruct(s, d) for (s, d, _, _) in outs]
    res = pl.pallas_call(
        body, name=name, grid=grid, in_specs=in_specs, out_specs=out_specs, out_shape=out_shape,
        scratch_shapes=[pltpu.VMEM(acc_shape, F32)],
        compiler_params=_cp(("parallel", "parallel", "arbitrary")),
    )(a, b, *[e[0] for e in extras], *([] if dep is None else [dep]))
    return res


BM, BN, BK = 1024, 1024, 2048


def mm_nn(name, a, w, out_dtype, epilogue=None, extras=(), n_out=1, out_dtypes=None):
    M, K = a.shape
    stacked = w.ndim == 3
    n4 = w.shape[-1]
    N = n4 * (N_CHIPS if stacked else 1)
    bm, bn, bk = _pick_rows(M, BM), _pick(n4, BN), _pick(K, BK)
    nb = n4 // bn
    if stacked:
        b_blk, b_map = (None, bk, bn), (lambda i, j, k: (j // nb, k, j % nb))
    else:
        b_blk, b_map = (bk, bn), (lambda i, j, k: (k, j))
    dts = out_dtypes if out_dtypes is not None else [out_dtype] * n_out
    outs = [((M, N), d, (bm, bn), lambda i, j: (i, j)) for d in dts]
    exs = [(e, (bm, bn), lambda i, j: (i, j)) for e in extras]
    return _matmul(name, a, w, (bm, bk), lambda i, j, k: (i, k), b_blk, b_map,
                   (((1,), (0,)), ((), ())), (M // bm, N // bn, K // bk), (bm, bn), outs, exs, epilogue)


def mm_nt(name, dy, w, out_dtype, epilogue=None, extras=(), dep=None):
    M, N = dy.shape
    stacked = w.ndim == 3
    K, n4 = w.shape[-2], w.shape[-1]
    bm, bn, bk = _pick_rows(M, BM), _pick(K, BN), _pick(n4, BK)
    nb = n4 // bk
    if stacked:
        b_blk, b_map = (None, bn, bk), (lambda i, j, k: (k // nb, j, k % nb))
    else:
        b_blk, b_map = (bn, bk), (lambda i, j, k: (j, k))
    outs = [((M, K), out_dtype, (bm, bn), lambda i, j: (i, j))]
    exs = [(e, (bm, bn), lambda i, j: (i, j)) for e in extras]
    return _matmul(name, dy, w, (bm, bk), lambda i, j, k: (i, k), b_blk, b_map,
                   (((1,), (1,)), ((), ())), (M // bm, K // bn, N // bk), (bm, bn), outs, exs, epilogue, dep)[0]


def mm_tn(name, a, dy, out_dtype, stacked, dep=None):
    M, K = a.shape
    N = dy.shape[1]
    n4 = N // N_CHIPS if stacked else N
    bm, bn, bk = _pick(K, BM), _pick(n4, BN), _pick_rows(M, BK)
    nb = n4 // bn
    if stacked:
        outs = [((N_CHIPS, K, n4), out_dtype, (None, bm, bn), lambda i, j: (j // nb, i, j % nb))]
    else:
        outs = [((K, N), out_dtype, (bm, bn), lambda i, j: (i, j))]
    return _matmul(name, a, dy, (bk, bm), lambda i, j, k: (k, i), (bk, bn), lambda i, j, k: (k, j),
                   (((0,), (0,)), ((), ())), (K // bm, N // bn, M // bk), (bm, bn), outs, dep=dep)[0]


def _split3(x):
    hi = x.astype(BF16)
    r = x - hi.astype(F32)
    mid = r.astype(BF16)
    lo = (r - mid.astype(F32)).astype(BF16)
    return hi, mid, lo


def _exact_dot(x, m, n):
    out = None
    for p in _split3(x)[:n]:
        d = jnp.dot(p, m, preferred_element_type=F32)
        out = d if out is None else out + d
    return out


def _row_spec(tr, width, cmap):
    return pl.BlockSpec((tr, width), lambda i, h: (i, cmap(h)))


def _full_spec(p):
    return pl.BlockSpec(p.shape, lambda i, h: (0,) * p.ndim)


def rowwise_fwd(name, fn, rows, params, outs, tr, nh=1):
    S = rows[0][0].shape[0]
    nr, npar = len(rows), len(params)

    def body(*refs):
        vals = fn(*[r[...].astype(F32) for r in refs[:nr]], *[p[...] for p in refs[nr:nr + npar]])
        for o, v in zip(refs[nr + npar:], vals):
            o[...] = v.astype(o.dtype)

    return pl.pallas_call(
        body, name=name, grid=(S // tr, nh),
        in_specs=[_row_spec(tr, w, cm) for (_, w, cm) in rows] + [_full_spec(p) for p in params],
        out_specs=[_row_spec(tr, w, cm) for (_, w, cm, _) in outs],
        out_shape=[jax.ShapeDtypeStruct((S, c), d) for (c, _, _, d) in outs],
        compiler_params=_cp(("parallel", "arbitrary")),
    )(*[r[0] for r in rows], *params)


def rowwise_bwd(name, fn, rows, params, cts, grads, tr, nh=1, n_diff=None, add=None):
    S = rows[0][0].shape[0]
    nr, npar, nct = len(rows), len(params), len(cts)
    n_diff = nr if n_diff is None else n_diff
    n_add = 1 if add is not None else 0

    def body(*refs):
        row_refs = refs[:nr]
        par_refs = refs[nr:nr + npar]
        ct_refs = refs[nr + npar:nr + npar + nct]
        add_refs = refs[nr + npar + nct:nr + npar + nct + n_add]
        o = nr + npar + nct + n_add
        g_refs = refs[o:o + n_diff]
        cp_refs = refs[o + n_diff:o + n_diff + n_add]
        pg_refs = refs[o + n_diff + n_add:]
        i, h = pl.program_id(0), pl.program_id(1)
        rv = [r[...].astype(F32) for r in row_refs]
        pv = [p[...] for p in par_refs]
        aux = rv[n_diff:]

        def f(*dp):
            return fn(*dp[:n_diff], *aux, *dp[n_diff:])

        _, vjp = jax.vjp(f, *rv[:n_diff], *pv)
        gs = vjp(tuple(c[...].astype(F32) for c in ct_refs))
        for n, (g_ref, (_, _, _, _, over)) in enumerate(zip(g_refs, grads)):
            g = gs[n]
            if n == 0 and add is not None:
                g = g + add_refs[0][...]
                cp_refs[0][...] = g.astype(BF16)
            if over:
                @pl.when(h == 0)
                def _(g_ref=g_ref):
                    g_ref[...] = jnp.zeros_like(g_ref)
                g_ref[...] += g.astype(g_ref.dtype)
            else:
                g_ref[...] = g.astype(g_ref.dtype)
        for pg_ref, g in zip(pg_refs, gs[n_diff:]):
            @pl.when((i == 0) & (h == 0))
            def _(pg_ref=pg_ref):
                pg_ref[...] = jnp.zeros_like(pg_ref)
            pg_ref[...] += g

    in_specs = [_row_spec(tr, w, cm) for (_, w, cm) in rows] + [_full_spec(p) for p in params]
    in_specs += [_row_spec(tr, w, cm) for (_, w, cm) in cts]
    operands = [r[0] for r in rows] + list(params) + [c[0] for c in cts]
    out_specs = [_row_spec(tr, w, cm) for (_, w, cm, _, _) in grads]
    out_shape = [jax.ShapeDtypeStruct((S, c), d) for (c, _, _, d, _) in grads]
    if add is not None:
        in_specs.append(_row_spec(tr, add[1], add[2]))
        operands.append(add[0])
        out_specs.append(_row_spec(tr, add[1], add[2]))
        out_shape.append(jax.ShapeDtypeStruct(add[0].shape, BF16))
    out_specs += [_full_spec(p) for p in params]
    out_shape += [jax.ShapeDtypeStruct(p.shape, F32) for p in params]
    return pl.pallas_call(
        body, name=name, grid=(S // tr, nh), in_specs=in_specs, out_specs=out_specs, out_shape=out_shape,
        compiler_params=_cp(("arbitrary", "arbitrary")),
    )(*operands)


def _c0(h):
    return 0


def _rms(x, g, n):
    return x * lax.rsqrt(jnp.sum(x * x, axis=-1, keepdims=True) * (1.0 / n) + EPS) * g


def _rmsnorm_fn(d):
    def fn(x, g):
        return (_rms(x, g, d),)
    return fn


def rmsnorm_fwd(name, x, g):
    S, D = x.shape
    return rowwise_fwd(name, _rmsnorm_fn(D), [(x, D, _c0)], [g], [(D, D, _c0, BF16)], _pick_rows(S, 256))[0]


def rmsnorm_bwd(name, x, g, dh, dres):
    S, D = x.shape
    return rowwise_bwd(name, _rmsnorm_fn(D), [(x, D, _c0)], [g], [(dh, D, _c0)], [(D, D, _c0, F32, False)],
                       _pick_rows(S, 256), add=(dres, D, _c0))


def loss_head(y, t):
    S, D = y.shape
    tr = _pick_rows(S, 256)

    def body(y_ref, t_ref, l_ref, d_ref, db_ref):
        @pl.when(pl.program_id(0) == 0)
        def _():
            l_ref[...] = jnp.zeros_like(l_ref)
        e = y_ref[...] - t_ref[...]
        l_ref[...] += 0.5 * jnp.sum(jnp.sum(e * e, axis=1, keepdims=True), axis=0, keepdims=True) * (1.0 / D)
        d = e * (1.0 / D)
        d_ref[...] = d
        db_ref[...] = d.astype(BF16)

    row = pl.BlockSpec((tr, D), lambda i: (i, 0))
    return pl.pallas_call(
        body, name="loss_head", grid=(S // tr,), in_specs=[row, row],
        out_specs=[pl.BlockSpec((1, 1), lambda i: (0, 0)), row, row],
        out_shape=[jax.ShapeDtypeStruct((1, 1), F32), jax.ShapeDtypeStruct((S, D), F32), jax.ShapeDtypeStruct((S, D), BF16)],
        compiler_params=_cp(("arbitrary",)),
    )(y, t)


def _dot_nt(a, b):
    return lax.dot_general(a, b, (((1,), (1,)), ((), ())), preferred_element_type=F32)


def _dot_tn(a, b):
    return lax.dot_general(a, b, (((0,), (0,)), ((), ())), preferred_element_type=F32)


AT_TQ, AT_TK = 512, 256


def attn_fwd(name, q, k, v, nh, dq, qoff, koff, voff, vstep, scale, cf_col=None, cf_row=None):
    S = q.shape[0]
    tq = _pick_rows(S, AT_TQ)
    tk = _pick_rows(S, AT_TK)
    nq = S // tq
    bias = cf_col is not None

    def body(*refs):
        if bias:
            q_ref, k_ref, v_ref, cfc_ref, cfr_ref, o_ref, of_ref, lse_ref = refs
        else:
            q_ref, k_ref, v_ref, o_ref, of_ref, lse_ref = refs
        i = pl.program_id(1)
        qv = q_ref[...].astype(BF16)
        row = i * tq + lax.broadcasted_iota(jnp.int32, (tq, tk), 0)
        coli = lax.broadcasted_iota(jnp.int32, (tq, tk), 1)
        nkb = ((i + 1) * tq + tk - 1) // tk

        def logits(j):
            return _dot_nt(qv, k_ref[pl.ds(pl.multiple_of(j * tk, tk), tk), :].astype(BF16)) * scale

        def step(j, carry):
            s, m, l, acc, acc_lo = carry
            s_next = logits(jnp.minimum(j + 1, nkb - 1))
            vs = v_ref[pl.ds(pl.multiple_of(j * tk, tk), tk), :].astype(BF16)
            if bias:
                s = s + cfc_ref[0] - cfr_ref[0, pl.ds(j, 1), :]
            s = jnp.where(j * tk + coli <= row, s, NEG)
            m_new = jnp.maximum(m, jnp.max(s, axis=1, keepdims=True))
            alpha = jnp.exp(m - m_new)
            p = jnp.exp(s - m_new)
            pb = p.astype(BF16)
            l = alpha * l + jnp.sum(p, axis=1, keepdims=True)
            acc = alpha * acc + jnp.dot(pb, vs, preferred_element_type=F32)
            acc_lo = alpha * acc_lo + jnp.dot((p - pb.astype(F32)).astype(BF16), vs, preferred_element_type=F32)
            return s_next, m_new, l, acc, acc_lo

        z = jnp.zeros((tq, HEAD_DIM), F32)
        _, m, l, acc, acc_lo = lax.fori_loop(
            0, nkb, step, (logits(0), jnp.full((tq, 1), NEG, F32), jnp.zeros((tq, 1), F32), z, z))
        o_ref[...] = (acc / l).astype(o_ref.dtype)
        of_ref[...] = (acc + acc_lo) / l
        lse_ref[0] = m + jnp.log(l)

    in_specs = [pl.BlockSpec((tq, dq), lambda h, i: (i, qoff + h)),
                pl.BlockSpec((S, dq), lambda h, i: (0, koff + h)),
                pl.BlockSpec((S, HEAD_DIM), lambda h, i: (0, voff + vstep * h))]
    operands = [q, k, v]
    if bias:
        in_specs += [pl.BlockSpec((1, tq, 1), lambda h, i: (h, i, 0)), pl.BlockSpec((1, S // tk, tk), lambda h, i: (h, 0, 0))]
        operands += [cf_col, cf_row]
    return pl.pallas_call(
        body, name=name, grid=(nh, nq), in_specs=in_specs,
        out_specs=[pl.BlockSpec((tq, HEAD_DIM), lambda h, i: (i, h)), pl.BlockSpec((tq, HEAD_DIM), lambda h, i: (i, h)),
                   pl.BlockSpec((1, tq, 1), lambda h, i: (h, i, 0))],
        out_shape=[jax.ShapeDtypeStruct((S, nh * HEAD_DIM), BF16), jax.ShapeDtypeStruct((S, nh * HEAD_DIM), F32),
                   jax.ShapeDtypeStruct((nh, S, 1), F32)],
        compiler_params=_cp(("parallel", "arbitrary")),
    )(*operands)


def attn_bwd(name, q, k, v, o, do, lse, nh, dq, qoff, koff, voff, vstep, scale, cf_col=None, cf_row=None):
    S = q.shape[0]
    tq = _pick_rows(S, AT_TQ)
    tk = _pick_rows(S, AT_TK)
    nq = S // tq
    bias = cf_col is not None

    def body(*refs):
        if bias:
            q_ref, k_ref, v_ref, o_ref, do_ref, lse_ref, cfc_ref, cfr_ref, dq_ref, dk_ref, dv_ref, dcf_ref = refs
        else:
            q_ref, k_ref, v_ref, o_ref, do_ref, lse_ref, dq_ref, dk_ref, dv_ref = refs
        dk_ref[...] = jnp.zeros_like(dk_ref)
        dv_ref[...] = jnp.zeros_like(dv_ref)
        if bias:
            dcf_ref[...] = jnp.zeros_like(dcf_ref)
        rowi = lax.broadcasted_iota(jnp.int32, (tq, tk), 0)
        coli = lax.broadcasted_iota(jnp.int32, (tq, tk), 1)

        def outer(i, _):
            roff = pl.multiple_of(i * tq, tq)
            qv = q_ref[pl.ds(roff, tq), :].astype(BF16)
            dov = do_ref[pl.ds(roff, tq), :]
            delta = jnp.sum(dov.astype(F32) * o_ref[pl.ds(roff, tq), :].astype(F32), axis=1, keepdims=True)
            lse = lse_ref[0, pl.ds(roff, tq), :]
            if bias:
                cq = cfc_ref[0, pl.ds(roff, tq), :]

            nkb = ((i + 1) * tq + tk - 1) // tk

            def products(j):
                off = pl.multiple_of(j * tk, tk)
                return (_dot_nt(qv, k_ref[pl.ds(off, tk), :].astype(BF16)) * scale,
                        _dot_nt(dov, v_ref[pl.ds(off, tk), :].astype(BF16)))

            def inner(j, carry):
                s, dp, dq_acc = carry
                s_next, dp_next = products(jnp.minimum(j + 1, nkb - 1))
                off = pl.multiple_of(j * tk, tk)
                ks = k_ref[pl.ds(off, tk), :].astype(BF16)
                if bias:
                    s = s + cq - cfr_ref[0, pl.ds(j, 1), :]
                p = jnp.where(j * tk + coli <= i * tq + rowi, jnp.exp(s - lse), 0.0)
                ds = p * (dp - delta)
                if bias:
                    dcf_ref[0, pl.ds(j, 1), :] -= jnp.sum(ds, axis=0, keepdims=True)
                dsb = (ds * scale).astype(BF16)
                dv_ref[pl.ds(off, tk), :] += _dot_tn(p.astype(BF16), dov)
                dk_ref[pl.ds(off, tk), :] += _dot_tn(dsb, qv)
                return s_next, dp_next, dq_acc + jnp.dot(dsb, ks, preferred_element_type=F32)

            dq_ref[pl.ds(roff, tq), :] = lax.fori_loop(0, nkb, inner, products(0) + (jnp.zeros((tq, dq), F32),))[2]
            return 0

        lax.fori_loop(0, nq, outer, 0)

    in_specs = [pl.BlockSpec((S, dq), lambda h: (0, qoff + h)),
                pl.BlockSpec((S, dq), lambda h: (0, koff + h)),
                pl.BlockSpec((S, HEAD_DIM), lambda h: (0, voff + vstep * h)),
                pl.BlockSpec((S, HEAD_DIM), lambda h: (0, h)),
                pl.BlockSpec((S, HEAD_DIM), lambda h: (0, h)),
                pl.BlockSpec((1, S, 1), lambda h: (h, 0, 0))]
    operands = [q, k, v, o, do, lse]
    out_specs = [pl.BlockSpec((S, dq), lambda h: (0, h)), pl.BlockSpec((S, dq), lambda h: (0, h)),
                 pl.BlockSpec((S, HEAD_DIM), lambda h: (0, h))]
    out_shape = [jax.ShapeDtypeStruct((S, nh * dq), F32), jax.ShapeDtypeStruct((S, nh * dq), F32),
                 jax.ShapeDtypeStruct((S, nh * HEAD_DIM), F32)]
    if bias:
        in_specs += [pl.BlockSpec((1, S, 1), lambda h: (h, 0, 0)), pl.BlockSpec((1, S // tk, tk), lambda h: (h, 0, 0))]
        operands += [cf_col, cf_row]
        out_specs.append(pl.BlockSpec((1, S // tk, tk), lambda h: (h, 0, 0)))
        out_shape.append(jax.ShapeDtypeStruct((nh, S // tk, tk), F32))
    return pl.pallas_call(
        body, name=name, grid=(nh,), in_specs=in_specs, out_specs=out_specs, out_shape=out_shape,
        compiler_params=_cp(("parallel",)),
    )(*operands)


SB_TQ, SB_TK = 512, 256


def _sb_block(z, strict, later_c, tri_after):
    lsn = jnp.minimum(-z, 0.0) - jnp.log(1.0 + jnp.exp(-jnp.abs(z)))
    lsp = lsn + z
    L = lsn if strict is None else jnp.where(strict, lsn, 0.0)
    later = _exact_dot(L, tri_after, 2) + later_c
    a = jnp.exp(lsp + later)
    return lsp, lsn, L, a if strict is None else jnp.where(strict, a, 0.0)


def _tri(n, after_strict):
    r = lax.broadcasted_iota(jnp.int32, (n, n), 0)
    c = lax.broadcasted_iota(jnp.int32, (n, n), 1)
    return jnp.where(r > c if after_strict else r >= c, 1.0, 0.0).astype(BF16)


def sb_fwd(name, qkv, nh):
    S = qkv.shape[0]
    tq = _pick_rows(S, SB_TQ)
    tk = _pick_rows(S, SB_TK)
    nq = S // tq
    scale = 1.0 / math.sqrt(HEAD_DIM)

    def body(q_ref, k_ref, v_ref, o_ref, of_ref):
        i = pl.program_id(1)
        qv = q_ref[...]
        tri = _tri(tk, True)
        row = i * tq + lax.broadcasted_iota(jnp.int32, (tq, tk), 0)
        coli = lax.broadcasted_iota(jnp.int32, (tq, tk), 1)

        def logits(j):
            return _dot_nt(qv, k_ref[pl.ds(pl.multiple_of(j * tk, tk), tk), :]) * scale

        nkb = ((i + 1) * tq + tk - 1) // tk

        def step(jj, carry, masked=True):
            z, later_c, acc, acc_lo = carry
            j = nkb - 1 - jj
            z_next = logits(jnp.maximum(j - 1, 0))
            vb = v_ref[pl.ds(pl.multiple_of(j * tk, tk), tk), :]
            _, _, L, a = _sb_block(z, (j * tk + coli < row) if masked else None, later_c, tri)
            ab = a.astype(BF16)
            acc = acc + jnp.dot(ab, vb, preferred_element_type=F32)
            acc_lo = acc_lo + jnp.dot((a - ab.astype(F32)).astype(BF16), vb, preferred_element_type=F32)
            return z_next, later_c + jnp.sum(L, axis=1, keepdims=True), acc, acc_lo

        zero = jnp.zeros((tq, HEAD_DIM), F32)
        _, _, acc, acc_lo = lax.fori_loop(0, nkb, step, (logits(nkb - 1), jnp.zeros((tq, 1), F32), zero, zero))
        o_ref[...] = acc.astype(o_ref.dtype)
        of_ref[...] = acc + acc_lo

    blk = pl.BlockSpec((tq, HEAD_DIM), lambda h, i: (i, h))
    return pl.pallas_call(
        body, name=name, grid=(nh, nq),
        in_specs=[blk,
                  pl.BlockSpec((S, HEAD_DIM), lambda h, i: (0, nh + h)),
                  pl.BlockSpec((S, HEAD_DIM), lambda h, i: (0, 2 * nh + h))],
        out_specs=[blk, blk],
        out_shape=[jax.ShapeDtypeStruct((S, nh * HEAD_DIM), BF16), jax.ShapeDtypeStruct((S, nh * HEAD_DIM), F32)],
        compiler_params=_cp(("parallel", "arbitrary")),
    )(qkv, qkv, qkv)


def sb_bwd(name, qkv, o, do, nh):
    S = qkv.shape[0]
    tq = _pick_rows(S, SB_TQ)
    tk = _pick_rows(S, SB_TK)
    nq = S // tq
    scale = 1.0 / math.sqrt(HEAD_DIM)

    def body(q_ref, k_ref, v_ref, o_ref, do_ref, dq_ref, dk_ref, dv_ref, dk_acc, dv_acc):
        dk_acc[...] = jnp.zeros_like(dk_acc)
        dv_acc[...] = jnp.zeros_like(dv_acc)
        tri = _tri(tk, True)
        tri_inc = _tri(tk, False)
        rowi = lax.broadcasted_iota(jnp.int32, (tq, tk), 0)
        coli = lax.broadcasted_iota(jnp.int32, (tq, tk), 1)

        def outer(i, _):
            roff = pl.multiple_of(i * tq, tq)
            qv = q_ref[pl.ds(roff, tq), :]
            dov = do_ref[pl.ds(roff, tq), :]
            dtot = jnp.sum(dov.astype(F32) * o_ref[pl.ds(roff, tq), :].astype(F32), axis=1, keepdims=True)
            row = i * tq + rowi

            def products(j):
                off = pl.multiple_of(j * tk, tk)
                return _dot_nt(qv, k_ref[pl.ds(off, tk), :]) * scale, _dot_nt(dov, v_ref[pl.ds(off, tk), :])

            nkb = ((i + 1) * tq + tk - 1) // tk

            def inner(jj, carry, masked=True):
                z, da, later_c, suf_c, dq_acc = carry
                j = nkb - 1 - jj
                z_next, da_next = products(jnp.maximum(j - 1, 0))
                off = pl.multiple_of(j * tk, tk)
                kb = k_ref[pl.ds(off, tk), :]
                strict = (j * tk + coli < row) if masked else None
                lsp, lsn, L, a = _sb_block(z, strict, later_c, tri)
                dl = da * a
                before = dtot - (suf_c + _exact_dot(dl, tri_inc, 3))
                dz = (dl * jnp.exp(lsn) - jnp.exp(lsp) * before) * scale
                if masked:
                    dz = jnp.where(strict, dz, 0.0)
                dzb = dz.astype(BF16)
                dv_acc[pl.ds(off, tk), :] += _dot_tn(a.astype(BF16), dov)
                dk_acc[pl.ds(off, tk), :] += _dot_tn(dzb, qv)
                return (z_next, da_next, later_c + jnp.sum(L, axis=1, keepdims=True), suf_c + jnp.sum(dl, axis=1, keepdims=True),
                        dq_acc + jnp.dot(dzb, kb, preferred_element_type=F32))

            z1 = jnp.zeros((tq, 1), F32)
            res = lax.fori_loop(0, nkb, inner, products(nkb - 1) + (z1, z1, jnp.zeros((tq, HEAD_DIM), F32)))
            dq_ref[pl.ds(roff, tq), :] = res[4].astype(dq_ref.dtype)
            return 0

        lax.fori_loop(0, nq, outer, 0)
        dk_ref[...] = dk_acc[...].astype(dk_ref.dtype)
        dv_ref[...] = dv_acc[...].astype(dv_ref.dtype)

    def col(off):
        return pl.BlockSpec((S, HEAD_DIM), lambda h: (0, off + h))

    dq, dk, dv = pl.pallas_call(
        body, name=name, grid=(nh,),
        in_specs=[col(0), col(nh), col(2 * nh), col(0), col(0)],
        out_specs=[col(0), col(0), col(0)],
        out_shape=[jax.ShapeDtypeStruct((S, nh * HEAD_DIM), BF16)] * 3,
        scratch_shapes=[pltpu.VMEM((S, HEAD_DIM), F32), pltpu.VMEM((S, HEAD_DIM), F32)],
        compiler_params=_cp(("parallel",)),
    )(qkv, qkv, qkv, o, do)
    return jnp.concatenate([dq, dk, dv], axis=1)


def seq_cumsum(name, x, reverse):
    S, W = x.shape

    tb = _pick_rows(S, 256)

    def body(x_ref, o_ref):
        parts = _split3(x_ref[...])
        c = lax.broadcasted_iota(jnp.int32, (tb, S), 1)
        for b in range(S // tb):
            r = b * tb + lax.broadcasted_iota(jnp.int32, (tb, S), 0)
            t = jnp.where(r <= c if reverse else r >= c, 1.0, 0.0).astype(BF16)
            out = None
            for p in parts:
                d = jnp.dot(t, p, preferred_element_type=F32)
                out = d if out is None else out + d
            o_ref[b * tb:(b + 1) * tb, :] = out

    return pl.pallas_call(body, name=name, out_shape=jax.ShapeDtypeStruct((S, W), F32),
                          compiler_params=pltpu.CompilerParams(vmem_limit_bytes=VMEM_LIMIT))(x)


def _adamw_math(w, g, m, v):
    c1 = 1.0 - ADAM_B1 ** ADAM_STEP
    c2 = 1.0 - ADAM_B2 ** ADAM_STEP
    nm = ADAM_B1 * m + (1.0 - ADAM_B1) * g
    nv = ADAM_B2 * v + (1.0 - ADAM_B2) * (g * g)
    return -ADAM_LR * ((nm / c1) / (jnp.sqrt(nv / c2) + ADAM_EPS) + ADAM_WD * w), nm, nv


def adamw(name, w, g, m, v):
    R, C = g.shape
    tr = _pick_rows(R, 256)

    def body(w_ref, g_ref, m_ref, v_ref, d_ref, nm_ref, nv_ref):
        d_ref[...], nm_ref[...], nv_ref[...] = _adamw_math(w_ref[...], g_ref[...], m_ref[...], v_ref[...])

    row = pl.BlockSpec((tr, C), lambda i: (i, 0))
    return pl.pallas_call(
        body, name=name, grid=(R // tr,), in_specs=[row] * 4, out_specs=[row] * 3,
        out_shape=[jax.ShapeDtypeStruct((R, C), F32)] * 3, compiler_params=_cp(("parallel",)),
    )(w, g, m, v)


def adamw_big(name, w, m, v, mine, other, c_arr, layer, prev, dep=None):
    L, R, C = w.shape
    r2 = R // 2
    tr = _pick_rows(r2, 256)
    nb = r2 // tr

    def body(c_ref, w_ref, m_ref, v_ref, mine_ref, other_ref, *rest):
        g_ref, d_ref, nm_ref, nv_ref = rest[-4:]
        g = jnp.where(pl.program_id(0) == c_ref[0], mine_ref[...], other_ref[...])
        g_ref[...] = g
        d_ref[...], nm_ref[...], nv_ref[...] = _adamw_math(w_ref[...], g, m_ref[...], v_ref[...])

    sel = pl.BlockSpec((None, tr, C), lambda hf, i, c: (layer, hf * nb + i, 0))
    in_specs = [sel, sel, sel,
                pl.BlockSpec((tr, C), lambda hf, i, c: (jnp.where(hf == c[0], i, 0), 0)),
                pl.BlockSpec((tr, C), lambda hf, i, c: (jnp.where(hf == c[0], 0, i), 0))]
    operands = [c_arr, w, m, v, mine, other]
    aliases = {}
    if prev is not None:
        in_specs += [ANY] * 4
        aliases = {len(operands) + k: k for k in range(4)}
        operands += list(prev)
    if dep is not None:
        in_specs += [ANY]
        operands += [dep]
    return pl.pallas_call(
        body, name=name,
        grid_spec=pltpu.PrefetchScalarGridSpec(num_scalar_prefetch=1, grid=(2, nb), in_specs=in_specs, out_specs=[sel] * 4),
        out_shape=[jax.ShapeDtypeStruct((L, R, C), F32)] * 4, input_output_aliases=aliases,
        compiler_params=_cp(("arbitrary", "arbitrary")),
    )(*operands)


ANY = pl.BlockSpec(memory_space=pl.ANY)


def _place():
    x, y, c = lax.axis_index("x"), lax.axis_index("y"), lax.axis_index("c")
    others = [(1 - x, y), (x, 1 - y), (1 - x, 1 - y)]
    return x, y, c, others


def add_half(name, g, other, c_arr):
    _, R, C = g.shape
    r2 = R // 2
    tr = _pick_rows(r2, 512)
    nb = r2 // tr

    def body(c_ref, g_ref, o_ref, out_ref):
        out_ref[...] = (g_ref[...].astype(F32) + o_ref[...].astype(F32)).astype(out_ref.dtype)

    return pl.pallas_call(
        body, name=name,
        grid_spec=pltpu.PrefetchScalarGridSpec(
            num_scalar_prefetch=1, grid=(N_CHIPS, nb),
            in_specs=[pl.BlockSpec((None, tr, C), lambda s, i, c: (s, c[0] * nb + i, 0)),
                      pl.BlockSpec((None, tr, C), lambda s, i, c: (s, i, 0))],
            out_specs=pl.BlockSpec((None, tr, C), lambda s, i, c: (s, i, 0))),
        out_shape=jax.ShapeDtypeStruct((N_CHIPS, r2, C), BF16),
        compiler_params=_cp(("parallel", "parallel")),
    )(c_arr, g, other)


def sum_slabs(name, got, parts, chip_arr):
    _, r2, C = parts.shape
    tr = _pick_rows(r2, 512)

    def body(chip_ref, *refs):
        own_ref, out_ref = refs[N_CHIPS], refs[N_CHIPS + 1]
        acc = None
        for s in range(N_CHIPS):
            v = jnp.where(chip_ref[0] == s, own_ref[...], refs[s][...]).astype(F32)
            acc = v if acc is None else acc + v
        out_ref[...] = acc

    def slab(s):
        return pl.BlockSpec((None, tr, C), lambda i, ch: (jnp.where(ch[0] == s, (s + 1) % N_CHIPS, s), i, 0))

    return pl.pallas_call(
        body, name=name,
        grid_spec=pltpu.PrefetchScalarGridSpec(
            num_scalar_prefetch=1, grid=(r2 // tr,),
            in_specs=[slab(s) for s in range(N_CHIPS)] + [pl.BlockSpec((None, tr, C), lambda i, ch: (ch[0], i, 0))],
            out_specs=pl.BlockSpec((tr, C), lambda i, ch: (i, 0))),
        out_shape=jax.ShapeDtypeStruct((r2, C), F32), compiler_params=_cp(("parallel",)),
    )(chip_arr, got, got, got, got, parts)


HBM = pl.BlockSpec(memory_space=pltpu.HBM)
SEM = pl.BlockSpec(memory_space=pltpu.SEMAPHORE)
EFFECT = pltpu.SideEffectType.DATAFLOW_SIDE_EFFECTING


def _in_hbm(a):
    return pltpu.with_memory_space_constraint(a, pltpu.HBM)


def _chip_copies(kind, shapes, src, land, send, recv, mine):
    x, y, c, others = _place()
    me = 2 * x + y
    cps = []

    def remote(s_ref, d_ref, k, to):
        cps.append(pltpu.make_async_remote_copy(src_ref=s_ref, dst_ref=d_ref, send_sem=send.at[k], recv_sem=recv.at[k],
                                                device_id=to, device_id_type=MESH))

    for t, shape in enumerate(shapes):
        if kind == "swap":
            r2 = shape[1] // 2
            remote(src[t].at[:, pl.ds((1 - c) * r2, r2), :], land[t], t, (x, y, 1 - c))
        elif kind == "whole":
            remote(src[t], land[t], t, (x, y, 1 - c))
        for j, (px, py) in enumerate(others if kind in ("gather", "scatter", "pass") else []):
            slot = me if mine else 2 * px + py
            if kind == "gather":
                r2 = shape[0] // 2
                rows = pl.ds(c * r2, r2)
                remote(src[t].at[rows, :], land[t].at[slot, rows, :], 3 * t + j, (px, py, c))
            elif kind == "scatter":
                remote(src[t].at[2 * px + py], land[t].at[slot], 3 * t + j, (px, py, c))
            else:
                r2 = shape[1] // 2
                rows = src[t].at[2 * px + py, pl.ds((c if mine else 1 - c) * r2, r2), :]
                remote(rows, rows, 3 * t + j, (x, y, 1 - c))
    return cps


def _n_copies(kind, n):
    return n if kind in ("swap", "whole") else 3 * n


def chips_start(name, kind, srcs, land_shapes, after=None):
    n, nl = len(srcs), len(land_shapes)
    shapes = [s.shape for s in srcs]
    n_buf = n + nl
    n_in = n_buf + (0 if after is None else 1)
    n_sem = _n_copies(kind, n)

    def body(*refs):
        src, land = refs[:n], refs[n:n_buf]
        send, recv = refs[n_in], refs[n_in + 1]
        token = refs[-1]
        for cp in _chip_copies(kind, shapes, src, land, send, recv, True):
            cp.start()
        token[...] = jnp.zeros_like(token)

    lands = [lax.empty(s, srcs[0].dtype) for s in land_shapes]
    out = pl.pallas_call(
        body, name=name,
        out_shape=(pltpu.SemaphoreType.DMA((n_sem,)), pltpu.SemaphoreType.DMA((n_sem,)))
        + tuple(pltpu.HBM(s.shape, s.dtype) for s in srcs) + tuple(pltpu.HBM(l.shape, l.dtype) for l in lands)
        + (jax.ShapeDtypeStruct((8, LANE), F32),),
        in_specs=(HBM,) * n_buf + (ANY,) * (n_in - n_buf),
        out_specs=(SEM, SEM) + (HBM,) * n_buf + (pl.BlockSpec(memory_space=pltpu.VMEM),),
        input_output_aliases={k: 2 + k for k in range(n_buf)},
        compiler_params=pltpu.CompilerParams(has_side_effects=EFFECT),
    )(*[_in_hbm(s) for s in srcs], *[_in_hbm(l) for l in lands], *([] if after is None else [after]))
    return (kind, shapes, out[0], out[1], out[2:2 + n], out[2 + n:2 + n_buf]), out[-1][:1, :1]


def chips_wait(name, handle, after):
    kind, shapes, send, recv, srcs, lands = handle
    n, n_buf = len(srcs), len(srcs) + len(lands)
    after = [a for a in (list(after) if isinstance(after, (list, tuple)) else [after]) if a is not None]

    def body(*refs):
        src, land = refs[:n], refs[n:n_buf]
        for cp in _chip_copies(kind, shapes, src, land, refs[n_buf], refs[n_buf + 1], True):
            cp.wait_send()
        for cp in _chip_copies(kind, shapes, src, land, refs[n_buf], refs[n_buf + 1], False):
            cp.wait_recv()

    out = pl.pallas_call(
        body, name=name,
        out_shape=tuple(pltpu.HBM(s.shape, s.dtype) for s in srcs) + tuple(pltpu.HBM(l.shape, l.dtype) for l in lands),
        in_specs=(HBM,) * n_buf + (SEM, SEM) + (ANY,) * len(after), out_specs=(HBM,) * n_buf,
        input_output_aliases={k: k for k in range(n_buf)},
        compiler_params=pltpu.CompilerParams(has_side_effects=EFFECT),
    )(*srcs, *lands, send, recv, *after)
    return list(out[n:]), list(out[:n])


def all_sum_small(name, v, dep=None):
    R = v.shape[0]
    n_dep = 0 if dep is None else 1

    def body(v_ref, *refs):
        out_ref, slots, send, recv = refs[n_dep:]
        x, y, c, _ = _place()
        me = 4 * x + 2 * y + c
        slots[me] = v_ref[...]
        cps = []
        for k in range(1, 8):
            dx, dy, dc = (k >> 2) & 1, (k >> 1) & 1, k & 1
            to = (x ^ dx, y ^ dy, c ^ dc)
            cps.append(pltpu.make_async_remote_copy(
                src_ref=v_ref, dst_ref=slots.at[me], send_sem=send.at[k - 1], recv_sem=recv.at[k - 1],
                device_id=to, device_id_type=MESH))
        for cp in cps:
            cp.start()
        for k in range(1, 8):
            dx, dy, dc = (k >> 2) & 1, (k >> 1) & 1, k & 1
            frm = 4 * (x ^ dx) + 2 * (y ^ dy) + (c ^ dc)
            pltpu.make_async_remote_copy(
                src_ref=v_ref, dst_ref=slots.at[frm], send_sem=send.at[k - 1], recv_sem=recv.at[k - 1],
                device_id=(x, y, c), device_id_type=MESH).wait_recv()
        for cp in cps:
            cp.wait_send()
        acc = slots[0]
        for d in range(1, 8):
            acc = acc + slots[d]
        out_ref[...] = acc

    vm = pl.BlockSpec(memory_space=pltpu.VMEM)
    return pl.pallas_call(
        body, name=name, in_specs=[vm] + [ANY] * n_dep, out_specs=vm, out_shape=jax.ShapeDtypeStruct((R, LANE), F32),
        scratch_shapes=[pltpu.VMEM((8, R, LANE), F32), pltpu.SemaphoreType.DMA((7,)), pltpu.SemaphoreType.DMA((7,))],
    )(v, *([] if dep is None else [dep]))


def _rope_mat(n, lo):
    half = MLA_ROPE // 2
    r = lax.broadcasted_iota(jnp.int32, (n, n), 0)
    c = lax.broadcasted_iota(jnp.int32, (n, n), 1)
    plus = (c >= lo + half) & (c < lo + 2 * half) & (r == c - half)
    minus = (c >= lo) & (c < lo + half) & (r == c + half)
    return (jnp.where(plus, 1.0, 0.0) - jnp.where(minus, 1.0, 0.0)).astype(BF16)


def _rope(v, cos, sin, lo):
    return v * cos + _exact_dot(v, _rope_mat(v.shape[-1], lo), 3) * sin


def rope_tables(pos):
    S = pos.shape[0]
    half = MLA_ROPE // 2
    inv = (np.float32(ROPE_THETA) ** (-np.arange(0, half, dtype=np.float32) * np.float32(2.0 / MLA_ROPE))).astype(np.float32)
    f1 = np.zeros((1, LANE), np.float32)
    f1[0, :MLA_ROPE] = np.tile(inv, 2)
    f2 = np.zeros((1, MLA_QK_PAD), np.float32)
    f2[0, MLA_NOPE:MLA_QK] = np.tile(inv, 2)
    tr = _pick_rows(S, 256)

    def body(p_ref, f1_ref, f2_ref, c1, s1, c2, s2):
        p = p_ref[...].astype(F32)
        a1 = p * f1_ref[...]
        a2 = p * f2_ref[...]
        c1[...] = jnp.cos(a1)
        s1[...] = jnp.sin(a1)
        c2[...] = jnp.cos(a2)
        s2[...] = jnp.sin(a2)

    def row(w):
        return pl.BlockSpec((tr, w), lambda i: (i, 0))

    def full(w):
        return pl.BlockSpec((1, w), lambda i: (0, 0))

    return pl.pallas_call(
        body, name="rope_tables", grid=(S // tr,), in_specs=[row(1), full(LANE), full(MLA_QK_PAD)],
        out_specs=[row(LANE), row(LANE), row(MLA_QK_PAD), row(MLA_QK_PAD)],
        out_shape=[jax.ShapeDtypeStruct((S, LANE), F32)] * 2 + [jax.ShapeDtypeStruct((S, MLA_QK_PAD), F32)] * 2,
        compiler_params=_cp(("parallel",)),
    )(pos, jnp.asarray(f1), jnp.asarray(f2))


def _log_sigmoid(z):
    return jnp.minimum(z, 0.0) - jnp.log(1.0 + jnp.exp(-jnp.abs(z)))


def _fox_qk_fn(q, k, gq, gk):
    return _rms(q, gq, HEAD_DIM), _rms(k, gk, HEAD_DIM)


def _fox_gate_fn(f, b):
    return (_log_sigmoid(f + b),)


def _mla_pre1_fn(q_rank, kv_rank):
    def fn(cq, ckv, kr, cos, sin, qn, kvn):
        return _rms(cq, qn, q_rank), _rms(ckv, kvn, kv_rank), _rope(kr, cos, sin, 0)
    return fn


def _mla_pre2_fn(qb, kn, kr, cos, sin, gq, gk):
    qh = _rms(_rope(qb, cos, sin, MLA_NOPE), gq, MLA_QK)
    kh = _rms(jnp.concatenate([kn, kr], axis=1), gk, MLA_QK)
    return qh, kh


def _pad_cols(a, n):
    return jnp.pad(a, ((0, 0), (0, n - a.shape[1])))


def _relu2(acc):
    r = jnp.maximum(acc, 0.0)
    return r * r, r


def _add(acc, res):
    return (acc + res,)


def _times_2r(acc, r):
    return (acc * (2.0 * r.astype(F32)),)


def kernel(x, positions, mix_norm, mlp_norm, sb_w_in, sb_w_out, fox_w_in, fox_b_f, fox_q_gain, fox_k_gain, fox_w_out, mla_w_in, mla_q_norm, mla_kv_norm, mla_w_uq, mla_w_ukv, mla_q_gain, mla_k_gain, mla_w_out, mlp_w1, mlp_w2, loss_target, m_mix_norm, m_mlp_norm, m_sb_w_in, m_sb_w_out, m_fox_w_in, m_fox_b_f, m_fox_q_gain, m_fox_k_gain, m_fox_w_out, m_mla_w_in, m_mla_q_norm, m_mla_kv_norm, m_mla_w_uq, m_mla_w_ukv, m_mla_q_gain, m_mla_k_gain, m_mla_w_out, m_mlp_w1, m_mlp_w2, v_mix_norm, v_mlp_norm, v_sb_w_in, v_sb_w_out, v_fox_w_in, v_fox_b_f, v_fox_q_gain, v_fox_k_gain, v_fox_w_out, v_mla_w_in, v_mla_q_norm, v_mla_kv_norm, v_mla_w_uq, v_mla_w_ukv, v_mla_q_gain, v_mla_k_gain, v_mla_w_out, v_mlp_w1, v_mlp_w2):
    S, D = x.shape[1], x.shape[2]
    nh = D // HEAD_DIM
    W = nh * HEAD_DIM
    depth = mix_norm.shape[0]
    q_rank, kv_rank = mla_w_uq.shape[1], mla_w_ukv.shape[1]
    n_fox_in = 3 * W + nh
    fox_pad = -(-n_fox_in // LANE) * LANE
    n_down = q_rank + kv_rank + MLA_ROPE
    down_pad = q_rank + kv_rank + LANE
    tr = _pick_rows(S, 256)
    tk = _pick_rows(S, 256)

    ax, ay, ac = lax.axis_index("x"), lax.axis_index("y"), lax.axis_index("c")
    chip = 2 * ax + ay
    c_arr = jnp.reshape(ac, (1,)).astype(jnp.int32)
    chip_arr = jnp.reshape(chip, (1,)).astype(jnp.int32)

    def bf(a):
        return a.astype(BF16)

    n_small_in = q_rank + kv_rank
    rows_in = -(-n_small_in // (8 * LANE)) * 8
    placed = jnp.zeros((rows_in * LANE,), F32)
    placed = lax.dynamic_update_slice(placed, mla_q_norm[0], (chip * mla_q_norm.shape[1],))
    placed = lax.dynamic_update_slice(placed, mla_kv_norm[0], (q_rank + chip * mla_kv_norm.shape[1],))
    placed = placed * (ac == 0).astype(F32)
    norms = all_sum_small("gather_norms", placed.reshape(rows_in, LANE)).reshape(-1)
    q_norm_full = norms[:q_rank].reshape(1, q_rank)
    kv_norm_full = norms[q_rank:q_rank + kv_rank].reshape(1, kv_rank)

    groups = [(i, part) for i in range(depth) for part in ("mix", "mlp")]
    G = {}

    def group_shards(i, part):
        kind, j = i % N_MIXERS, i // N_MIXERS
        if part == "mlp":
            return ["w1", "w2"], [bf(mlp_w1[i]), bf(mlp_w2[i])]
        if kind == 0:
            return ["w_in", "w_out"], [bf(sb_w_in[j]), bf(sb_w_out[j])]
        if kind == 1:
            return ["w_in", "w_out"], [bf(fox_w_in[j]), bf(fox_w_out[j])]
        return ["w_in", "w_uq", "w_ukv", "w_out"], [bf(mla_w_in[j]), bf(mla_w_uq[j]), bf(mla_w_ukv[j]), bf(mla_w_out[j])]

    def cross_chips_start(key, after):
        names, shards = group_shards(*key)
        handle, token = chips_start(f"gather_start_{key[1]}_{key[0]}", "gather", shards,
                                    [(N_CHIPS,) + s.shape for s in shards], after)
        G[key] = dict(names=names, ici=handle, token=token)
        return token

    def cross_cores_start(key, after):
        g = G[key]
        g["lands"], g["shards"] = chips_wait(f"gather_wait_{key[1]}_{key[0]}", g["ici"], after)
        g["d2d"], token = chips_start(f"pass_start_{key[1]}_{key[0]}", "pass", g["lands"], [])
        return token

    def group_ready(key, after):
        g = G[key]
        _, bufs = chips_wait(f"pass_wait_{key[1]}_{key[0]}", g["d2d"], after)
        return dict(zip(g["names"], [lax.dynamic_update_slice(b, s[None], (chip, 0, 0)) for b, s in zip(bufs, g["shards"])]))

    def before_part(s, after):
        key = groups[s]
        if key not in G:
            cross_chips_start(key, norms)
        if "d2d" not in G[key]:
            cross_cores_start(key, after)
        L = group_ready(key, after)
        token = jnp.zeros((1, 1), F32)
        if s + 1 < len(groups) and groups[s + 1] in G:
            token = token + cross_cores_start(groups[s + 1], after)
        last = L[G[key]["names"][0]]
        n_ahead = GATHER_AHEAD - (s == 0)
        for ahead in groups[s + 1:s + 1 + n_ahead]:
            if ahead not in G:
                token = token + cross_chips_start(ahead, last)
            last = G[ahead]["token"]
        return L, token

    def rows_stacked(w):
        return w.reshape(w.shape[0] * w.shape[1], w.shape[2])

    xc = x[0]
    saved = []
    layers = []
    tables = None
    for i in range(depth):
        kind, j = i % N_MIXERS, i // N_MIXERS
        L, token = before_part(2 * i, xc if i else None)
        g1 = mix_norm[i:i + 1] + token
        hb = rmsnorm_fwd(f"norm1_{i}", xc, g1)
        st = dict(x=xc, hb=hb)
        if kind == 0:
            qkv = mm_nn(f"sb_proj_{i}", hb, L["w_in"], BF16)[0]
            o, o_f32 = sb_fwd(f"sb_attn_{i}", qkv, nh)
            st.update(qkv=qkv, o_f32=o_f32)
        elif kind == 1:
            w_in = _pad_cols(jnp.concatenate([L["w_in"][s] for s in range(N_CHIPS)], axis=1), fox_pad)
            proj = mm_nn(f"fox_proj_{i}", hb, w_in, F32)[0]
            gq, gk = fox_q_gain[j:j + 1], fox_k_gain[j:j + 1]
            qk_rows = [(proj, HEAD_DIM, lambda h: h), (proj, HEAD_DIM, lambda h: nh + h)]
            qh, kh = rowwise_fwd(f"fox_qk_{i}", _fox_qk_fn, qk_rows, [gq, gk],
                                 [(W, HEAD_DIM, lambda h: h, BF16)] * 2, tr, nh)
            b_pad = _pad_cols(fox_b_f[j:j + 1], LANE)
            gate_rows = [(proj, LANE, lambda h: 3 * nh)]
            logf = rowwise_fwd(f"fox_gate_{i}", _fox_gate_fn, gate_rows, [b_pad], [(LANE, LANE, _c0, F32)], tr)[0]
            cf = seq_cumsum(f"fox_cf_{i}", logf, False)[:, :nh].T
            cf_col, cf_row = cf.reshape(nh, S, 1), cf.reshape(nh, S // tk, tk)
            scale = 1.0 / math.sqrt(HEAD_DIM)
            o, o_f32, lse = attn_fwd(f"fox_attn_{i}", qh, kh, proj, nh, HEAD_DIM, 0, 0, 2 * nh, 1, scale, cf_col, cf_row)
            st.update(w_in=w_in, proj=proj, qk_rows=qk_rows, gq=gq, gk=gk, qh=qh, kh=kh, b_pad=b_pad, gate_rows=gate_rows,
                      cf_col=cf_col, cf_row=cf_row, lse=lse, scale=scale, o_f32=o_f32)
        else:
            w_in = _pad_cols(rows_stacked(L["w_in"]), down_pad)
            down = mm_nn(f"mla_down_{i}", hb, w_in, F32)[0]
            if tables is None:
                tables = rope_tables(positions.reshape(S, 1))
            cos1, sin1, cos2, sin2 = tables
            pre1_rows = [(down[:, :q_rank], q_rank, _c0), (down[:, q_rank:q_rank + kv_rank], kv_rank, _c0),
                         (down[:, q_rank + kv_rank:], LANE, _c0), (cos1, LANE, _c0), (sin1, LANE, _c0)]
            pre1_fn = _mla_pre1_fn(q_rank, kv_rank)
            c_q, c_kv, k_rope = rowwise_fwd(
                f"mla_pre1_{i}", pre1_fn, pre1_rows, [q_norm_full, kv_norm_full],
                [(q_rank, q_rank, _c0, BF16), (kv_rank, kv_rank, _c0, BF16), (LANE, LANE, _c0, F32)], tr)
            qfull = mm_nn(f"mla_uq_{i}", c_q, L["w_uq"], F32)[0]
            kv = mm_nn(f"mla_ukv_{i}", c_kv, L["w_ukv"], F32)[0]
            qpad = jnp.pad(qfull.reshape(S, nh, MLA_QK), ((0, 0), (0, 0), (0, MLA_QK_PAD - MLA_QK))).reshape(S, nh * MLA_QK_PAD)
            gq, gk = _pad_cols(mla_q_gain[j:j + 1], MLA_QK_PAD), _pad_cols(mla_k_gain[j:j + 1], MLA_QK_PAD)
            pre2_rows = [(qpad, MLA_QK_PAD, lambda h: h), (kv, MLA_NOPE, lambda h: 2 * h), (k_rope, LANE, _c0),
                         (cos2, MLA_QK_PAD, _c0), (sin2, MLA_QK_PAD, _c0)]
            qh, kh = rowwise_fwd(f"mla_pre2_{i}", _mla_pre2_fn, pre2_rows, [gq, gk],
                                 [(nh * MLA_QK_PAD, MLA_QK_PAD, lambda h: h, BF16)] * 2, tr, nh)
            scale = 1.0 / math.sqrt(MLA_QK)
            o, o_f32, lse = attn_fwd(f"mla_attn_{i}", qh, kh, kv, nh, MLA_QK_PAD, 0, 0, 1, 2, scale)
            st.update(w_in=w_in, pre1_rows=pre1_rows, pre1_fn=pre1_fn, c_q=c_q, c_kv=c_kv, pre2_rows=pre2_rows, gq=gq, gk=gk,
                      qh=qh, kh=kh, kv=kv, lse=lse, scale=scale, o_f32=o_f32)
        x1 = mm_nn(f"mix_out_{i}", o, rows_stacked(L["w_out"]), F32, epilogue=_add, extras=(xc,))[0]
        L_mlp, token = before_part(2 * i + 1, x1)
        L.update(L_mlp)
        layers.append(L)
        g2 = mlp_norm[i:i + 1] + token
        h2 = rmsnorm_fwd(f"norm2_{i}", x1, g2)
        a, r = mm_nn(f"mlp_up_{i}", h2, L["w1"], None, epilogue=_relu2, out_dtypes=[BF16, BF16])
        xc = mm_nn(f"mlp_down_{i}", a, rows_stacked(L["w2"]), F32, epilogue=_add, extras=(x1,))[0]
        st.update(o=o, x1=x1, h2=h2, a=a, r=r, g1=g1, g2=g2)
        saved.append(st)

    loss_local, dx, dxb = loss_head(xc, loss_target[0])
    loss = lax.psum(loss_local[0, 0], ("x", "y", "c"))

    small = {}

    def stack_rows(g):
        return g.reshape(N_CHIPS, g.shape[0] // N_CHIPS, g.shape[1])

    full = [dict() for _ in range(depth)]
    sums = []

    def sums_advance(after):
        token = jnp.zeros((1, 1), F32)
        for e in sums:
            tag = e["tag"]
            if "back" in e:
                others, halves = chips_wait(f"sendh_wait_{tag}", e.pop("back"), after)
                full[e["layer"]].update(zip(e["names"], zip(halves, others)))
                e["done"] = True
            elif "chips" in e and e["hold"]:
                e["hold"] -= 1
            elif "chips" in e:
                got, parts = chips_wait(f"xchg_wait_{tag}", e.pop("chips"), after)
                halves = [sum_slabs(f"sum4_{tag}_{t}", g, p, chip_arr) for t, (g, p) in enumerate(zip(got, parts))]
                e["back"], tok = chips_start(f"sendh_start_{tag}", "whole", halves, [h.shape for h in halves])
                token = token + tok
            elif "pair" in e:
                theirs, grads = chips_wait(f"swap_wait_{tag}", e.pop("pair"), after)
                parts = [add_half(f"addh_{tag}_{t}", g, o, c_arr) for t, (g, o) in enumerate(zip(grads, theirs))]
                e["chips"], tok = chips_start(f"xchg_start_{tag}", "scatter", parts, [p.shape for p in parts])
                token = token + tok
            elif not e.get("done"):
                grads = e.pop("grads")
                e["pair"], tok = chips_start(f"swap_start_{tag}", "swap", grads,
                                             [(N_CHIPS, g.shape[1] // 2, g.shape[2]) for g in grads])
                token = token + tok
        sums[:] = [e for e in sums if not e.get("done")]
        return token

    def sums_add(i, part, gr, after):
        sums.append(dict(layer=i, tag=f"{part}_{i}", names=list(gr), grads=list(gr.values()), hold=int(part == "mlp")))
        return sums_advance(after)

    dep = None
    for i in reversed(range(depth)):
        kind, j = i % N_MIXERS, i // N_MIXERS
        L, st = layers[i], saved[i]
        gr = {}
        gr["w2"] = stack_rows(mm_tn(f"mlp_dw2_{i}", st["a"], dxb, BF16, False, dep=dep))
        du = mm_nt(f"mlp_du_{i}", dxb, rows_stacked(L["w2"]), BF16, epilogue=_times_2r, extras=(st["r"],), dep=dep)
        gr["w1"] = mm_tn(f"mlp_dw1_{i}", st["h2"], du, BF16, True)
        dh2 = mm_nt(f"mlp_dh_{i}", du, L["w1"], F32)
        dx1, dx1b, dg2 = rmsnorm_bwd(f"norm2_bwd_{i}", st["x1"], st["g2"], dh2, dx)
        small[("mlp_norm", i)] = dg2
        dep = sums_add(i, "mlp", gr, dx1)
        gr = {}
        gr["w_out"] = stack_rows(mm_tn(f"mix_dwout_{i}", st["o"], dx1b, BF16, False, dep=dep))
        do = mm_nt(f"mix_do_{i}", dx1b, rows_stacked(L["w_out"]), BF16, dep=dep)
        if kind == 0:
            dqkv = sb_bwd(f"sb_attn_bwd_{i}", st["qkv"], st["o_f32"], do, nh)
            gr["w_in"] = mm_tn(f"sb_dwin_{i}", st["hb"], dqkv, BF16, True)
            dh = mm_nt(f"sb_dh_{i}", dqkv, L["w_in"], F32)
        elif kind == 1:
            dqh, dkh, dv, dcf = attn_bwd(f"fox_attn_bwd_{i}", st["qh"], st["kh"], st["proj"], st["o_f32"], do, st["lse"], nh,
                                         HEAD_DIM, 0, 0, 2 * nh, 1, st["scale"], st["cf_col"], st["cf_row"])
            dcf_s = _pad_cols(dcf.reshape(nh, S).T, LANE)
            dlogf = seq_cumsum(f"fox_dcf_{i}", dcf_s, True)
            dgate, db = rowwise_bwd(f"fox_gate_bwd_{i}", _fox_gate_fn, st["gate_rows"], [st["b_pad"]], [(dlogf, LANE, _c0)],
                                    [(LANE, LANE, _c0, BF16, False)], tr)
            dq, dk, dgq, dgk = rowwise_bwd(
                f"fox_qk_bwd_{i}", _fox_qk_fn, st["qk_rows"], [st["gq"], st["gk"]],
                [(dqh, HEAD_DIM, lambda h: h), (dkh, HEAD_DIM, lambda h: h)], [(W, HEAD_DIM, lambda h: h, BF16, False)] * 2, tr, nh)
            small[("fox_b_f", j)] = db[:, :nh]
            small[("fox_q_gain", j)] = dgq
            small[("fox_k_gain", j)] = dgk
            dproj = jnp.concatenate([dq, dk, bf(dv), dgate], axis=1)
            dw = mm_tn(f"fox_dwin_{i}", st["hb"], dproj, BF16, False)
            n4 = n_fox_in // N_CHIPS
            gr["w_in"] = jnp.stack([dw[:, s * n4:(s + 1) * n4] for s in range(N_CHIPS)])
            dh = mm_nt(f"fox_dh_{i}", dproj, st["w_in"], F32)
        else:
            dqh, dkh, dv = attn_bwd(f"mla_attn_bwd_{i}", st["qh"], st["kh"], st["kv"], st["o_f32"], do, st["lse"], nh,
                                    MLA_QK_PAD, 0, 0, 1, 2, st["scale"])
            dqpad, dkn, dkr, dgq, dgk = rowwise_bwd(
                f"mla_pre2_bwd_{i}", _mla_pre2_fn, st["pre2_rows"], [st["gq"], st["gk"]],
                [(dqh, MLA_QK_PAD, lambda h: h), (dkh, MLA_QK_PAD, lambda h: h)],
                [(nh * MLA_QK_PAD, MLA_QK_PAD, lambda h: h, BF16, False), (W, MLA_NOPE, lambda h: h, BF16, False),
                 (LANE, LANE, _c0, F32, True)], tr, nh, n_diff=3)
            small[("mla_q_gain", j)] = dgq[:, :MLA_QK]
            small[("mla_k_gain", j)] = dgk[:, :MLA_QK]
            dqfull = dqpad.reshape(S, nh, MLA_QK_PAD)[:, :, :MLA_QK].reshape(S, nh * MLA_QK)
            dkv = jnp.stack([dkn.reshape(S, nh, MLA_NOPE), bf(dv).reshape(S, nh, MLA_V)], axis=2).reshape(S, nh * (MLA_NOPE + MLA_V))
            gr["w_uq"] = mm_tn(f"mla_dwuq_{i}", st["c_q"], dqfull, BF16, True)
            dc_q = mm_nt(f"mla_dcq_{i}", dqfull, L["w_uq"], F32)
            gr["w_ukv"] = mm_tn(f"mla_dwukv_{i}", st["c_kv"], dkv, BF16, True)
            dc_kv = mm_nt(f"mla_dckv_{i}", dkv, L["w_ukv"], F32)
            d1, d2, d3, dqn, dkvn = rowwise_bwd(
                f"mla_pre1_bwd_{i}", st["pre1_fn"], st["pre1_rows"], [q_norm_full, kv_norm_full],
                [(dc_q, q_rank, _c0), (dc_kv, kv_rank, _c0), (dkr, LANE, _c0)],
                [(q_rank, q_rank, _c0, BF16, False), (kv_rank, kv_rank, _c0, BF16, False), (LANE, LANE, _c0, BF16, False)],
                tr, n_diff=3)
            small[("mla_q_norm", j)] = dqn
            small[("mla_kv_norm", j)] = dkvn
            ddown = jnp.concatenate([d1, d2, d3], axis=1)
            dw = mm_tn(f"mla_dwin_{i}", st["hb"], ddown, BF16, False)
            gr["w_in"] = stack_rows(dw[:, :n_down])
            dh = mm_nt(f"mla_dh_{i}", ddown, st["w_in"], F32)
        dx, dxb, dg1 = rmsnorm_bwd(f"norm1_bwd_{i}", st["x"], st["g1"], dh, dx1)
        small[("mix_norm", i)] = dg1
        dep = sums_add(i, "mix", gr, dx)

    def holders(kind_of):
        return [i for i in range(depth) if kind_of is None or i % N_MIXERS == kind_of]

    tensors = {
        "sb_w_in": (sb_w_in, m_sb_w_in, v_sb_w_in, "w_in", holders(0)),
        "sb_w_out": (sb_w_out, m_sb_w_out, v_sb_w_out, "w_out", holders(0)),
        "fox_w_in": (fox_w_in, m_fox_w_in, v_fox_w_in, "w_in", holders(1)),
        "fox_w_out": (fox_w_out, m_fox_w_out, v_fox_w_out, "w_out", holders(1)),
        "mla_w_in": (mla_w_in, m_mla_w_in, v_mla_w_in, "w_in", holders(2)),
        "mla_w_uq": (mla_w_uq, m_mla_w_uq, v_mla_w_uq, "w_uq", holders(2)),
        "mla_w_ukv": (mla_w_ukv, m_mla_w_ukv, v_mla_w_ukv, "w_ukv", holders(2)),
        "mla_w_out": (mla_w_out, m_mla_w_out, v_mla_w_out, "w_out", holders(2)),
        "mlp_w1": (mlp_w1, m_mlp_w1, v_mlp_w1, "w1", holders(None)),
        "mlp_w2": (mlp_w2, m_mlp_w2, v_mlp_w2, "w2", holders(None)),
    }
    updated = {n: None for n in tensors}

    def update_ready(dep):
        outs = []
        for n, (w, m, v, key, held) in tensors.items():
            for l in reversed(range(len(held))):
                if (n, l) not in applied and key in full[held[l]]:
                    mine, other = full[held[l]][key]
                    updated[n] = adamw_big(f"adamw_{n}_{l}", w, m, v, mine, other, c_arr, l, updated[n], dep)
                    applied.add((n, l))
                    outs.append(updated[n][3][l, :8, :LANE])
        return outs

    applied = set()
    after = dx
    while sums:
        outs = update_ready(dep)
        after = outs if outs else after
        dep = sums_advance(after)

    last = update_ready(dep)

    keys = list(small)
    flat = jnp.concatenate([small[k].reshape(-1) for k in keys])
    rows_g = -(-flat.shape[0] // (8 * LANE)) * 8
    flat = jnp.pad(flat, (0, rows_g * LANE - flat.shape[0]))
    tail = last or (after if isinstance(after, list) else [after])
    summed = all_sum_small("sum_small", flat.reshape(rows_g, LANE), dep=tail[0]).reshape(-1)
    sg, off = {}, 0
    for k in keys:
        n = small[k].size
        sg[k] = summed[off:off + n].reshape(small[k].shape)
        off += n

    def update_big(name):
        return list(updated[name])

    def update_small(name, w, m, v, g):
        return [g] + list(adamw(f"adamw_{name}", w, g, m, v))

    def small_rows(name, count):
        return jnp.concatenate([sg[(name, l)] for l in range(count)], axis=0)

    def my_part(g, n):
        return lax.dynamic_slice(g, (0, chip * n), (g.shape[0], n))

    res = {
        "mix_norm": update_small("mix_norm", mix_norm, m_mix_norm, v_mix_norm, small_rows("mix_norm", depth)),
        "mlp_norm": update_small("mlp_norm", mlp_norm, m_mlp_norm, v_mlp_norm, small_rows("mlp_norm", depth)),
        "sb_w_in": update_big("sb_w_in"),
        "sb_w_out": update_big("sb_w_out"),
        "fox_w_in": update_big("fox_w_in"),
        "fox_b_f": update_small("fox_b_f", fox_b_f, m_fox_b_f, v_fox_b_f, small_rows("fox_b_f", fox_b_f.shape[0])),
        "fox_q_gain": update_small("fox_q_gain", fox_q_gain, m_fox_q_gain, v_fox_q_gain, small_rows("fox_q_gain", fox_q_gain.shape[0])),
        "fox_k_gain": update_small("fox_k_gain", fox_k_gain, m_fox_k_gain, v_fox_k_gain, small_rows("fox_k_gain", fox_k_gain.shape[0])),
        "fox_w_out": update_big("fox_w_out"),
        "mla_w_in": update_big("mla_w_in"),
        "mla_q_norm": update_small("mla_q_norm", mla_q_norm, m_mla_q_norm, v_mla_q_norm,
                                   my_part(small_rows("mla_q_norm", mla_q_norm.shape[0]), mla_q_norm.shape[1])),
        "mla_kv_norm": update_small("mla_kv_norm", mla_kv_norm, m_mla_kv_norm, v_mla_kv_norm,
                                    my_part(small_rows("mla_kv_norm", mla_kv_norm.shape[0]), mla_kv_norm.shape[1])),
        "mla_w_uq": update_big("mla_w_uq"),
        "mla_w_ukv": update_big("mla_w_ukv"),
        "mla_q_gain": update_small("mla_q_gain", mla_q_gain, m_mla_q_gain, v_mla_q_gain, small_rows("mla_q_gain", mla_q_gain.shape[0])),
        "mla_k_gain": update_small("mla_k_gain", mla_k_gain, m_mla_k_gain, v_mla_k_gain, small_rows("mla_k_gain", mla_k_gain.shape[0])),
        "mla_w_out": update_big("mla_w_out"),
        "mlp_w1": update_big("mlp_w1"),
        "mlp_w2": update_big("mlp_w2"),
    }
    order = ["mix_norm", "mlp_norm", "sb_w_in", "sb_w_out", "fox_w_in", "fox_b_f", "fox_q_gain", "fox_k_gain", "fox_w_out",
             "mla_w_in", "mla_q_norm", "mla_kv_norm", "mla_w_uq", "mla_w_ukv", "mla_q_gain", "mla_k_gain", "mla_w_out",
             "mlp_w1", "mlp_w2"]
    outs = [loss, dx.reshape(x.shape)]
    for k in range(4):
        outs += [res[n][k] for n in order]
    return tuple(outs)
```

```python
import functools
import math

import numpy as np
import jax
import jax.numpy as jnp
from jax import lax
from jax.experimental import pallas as pl
from jax.experimental.pallas import tpu as pltpu

F32 = jnp.float32
BF16 = jnp.bfloat16
MESH = pl.DeviceIdType.MESH

EPS = 1e-6
HEAD_DIM = 128
MLA_NOPE = 128
MLA_ROPE = 64
MLA_V = 128
MLA_QK = MLA_NOPE + MLA_ROPE
MLA_QK_PAD = 256
ROPE_THETA = 10000.0
N_MIXERS = 3
ADAM_LR = 0.001
ADAM_B1 = 0.9
ADAM_B2 = 0.999
ADAM_EPS = 1e-08
ADAM_WD = 0.01
ADAM_STEP = 10

LANE = 128
N_CHIPS = 4
GATHER_AHEAD = 3
VMEM_LIMIT = 48 * 1024 * 1024
NEG = -1e30


def _cp(sem):
    return pltpu.CompilerParams(dimension_semantics=sem, vmem_limit_bytes=VMEM_LIMIT)


def _pick(dim, cap):
    best = None
    b = LANE
    while b <= min(dim, cap):
        if dim % b == 0:
            best = b
        b += LANE
    return best if best is not None else dim


def _pick_rows(dim, cap):
    b = min(dim, cap)
    while dim % b:
        b -= 8
    return b


def _matmul(name, a, b, a_blk, a_map, b_blk, b_map, dn, grid, acc_shape, outs, extras=(), epilogue=None, dep=None):
    nk = grid[2]
    n_ex, n_out = len(extras), len(outs)
    n_dep = 0 if dep is None else 1

    def body(*refs):
        a_ref, b_ref = refs[0], refs[1]
        ex_refs = refs[2:2 + n_ex]
        out_refs = refs[2 + n_ex + n_dep:2 + n_ex + n_dep + n_out]
        acc = refs[-1]
        k = pl.program_id(2)

        @pl.when(k == 0)
        def _():
            acc[...] = jnp.zeros_like(acc)

        acc[...] += lax.dot_general(a_ref[...], b_ref[...], dn, preferred_element_type=F32)

        @pl.when(k == nk - 1)
        def _():
            res = acc[...]
            vals = epilogue(res, *[e[...] for e in ex_refs]) if epilogue is not None else (res,)
            for o, v in zip(out_refs, vals):
                o[...] = v.astype(o.dtype)

    in_specs = [pl.BlockSpec(a_blk, a_map), pl.BlockSpec(b_blk, b_map)]
    in_specs += [pl.BlockSpec(blk, lambda i, j, k, m=m: m(i, j)) for (_, blk, m) in extras]
    in_specs += [pl.BlockSpec(memory_space=pl.ANY)] * n_dep
    out_specs = [pl.BlockSpec(blk, lambda i, j, k, m=m: m(i, j)) for (_, _, blk, m) in outs]
    out_shape = [jax.ShapeDtypeStruct(s, d) for (s, d, _, _) in outs]
    res = pl.pallas_call(
        body, name=name, grid=grid, in_specs=in_specs, out_specs=out_specs, out_shape=out_shape,
        scratch_shapes=[pltpu.VMEM(acc_shape, F32)],
        compiler_params=_cp(("parallel", "parallel", "arbitrary")),
    )(a, b, *[e[0] for e in extras], *([] if dep is None else [dep]))
    return res


BM, BN, BK = 1024, 1024, 2048


def mm_nn(name, a, w, out_dtype, epilogue=None, extras=(), n_out=1, out_dtypes=None):
    M, K = a.shape
    stacked = w.ndim == 3
    n4 = w.shape[-1]
    N = n4 * (N_CHIPS if stacked else 1)
    bm, bn, bk = _pick_rows(M, BM), _pick(n4, BN), _pick(K, BK)
    nb = n4 // bn
    if stacked:
        b_blk, b_map = (None, bk, bn), (lambda i, j, k: (j // nb, k, j % nb))
    else:
        b_blk, b_map = (bk, bn), (lambda i, j, k: (k, j))
    dts = out_dtypes if out_dtypes is not None else [out_dtype] * n_out
    outs = [((M, N), d, (bm, bn), lambda i, j: (i, j)) for d in dts]
    exs = [(e, (bm, bn), lambda i, j: (i, j)) for e in extras]
    return _matmul(name, a, w, (bm, bk), lambda i, j, k: (i, k), b_blk, b_map,
                   (((1,), (0,)), ((), ())), (M // bm, N // bn, K // bk), (bm, bn), outs, exs, epilogue)


def mm_nt(name, dy, w, out_dtype, epilogue=None, extras=(), dep=None):
    M, N = dy.shape
    stacked = w.ndim == 3
    K, n4 = w.shape[-2], w.shape[-1]
    bm, bn, bk = _pick_rows(M, BM), _pick(K, BN), _pick(n4, BK)
    nb = n4 // bk
    if stacked:
        b_blk, b_map = (None, bn, bk), (lambda i, j, k: (k // nb, j, k % nb))
    else:
        b_blk, b_map = (bn, bk), (lambda i, j, k: (j, k))
    outs = [((M, K), out_dtype, (bm, bn), lambda i, j: (i, j))]
    exs = [(e, (bm, bn), lambda i, j: (i, j)) for e in extras]
    return _matmul(name, dy, w, (bm, bk), lambda i, j, k: (i, k), b_blk, b_map,
                   (((1,), (1,)), ((), ())), (M // bm, K // bn, N // bk), (bm, bn), outs, exs, epilogue, dep)[0]


def mm_tn(name, a, dy, out_dtype, stacked, dep=None):
    M, K = a.shape
    N = dy.shape[1]
    n4 = N // N_CHIPS if stacked else N
    bm, bn, bk = _pick(K, BM), _pick(n4, BN), _pick_rows(M, BK)
    nb = n4 // bn
    if stacked:
        outs = [((N_CHIPS, K, n4), out_dtype, (None, bm, bn), lambda i, j: (j // nb, i, j % nb))]
    else:
        outs = [((K, N), out_dtype, (bm, bn), lambda i, j: (i, j))]
    return _matmul(name, a, dy, (bk, bm), lambda i, j, k: (k, i), (bk, bn), lambda i, j, k: (k, j),
                   (((0,), (0,)), ((), ())), (K // bm, N // bn, M // bk), (bm, bn), outs, dep=dep)[0]


def _split3(x):
    hi = x.astype(BF16)
    r = x - hi.astype(F32)
    mid = r.astype(BF16)
    lo = (r - mid.astype(F32)).astype(BF16)
    return hi, mid, lo


def _exact_dot(x, m, n):
    out = None
    for p in _split3(x)[:n]:
        d = jnp.dot(p, m, preferred_element_type=F32)
        out = d if out is None else out + d
    return out


def _row_spec(tr, width, cmap):
    return pl.BlockSpec((tr, width), lambda i, h: (i, cmap(h)))


def _full_spec(p):
    return pl.BlockSpec(p.shape, lambda i, h: (0,) * p.ndim)


def rowwise_fwd(name, fn, rows, params, outs, tr, nh=1):
    S = rows[0][0].shape[0]
    nr, npar = len(rows), len(params)

    def body(*refs):
        vals = fn(*[r[...].astype(F32) for r in refs[:nr]], *[p[...] for p in refs[nr:nr + npar]])
        for o, v in zip(refs[nr + npar:], vals):
            o[...] = v.astype(o.dtype)

    return pl.pallas_call(
        body, name=name, grid=(S // tr, nh),
        in_specs=[_row_spec(tr, w, cm) for (_, w, cm) in rows] + [_full_spec(p) for p in params],
        out_specs=[_row_spec(tr, w, cm) for (_, w, cm, _) in outs],
        out_shape=[jax.ShapeDtypeStruct((S, c), d) for (c, _, _, d) in outs],
        compiler_params=_cp(("parallel", "arbitrary")),
    )(*[r[0] for r in rows], *params)


def rowwise_bwd(name, fn, rows, params, cts, grads, tr, nh=1, n_diff=None, add=None):
    S = rows[0][0].shape[0]
    nr, npar, nct = len(rows), len(params), len(cts)
    n_diff = nr if n_diff is None else n_diff
    n_add = 1 if add is not None else 0

    def body(*refs):
        row_refs = refs[:nr]
        par_refs = refs[nr:nr + npar]
        ct_refs = refs[nr + npar:nr + npar + nct]
        add_refs = refs[nr + npar + nct:nr + npar + nct + n_add]
        o = nr + npar + nct + n_add
        g_refs = refs[o:o + n_diff]
        cp_refs = refs[o + n_diff:o + n_diff + n_add]
        pg_refs = refs[o + n_diff + n_add:]
        i, h = pl.program_id(0), pl.program_id(1)
        rv = [r[...].astype(F32) for r in row_refs]
        pv = [p[...] for p in par_refs]
        aux = rv[n_diff:]

        def f(*dp):
            return fn(*dp[:n_diff], *aux, *dp[n_diff:])

        _, vjp = jax.vjp(f, *rv[:n_diff], *pv)
        gs = vjp(tuple(c[...].astype(F32) for c in ct_refs))
        for n, (g_ref, (_, _, _, _, over)) in enumerate(zip(g_refs, grads)):
            g = gs[n]
            if n == 0 and add is not None:
                g = g + add_refs[0][...]
                cp_refs[0][...] = g.astype(BF16)
            if over:
                @pl.when(h == 0)
                def _(g_ref=g_ref):
                    g_ref[...] = jnp.zeros_like(g_ref)
                g_ref[...] += g.astype(g_ref.dtype)
            else:
                g_ref[...] = g.astype(g_ref.dtype)
        for pg_ref, g in zip(pg_refs, gs[n_diff:]):
            @pl.when((i == 0) & (h == 0))
            def _(pg_ref=pg_ref):
                pg_ref[...] = jnp.zeros_like(pg_ref)
            pg_ref[...] += g

    in_specs = [_row_spec(tr, w, cm) for (_, w, cm) in rows] + [_full_spec(p) for p in params]
    in_specs += [_row_spec(tr, w, cm) for (_, w, cm) in cts]
    operands = [r[0] for r in rows] + list(params) + [c[0] for c in cts]
    out_specs = [_row_spec(tr, w, cm) for (_, w, cm, _, _) in grads]
    out_shape = [jax.ShapeDtypeStruct((S, c), d) for (c, _, _, d, _) in grads]
    if add is not None:
        in_specs.append(_row_spec(tr, add[1], add[2]))
        operands.append(add[0])
        out_specs.append(_row_spec(tr, add[1], add[2]))
        out_shape.append(jax.ShapeDtypeStruct(add[0].shape, BF16))
    out_specs += [_full_spec(p) for p in params]
    out_shape += [jax.ShapeDtypeStruct(p.shape, F32) for p in params]
    return pl.pallas_call(
        body, name=name, grid=(S // tr, nh), in_specs=in_specs, out_specs=out_specs, out_shape=out_shape,
        compiler_params=_cp(("arbitrary", "arbitrary")),
    )(*operands)


def _c0(h):
    return 0


def _rms(x, g, n):
    return x * lax.rsqrt(jnp.sum(x * x, axis=-1, keepdims=True) * (1.0 / n) + EPS) * g


def _rmsnorm_fn(d):
    def fn(x, g):
        return (_rms(x, g, d),)
    return fn


def rmsnorm_fwd(name, x, g):
    S, D = x.shape
    return rowwise_fwd(name, _rmsnorm_fn(D), [(x, D, _c0)], [g], [(D, D, _c0, BF16)], _pick_rows(S, 256))[0]


def rmsnorm_bwd(name, x, g, dh, dres):
    S, D = x.shape
    return rowwise_bwd(name, _rmsnorm_fn(D), [(x, D, _c0)], [g], [(dh, D, _c0)], [(D, D, _c0, F32, False)],
                       _pick_rows(S, 256), add=(dres, D, _c0))


def loss_head(y, t):
    S, D = y.shape
    tr = _pick_rows(S, 256)

    def body(y_ref, t_ref, l_ref, d_ref, db_ref):
        @pl.when(pl.program_id(0) == 0)
        def _():
            l_ref[...] = jnp.zeros_like(l_ref)
        e = y_ref[...] - t_ref[...]
        l_ref[...] += 0.5 * jnp.sum(jnp.sum(e * e, axis=1, keepdims=True), axis=0, keepdims=True) * (1.0 / D)
        d = e * (1.0 / D)
        d_ref[...] = d
        db_ref[...] = d.astype(BF16)

    row = pl.BlockSpec((tr, D), lambda i: (i, 0))
    return pl.pallas_call(
        body, name="loss_head", grid=(S // tr,), in_specs=[row, row],
        out_specs=[pl.BlockSpec((1, 1), lambda i: (0, 0)), row, row],
        out_shape=[jax.ShapeDtypeStruct((1, 1), F32), jax.ShapeDtypeStruct((S, D), F32), jax.ShapeDtypeStruct((S, D), BF16)],
        compiler_params=_cp(("arbitrary",)),
    )(y, t)


def _dot_nt(a, b):
    return lax.dot_general(a, b, (((1,), (1,)), ((), ())), preferred_element_type=F32)


def _dot_tn(a, b):
    return lax.dot_general(a, b, (((0,), (0,)), ((), ())), preferred_element_type=F32)


AT_TQ, AT_TK = 512, 256


def attn_fwd(name, q, k, v, nh, dq, qoff, koff, voff, vstep, scale, cf_col=None, cf_row=None):
    S = q.shape[0]
    tq = _pick_rows(S, AT_TQ)
    tk = _pick_rows(S, AT_TK)
    nq = S // tq
    bias = cf_col is not None

    def body(*refs):
        if bias:
            q_ref, k_ref, v_ref, cfc_ref, cfr_ref, o_ref, of_ref, lse_ref = refs
        else:
            q_ref, k_ref, v_ref, o_ref, of_ref, lse_ref = refs
        i = pl.program_id(1)
        qv = q_ref[...].astype(BF16)
        row = i * tq + lax.broadcasted_iota(jnp.int32, (tq, tk), 0)
        coli = lax.broadcasted_iota(jnp.int32, (tq, tk), 1)
        nkb = ((i + 1) * tq + tk - 1) // tk

        def logits(j):
            return _dot_nt(qv, k_ref[pl.ds(pl.multiple_of(j * tk, tk), tk), :].astype(BF16)) * scale

        def step(j, carry):
            s, m, l, acc, acc_lo = carry
            s_next = logits(jnp.minimum(j + 1, nkb - 1))
            vs = v_ref[pl.ds(pl.multiple_of(j * tk, tk), tk), :].astype(BF16)
            if bias:
                s = s + cfc_ref[0] - cfr_ref[0, pl.ds(j, 1), :]
            s = jnp.where(j * tk + coli <= row, s, NEG)
            m_new = jnp.maximum(m, jnp.max(s, axis=1, keepdims=True))
            alpha = jnp.exp(m - m_new)
            p = jnp.exp(s - m_new)
            pb = p.astype(BF16)
            l = alpha * l + jnp.sum(p, axis=1, keepdims=True)
            acc = alpha * acc + jnp.dot(pb, vs, preferred_element_type=F32)
            acc_lo = alpha * acc_lo + jnp.dot((p - pb.astype(F32)).astype(BF16), vs, preferred_element_type=F32)
            return s_next, m_new, l, acc, acc_lo

        z = jnp.zeros((tq, HEAD_DIM), F32)
        _, m, l, acc, acc_lo = lax.fori_loop(
            0, nkb, step, (logits(0), jnp.full((tq, 1), NEG, F32), jnp.zeros((tq, 1), F32), z, z))
        o_ref[...] = (acc / l).astype(o_ref.dtype)
        of_ref[...] = (acc + acc_lo) / l
        lse_ref[0] = m + jnp.log(l)

    in_specs = [pl.BlockSpec((tq, dq), lambda h, i: (i, qoff + h)),
                pl.BlockSpec((S, dq), lambda h, i: (0, koff + h)),
                pl.BlockSpec((S, HEAD_DIM), lambda h, i: (0, voff + vstep * h))]
    operands = [q, k, v]
    if bias:
        in_specs += [pl.BlockSpec((1, tq, 1), lambda h, i: (h, i, 0)), pl.BlockSpec((1, S // tk, tk), lambda h, i: (h, 0, 0))]
        operands += [cf_col, cf_row]
    return pl.pallas_call(
        body, name=name, grid=(nh, nq), in_specs=in_specs,
        out_specs=[pl.BlockSpec((tq, HEAD_DIM), lambda h, i: (i, h)), pl.BlockSpec((tq, HEAD_DIM), lambda h, i: (i, h)),
                   pl.BlockSpec((1, tq, 1), lambda h, i: (h, i, 0))],
        out_shape=[jax.ShapeDtypeStruct((S, nh * HEAD_DIM), BF16), jax.ShapeDtypeStruct((S, nh * HEAD_DIM), F32),
                   jax.ShapeDtypeStruct((nh, S, 1), F32)],
        compiler_params=_cp(("parallel", "arbitrary")),
    )(*operands)


def attn_bwd(name, q, k, v, o, do, lse, nh, dq, qoff, koff, voff, vstep, scale, cf_col=None, cf_row=None):
    S = q.shape[0]
    tq = _pick_rows(S, AT_TQ)
    tk = _pick_rows(S, AT_TK)
    nq = S // tq
    bias = cf_col is not None

    def body(*refs):
        if bias:
            q_ref, k_ref, v_ref, o_ref, do_ref, lse_ref, cfc_ref, cfr_ref, dq_ref, dk_ref, dv_ref, dcf_ref = refs
        else:
            q_ref, k_ref, v_ref, o_ref, do_ref, lse_ref, dq_ref, dk_ref, dv_ref = refs
        dk_ref[...] = jnp.zeros_like(dk_ref)
        dv_ref[...] = jnp.zeros_like(dv_ref)
        if bias:
            dcf_ref[...] = jnp.zeros_like(dcf_ref)
        rowi = lax.broadcasted_iota(jnp.int32, (tq, tk), 0)
        coli = lax.broadcasted_iota(jnp.int32, (tq, tk), 1)

        def outer(i, _):
            roff = pl.multiple_of(i * tq, tq)
            qv = q_ref[pl.ds(roff, tq), :].astype(BF16)
            dov = do_ref[pl.ds(roff, tq), :]
            delta = jnp.sum(dov.astype(F32) * o_ref[pl.ds(roff, tq), :].astype(F32), axis=1, keepdims=True)
            lse = lse_ref[0, pl.ds(roff, tq), :]
            if bias:
                cq = cfc_ref[0, pl.ds(roff, tq), :]

            nkb = ((i + 1) * tq + tk - 1) // tk

            def products(j):
                off = pl.multiple_of(j * tk, tk)
                return (_dot_nt(qv, k_ref[pl.ds(off, tk), :].astype(BF16)) * scale,
                        _dot_nt(dov, v_ref[pl.ds(off, tk), :].astype(BF16)))

            def inner(j, carry):
                s, dp, dq_acc = carry
                s_next, dp_next = products(jnp.minimum(j + 1, nkb - 1))
                off = pl.multiple_of(j * tk, tk)
                ks = k_ref[pl.ds(off, tk), :].astype(BF16)
                if bias:
                    s = s + cq - cfr_ref[0, pl.ds(j, 1), :]
                p = jnp.where(j * tk + coli <= i * tq + rowi, jnp.exp(s - lse), 0.0)
                ds = p * (dp - delta)
                if bias:
                    dcf_ref[0, pl.ds(j, 1), :] -= jnp.sum(ds, axis=0, keepdims=True)
                dsb = (ds * scale).astype(BF16)
                dv_ref[pl.ds(off, tk), :] += _dot_tn(p.astype(BF16), dov)
                dk_ref[pl.ds(off, tk), :] += _dot_tn(dsb, qv)
                return s_next, dp_next, dq_acc + jnp.dot(dsb, ks, preferred_element_type=F32)

            dq_ref[pl.ds(roff, tq), :] = lax.fori_loop(0, nkb, inner, products(0) + (jnp.zeros((tq, dq), F32),))[2]
            return 0

        lax.fori_loop(0, nq, outer, 0)

    in_specs = [pl.BlockSpec((S, dq), lambda h: (0, qoff + h)),
                pl.BlockSpec((S, dq), lambda h: (0, koff + h)),
                pl.BlockSpec((S, HEAD_DIM), lambda h: (0, voff + vstep * h)),
                pl.BlockSpec((S, HEAD_DIM), lambda h: (0, h)),
                pl.BlockSpec((S, HEAD_DIM), lambda h: (0, h)),
                pl.BlockSpec((1, S, 1), lambda h: (h, 0, 0))]
    operands = [q, k, v, o, do, lse]
    out_specs = [pl.BlockSpec((S, dq), lambda h: (0, h)), pl.BlockSpec((S, dq), lambda h: (0, h)),
                 pl.BlockSpec((S, HEAD_DIM), lambda h: (0, h))]
    out_shape = [jax.ShapeDtypeStruct((S, nh * dq), F32), jax.ShapeDtypeStruct((S, nh * dq), F32),
                 jax.ShapeDtypeStruct((S, nh * HEAD_DIM), F32)]
    if bias:
        in_specs += [pl.BlockSpec((1, S, 1), lambda h: (h, 0, 0)), pl.BlockSpec((1, S // tk, tk), lambda h: (h, 0, 0))]
        operands += [cf_col, cf_row]
        out_specs.append(pl.BlockSpec((1, S // tk, tk), lambda h: (h, 0, 0)))
        out_shape.append(jax.ShapeDtypeStruct((nh, S // tk, tk), F32))
    return pl.pallas_call(
        body, name=name, grid=(nh,), in_specs=in_specs, out_specs=out_specs, out_shape=out_shape,
        compiler_params=_cp(("parallel",)),
    )(*operands)


SB_TQ, SB_TK = 512, 256


def _sb_block(z, strict, later_c, tri_after):
    lsn = jnp.minimum(-z, 0.0) - jnp.log(1.0 + jnp.exp(-jnp.abs(z)))
    lsp = lsn + z
    L = lsn if strict is None else jnp.where(strict, lsn, 0.0)
    later = _exact_dot(L, tri_after, 2) + later_c
    a = jnp.exp(lsp + later)
    return lsp, lsn, L, a if strict is None else jnp.where(strict, a, 0.0)


def _tri(n, after_strict):
    r = lax.broadcasted_iota(jnp.int32, (n, n), 0)
    c = lax.broadcasted_iota(jnp.int32, (n, n), 1)
    return jnp.where(r > c if after_strict else r >= c, 1.0, 0.0).astype(BF16)


def sb_fwd(name, qkv, nh):
    S = qkv.shape[0]
    tq = _pick_rows(S, SB_TQ)
    tk = _pick_rows(S, SB_TK)
    nq = S // tq
    scale = 1.0 / math.sqrt(HEAD_DIM)

    def body(q_ref, k_ref, v_ref, o_ref, of_ref):
        i = pl.program_id(1)
        qv = q_ref[...]
        tri = _tri(tk, True)
        row = i * tq + lax.broadcasted_iota(jnp.int32, (tq, tk), 0)
        coli = lax.broadcasted_iota(jnp.int32, (tq, tk), 1)

        def logits(j):
            return _dot_nt(qv, k_ref[pl.ds(pl.multiple_of(j * tk, tk), tk), :]) * scale

        nkb = ((i + 1) * tq + tk - 1) // tk

        def step(jj, carry, masked=True):
            z, later_c, acc, acc_lo = carry
            j = nkb - 1 - jj
            z_next = logits(jnp.maximum(j - 1, 0))
            vb = v_ref[pl.ds(pl.multiple_of(j * tk, tk), tk), :]
            _, _, L, a = _sb_block(z, (j * tk + coli < row) if masked else None, later_c, tri)
            ab = a.astype(BF16)
            acc = acc + jnp.dot(ab, vb, preferred_element_type=F32)
            acc_lo = acc_lo + jnp.dot((a - ab.astype(F32)).astype(BF16), vb, preferred_element_type=F32)
            return z_next, later_c + jnp.sum(L, axis=1, keepdims=True), acc, acc_lo

        zero = jnp.zeros((tq, HEAD_DIM), F32)
        _, _, acc, acc_lo = lax.fori_loop(0, nkb, step, (logits(nkb - 1), jnp.zeros((tq, 1), F32), zero, zero))
        o_ref[...] = acc.astype(o_ref.dtype)
        of_ref[...] = acc + acc_lo

    blk = pl.BlockSpec((tq, HEAD_DIM), lambda h, i: (i, h))
    return pl.pallas_call(
        body, name=name, grid=(nh, nq),
        in_specs=[blk,
                  pl.BlockSpec((S, HEAD_DIM), lambda h, i: (0, nh + h)),
                  pl.BlockSpec((S, HEAD_DIM), lambda h, i: (0, 2 * nh + h))],
        out_specs=[blk, blk],
        out_shape=[jax.ShapeDtypeStruct((S, nh * HEAD_DIM), BF16), jax.ShapeDtypeStruct((S, nh * HEAD_DIM), F32)],
        compiler_params=_cp(("parallel", "arbitrary")),
    )(qkv, qkv, qkv)


def sb_bwd(name, qkv, o, do, nh):
    S = qkv.shape[0]
    tq = _pick_rows(S, SB_TQ)
    tk = _pick_rows(S, SB_TK)
    nq = S // tq
    scale = 1.0 / math.sqrt(HEAD_DIM)

    def body(q_ref, k_ref, v_ref, o_ref, do_ref, dq_ref, dk_ref, dv_ref, dk_acc, dv_acc):
        dk_acc[...] = jnp.zeros_like(dk_acc)
        dv_acc[...] = jnp.zeros_like(dv_acc)
        tri = _tri(tk, True)
        tri_inc = _tri(tk, False)
        rowi = lax.broadcasted_iota(jnp.int32, (tq, tk), 0)
        coli = lax.broadcasted_iota(jnp.int32, (tq, tk), 1)

        def outer(i, _):
            roff = pl.multiple_of(i * tq, tq)
            qv = q_ref[pl.ds(roff, tq), :]
            dov = do_ref[pl.ds(roff, tq), :]
            dtot = jnp.sum(dov.astype(F32) * o_ref[pl.ds(roff, tq), :].astype(F32), axis=1, keepdims=True)
            row = i * tq + rowi

            def products(j):
                off = pl.multiple_of(j * tk, tk)
                return _dot_nt(qv, k_ref[pl.ds(off, tk), :]) * scale, _dot_nt(dov, v_ref[pl.ds(off, tk), :])

            nkb = ((i + 1) * tq + tk - 1) // tk

            def inner(jj, carry, masked=True):
                z, da, later_c, suf_c, dq_acc = carry
                j = nkb - 1 - jj
                z_next, da_next = products(jnp.maximum(j - 1, 0))
                off = pl.multiple_of(j * tk, tk)
                kb = k_ref[pl.ds(off, tk), :]
                strict = (j * tk + coli < row) if masked else None
                lsp, lsn, L, a = _sb_block(z, strict, later_c, tri)
                dl = da * a
                before = dtot - (suf_c + _exact_dot(dl, tri_inc, 3))
                dz = (dl * jnp.exp(lsn) - jnp.exp(lsp) * before) * scale
                if masked:
                    dz = jnp.where(strict, dz, 0.0)
                dzb = dz.astype(BF16)
                dv_acc[pl.ds(off, tk), :] += _dot_tn(a.astype(BF16), dov)
                dk_acc[pl.ds(off, tk), :] += _dot_tn(dzb, qv)
                return (z_next, da_next, later_c + jnp.sum(L, axis=1, keepdims=True), suf_c + jnp.sum(dl, axis=1, keepdims=True),
                        dq_acc + jnp.dot(dzb, kb, preferred_element_type=F32))

            z1 = jnp.zeros((tq, 1), F32)
            res = lax.fori_loop(0, nkb, inner, products(nkb - 1) + (z1, z1, jnp.zeros((tq, HEAD_DIM), F32)))
            dq_ref[pl.ds(roff, tq), :] = res[4].astype(dq_ref.dtype)
            return 0

        lax.fori_loop(0, nq, outer, 0)
        dk_ref[...] = dk_acc[...].astype(dk_ref.dtype)
        dv_ref[...] = dv_acc[...].astype(dv_ref.dtype)

    def col(off):
        return pl.BlockSpec((S, HEAD_DIM), lambda h: (0, off + h))

    dq, dk, dv = pl.pallas_call(
        body, name=name, grid=(nh,),
        in_specs=[col(0), col(nh), col(2 * nh), col(0), col(0)],
        out_specs=[col(0), col(0), col(0)],
        out_shape=[jax.ShapeDtypeStruct((S, nh * HEAD_DIM), BF16)] * 3,
        scratch_shapes=[pltpu.VMEM((S, HEAD_DIM), F32), pltpu.VMEM((S, HEAD_DIM), F32)],
        compiler_params=_cp(("parallel",)),
    )(qkv, qkv, qkv, o, do)
    return jnp.concatenate([dq, dk, dv], axis=1)


def seq_cumsum(name, x, reverse):
    S, W = x.shape

    tb = _pick_rows(S, 256)

    def body(x_ref, o_ref):
        parts = _split3(x_ref[...])
        c = lax.broadcasted_iota(jnp.int32, (tb, S), 1)
        for b in range(S // tb):
            r = b * tb + lax.broadcasted_iota(jnp.int32, (tb, S), 0)
            t = jnp.where(r <= c if reverse else r >= c, 1.0, 0.0).astype(BF16)
            out = None
            for p in parts:
                d = jnp.dot(t, p, preferred_element_type=F32)
                out = d if out is None else out + d
            o_ref[b * tb:(b + 1) * tb, :] = out

    return pl.pallas_call(body, name=name, out_shape=jax.ShapeDtypeStruct((S, W), F32),
                          compiler_params=pltpu.CompilerParams(vmem_limit_bytes=VMEM_LIMIT))(x)


def _adamw_math(w, g, m, v):
    c1 = 1.0 - ADAM_B1 ** ADAM_STEP
    c2 = 1.0 - ADAM_B2 ** ADAM_STEP
    nm = ADAM_B1 * m + (1.0 - ADAM_B1) * g
    nv = ADAM_B2 * v + (1.0 - ADAM_B2) * (g * g)
    return -ADAM_LR * ((nm / c1) / (jnp.sqrt(nv / c2) + ADAM_EPS) + ADAM_WD * w), nm, nv


def adamw(name, w, g, m, v):
    R, C = g.shape
    tr = _pick_rows(R, 256)

    def body(w_ref, g_ref, m_ref, v_ref, d_ref, nm_ref, nv_ref):
        d_ref[...], nm_ref[...], nv_ref[...] = _adamw_math(w_ref[...], g_ref[...], m_ref[...], v_ref[...])

    row = pl.BlockSpec((tr, C), lambda i: (i, 0))
    return pl.pallas_call(
        body, name=name, grid=(R // tr,), in_specs=[row] * 4, out_specs=[row] * 3,
        out_shape=[jax.ShapeDtypeStruct((R, C), F32)] * 3, compiler_params=_cp(("parallel",)),
    )(w, g, m, v)


def adamw_big(name, w, m, v, mine, other, c_arr, layer, prev, dep=None):
    L, R, C = w.shape
    r2 = R // 2
    tr = _pick_rows(r2, 256)
    nb = r2 // tr

    def body(c_ref, w_ref, m_ref, v_ref, mine_ref, other_ref, *rest):
        g_ref, d_ref, nm_ref, nv_ref = rest[-4:]
        g = jnp.where(pl.program_id(0) == c_ref[0], mine_ref[...], other_ref[...])
        g_ref[...] = g
        d_ref[...], nm_ref[...], nv_ref[...] = _adamw_math(w_ref[...], g, m_ref[...], v_ref[...])

    sel = pl.BlockSpec((None, tr, C), lambda hf, i, c: (layer, hf * nb + i, 0))
    in_specs = [sel, sel, sel,
                pl.BlockSpec((tr, C), lambda hf, i, c: (jnp.where(hf == c[0], i, 0), 0)),
                pl.BlockSpec((tr, C), lambda hf, i, c: (jnp.where(hf == c[0], 0, i), 0))]
    operands = [c_arr, w, m, v, mine, other]
    aliases = {}
    if prev is not None:
        in_specs += [ANY] * 4
        aliases = {len(operands) + k: k for k in range(4)}
        operands += list(prev)
    if dep is not None:
        in_specs += [ANY]
        operands += [dep]
    return pl.pallas_call(
        body, name=name,
        grid_spec=pltpu.PrefetchScalarGridSpec(num_scalar_prefetch=1, grid=(2, nb), in_specs=in_specs, out_specs=[sel] * 4),
        out_shape=[jax.ShapeDtypeStruct((L, R, C), F32)] * 4, input_output_aliases=aliases,
        compiler_params=_cp(("arbitrary", "arbitrary")),
    )(*operands)


ANY = pl.BlockSpec(memory_space=pl.ANY)


def _place():
    x, y, c = lax.axis_index("x"), lax.axis_index("y"), lax.axis_index("c")
    others = [(1 - x, y), (x, 1 - y), (1 - x, 1 - y)]
    return x, y, c, others


def add_half(name, g, other, c_arr):
    _, R, C = g.shape
    r2 = R // 2
    tr = _pick_rows(r2, 512)
    nb = r2 // tr

    def body(c_ref, g_ref, o_ref, out_ref):
        out_ref[...] = (g_ref[...].astype(F32) + o_ref[...].astype(F32)).astype(out_ref.dtype)

    return pl.pallas_call(
        body, name=name,
        grid_spec=pltpu.PrefetchScalarGridSpec(
            num_scalar_prefetch=1, grid=(N_CHIPS, nb),
            in_specs=[pl.BlockSpec((None, tr, C), lambda s, i, c: (s, c[0] * nb + i, 0)),
                      pl.BlockSpec((None, tr, C), lambda s, i, c: (s, i, 0))],
            out_specs=pl.BlockSpec((None, tr, C), lambda s, i, c: (s, i, 0))),
        out_shape=jax.ShapeDtypeStruct((N_CHIPS, r2, C), BF16),
        compiler_params=_cp(("parallel", "parallel")),
    )(c_arr, g, other)


def sum_slabs(name, got, parts, chip_arr):
    _, r2, C = parts.shape
    tr = _pick_rows(r2, 512)

    def body(chip_ref, *refs):
        own_ref, out_ref = refs[N_CHIPS], refs[N_CHIPS + 1]
        acc = None
        for s in range(N_CHIPS):
            v = jnp.where(chip_ref[0] == s, own_ref[...], refs[s][...]).astype(F32)
            acc = v if acc is None else acc + v
        out_ref[...] = acc

    def slab(s):
        return pl.BlockSpec((None, tr, C), lambda i, ch: (jnp.where(ch[0] == s, (s + 1) % N_CHIPS, s), i, 0))

    return pl.pallas_call(
        body, name=name,
        grid_spec=pltpu.PrefetchScalarGridSpec(
            num_scalar_prefetch=1, grid=(r2 // tr,),
            in_specs=[slab(s) for s in range(N_CHIPS)] + [pl.BlockSpec((None, tr, C), lambda i, ch: (ch[0], i, 0))],
            out_specs=pl.BlockSpec((tr, C), lambda i, ch: (i, 0))),
        out_shape=jax.ShapeDtypeStruct((r2, C), F32), compiler_params=_cp(("parallel",)),
    )(chip_arr, got, got, got, got, parts)


HBM = pl.BlockSpec(memory_space=pltpu.HBM)
SEM = pl.BlockSpec(memory_space=pltpu.SEMAPHORE)
EFFECT = pltpu.SideEffectType.DATAFLOW_SIDE_EFFECTING


def _in_hbm(a):
    return pltpu.with_memory_space_constraint(a, pltpu.HBM)


def _chip_copies(kind, shapes, src, land, send, recv, mine):
    x, y, c, others = _place()
    me = 2 * x + y
    cps = []

    def remote(s_ref, d_ref, k, to):
        cps.append(pltpu.make_async_remote_copy(src_ref=s_ref, dst_ref=d_ref, send_sem=send.at[k], recv_sem=recv.at[k],
                                                device_id=to, device_id_type=MESH))

    for t, shape in enumerate(shapes):
        if kind == "swap":
            r2 = shape[1] // 2
            remote(src[t].at[:, pl.ds((1 - c) * r2, r2), :], land[t], t, (x, y, 1 - c))
        elif kind == "whole":
            remote(src[t], land[t], t, (x, y, 1 - c))
        for j, (px, py) in enumerate(others if kind in ("gather", "scatter", "pass") else []):
            slot = me if mine else 2 * px + py
            if kind == "gather":
                r2 = shape[0] // 2
                rows = pl.ds(c * r2, r2)
                remote(src[t].at[rows, :], land[t].at[slot, rows, :], 3 * t + j, (px, py, c))
            elif kind == "scatter":
                remote(src[t].at[2 * px + py], land[t].at[slot], 3 * t + j, (px, py, c))
            else:
                r2 = shape[1] // 2
                rows = src[t].at[2 * px + py, pl.ds((c if mine else 1 - c) * r2, r2), :]
                remote(rows, rows, 3 * t + j, (x, y, 1 - c))
    return cps


def _n_copies(kind, n):
    return n if kind in ("swap", "whole") else 3 * n


def chips_start(name, kind, srcs, land_shapes, after=None):
    n, nl = len(srcs), len(land_shapes)
    shapes = [s.shape for s in srcs]
    n_buf = n + nl
    n_in = n_buf + (0 if after is None else 1)
    n_sem = _n_copies(kind, n)

    def body(*refs):
        src, land = refs[:n], refs[n:n_buf]
        send, recv = refs[n_in], refs[n_in + 1]
        token = refs[-1]
        for cp in _chip_copies(kind, shapes, src, land, send, recv, True):
            cp.start()
        token[...] = jnp.zeros_like(token)

    lands = [lax.empty(s, srcs[0].dtype) for s in land_shapes]
    out = pl.pallas_call(
        body, name=name,
        out_shape=(pltpu.SemaphoreType.DMA((n_sem,)), pltpu.SemaphoreType.DMA((n_sem,)))
        + tuple(pltpu.HBM(s.shape, s.dtype) for s in srcs) + tuple(pltpu.HBM(l.shape, l.dtype) for l in lands)
        + (jax.ShapeDtypeStruct((8, LANE), F32),),
        in_specs=(HBM,) * n_buf + (ANY,) * (n_in - n_buf),
        out_specs=(SEM, SEM) + (HBM,) * n_buf + (pl.BlockSpec(memory_space=pltpu.VMEM),),
        input_output_aliases={k: 2 + k for k in range(n_buf)},
        compiler_params=pltpu.CompilerParams(has_side_effects=EFFECT),
    )(*[_in_hbm(s) for s in srcs], *[_in_hbm(l) for l in lands], *([] if after is None else [after]))
    return (kind, shapes, out[0], out[1], out[2:2 + n], out[2 + n:2 + n_buf]), out[-1][:1, :1]


def chips_wait(name, handle, after):
    kind, shapes, send, recv, srcs, lands = handle
    n, n_buf = len(srcs), len(srcs) + len(lands)
    after = [a for a in (list(after) if isinstance(after, (list, tuple)) else [after]) if a is not None]

    def body(*refs):
        src, land = refs[:n], refs[n:n_buf]
        for cp in _chip_copies(kind, shapes, src, land, refs[n_buf], refs[n_buf + 1], True):
            cp.wait_send()
        for cp in _chip_copies(kind, shapes, src, land, refs[n_buf], refs[n_buf + 1], False):
            cp.wait_recv()

    out = pl.pallas_call(
        body, name=name,
        out_shape=tuple(pltpu.HBM(s.shape, s.dtype) for s in srcs) + tuple(pltpu.HBM(l.shape, l.dtype) for l in lands),
        in_specs=(HBM,) * n_buf + (SEM, SEM) + (ANY,) * len(after), out_specs=(HBM,) * n_buf,
        input_output_aliases={k: k for k in range(n_buf)},
        compiler_params=pltpu.CompilerParams(has_side_effects=EFFECT),
    )(*srcs, *lands, send, recv, *after)
    return list(out[n:]), list(out[:n])


def all_sum_small(name, v, dep=None):
    R = v.shape[0]
    n_dep = 0 if dep is None else 1

    def body(v_ref, *refs):
        out_ref, slots, send, recv = refs[n_dep:]
        x, y, c, _ = _place()
        me = 4 * x + 2 * y + c
        slots[me] = v_ref[...]
        cps = []
        for k in range(1, 8):
            dx, dy, dc = (k >> 2) & 1, (k >> 1) & 1, k & 1
            to = (x ^ dx, y ^ dy, c ^ dc)
            cps.append(pltpu.make_async_remote_copy(
                src_ref=v_ref, dst_ref=slots.at[me], send_sem=send.at[k - 1], recv_sem=recv.at[k - 1],
                device_id=to, device_id_type=MESH))
        for cp in cps:
            cp.start()
        for k in range(1, 8):
            dx, dy, dc = (k >> 2) & 1, (k >> 1) & 1, k & 1
            frm = 4 * (x ^ dx) + 2 * (y ^ dy) + (c ^ dc)
            pltpu.make_async_remote_copy(
                src_ref=v_ref, dst_ref=slots.at[frm], send_sem=send.at[k - 1], recv_sem=recv.at[k - 1],
                device_id=(x, y, c), device_id_type=MESH).wait_recv()
        for cp in cps:
            cp.wait_send()
        acc = slots[0]
        for d in range(1, 8):
            acc = acc + slots[d]
        out_ref[...] = acc

    vm = pl.BlockSpec(memory_space=pltpu.VMEM)
    return pl.pallas_call(
        body, name=name, in_specs=[vm] + [ANY] * n_dep, out_specs=vm, out_shape=jax.ShapeDtypeStruct((R, LANE), F32),
        scratch_shapes=[pltpu.VMEM((8, R, LANE), F32), pltpu.SemaphoreType.DMA((7,)), pltpu.SemaphoreType.DMA((7,))],
    )(v, *([] if dep is None else [dep]))


def _rope_mat(n, lo):
    half = MLA_ROPE // 2
    r = lax.broadcasted_iota(jnp.int32, (n, n), 0)
    c = lax.broadcasted_iota(jnp.int32, (n, n), 1)
    plus = (c >= lo + half) & (c < lo + 2 * half) & (r == c - half)
    minus = (c >= lo) & (c < lo + half) & (r == c + half)
    return (jnp.where(plus, 1.0, 0.0) - jnp.where(minus, 1.0, 0.0)).astype(BF16)


def _rope(v, cos, sin, lo):
    return v * cos + _exact_dot(v, _rope_mat(v.shape[-1], lo), 3) * sin


def rope_tables(pos):
    S = pos.shape[0]
    half = MLA_ROPE // 2
    inv = (np.float32(ROPE_THETA) ** (-np.arange(0, half, dtype=np.float32) * np.float32(2.0 / MLA_ROPE))).astype(np.float32)
    f1 = np.zeros((1, LANE), np.float32)
    f1[0, :MLA_ROPE] = np.tile(inv, 2)
    f2 = np.zeros((1, MLA_QK_PAD), np.float32)
    f2[0, MLA_NOPE:MLA_QK] = np.tile(inv, 2)
    tr = _pick_rows(S, 256)

    def body(p_ref, f1_ref, f2_ref, c1, s1, c2, s2):
        p = p_ref[...].astype(F32)
        a1 = p * f1_ref[...]
        a2 = p * f2_ref[...]
        c1[...] = jnp.cos(a1)
        s1[...] = jnp.sin(a1)
        c2[...] = jnp.cos(a2)
        s2[...] = jnp.sin(a2)

    def row(w):
        return pl.BlockSpec((tr, w), lambda i: (i, 0))

    def full(w):
        return pl.BlockSpec((1, w), lambda i: (0, 0))

    return pl.pallas_call(
        body, name="rope_tables", grid=(S // tr,), in_specs=[row(1), full(LANE), full(MLA_QK_PAD)],
        out_specs=[row(LANE), row(LANE), row(MLA_QK_PAD), row(MLA_QK_PAD)],
        out_shape=[jax.ShapeDtypeStruct((S, LANE), F32)] * 2 + [jax.ShapeDtypeStruct((S, MLA_QK_PAD), F32)] * 2,
        compiler_params=_cp(("parallel",)),
    )(pos, jnp.asarray(f1), jnp.asarray(f2))


def _log_sigmoid(z):
    return jnp.minimum(z, 0.0) - jnp.log(1.0 + jnp.exp(-jnp.abs(z)))


def _fox_qk_fn(q, k, gq, gk):
    return _rms(q, gq, HEAD_DIM), _rms(k, gk, HEAD_DIM)


def _fox_gate_fn(f, b):
    return (_log_sigmoid(f + b),)


def _mla_pre1_fn(q_rank, kv_rank):
    def fn(cq, ckv, kr, cos, sin, qn, kvn):
        return _rms(cq, qn, q_rank), _rms(ckv, kvn, kv_rank), _rope(kr, cos, sin, 0)
    return fn


def _mla_pre2_fn(qb, kn, kr, cos, sin, gq, gk):
    qh = _rms(_rope(qb, cos, sin, MLA_NOPE), gq, MLA_QK)
    kh = _rms(jnp.concatenate([kn, kr], axis=1), gk, MLA_QK)
    return qh, kh


def _pad_cols(a, n):
    return jnp.pad(a, ((0, 0), (0, n - a.shape[1])))


def _relu2(acc):
    r = jnp.maximum(acc, 0.0)
    return r * r, r


def _add(acc, res):
    return (acc + res,)


def _times_2r(acc, r):
    return (acc * (2.0 * r.astype(F32)),)


def kernel(x, positions, mix_norm, mlp_norm, sb_w_in, sb_w_out, fox_w_in, fox_b_f, fox_q_gain, fox_k_gain, fox_w_out, mla_w_in, mla_q_norm, mla_kv_norm, mla_w_uq, mla_w_ukv, mla_q_gain, mla_k_gain, mla_w_out, mlp_w1, mlp_w2, loss_target, m_mix_norm, m_mlp_norm, m_sb_w_in, m_sb_w_out, m_fox_w_in, m_fox_b_f, m_fox_q_gain, m_fox_k_gain, m_fox_w_out, m_mla_w_in, m_mla_q_norm, m_mla_kv_norm, m_mla_w_uq, m_mla_w_ukv, m_mla_q_gain, m_mla_k_gain, m_mla_w_out, m_mlp_w1, m_mlp_w2, v_mix_norm, v_mlp_norm, v_sb_w_in, v_sb_w_out, v_fox_w_in, v_fox_b_f, v_fox_q_gain, v_fox_k_gain, v_fox_w_out, v_mla_w_in, v_mla_q_norm, v_mla_kv_norm, v_mla_w_uq, v_mla_w_ukv, v_mla_q_gain, v_mla_k_gain, v_mla_w_out, v_mlp_w1, v_mlp_w2):
    S, D = x.shape[1], x.shape[2]
    nh = D // HEAD_DIM
    W = nh * HEAD_DIM
    depth = mix_norm.shape[0]
    q_rank, kv_rank = mla_w_uq.shape[1], mla_w_ukv.shape[1]
    n_fox_in = 3 * W + nh
    fox_pad = -(-n_fox_in // LANE) * LANE
    n_down = q_rank + kv_rank + MLA_ROPE
    down_pad = q_rank + kv_rank + LANE
    tr = _pick_rows(S, 256)
    tk = _pick_rows(S, 256)

    ax, ay, ac = lax.axis_index("x"), lax.axis_index("y"), lax.axis_index("c")
    chip = 2 * ax + ay
    c_arr = jnp.reshape(ac, (1,)).astype(jnp.int32)
    chip_arr = jnp.reshape(chip, (1,)).astype(jnp.int32)

    def bf(a):
        return a.astype(BF16)

    n_small_in = q_rank + kv_rank
    rows_in = -(-n_small_in // (8 * LANE)) * 8
    placed = jnp.zeros((rows_in * LANE,), F32)
    placed = lax.dynamic_update_slice(placed, mla_q_norm[0], (chip * mla_q_norm.shape[1],))
    placed = lax.dynamic_update_slice(placed, mla_kv_norm[0], (q_rank + chip * mla_kv_norm.shape[1],))
    placed = placed * (ac == 0).astype(F32)
    norms = all_sum_small("gather_norms", placed.reshape(rows_in, LANE)).reshape(-1)
    q_norm_full = norms[:q_rank].reshape(1, q_rank)
    kv_norm_full = norms[q_rank:q_rank + kv_rank].reshape(1, kv_rank)

    groups = [(i, part) for i in range(depth) for part in ("mix", "mlp")]
    G = {}

    casts = {}

    def group_shards(i, part):
        if (i, part) in casts:
            return casts[(i, part)]
        kind, j = i % N_MIXERS, i // N_MIXERS
        zero = G[groups[0]]["token"] if groups[0] in G else None

        def cast(w):
            return bf(w if zero is None else w + zero.reshape((1,) * w.ndim))

        if part == "mlp":
            names, ws = ["w1", "w2"], [mlp_w1[i], mlp_w2[i]]
        elif kind == 0:
            names, ws = ["w_in", "w_out"], [sb_w_in[j], sb_w_out[j]]
        elif kind == 1:
            names, ws = ["w_in", "w_out"], [fox_w_in[j], fox_w_out[j]]
        else:
            names, ws = ["w_in", "w_uq", "w_ukv", "w_out"], [mla_w_in[j], mla_w_uq[j], mla_w_ukv[j], mla_w_out[j]]
        casts[(i, part)] = names, [cast(w) for w in ws]
        return casts[(i, part)]

    def cross_chips_start(key, after):
        names, shards = group_shards(*key)
        handle, token = chips_start(f"gather_start_{key[1]}_{key[0]}", "gather", shards,
                                    [(N_CHIPS,) + s.shape for s in shards], after)
        G[key] = dict(names=names, ici=handle, token=token)
        return token

    def cross_cores_start(key, after):
        g = G[key]
        g["lands"], g["shards"] = chips_wait(f"gather_wait_{key[1]}_{key[0]}", g["ici"], after)
        g["d2d"], token = chips_start(f"pass_start_{key[1]}_{key[0]}", "pass", g["lands"], [])
        return token

    def group_ready(key, after):
        g = G[key]
        _, bufs = chips_wait(f"pass_wait_{key[1]}_{key[0]}", g["d2d"], after)
        return dict(zip(g["names"], [lax.dynamic_update_slice(b, s[None], (chip, 0, 0)) for b, s in zip(bufs, g["shards"])]))

    def before_part(s, after):
        key = groups[s]
        if key not in G:
            cross_chips_start(key, norms)
            after = [b for k in groups[1:] for b in group_shards(*k)[1]]
        if "d2d" not in G[key]:
            cross_cores_start(key, after)
        L = group_ready(key, after)
        token = jnp.zeros((1, 1), F32)
        if s + 1 < len(groups) and groups[s + 1] in G:
            token = token + cross_cores_start(groups[s + 1], after)
        last = L[G[key]["names"][0]]
        n_ahead = GATHER_AHEAD - (s == 0)
        for ahead in groups[s + 1:s + 1 + n_ahead]:
            if ahead not in G:
                token = token + cross_chips_start(ahead, last)
            last = G[ahead]["token"]
        return L, token

    def rows_stacked(w):
        return w.reshape(w.shape[0] * w.shape[1], w.shape[2])

    xc = x[0]
    saved = []
    layers = []
    tables = None
    for i in range(depth):
        kind, j = i % N_MIXERS, i // N_MIXERS
        L, token = before_part(2 * i, xc if i else None)
        g1 = mix_norm[i:i + 1] + token
        hb = rmsnorm_fwd(f"norm1_{i}", xc, g1)
        st = dict(x=xc, hb=hb)
        if kind == 0:
            qkv = mm_nn(f"sb_proj_{i}", hb, L["w_in"], BF16)[0]
            o, o_f32 = sb_fwd(f"sb_attn_{i}", qkv, nh)
            st.update(qkv=qkv, o_f32=o_f32)
        elif kind == 1:
            w_in = _pad_cols(jnp.concatenate([L["w_in"][s] for s in range(N_CHIPS)], axis=1), fox_pad)
            proj = mm_nn(f"fox_proj_{i}", hb, w_in, F32)[0]
            gq, gk = fox_q_gain[j:j + 1], fox_k_gain[j:j + 1]
            qk_rows = [(proj, HEAD_DIM, lambda h: h), (proj, HEAD_DIM, lambda h: nh + h)]
            qh, kh = rowwise_fwd(f"fox_qk_{i}", _fox_qk_fn, qk_rows, [gq, gk],
                                 [(W, HEAD_DIM, lambda h: h, BF16)] * 2, tr, nh)
            b_pad = _pad_cols(fox_b_f[j:j + 1], LANE)
            gate_rows = [(proj, LANE, lambda h: 3 * nh)]
            logf = rowwise_fwd(f"fox_gate_{i}", _fox_gate_fn, gate_rows, [b_pad], [(LANE, LANE, _c0, F32)], tr)[0]
            cf = seq_cumsum(f"fox_cf_{i}", logf, False)[:, :nh].T
            cf_col, cf_row = cf.reshape(nh, S, 1), cf.reshape(nh, S // tk, tk)
            scale = 1.0 / math.sqrt(HEAD_DIM)
            o, o_f32, lse = attn_fwd(f"fox_attn_{i}", qh, kh, proj, nh, HEAD_DIM, 0, 0, 2 * nh, 1, scale, cf_col, cf_row)
            st.update(w_in=w_in, proj=proj, qk_rows=qk_rows, gq=gq, gk=gk, qh=qh, kh=kh, b_pad=b_pad, gate_rows=gate_rows,
                      cf_col=cf_col, cf_row=cf_row, lse=lse, scale=scale, o_f32=o_f32)
        else:
            w_in = _pad_cols(rows_stacked(L["w_in"]), down_pad)
            down = mm_nn(f"mla_down_{i}", hb, w_in, F32)[0]
            if tables is None:
                tables = rope_tables(positions.reshape(S, 1))
            cos1, sin1, cos2, sin2 = tables
            pre1_rows = [(down[:, :q_rank], q_rank, _c0), (down[:, q_rank:q_rank + kv_rank], kv_rank, _c0),
                         (down[:, q_rank + kv_rank:], LANE, _c0), (cos1, LANE, _c0), (sin1, LANE, _c0)]
            pre1_fn = _mla_pre1_fn(q_rank, kv_rank)
            c_q, c_kv, k_rope = rowwise_fwd(
                f"mla_pre1_{i}", pre1_fn, pre1_rows, [q_norm_full, kv_norm_full],
                [(q_rank, q_rank, _c0, BF16), (kv_rank, kv_rank, _c0, BF16), (LANE, LANE, _c0, F32)], tr)
            qfull = mm_nn(f"mla_uq_{i}", c_q, L["w_uq"], F32)[0]
            kv = mm_nn(f"mla_ukv_{i}", c_kv, L["w_ukv"], F32)[0]
            qpad = jnp.pad(qfull.reshape(S, nh, MLA_QK), ((0, 0), (0, 0), (0, MLA_QK_PAD - MLA_QK))).reshape(S, nh * MLA_QK_PAD)
            gq, gk = _pad_cols(mla_q_gain[j:j + 1], MLA_QK_PAD), _pad_cols(mla_k_gain[j:j + 1], MLA_QK_PAD)
            pre2_rows = [(qpad, MLA_QK_PAD, lambda h: h), (kv, MLA_NOPE, lambda h: 2 * h), (k_rope, LANE, _c0),
                         (cos2, MLA_QK_PAD, _c0), (sin2, MLA_QK_PAD, _c0)]
            qh, kh = rowwise_fwd(f"mla_pre2_{i}", _mla_pre2_fn, pre2_rows, [gq, gk],
                                 [(nh * MLA_QK_PAD, MLA_QK_PAD, lambda h: h, BF16)] * 2, tr, nh)
            scale = 1.0 / math.sqrt(MLA_QK)
            o, o_f32, lse = attn_fwd(f"mla_attn_{i}", qh, kh, kv, nh, MLA_QK_PAD, 0, 0, 1, 2, scale)
            st.update(w_in=w_in, pre1_rows=pre1_rows, pre1_fn=pre1_fn, c_q=c_q, c_kv=c_kv, pre2_rows=pre2_rows, gq=gq, gk=gk,
                      qh=qh, kh=kh, kv=kv, lse=lse, scale=scale, o_f32=o_f32)
        x1 = mm_nn(f"mix_out_{i}", o, rows_stacked(L["w_out"]), F32, epilogue=_add, extras=(xc,))[0]
        L_mlp, token = before_part(2 * i + 1, x1)
        L.update(L_mlp)
        layers.append(L)
        g2 = mlp_norm[i:i + 1] + token
        h2 = rmsnorm_fwd(f"norm2_{i}", x1, g2)
        a, r = mm_nn(f"mlp_up_{i}", h2, L["w1"], None, epilogue=_relu2, out_dtypes=[BF16, BF16])
        xc = mm_nn(f"mlp_down_{i}", a, rows_stacked(L["w2"]), F32, epilogue=_add, extras=(x1,))[0]
        st.update(o=o, x1=x1, h2=h2, a=a, r=r, g1=g1, g2=g2)
        saved.append(st)

    loss_local, dx, dxb = loss_head(xc, loss_target[0])
    loss = lax.psum(loss_local[0, 0], ("x", "y", "c"))

    small = {}

    def stack_rows(g):
        return g.reshape(N_CHIPS, g.shape[0] // N_CHIPS, g.shape[1])

    full = [dict() for _ in range(depth)]
    sums = []

    def sums_advance(after):
        token = jnp.zeros((1, 1), F32)
        for e in sums:
            tag = e["tag"]
            if "back" in e:
                others, halves = chips_wait(f"sendh_wait_{tag}", e.pop("back"), after)
                full[e["layer"]].update(zip(e["names"], zip(halves, others)))
                e["done"] = True
            elif "chips" in e and e["hold"]:
                e["hold"] -= 1
            elif "chips" in e:
                got, parts = chips_wait(f"xchg_wait_{tag}", e.pop("chips"), after)
                halves = [sum_slabs(f"sum4_{tag}_{t}", g, p, chip_arr) for t, (g, p) in enumerate(zip(got, parts))]
                e["back"], tok = chips_start(f"sendh_start_{tag}", "whole", halves, [h.shape for h in halves])
                token = token + tok
            elif "pair" in e:
                theirs, grads = chips_wait(f"swap_wait_{tag}", e.pop("pair"), after)
                parts = [add_half(f"addh_{tag}_{t}", g, o, c_arr) for t, (g, o) in enumerate(zip(grads, theirs))]
                e["chips"], tok = chips_start(f"xchg_start_{tag}", "scatter", parts, [p.shape for p in parts])
                token = token + tok
            elif not e.get("done"):
                grads = e.pop("grads")
                e["pair"], tok = chips_start(f"swap_start_{tag}", "swap", grads,
                                             [(N_CHIPS, g.shape[1] // 2, g.shape[2]) for g in grads])
                token = token + tok
        sums[:] = [e for e in sums if not e.get("done")]
        return token

    def sums_add(i, part, gr, after):
        sums.append(dict(layer=i, tag=f"{part}_{i}", names=list(gr), grads=list(gr.values()), hold=int(part == "mlp")))
        return sums_advance(after)

    dep = None
    for i in reversed(range(depth)):
        kind, j = i % N_MIXERS, i // N_MIXERS
        L, st = layers[i], saved[i]
        gr = {}
        gr["w2"] = stack_rows(mm_tn(f"mlp_dw2_{i}", st["a"], dxb, BF16, False, dep=dep))
        du = mm_nt(f"mlp_du_{i}", dxb, rows_stacked(L["w2"]), BF16, epilogue=_times_2r, extras=(st["r"],), dep=dep)
        gr["w1"] = mm_tn(f"mlp_dw1_{i}", st["h2"], du, BF16, True)
        dh2 = mm_nt(f"mlp_dh_{i}", du, L["w1"], F32)
        dx1, dx1b, dg2 = rmsnorm_bwd(f"norm2_bwd_{i}", st["x1"], st["g2"], dh2, dx)
        small[("mlp_norm", i)] = dg2
        dep = sums_add(i, "mlp", gr, dx1)
        gr = {}
        gr["w_out"] = stack_rows(mm_tn(f"mix_dwout_{i}", st["o"], dx1b, BF16, False, dep=dep))
        do = mm_nt(f"mix_do_{i}", dx1b, rows_stacked(L["w_out"]), BF16, dep=dep)
        if kind == 0:
            dqkv = sb_bwd(f"sb_attn_bwd_{i}", st["qkv"], st["o_f32"], do, nh)
            gr["w_in"] = mm_tn(f"sb_dwin_{i}", st["hb"], dqkv, BF16, True)
            dh = mm_nt(f"sb_dh_{i}", dqkv, L["w_in"], F32)
        elif kind == 1:
            dqh, dkh, dv, dcf = attn_bwd(f"fox_attn_bwd_{i}", st["qh"], st["kh"], st["proj"], st["o_f32"], do, st["lse"], nh,
                                         HEAD_DIM, 0, 0, 2 * nh, 1, st["scale"], st["cf_col"], st["cf_row"])
            dcf_s = _pad_cols(dcf.reshape(nh, S).T, LANE)
            dlogf = seq_cumsum(f"fox_dcf_{i}", dcf_s, True)
            dgate, db = rowwise_bwd(f"fox_gate_bwd_{i}", _fox_gate_fn, st["gate_rows"], [st["b_pad"]], [(dlogf, LANE, _c0)],
                                    [(LANE, LANE, _c0, BF16, False)], tr)
            dq, dk, dgq, dgk = rowwise_bwd(
                f"fox_qk_bwd_{i}", _fox_qk_fn, st["qk_rows"], [st["gq"], st["gk"]],
                [(dqh, HEAD_DIM, lambda h: h), (dkh, HEAD_DIM, lambda h: h)], [(W, HEAD_DIM, lambda h: h, BF16, False)] * 2, tr, nh)
            small[("fox_b_f", j)] = db[:, :nh]
            small[("fox_q_gain", j)] = dgq
            small[("fox_k_gain", j)] = dgk
            dproj = jnp.concatenate([dq, dk, bf(dv), dgate], axis=1)
            dw = mm_tn(f"fox_dwin_{i}", st["hb"], dproj, BF16, False)
            n4 = n_fox_in // N_CHIPS
            gr["w_in"] = jnp.stack([dw[:, s * n4:(s + 1) * n4] for s in range(N_CHIPS)])
            dh = mm_nt(f"fox_dh_{i}", dproj, st["w_in"], F32)
        else:
            dqh, dkh, dv = attn_bwd(f"mla_attn_bwd_{i}", st["qh"], st["kh"], st["kv"], st["o_f32"], do, st["lse"], nh,
                                    MLA_QK_PAD, 0, 0, 1, 2, st["scale"])
            dqpad, dkn, dkr, dgq, dgk = rowwise_bwd(
                f"mla_pre2_bwd_{i}", _mla_pre2_fn, st["pre2_rows"], [st["gq"], st["gk"]],
                [(dqh, MLA_QK_PAD, lambda h: h), (dkh, MLA_QK_PAD, lambda h: h)],
                [(nh * MLA_QK_PAD, MLA_QK_PAD, lambda h: h, BF16, False), (W, MLA_NOPE, lambda h: h, BF16, False),
                 (LANE, LANE, _c0, F32, True)], tr, nh, n_diff=3)
            small[("mla_q_gain", j)] = dgq[:, :MLA_QK]
            small[("mla_k_gain", j)] = dgk[:, :MLA_QK]
            dqfull = dqpad.reshape(S, nh, MLA_QK_PAD)[:, :, :MLA_QK].reshape(S, nh * MLA_QK)
            dkv = jnp.stack([dkn.reshape(S, nh, MLA_NOPE), bf(dv).reshape(S, nh, MLA_V)], axis=2).reshape(S, nh * (MLA_NOPE + MLA_V))
            gr["w_uq"] = mm_tn(f"mla_dwuq_{i}", st["c_q"], dqfull, BF16, True)
            dc_q = mm_nt(f"mla_dcq_{i}", dqfull, L["w_uq"], F32)
            gr["w_ukv"] = mm_tn(f"mla_dwukv_{i}", st["c_kv"], dkv, BF16, True)
            dc_kv = mm_nt(f"mla_dckv_{i}", dkv, L["w_ukv"], F32)
            d1, d2, d3, dqn, dkvn = rowwise_bwd(
                f"mla_pre1_bwd_{i}", st["pre1_fn"], st["pre1_rows"], [q_norm_full, kv_norm_full],
                [(dc_q, q_rank, _c0), (dc_kv, kv_rank, _c0), (dkr, LANE, _c0)],
                [(q_rank, q_rank, _c0, BF16, False), (kv_rank, kv_rank, _c0, BF16, False), (LANE, LANE, _c0, BF16, False)],
                tr, n_diff=3)
            small[("mla_q_norm", j)] = dqn
            small[("mla_kv_norm", j)] = dkvn
            ddown = jnp.concatenate([d1, d2, d3], axis=1)
            dw = mm_tn(f"mla_dwin_{i}", st["hb"], ddown, BF16, False)
            gr["w_in"] = stack_rows(dw[:, :n_down])
            dh = mm_nt(f"mla_dh_{i}", ddown, st["w_in"], F32)
        dx, dxb, dg1 = rmsnorm_bwd(f"norm1_bwd_{i}", st["x"], st["g1"], dh, dx1)
        small[("mix_norm", i)] = dg1
        dep = sums_add(i, "mix", gr, dx)

    def holders(kind_of):
        return [i for i in range(depth) if kind_of is None or i % N_MIXERS == kind_of]

    tensors = {
        "sb_w_in": (sb_w_in, m_sb_w_in, v_sb_w_in, "w_in", holders(0)),
        "sb_w_out": (sb_w_out, m_sb_w_out, v_sb_w_out, "w_out", holders(0)),
        "fox_w_in": (fox_w_in, m_fox_w_in, v_fox_w_in, "w_in", holders(1)),
        "fox_w_out": (fox_w_out, m_fox_w_out, v_fox_w_out, "w_out", holders(1)),
        "mla_w_in": (mla_w_in, m_mla_w_in, v_mla_w_in, "w_in", holders(2)),
        "mla_w_uq": (mla_w_uq, m_mla_w_uq, v_mla_w_uq, "w_uq", holders(2)),
        "mla_w_ukv": (mla_w_ukv, m_mla_w_ukv, v_mla_w_ukv, "w_ukv", holders(2)),
        "mla_w_out": (mla_w_out, m_mla_w_out, v_mla_w_out, "w_out", holders(2)),
        "mlp_w1": (mlp_w1, m_mlp_w1, v_mlp_w1, "w1", holders(None)),
        "mlp_w2": (mlp_w2, m_mlp_w2, v_mlp_w2, "w2", holders(None)),
    }
    updated = {n: None for n in tensors}

    def update_ready(dep):
        outs = []
        for n, (w, m, v, key, held) in tensors.items():
            for l in reversed(range(len(held))):
                if (n, l) not in applied and key in full[held[l]]:
                    mine, other = full[held[l]][key]
                    updated[n] = adamw_big(f"adamw_{n}_{l}", w, m, v, mine, other, c_arr, l, updated[n], dep)
                    applied.add((n, l))
                    outs.append(updated[n][3][l, :8, :LANE])
        return outs

    applied = set()
    after = dx
    while sums:
        outs = update_ready(dep)
        after = outs if outs else after
        dep = sums_advance(after)

    last = update_ready(dep)

    keys = list(small)
    flat = jnp.concatenate([small[k].reshape(-1) for k in keys])
    rows_g = -(-flat.shape[0] // (8 * LANE)) * 8
    flat = jnp.pad(flat, (0, rows_g * LANE - flat.shape[0]))
    tail = last or (after if isinstance(after, list) else [after])
    summed = all_sum_small("sum_small", flat.reshape(rows_g, LANE), dep=tail[0]).reshape(-1)
    sg, off = {}, 0
    for k in keys:
        n = small[k].size
        sg[k] = summed[off:off + n].reshape(small[k].shape)
        off += n

    def update_big(name):
        return list(updated[name])

    def update_small(name, w, m, v, g):
        return [g] + list(adamw(f"adamw_{name}", w, g, m, v))

    def small_rows(name, count):
        return jnp.concatenate([sg[(name, l)] for l in range(count)], axis=0)

    def my_part(g, n):
        return lax.dynamic_slice(g, (0, chip * n), (g.shape[0], n))

    res = {
        "mix_norm": update_small("mix_norm", mix_norm, m_mix_norm, v_mix_norm, small_rows("mix_norm", depth)),
        "mlp_norm": update_small("mlp_norm", mlp_norm, m_mlp_norm, v_mlp_norm, small_rows("mlp_norm", depth)),
        "sb_w_in": update_big("sb_w_in"),
        "sb_w_out": update_big("sb_w_out"),
        "fox_w_in": update_big("fox_w_in"),
        "fox_b_f": update_small("fox_b_f", fox_b_f, m_fox_b_f, v_fox_b_f, small_rows("fox_b_f", fox_b_f.shape[0])),
        "fox_q_gain": update_small("fox_q_gain", fox_q_gain, m_fox_q_gain, v_fox_q_gain, small_rows("fox_q_gain", fox_q_gain.shape[0])),
        "fox_k_gain": update_small("fox_k_gain", fox_k_gain, m_fox_k_gain, v_fox_k_gain, small_rows("fox_k_gain", fox_k_gain.shape[0])),
        "fox_w_out": update_big("fox_w_out"),
        "mla_w_in": update_big("mla_w_in"),
        "mla_q_norm": update_small("mla_q_norm", mla_q_norm, m_mla_q_norm, v_mla_q_norm,
                                   my_part(small_rows("mla_q_norm", mla_q_norm.shape[0]), mla_q_norm.shape[1])),
        "mla_kv_norm": update_small("mla_kv_norm", mla_kv_norm, m_mla_kv_norm, v_mla_kv_norm,
                                    my_part(small_rows("mla_kv_norm", mla_kv_norm.shape[0]), mla_kv_norm.shape[1])),
        "mla_w_uq": update_big("mla_w_uq"),
        "mla_w_ukv": update_big("mla_w_ukv"),
        "mla_q_gain": update_small("mla_q_gain", mla_q_gain, m_mla_q_gain, v_mla_q_gain, small_rows("mla_q_gain", mla_q_gain.shape[0])),
        "mla_k_gain": update_small("mla_k_gain", mla_k_gain, m_mla_k_gain, v_mla_k_gain, small_rows("mla_k_gain", mla_k_gain.shape[0])),
        "mla_w_out": update_big("mla_w_out"),
        "mlp_w1": update_big("mlp_w1"),
        "mlp_w2": update_big("mlp_w2"),
    }
    order = ["mix_norm", "mlp_norm", "sb_w_in", "sb_w_out", "fox_w_in", "fox_b_f", "fox_q_gain", "fox_k_gain", "fox_w_out",
             "mla_w_in", "mla_q_norm", "mla_kv_norm", "mla_w_uq", "mla_w_ukv", "mla_q_gain", "mla_k_gain", "mla_w_out",
             "mlp_w1", "mlp_w2"]
    outs = [loss, dx.reshape(x.shape)]
    for k in range(4):
        outs += [res[n][k] for n in order]
    return tuple(outs)
```

```python
import functools
import math

import numpy as np
import jax
import jax.numpy as jnp
from jax import lax
from jax.experimental import pallas as pl
from jax.experimental.pallas import tpu as pltpu

F32 = jnp.float32
BF16 = jnp.bfloat16
MESH = pl.DeviceIdType.MESH

EPS = 1e-6
HEAD_DIM = 128
MLA_NOPE = 128
MLA_ROPE = 64
MLA_V = 128
MLA_QK = MLA_NOPE + MLA_ROPE
MLA_QK_PAD = 256
ROPE_THETA = 10000.0
N_MIXERS = 3
ADAM_LR = 0.001
ADAM_B1 = 0.9
ADAM_B2 = 0.999
ADAM_EPS = 1e-08
ADAM_WD = 0.01
ADAM_STEP = 10

LANE = 128
N_CHIPS = 4
GATHER_AHEAD = 3
VMEM_LIMIT = 48 * 1024 * 1024
NEG = -1e30


def _cp(sem):
    return pltpu.CompilerParams(dimension_semantics=sem, vmem_limit_bytes=VMEM_LIMIT)


def _pick(dim, cap):
    best = None
    b = LANE
    while b <= min(dim, cap):
        if dim % b == 0:
            best = b
        b += LANE
    return best if best is not None else dim


def _pick_rows(dim, cap):
    b = min(dim, cap)
    while dim % b:
        b -= 8
    return b


def _matmul(name, a, b, a_blk, a_map, b_blk, b_map, dn, grid, acc_shape, outs, extras=(), epilogue=None, dep=None):
    nk = grid[2]
    n_ex, n_out = len(extras), len(outs)
    n_dep = 0 if dep is None else 1

    def body(*refs):
        a_ref, b_ref = refs[0], refs[1]
        ex_refs = refs[2:2 + n_ex]
        out_refs = refs[2 + n_ex + n_dep:2 + n_ex + n_dep + n_out]
        acc = refs[-1]
        k = pl.program_id(2)

        @pl.when(k == 0)
        def _():
            acc[...] = jnp.zeros_like(acc)

        acc[...] += lax.dot_general(a_ref[...], b_ref[...], dn, preferred_element_type=F32)

        @pl.when(k == nk - 1)
        def _():
            res = acc[...]
            vals = epilogue(res, *[e[...] for e in ex_refs]) if epilogue is not None else (res,)
            for o, v in zip(out_refs, vals):
                o[...] = v.astype(o.dtype)

    in_specs = [pl.BlockSpec(a_blk, a_map), pl.BlockSpec(b_blk, b_map)]
    in_specs += [pl.BlockSpec(blk, lambda i, j, k, m=m: m(i, j)) for (_, blk, m) in extras]
    in_specs += [pl.BlockSpec(memory_space=pl.ANY)] * n_dep
    out_specs = [pl.BlockSpec(blk, lambda i, j, k, m=m: m(i, j)) for (_, _, blk, m) in outs]
    out_shape = [jax.ShapeDtypeStruct(s, d) for (s, d, _, _) in outs]
    res = pl.pallas_call(
        body, name=name, grid=grid, in_specs=in_specs, out_specs=out_specs, out_shape=out_shape,
        scratch_shapes=[pltpu.VMEM(acc_shape, F32)],
        compiler_params=_cp(("parallel", "parallel", "arbitrary")),
    )(a, b, *[e[0] for e in extras], *([] if dep is None else [dep]))
    return res


BM, BN, BK = 1024, 1024, 2048


def mm_nn(name, a, w, out_dtype, epilogue=None, extras=(), n_out=1, out_dtypes=None):
    M, K = a.shape
    stacked = w.ndim == 3
    n4 = w.shape[-1]
    N = n4 * (N_CHIPS if stacked else 1)
    bm, bn, bk = _pick_rows(M, BM), _pick(n4, BN), _pick(K, BK)
    nb = n4 // bn
    if stacked:
        b_blk, b_map = (None, bk, bn), (lambda i, j, k: (j // nb, k, j % nb))
    else:
        b_blk, b_map = (bk, bn), (lambda i, j, k: (k, j))
    dts = out_dtypes if out_dtypes is not None else [out_dtype] * n_out
    outs = [((M, N), d, (bm, bn), lambda i, j: (i, j)) for d in dts]
    exs = [(e, (bm, bn), lambda i, j: (i, j)) for e in extras]
    return _matmul(name, a, w, (bm, bk), lambda i, j, k: (i, k), b_blk, b_map,
                   (((1,), (0,)), ((), ())), (M // bm, N // bn, K // bk), (bm, bn), outs, exs, epilogue)


def mm_nt(name, dy, w, out_dtype, epilogue=None, extras=(), dep=None):
    M, N = dy.shape
    stacked = w.ndim == 3
    K, n4 = w.shape[-2], w.shape[-1]
    bm, bn, bk = _pick_rows(M, BM), _pick(K, BN), _pick(n4, BK)
    nb = n4 // bk
    if stacked:
        b_blk, b_map = (None, bn, bk), (lambda i, j, k: (k // nb, j, k % nb))
    else:
        b_blk, b_map = (bn, bk), (lambda i, j, k: (j, k))
    outs = [((M, K), out_dtype, (bm, bn), lambda i, j: (i, j))]
    exs = [(e, (bm, bn), lambda i, j: (i, j)) for e in extras]
    return _matmul(name, dy, w, (bm, bk), lambda i, j, k: (i, k), b_blk, b_map,
                   (((1,), (1,)), ((), ())), (M // bm, K // bn, N // bk), (bm, bn), outs, exs, epilogue, dep)[0]


def mm_tn(name, a, dy, out_dtype, stacked, dep=None):
    M, K = a.shape
    N = dy.shape[1]
    n4 = N // N_CHIPS if stacked else N
    bm, bn, bk = _pick(K, BM), _pick(n4, BN), _pick_rows(M, BK)
    nb = n4 // bn
    if stacked:
        outs = [((N_CHIPS, K, n4), out_dtype, (None, bm, bn), lambda i, j: (j // nb, i, j % nb))]
    else:
        outs = [((K, N), out_dtype, (bm, bn), lambda i, j: (i, j))]
    return _matmul(name, a, dy, (bk, bm), lambda i, j, k: (k, i), (bk, bn), lambda i, j, k: (k, j),
                   (((0,), (0,)), ((), ())), (K // bm, N // bn, M // bk), (bm, bn), outs, dep=dep)[0]


def _split3(x):
    hi = x.astype(BF16)
    r = x - hi.astype(F32)
    mid = r.astype(BF16)
    lo = (r - mid.astype(F32)).astype(BF16)
    return hi, mid, lo


def _exact_dot(x, m, n):
    out = None
    for p in _split3(x)[:n]:
        d = jnp.dot(p, m, preferred_element_type=F32)
        out = d if out is None else out + d
    return out


def _row_spec(tr, width, cmap):
    return pl.BlockSpec((tr, width), lambda i, h: (i, cmap(h)))


def _full_spec(p):
    return pl.BlockSpec(p.shape, lambda i, h: (0,) * p.ndim)


def rowwise_fwd(name, fn, rows, params, outs, tr, nh=1):
    S = rows[0][0].shape[0]
    nr, npar = len(rows), len(params)

    def body(*refs):
        vals = fn(*[r[...].astype(F32) for r in refs[:nr]], *[p[...] for p in refs[nr:nr + npar]])
        for o, v in zip(refs[nr + npar:], vals):
            o[...] = v.astype(o.dtype)

    return pl.pallas_call(
        body, name=name, grid=(S // tr, nh),
        in_specs=[_row_spec(tr, w, cm) for (_, w, cm) in rows] + [_full_spec(p) for p in params],
        out_specs=[_row_spec(tr, w, cm) for (_, w, cm, _) in outs],
        out_shape=[jax.ShapeDtypeStruct((S, c), d) for (c, _, _, d) in outs],
        compiler_params=_cp(("parallel", "arbitrary")),
    )(*[r[0] for r in rows], *params)


def rowwise_bwd(name, fn, rows, params, cts, grads, tr, nh=1, n_diff=None, add=None):
    S = rows[0][0].shape[0]
    nr, npar, nct = len(rows), len(params), len(cts)
    n_diff = nr if n_diff is None else n_diff
    n_add = 1 if add is not None else 0

    def body(*refs):
        row_refs = refs[:nr]
        par_refs = refs[nr:nr + npar]
        ct_refs = refs[nr + npar:nr + npar + nct]
        add_refs = refs[nr + npar + nct:nr + npar + nct + n_add]
        o = nr + npar + nct + n_add
        g_refs = refs[o:o + n_diff]
        cp_refs = refs[o + n_diff:o + n_diff + n_add]
        pg_refs = refs[o + n_diff + n_add:]
        i, h = pl.program_id(0), pl.program_id(1)
        rv = [r[...].astype(F32) for r in row_refs]
        pv = [p[...] for p in par_refs]
        aux = rv[n_diff:]

        def f(*dp):
            return fn(*dp[:n_diff], *aux, *dp[n_diff:])

        _, vjp = jax.vjp(f, *rv[:n_diff], *pv)
        gs = vjp(tuple(c[...].astype(F32) for c in ct_refs))
        for n, (g_ref, (_, _, _, _, over)) in enumerate(zip(g_refs, grads)):
            g = gs[n]
            if n == 0 and add is not None:
                g = g + add_refs[0][...]
                cp_refs[0][...] = g.astype(BF16)
            if over:
                @pl.when(h == 0)
                def _(g_ref=g_ref):
                    g_ref[...] = jnp.zeros_like(g_ref)
                g_ref[...] += g.astype(g_ref.dtype)
            else:
                g_ref[...] = g.astype(g_ref.dtype)
        for pg_ref, g in zip(pg_refs, gs[n_diff:]):
            @pl.when((i == 0) & (h == 0))
            def _(pg_ref=pg_ref):
                pg_ref[...] = jnp.zeros_like(pg_ref)
            pg_ref[...] += g

    in_specs = [_row_spec(tr, w, cm) for (_, w, cm) in rows] + [_full_spec(p) for p in params]
    in_specs += [_row_spec(tr, w, cm) for (_, w, cm) in cts]
    operands = [r[0] for r in rows] + list(params) + [c[0] for c in cts]
    out_specs = [_row_spec(tr, w, cm) for (_, w, cm, _, _) in grads]
    out_shape = [jax.ShapeDtypeStruct((S, c), d) for (c, _, _, d, _) in grads]
    if add is not None:
        in_specs.append(_row_spec(tr, add[1], add[2]))
        operands.append(add[0])
        out_specs.append(_row_spec(tr, add[1], add[2]))
        out_shape.append(jax.ShapeDtypeStruct(add[0].shape, BF16))
    out_specs += [_full_spec(p) for p in params]
    out_shape += [jax.ShapeDtypeStruct(p.shape, F32) for p in params]
    return pl.pallas_call(
        body, name=name, grid=(S // tr, nh), in_specs=in_specs, out_specs=out_specs, out_shape=out_shape,
        compiler_params=_cp(("arbitrary", "arbitrary")),
    )(*operands)


def _c0(h):
    return 0


def _rms(x, g, n):
    return x * lax.rsqrt(jnp.sum(x * x, axis=-1, keepdims=True) * (1.0 / n) + EPS) * g


def _rmsnorm_fn(d):
    def fn(x, g):
        return (_rms(x, g, d),)
    return fn


def rmsnorm_fwd(name, x, g):
    S, D = x.shape
    return rowwise_fwd(name, _rmsnorm_fn(D), [(x, D, _c0)], [g], [(D, D, _c0, BF16)], _pick_rows(S, 256))[0]


def rmsnorm_bwd(name, x, g, dh, dres):
    S, D = x.shape
    return rowwise_bwd(name, _rmsnorm_fn(D), [(x, D, _c0)], [g], [(dh, D, _c0)], [(D, D, _c0, F32, False)],
                       _pick_rows(S, 256), add=(dres, D, _c0))


def loss_head(y, t):
    S, D = y.shape
    tr = _pick_rows(S, 256)

    def body(y_ref, t_ref, l_ref, d_ref, db_ref):
        @pl.when(pl.program_id(0) == 0)
        def _():
            l_ref[...] = jnp.zeros_like(l_ref)
        e = y_ref[...] - t_ref[...]
        l_ref[...] += 0.5 * jnp.sum(jnp.sum(e * e, axis=1, keepdims=True), axis=0, keepdims=True) * (1.0 / D)
        d = e * (1.0 / D)
        d_ref[...] = d
        db_ref[...] = d.astype(BF16)

    row = pl.BlockSpec((tr, D), lambda i: (i, 0))
    return pl.pallas_call(
        body, name="loss_head", grid=(S // tr,), in_specs=[row, row],
        out_specs=[pl.BlockSpec((1, 1), lambda i: (0, 0)), row, row],
        out_shape=[jax.ShapeDtypeStruct((1, 1), F32), jax.ShapeDtypeStruct((S, D), F32), jax.ShapeDtypeStruct((S, D), BF16)],
        compiler_params=_cp(("arbitrary",)),
    )(y, t)


def _dot_nt(a, b):
    return lax.dot_general(a, b, (((1,), (1,)), ((), ())), preferred_element_type=F32)


def _dot_tn(a, b):
    return lax.dot_general(a, b, (((0,), (0,)), ((), ())), preferred_element_type=F32)


AT_TQ, AT_TK = 512, 256


def attn_fwd(name, q, k, v, nh, dq, qoff, koff, voff, vstep, scale, cf_col=None, cf_row=None):
    S = q.shape[0]
    tq = _pick_rows(S, AT_TQ)
    tk = _pick_rows(S, AT_TK)
    nq = S // tq
    bias = cf_col is not None

    def body(*refs):
        if bias:
            q_ref, k_ref, v_ref, cfc_ref, cfr_ref, o_ref, of_ref, lse_ref = refs
        else:
            q_ref, k_ref, v_ref, o_ref, of_ref, lse_ref = refs
        i = pl.program_id(1)
        qv = q_ref[...].astype(BF16)
        row = i * tq + lax.broadcasted_iota(jnp.int32, (tq, tk), 0)
        coli = lax.broadcasted_iota(jnp.int32, (tq, tk), 1)
        nkb = ((i + 1) * tq + tk - 1) // tk

        def logits(j):
            return _dot_nt(qv, k_ref[pl.ds(pl.multiple_of(j * tk, tk), tk), :].astype(BF16)) * scale

        def step(j, carry):
            s, m, l, acc, acc_lo = carry
            s_next = logits(jnp.minimum(j + 1, nkb - 1))
            vs = v_ref[pl.ds(pl.multiple_of(j * tk, tk), tk), :].astype(BF16)
            if bias:
                s = s + cfc_ref[0] - cfr_ref[0, pl.ds(j, 1), :]
            s = jnp.where(j * tk + coli <= row, s, NEG)
            m_new = jnp.maximum(m, jnp.max(s, axis=1, keepdims=True))
            alpha = jnp.exp(m - m_new)
            p = jnp.exp(s - m_new)
            pb = p.astype(BF16)
            l = alpha * l + jnp.sum(p, axis=1, keepdims=True)
            acc = alpha * acc + jnp.dot(pb, vs, preferred_element_type=F32)
            acc_lo = alpha * acc_lo + jnp.dot((p - pb.astype(F32)).astype(BF16), vs, preferred_element_type=F32)
            return s_next, m_new, l, acc, acc_lo

        z = jnp.zeros((tq, HEAD_DIM), F32)
        _, m, l, acc, acc_lo = lax.fori_loop(
            0, nkb, step, (logits(0), jnp.full((tq, 1), NEG, F32), jnp.zeros((tq, 1), F32), z, z))
        o_ref[...] = (acc / l).astype(o_ref.dtype)
        of_ref[...] = (acc + acc_lo) / l
        lse_ref[0] = m + jnp.log(l)

    in_specs = [pl.BlockSpec((tq, dq), lambda h, i: (i, qoff + h)),
                pl.BlockSpec((S, dq), lambda h, i: (0, koff + h)),
                pl.BlockSpec((S, HEAD_DIM), lambda h, i: (0, voff + vstep * h))]
    operands = [q, k, v]
    if bias:
        in_specs += [pl.BlockSpec((1, tq, 1), lambda h, i: (h, i, 0)), pl.BlockSpec((1, S // tk, tk), lambda h, i: (h, 0, 0))]
        operands += [cf_col, cf_row]
    return pl.pallas_call(
        body, name=name, grid=(nh, nq), in_specs=in_specs,
        out_specs=[pl.BlockSpec((tq, HEAD_DIM), lambda h, i: (i, h)), pl.BlockSpec((tq, HEAD_DIM), lambda h, i: (i, h)),
                   pl.BlockSpec((1, tq, 1), lambda h, i: (h, i, 0))],
        out_shape=[jax.ShapeDtypeStruct((S, nh * HEAD_DIM), BF16), jax.ShapeDtypeStruct((S, nh * HEAD_DIM), F32),
                   jax.ShapeDtypeStruct((nh, S, 1), F32)],
        compiler_params=_cp(("parallel", "arbitrary")),
    )(*operands)


def attn_bwd(name, q, k, v, o, do, lse, nh, dq, qoff, koff, voff, vstep, scale, cf_col=None, cf_row=None):
    S = q.shape[0]
    tq = _pick_rows(S, AT_TQ)
    tk = _pick_rows(S, AT_TK)
    nq = S // tq
    bias = cf_col is not None

    def body(*refs):
        if bias:
            q_ref, k_ref, v_ref, o_ref, do_ref, lse_ref, cfc_ref, cfr_ref, dq_ref, dk_ref, dv_ref, dcf_ref = refs
        else:
            q_ref, k_ref, v_ref, o_ref, do_ref, lse_ref, dq_ref, dk_ref, dv_ref = refs
        dk_ref[...] = jnp.zeros_like(dk_ref)
        dv_ref[...] = jnp.zeros_like(dv_ref)
        if bias:
            dcf_ref[...] = jnp.zeros_like(dcf_ref)
        rowi = lax.broadcasted_iota(jnp.int32, (tq, tk), 0)
        coli = lax.broadcasted_iota(jnp.int32, (tq, tk), 1)

        def outer(i, _):
            roff = pl.multiple_of(i * tq, tq)
            qv = q_ref[pl.ds(roff, tq), :].astype(BF16)
            dov = do_ref[pl.ds(roff, tq), :]
            delta = jnp.sum(dov.astype(F32) * o_ref[pl.ds(roff, tq), :].astype(F32), axis=1, keepdims=True)
            lse = lse_ref[0, pl.ds(roff, tq), :]
            if bias:
                cq = cfc_ref[0, pl.ds(roff, tq), :]

            nkb = ((i + 1) * tq + tk - 1) // tk

            def products(j):
                off = pl.multiple_of(j * tk, tk)
                return (_dot_nt(qv, k_ref[pl.ds(off, tk), :].astype(BF16)) * scale,
                        _dot_nt(dov, v_ref[pl.ds(off, tk), :].astype(BF16)))

            def inner(j, carry):
                s, dp, dq_acc = carry
                s_next, dp_next = products(jnp.minimum(j + 1, nkb - 1))
                off = pl.multiple_of(j * tk, tk)
                ks = k_ref[pl.ds(off, tk), :].astype(BF16)
                if bias:
                    s = s + cq - cfr_ref[0, pl.ds(j, 1), :]
                p = jnp.where(j * tk + coli <= i * tq + rowi, jnp.exp(s - lse), 0.0)
                ds = p * (dp - delta)
                if bias:
                    dcf_ref[0, pl.ds(j, 1), :] -= jnp.sum(ds, axis=0, keepdims=True)
                dsb = (ds * scale).astype(BF16)
                dv_ref[pl.ds(off, tk), :] += _dot_tn(p.astype(BF16), dov)
                dk_ref[pl.ds(off, tk), :] += _dot_tn(dsb, qv)
                return s_next, dp_next, dq_acc + jnp.dot(dsb, ks, preferred_element_type=F32)

            dq_ref[pl.ds(roff, tq), :] = lax.fori_loop(0, nkb, inner, products(0) + (jnp.zeros((tq, dq), F32),))[2]
            return 0

        lax.fori_loop(0, nq, outer, 0)

    in_specs = [pl.BlockSpec((S, dq), lambda h: (0, qoff + h)),
                pl.BlockSpec((S, dq), lambda h: (0, koff + h)),
                pl.BlockSpec((S, HEAD_DIM), lambda h: (0, voff + vstep * h)),
                pl.BlockSpec((S, HEAD_DIM), lambda h: (0, h)),
                pl.BlockSpec((S, HEAD_DIM), lambda h: (0, h)),
                pl.BlockSpec((1, S, 1), lambda h: (h, 0, 0))]
    operands = [q, k, v, o, do, lse]
    out_specs = [pl.BlockSpec((S, dq), lambda h: (0, h)), pl.BlockSpec((S, dq), lambda h: (0, h)),
                 pl.BlockSpec((S, HEAD_DIM), lambda h: (0, h))]
    out_shape = [jax.ShapeDtypeStruct((S, nh * dq), F32), jax.ShapeDtypeStruct((S, nh * dq), F32),
                 jax.ShapeDtypeStruct((S, nh * HEAD_DIM), F32)]
    if bias:
        in_specs += [pl.BlockSpec((1, S, 1), lambda h: (h, 0, 0)), pl.BlockSpec((1, S // tk, tk), lambda h: (h, 0, 0))]
        operands += [cf_col, cf_row]
        out_specs.append(pl.BlockSpec((1, S // tk, tk), lambda h: (h, 0, 0)))
        out_shape.append(jax.ShapeDtypeStruct((nh, S // tk, tk), F32))
    return pl.pallas_call(
        body, name=name, grid=(nh,), in_specs=in_specs, out_specs=out_specs, out_shape=out_shape,
        compiler_params=_cp(("parallel",)),
    )(*operands)


SB_TQ, SB_TK = 512, 256


def _sb_block(z, strict, later_c, tri_after):
    lsn = jnp.minimum(-z, 0.0) - jnp.log(1.0 + jnp.exp(-jnp.abs(z)))
    lsp = lsn + z
    L = lsn if strict is None else jnp.where(strict, lsn, 0.0)
    later = _exact_dot(L, tri_after, 2) + later_c
    a = jnp.exp(lsp + later)
    return lsp, lsn, L, a if strict is None else jnp.where(strict, a, 0.0)


def _tri(n, after_strict):
    r = lax.broadcasted_iota(jnp.int32, (n, n), 0)
    c = lax.broadcasted_iota(jnp.int32, (n, n), 1)
    return jnp.where(r > c if after_strict else r >= c, 1.0, 0.0).astype(BF16)


def sb_fwd(name, qkv, nh):
    S = qkv.shape[0]
    tq = _pick_rows(S, SB_TQ)
    tk = _pick_rows(S, SB_TK)
    nq = S // tq
    scale = 1.0 / math.sqrt(HEAD_DIM)

    def body(q_ref, k_ref, v_ref, o_ref, of_ref):
        i = pl.program_id(1)
        qv = q_ref[...]
        tri = _tri(tk, True)
        row = i * tq + lax.broadcasted_iota(jnp.int32, (tq, tk), 0)
        coli = lax.broadcasted_iota(jnp.int32, (tq, tk), 1)

        def logits(j):
            return _dot_nt(qv, k_ref[pl.ds(pl.multiple_of(j * tk, tk), tk), :]) * scale

        nkb = ((i + 1) * tq + tk - 1) // tk

        def step(jj, carry, masked=True):
            z, later_c, acc, acc_lo = carry
            j = nkb - 1 - jj
            z_next = logits(jnp.maximum(j - 1, 0))
            vb = v_ref[pl.ds(pl.multiple_of(j * tk, tk), tk), :]
            _, _, L, a = _sb_block(z, (j * tk + coli < row) if masked else None, later_c, tri)
            ab = a.astype(BF16)
            acc = acc + jnp.dot(ab, vb, preferred_element_type=F32)
            acc_lo = acc_lo + jnp.dot((a - ab.astype(F32)).astype(BF16), vb, preferred_element_type=F32)
            return z_next, later_c + jnp.sum(L, axis=1, keepdims=True), acc, acc_lo

        zero = jnp.zeros((tq, HEAD_DIM), F32)
        _, _, acc, acc_lo = lax.fori_loop(0, nkb, step, (logits(nkb - 1), jnp.zeros((tq, 1), F32), zero, zero))
        o_ref[...] = acc.astype(o_ref.dtype)
        of_ref[...] = acc + acc_lo

    blk = pl.BlockSpec((tq, HEAD_DIM), lambda h, i: (i, h))
    return pl.pallas_call(
        body, name=name, grid=(nh, nq),
        in_specs=[blk,
                  pl.BlockSpec((S, HEAD_DIM), lambda h, i: (0, nh + h)),
                  pl.BlockSpec((S, HEAD_DIM), lambda h, i: (0, 2 * nh + h))],
        out_specs=[blk, blk],
        out_shape=[jax.ShapeDtypeStruct((S, nh * HEAD_DIM), BF16), jax.ShapeDtypeStruct((S, nh * HEAD_DIM), F32)],
        compiler_params=_cp(("parallel", "arbitrary")),
    )(qkv, qkv, qkv)


def sb_bwd(name, qkv, o, do, nh):
    S = qkv.shape[0]
    tq = _pick_rows(S, SB_TQ)
    tk = _pick_rows(S, SB_TK)
    nq = S // tq
    scale = 1.0 / math.sqrt(HEAD_DIM)

    def body(q_ref, k_ref, v_ref, o_ref, do_ref, dq_ref, dk_ref, dv_ref, dk_acc, dv_acc):
        dk_acc[...] = jnp.zeros_like(dk_acc)
        dv_acc[...] = jnp.zeros_like(dv_acc)
        tri = _tri(tk, True)
        tri_inc = _tri(tk, False)
        rowi = lax.broadcasted_iota(jnp.int32, (tq, tk), 0)
        coli = lax.broadcasted_iota(jnp.int32, (tq, tk), 1)

        def outer(i, _):
            roff = pl.multiple_of(i * tq, tq)
            qv = q_ref[pl.ds(roff, tq), :]
            dov = do_ref[pl.ds(roff, tq), :]
            dtot = jnp.sum(dov.astype(F32) * o_ref[pl.ds(roff, tq), :].astype(F32), axis=1, keepdims=True)
            row = i * tq + rowi

            def products(j):
                off = pl.multiple_of(j * tk, tk)
                return _dot_nt(qv, k_ref[pl.ds(off, tk), :]) * scale, _dot_nt(dov, v_ref[pl.ds(off, tk), :])

            nkb = ((i + 1) * tq + tk - 1) // tk

            def inner(jj, carry, masked=True):
                z, da, later_c, suf_c, dq_acc = carry
                j = nkb - 1 - jj
                z_next, da_next = products(jnp.maximum(j - 1, 0))
                off = pl.multiple_of(j * tk, tk)
                kb = k_ref[pl.ds(off, tk), :]
                strict = (j * tk + coli < row) if masked else None
                lsp, lsn, L, a = _sb_block(z, strict, later_c, tri)
                dl = da * a
                before = dtot - (suf_c + _exact_dot(dl, tri_inc, 3))
                dz = (dl * jnp.exp(lsn) - jnp.exp(lsp) * before) * scale
                if masked:
                    dz = jnp.where(strict, dz, 0.0)
                dzb = dz.astype(BF16)
                dv_acc[pl.ds(off, tk), :] += _dot_tn(a.astype(BF16), dov)
                dk_acc[pl.ds(off, tk), :] += _dot_tn(dzb, qv)
                return (z_next, da_next, later_c + jnp.sum(L, axis=1, keepdims=True), suf_c + jnp.sum(dl, axis=1, keepdims=True),
                        dq_acc + jnp.dot(dzb, kb, preferred_element_type=F32))

            z1 = jnp.zeros((tq, 1), F32)
            res = lax.fori_loop(0, nkb, inner, products(nkb - 1) + (z1, z1, jnp.zeros((tq, HEAD_DIM), F32)))
            dq_ref[pl.ds(roff, tq), :] = res[4].astype(dq_ref.dtype)
            return 0

        lax.fori_loop(0, nq, outer, 0)
        dk_ref[...] = dk_acc[...].astype(dk_ref.dtype)
        dv_ref[...] = dv_acc[...].astype(dv_ref.dtype)

    def col(off):
        return pl.BlockSpec((S, HEAD_DIM), lambda h: (0, off + h))

    dq, dk, dv = pl.pallas_call(
        body, name=name, grid=(nh,),
        in_specs=[col(0), col(nh), col(2 * nh), col(0), col(0)],
        out_specs=[col(0), col(0), col(0)],
        out_shape=[jax.ShapeDtypeStruct((S, nh * HEAD_DIM), BF16)] * 3,
        scratch_shapes=[pltpu.VMEM((S, HEAD_DIM), F32), pltpu.VMEM((S, HEAD_DIM), F32)],
        compiler_params=_cp(("parallel",)),
    )(qkv, qkv, qkv, o, do)
    return jnp.concatenate([dq, dk, dv], axis=1)


def seq_cumsum(name, x, reverse):
    S, W = x.shape

    tb = _pick_rows(S, 256)

    def body(x_ref, o_ref):
        parts = _split3(x_ref[...])
        c = lax.broadcasted_iota(jnp.int32, (tb, S), 1)
        for b in range(S // tb):
            r = b * tb + lax.broadcasted_iota(jnp.int32, (tb, S), 0)
            t = jnp.where(r <= c if reverse else r >= c, 1.0, 0.0).astype(BF16)
            out = None
            for p in parts:
                d = jnp.dot(t, p, preferred_element_type=F32)
                out = d if out is None else out + d
            o_ref[b * tb:(b + 1) * tb, :] = out

    return pl.pallas_call(body, name=name, out_shape=jax.ShapeDtypeStruct((S, W), F32),
                          compiler_params=pltpu.CompilerParams(vmem_limit_bytes=VMEM_LIMIT))(x)


def _adamw_math(w, g, m, v):
    c1 = 1.0 - ADAM_B1 ** ADAM_STEP
    c2 = 1.0 - ADAM_B2 ** ADAM_STEP
    nm = ADAM_B1 * m + (1.0 - ADAM_B1) * g
    nv = ADAM_B2 * v + (1.0 - ADAM_B2) * (g * g)
    return -ADAM_LR * ((nm / c1) / (jnp.sqrt(nv / c2) + ADAM_EPS) + ADAM_WD * w), nm, nv


def adamw(name, w, g, m, v):
    R, C = g.shape
    tr = _pick_rows(R, 256)

    def body(w_ref, g_ref, m_ref, v_ref, d_ref, nm_ref, nv_ref):
        d_ref[...], nm_ref[...], nv_ref[...] = _adamw_math(w_ref[...], g_ref[...], m_ref[...], v_ref[...])

    row = pl.BlockSpec((tr, C), lambda i: (i, 0))
    return pl.pallas_call(
        body, name=name, grid=(R // tr,), in_specs=[row] * 4, out_specs=[row] * 3,
        out_shape=[jax.ShapeDtypeStruct((R, C), F32)] * 3, compiler_params=_cp(("parallel",)),
    )(w, g, m, v)


def adamw_big(name, w, m, v, mine, other, c_arr, layer, prev, dep=None):
    L, R, C = w.shape
    r2 = R // 2
    tr = _pick_rows(r2, 256)
    nb = r2 // tr

    def body(c_ref, w_ref, m_ref, v_ref, mine_ref, other_ref, *rest):
        g_ref, d_ref, nm_ref, nv_ref = rest[-4:]
        g = jnp.where(pl.program_id(0) == c_ref[0], mine_ref[...], other_ref[...])
        g_ref[...] = g
        d_ref[...], nm_ref[...], nv_ref[...] = _adamw_math(w_ref[...], g, m_ref[...], v_ref[...])

    sel = pl.BlockSpec((None, tr, C), lambda hf, i, c: (layer, hf * nb + i, 0))
    in_specs = [sel, sel, sel,
                pl.BlockSpec((tr, C), lambda hf, i, c: (jnp.where(hf == c[0], i, 0), 0)),
                pl.BlockSpec((tr, C), lambda hf, i, c: (jnp.where(hf == c[0], 0, i), 0))]
    operands = [c_arr, w, m, v, mine, other]
    aliases = {}
    if prev is not None:
        in_specs += [ANY] * 4
        aliases = {len(operands) + k: k for k in range(4)}
        operands += list(prev)
    if dep is not None:
        in_specs += [ANY]
        operands += [dep]
    return pl.pallas_call(
        body, name=name,
        grid_spec=pltpu.PrefetchScalarGridSpec(num_scalar_prefetch=1, grid=(2, nb), in_specs=in_specs, out_specs=[sel] * 4),
        out_shape=[jax.ShapeDtypeStruct((L, R, C), F32)] * 4, input_output_aliases=aliases,
        compiler_params=_cp(("arbitrary", "arbitrary")),
    )(*operands)


ANY = pl.BlockSpec(memory_space=pl.ANY)


def _place():
    x, y, c = lax.axis_index("x"), lax.axis_index("y"), lax.axis_index("c")
    others = [(1 - x, y), (x, 1 - y), (1 - x, 1 - y)]
    return x, y, c, others


def add_half(name, g, other, c_arr):
    _, R, C = g.shape
    r2 = R // 2
    tr = _pick_rows(r2, 512)
    nb = r2 // tr

    def body(c_ref, g_ref, o_ref, out_ref):
        out_ref[...] = (g_ref[...].astype(F32) + o_ref[...].astype(F32)).astype(out_ref.dtype)

    return pl.pallas_call(
        body, name=name,
        grid_spec=pltpu.PrefetchScalarGridSpec(
            num_scalar_prefetch=1, grid=(N_CHIPS, nb),
            in_specs=[pl.BlockSpec((None, tr, C), lambda s, i, c: (s, c[0] * nb + i, 0)),
                      pl.BlockSpec((None, tr, C), lambda s, i, c: (s, i, 0))],
            out_specs=pl.BlockSpec((None, tr, C), lambda s, i, c: (s, i, 0))),
        out_shape=jax.ShapeDtypeStruct((N_CHIPS, r2, C), BF16),
        compiler_params=_cp(("parallel", "parallel")),
    )(c_arr, g, other)


def sum_slabs(name, got, parts, chip_arr):
    _, r2, C = parts.shape
    tr = _pick_rows(r2, 512)

    def body(chip_ref, *refs):
        own_ref, out_ref = refs[N_CHIPS], refs[N_CHIPS + 1]
        acc = None
        for s in range(N_CHIPS):
            v = jnp.where(chip_ref[0] == s, own_ref[...], refs[s][...]).astype(F32)
            acc = v if acc is None else acc + v
        out_ref[...] = acc

    def slab(s):
        return pl.BlockSpec((None, tr, C), lambda i, ch: (jnp.where(ch[0] == s, (s + 1) % N_CHIPS, s), i, 0))

    return pl.pallas_call(
        body, name=name,
        grid_spec=pltpu.PrefetchScalarGridSpec(
            num_scalar_prefetch=1, grid=(r2 // tr,),
            in_specs=[slab(s) for s in range(N_CHIPS)] + [pl.BlockSpec((None, tr, C), lambda i, ch: (ch[0], i, 0))],
            out_specs=pl.BlockSpec((tr, C), lambda i, ch: (i, 0))),
        out_shape=jax.ShapeDtypeStruct((r2, C), F32), compiler_params=_cp(("parallel",)),
    )(chip_arr, got, got, got, got, parts)


HBM = pl.BlockSpec(memory_space=pltpu.HBM)
SEM = pl.BlockSpec(memory_space=pltpu.SEMAPHORE)
EFFECT = pltpu.SideEffectType.DATAFLOW_SIDE_EFFECTING


def _in_hbm(a):
    return pltpu.with_memory_space_constraint(a, pltpu.HBM)


def _chip_copies(kind, shapes, src, land, send, recv, mine):
    x, y, c, others = _place()
    me = 2 * x + y
    cps = []

    def remote(s_ref, d_ref, k, to):
        cps.append(pltpu.make_async_remote_copy(src_ref=s_ref, dst_ref=d_ref, send_sem=send.at[k], recv_sem=recv.at[k],
                                                device_id=to, device_id_type=MESH))

    for t, shape in enumerate(shapes):
        if kind == "swap":
            r2 = shape[1] // 2
            remote(src[t].at[:, pl.ds((1 - c) * r2, r2), :], land[t], t, (x, y, 1 - c))
        elif kind == "whole":
            remote(src[t], land[t], t, (x, y, 1 - c))
        for j, (px, py) in enumerate(others if kind in ("gather", "scatter", "pass") else []):
            slot = me if mine else 2 * px + py
            if kind == "gather":
                r2 = shape[0] // 2
                rows = pl.ds(c * r2, r2)
                remote(src[t].at[rows, :], land[t].at[slot, rows, :], 3 * t + j, (px, py, c))
            elif kind == "scatter":
                remote(src[t].at[2 * px + py], land[t].at[slot], 3 * t + j, (px, py, c))
            else:
                r2 = shape[1] // 2
                rows = src[t].at[2 * px + py, pl.ds((c if mine else 1 - c) * r2, r2), :]
                remote(rows, rows, 3 * t + j, (x, y, 1 - c))
    return cps


def _n_copies(kind, n):
    return n if kind in ("swap", "whole") else 3 * n


def chips_start(name, kind, srcs, land_shapes, after=None):
    n, nl = len(srcs), len(land_shapes)
    shapes = [s.shape for s in srcs]
    n_buf = n + nl
    n_in = n_buf + (0 if after is None else 1)
    n_sem = _n_copies(kind, n)

    def body(*refs):
        src, land = refs[:n], refs[n:n_buf]
        send, recv = refs[n_in], refs[n_in + 1]
        token = refs[-1]
        for cp in _chip_copies(kind, shapes, src, land, send, recv, True):
            cp.start()
        token[...] = jnp.zeros_like(token)

    lands = [lax.empty(s, srcs[0].dtype) for s in land_shapes]
    out = pl.pallas_call(
        body, name=name,
        out_shape=(pltpu.SemaphoreType.DMA((n_sem,)), pltpu.SemaphoreType.DMA((n_sem,)))
        + tuple(pltpu.HBM(s.shape, s.dtype) for s in srcs) + tuple(pltpu.HBM(l.shape, l.dtype) for l in lands)
        + (jax.ShapeDtypeStruct((8, LANE), F32),),
        in_specs=(HBM,) * n_buf + (ANY,) * (n_in - n_buf),
        out_specs=(SEM, SEM) + (HBM,) * n_buf + (pl.BlockSpec(memory_space=pltpu.VMEM),),
        input_output_aliases={k: 2 + k for k in range(n_buf)},
        compiler_params=pltpu.CompilerParams(has_side_effects=EFFECT),
    )(*[_in_hbm(s) for s in srcs], *[_in_hbm(l) for l in lands], *([] if after is None else [after]))
    return (kind, shapes, out[0], out[1], out[2:2 + n], out[2 + n:2 + n_buf]), out[-1][:1, :1]


def chips_wait(name, handle, after):
    kind, shapes, send, recv, srcs, lands = handle
    n, n_buf = len(srcs), len(srcs) + len(lands)
    after = [a for a in (list(after) if isinstance(after, (list, tuple)) else [after]) if a is not None]

    def body(*refs):
        src, land = refs[:n], refs[n:n_buf]
        for cp in _chip_copies(kind, shapes, src, land, refs[n_buf], refs[n_buf + 1], True):
            cp.wait_send()
        for cp in _chip_copies(kind, shapes, src, land, refs[n_buf], refs[n_buf + 1], False):
            cp.wait_recv()

    out = pl.pallas_call(
        body, name=name,
        out_shape=tuple(pltpu.HBM(s.shape, s.dtype) for s in srcs) + tuple(pltpu.HBM(l.shape, l.dtype) for l in lands),
        in_specs=(HBM,) * n_buf + (SEM, SEM) + (ANY,) * len(after), out_specs=(HBM,) * n_buf,
        input_output_aliases={k: k for k in range(n_buf)},
        compiler_params=pltpu.CompilerParams(has_side_effects=EFFECT),
    )(*srcs, *lands, send, recv, *after)
    return list(out[n:]), list(out[:n])


def all_sum_small(name, v, dep=None):
    R = v.shape[0]
    n_dep = 0 if dep is None else 1

    def body(v_ref, *refs):
        out_ref, slots, send, recv = refs[n_dep:]
        x, y, c, _ = _place()
        me = 4 * x + 2 * y + c
        slots[me] = v_ref[...]
        cps = []
        for k in range(1, 8):
            dx, dy, dc = (k >> 2) & 1, (k >> 1) & 1, k & 1
            to = (x ^ dx, y ^ dy, c ^ dc)
            cps.append(pltpu.make_async_remote_copy(
                src_ref=v_ref, dst_ref=slots.at[me], send_sem=send.at[k - 1], recv_sem=recv.at[k - 1],
                device_id=to, device_id_type=MESH))
        for cp in cps:
            cp.start()
        for k in range(1, 8):
            dx, dy, dc = (k >> 2) & 1, (k >> 1) & 1, k & 1
            frm = 4 * (x ^ dx) + 2 * (y ^ dy) + (c ^ dc)
            pltpu.make_async_remote_copy(
                src_ref=v_ref, dst_ref=slots.at[frm], send_sem=send.at[k - 1], recv_sem=recv.at[k - 1],
                device_id=(x, y, c), device_id_type=MESH).wait_recv()
        for cp in cps:
            cp.wait_send()
        acc = slots[0]
        for d in range(1, 8):
            acc = acc + slots[d]
        out_ref[...] = acc

    vm = pl.BlockSpec(memory_space=pltpu.VMEM)
    return pl.pallas_call(
        body, name=name, in_specs=[vm] + [ANY] * n_dep, out_specs=vm, out_shape=jax.ShapeDtypeStruct((R, LANE), F32),
        scratch_shapes=[pltpu.VMEM((8, R, LANE), F32), pltpu.SemaphoreType.DMA((7,)), pltpu.SemaphoreType.DMA((7,))],
    )(v, *([] if dep is None else [dep]))


def _rope_mat(n, lo):
    half = MLA_ROPE // 2
    r = lax.broadcasted_iota(jnp.int32, (n, n), 0)
    c = lax.broadcasted_iota(jnp.int32, (n, n), 1)
    plus = (c >= lo + half) & (c < lo + 2 * half) & (r == c - half)
    minus = (c >= lo) & (c < lo + half) & (r == c + half)
    return (jnp.where(plus, 1.0, 0.0) - jnp.where(minus, 1.0, 0.0)).astype(BF16)


def _rope(v, cos, sin, lo):
    return v * cos + _exact_dot(v, _rope_mat(v.shape[-1], lo), 3) * sin


def rope_tables(pos):
    S = pos.shape[0]
    half = MLA_ROPE // 2
    inv = (np.float32(ROPE_THETA) ** (-np.arange(0, half, dtype=np.float32) * np.float32(2.0 / MLA_ROPE))).astype(np.float32)
    f1 = np.zeros((1, LANE), np.float32)
    f1[0, :MLA_ROPE] = np.tile(inv, 2)
    f2 = np.zeros((1, MLA_QK_PAD), np.float32)
    f2[0, MLA_NOPE:MLA_QK] = np.tile(inv, 2)
    tr = _pick_rows(S, 256)

    def body(p_ref, f1_ref, f2_ref, c1, s1, c2, s2):
        p = p_ref[...].astype(F32)
        a1 = p * f1_ref[...]
        a2 = p * f2_ref[...]
        c1[...] = jnp.cos(a1)
        s1[...] = jnp.sin(a1)
        c2[...] = jnp.cos(a2)
        s2[...] = jnp.sin(a2)

    def row(w):
        return pl.BlockSpec((tr, w), lambda i: (i, 0))

    def full(w):
        return pl.BlockSpec((1, w), lambda i: (0, 0))

    return pl.pallas_call(
        body, name="rope_tables", grid=(S // tr,), in_specs=[row(1), full(LANE), full(MLA_QK_PAD)],
        out_specs=[row(LANE), row(LANE), row(MLA_QK_PAD), row(MLA_QK_PAD)],
        out_shape=[jax.ShapeDtypeStruct((S, LANE), F32)] * 2 + [jax.ShapeDtypeStruct((S, MLA_QK_PAD), F32)] * 2,
        compiler_params=_cp(("parallel",)),
    )(pos, jnp.asarray(f1), jnp.asarray(f2))


def _log_sigmoid(z):
    return jnp.minimum(z, 0.0) - jnp.log(1.0 + jnp.exp(-jnp.abs(z)))


def _fox_qk_fn(q, k, gq, gk):
    return _rms(q, gq, HEAD_DIM), _rms(k, gk, HEAD_DIM)


def _fox_gate_fn(f, b):
    return (_log_sigmoid(f + b),)


def _mla_pre1_fn(q_rank, kv_rank):
    def fn(cq, ckv, kr, cos, sin, qn, kvn):
        return _rms(cq, qn, q_rank), _rms(ckv, kvn, kv_rank), _rope(kr, cos, sin, 0)
    return fn


def _mla_pre2_fn(qb, kn, kr, cos, sin, gq, gk):
    qh = _rms(_rope(qb, cos, sin, MLA_NOPE), gq, MLA_QK)
    kh = _rms(jnp.concatenate([kn, kr], axis=1), gk, MLA_QK)
    return qh, kh


def _pad_cols(a, n):
    return jnp.pad(a, ((0, 0), (0, n - a.shape[1])))


def _relu2(acc):
    r = jnp.maximum(acc, 0.0)
    return r * r, r


def _add(acc, res):
    return (acc + res,)


def _times_2r(acc, r):
    return (acc * (2.0 * r.astype(F32)),)


def kernel(x, positions, mix_norm, mlp_norm, sb_w_in, sb_w_out, fox_w_in, fox_b_f, fox_q_gain, fox_k_gain, fox_w_out, mla_w_in, mla_q_norm, mla_kv_norm, mla_w_uq, mla_w_ukv, mla_q_gain, mla_k_gain, mla_w_out, mlp_w1, mlp_w2, loss_target, m_mix_norm, m_mlp_norm, m_sb_w_in, m_sb_w_out, m_fox_w_in, m_fox_b_f, m_fox_q_gain, m_fox_k_gain, m_fox_w_out, m_mla_w_in, m_mla_q_norm, m_mla_kv_norm, m_mla_w_uq, m_mla_w_ukv, m_mla_q_gain, m_mla_k_gain, m_mla_w_out, m_mlp_w1, m_mlp_w2, v_mix_norm, v_mlp_norm, v_sb_w_in, v_sb_w_out, v_fox_w_in, v_fox_b_f, v_fox_q_gain, v_fox_k_gain, v_fox_w_out, v_mla_w_in, v_mla_q_norm, v_mla_kv_norm, v_mla_w_uq, v_mla_w_ukv, v_mla_q_gain, v_mla_k_gain, v_mla_w_out, v_mlp_w1, v_mlp_w2):
    S, D = x.shape[1], x.shape[2]
    nh = D // HEAD_DIM
    W = nh * HEAD_DIM
    depth = mix_norm.shape[0]
    q_rank, kv_rank = mla_w_uq.shape[1], mla_w_ukv.shape[1]
    n_fox_in = 3 * W + nh
    fox_pad = -(-n_fox_in // LANE) * LANE
    n_down = q_rank + kv_rank + MLA_ROPE
    down_pad = q_rank + kv_rank + LANE
    tr = _pick_rows(S, 512)
    tk = _pick_rows(S, AT_TK)

    ax, ay, ac = lax.axis_index("x"), lax.axis_index("y"), lax.axis_index("c")
    chip = 2 * ax + ay
    c_arr = jnp.reshape(ac, (1,)).astype(jnp.int32)
    chip_arr = jnp.reshape(chip, (1,)).astype(jnp.int32)

    def bf(a):
        return a.astype(BF16)

    n_small_in = q_rank + kv_rank
    rows_in = -(-n_small_in // (8 * LANE)) * 8
    placed = jnp.zeros((rows_in * LANE,), F32)
    placed = lax.dynamic_update_slice(placed, mla_q_norm[0], (chip * mla_q_norm.shape[1],))
    placed = lax.dynamic_update_slice(placed, mla_kv_norm[0], (q_rank + chip * mla_kv_norm.shape[1],))
    placed = placed * (ac == 0).astype(F32)
    norms = all_sum_small("gather_norms", placed.reshape(rows_in, LANE)).reshape(-1)
    q_norm_full = norms[:q_rank].reshape(1, q_rank)
    kv_norm_full = norms[q_rank:q_rank + kv_rank].reshape(1, kv_rank)

    groups = [(i, part) for i in range(depth) for part in ("mix", "mlp")]
    G = {}

    casts = {}

    def group_shards(i, part):
        if (i, part) in casts:
            return casts[(i, part)]
        kind, j = i % N_MIXERS, i // N_MIXERS
        zero = G[groups[0]]["token"] if groups[0] in G else None

        def cast(w):
            return bf(w if zero is None else w + zero.reshape((1,) * w.ndim))

        if part == "mlp":
            names, ws = ["w1", "w2"], [mlp_w1[i], mlp_w2[i]]
        elif kind == 0:
            names, ws = ["w_in", "w_out"], [sb_w_in[j], sb_w_out[j]]
        elif kind == 1:
            names, ws = ["w_in", "w_out"], [fox_w_in[j], fox_w_out[j]]
        else:
            names, ws = ["w_in", "w_uq", "w_ukv", "w_out"], [mla_w_in[j], mla_w_uq[j], mla_w_ukv[j], mla_w_out[j]]
        casts[(i, part)] = names, [cast(w) for w in ws]
        return casts[(i, part)]

    def cross_chips_start(key, after):
        names, shards = group_shards(*key)
        handle, token = chips_start(f"gather_start_{key[1]}_{key[0]}", "gather", shards,
                                    [(N_CHIPS,) + s.shape for s in shards], after)
        G[key] = dict(names=names, ici=handle, token=token)
        return token

    def cross_cores_start(key, after):
        g = G[key]
        g["lands"], g["shards"] = chips_wait(f"gather_wait_{key[1]}_{key[0]}", g["ici"], after)
        g["d2d"], token = chips_start(f"pass_start_{key[1]}_{key[0]}", "pass", g["lands"], [])
        return token

    def group_ready(key, after):
        g = G[key]
        _, bufs = chips_wait(f"pass_wait_{key[1]}_{key[0]}", g["d2d"], after)
        return dict(zip(g["names"], [lax.dynamic_update_slice(b, s[None], (chip, 0, 0)) for b, s in zip(bufs, g["shards"])]))

    def before_part(s, after):
        key = groups[s]
        if key not in G:
            cross_chips_start(key, norms)
            after = [b for k in groups[1:] for b in group_shards(*k)[1]]
        if "d2d" not in G[key]:
            cross_cores_start(key, after)
        L = group_ready(key, after)
        token = jnp.zeros((1, 1), F32)
        if s + 1 < len(groups) and groups[s + 1] in G:
            token = token + cross_cores_start(groups[s + 1], after)
        last = L[G[key]["names"][0]]
        n_ahead = GATHER_AHEAD - (s == 0)
        for ahead in groups[s + 1:s + 1 + n_ahead]:
            if ahead not in G:
                token = token + cross_chips_start(ahead, last)
            last = G[ahead]["token"]
        return L, token

    def rows_stacked(w):
        return w.reshape(w.shape[0] * w.shape[1], w.shape[2])

    xc = x[0]
    saved = []
    layers = []
    tables = None
    for i in range(depth):
        kind, j = i % N_MIXERS, i // N_MIXERS
        L, token = before_part(2 * i, xc if i else None)
        g1 = mix_norm[i:i + 1] + token
        hb = rmsnorm_fwd(f"norm1_{i}", xc, g1)
        st = dict(x=xc, hb=hb)
        if kind == 0:
            qkv = mm_nn(f"sb_proj_{i}", hb, L["w_in"], BF16)[0]
            o, o_f32 = sb_fwd(f"sb_attn_{i}", qkv, nh)
            st.update(qkv=qkv, o_f32=o_f32)
        elif kind == 1:
            w_in = _pad_cols(jnp.concatenate([L["w_in"][s] for s in range(N_CHIPS)], axis=1), fox_pad)
            proj = mm_nn(f"fox_proj_{i}", hb, w_in, F32)[0]
            gq, gk = fox_q_gain[j:j + 1], fox_k_gain[j:j + 1]
            qk_rows = [(proj, HEAD_DIM, lambda h: h), (proj, HEAD_DIM, lambda h: nh + h)]
            qh, kh = rowwise_fwd(f"fox_qk_{i}", _fox_qk_fn, qk_rows, [gq, gk],
                                 [(W, HEAD_DIM, lambda h: h, BF16)] * 2, tr, nh)
            b_pad = _pad_cols(fox_b_f[j:j + 1], LANE)
            gate_rows = [(proj, LANE, lambda h: 3 * nh)]
            logf = rowwise_fwd(f"fox_gate_{i}", _fox_gate_fn, gate_rows, [b_pad], [(LANE, LANE, _c0, F32)], tr)[0]
            cf = seq_cumsum(f"fox_cf_{i}", logf, False)[:, :nh].T
            cf_col, cf_row = cf.reshape(nh, S, 1), cf.reshape(nh, S // tk, tk)
            scale = 1.0 / math.sqrt(HEAD_DIM)
            o, o_f32, lse = attn_fwd(f"fox_attn_{i}", qh, kh, proj, nh, HEAD_DIM, 0, 0, 2 * nh, 1, scale, cf_col, cf_row)
            st.update(w_in=w_in, proj=proj, qk_rows=qk_rows, gq=gq, gk=gk, qh=qh, kh=kh, b_pad=b_pad, gate_rows=gate_rows,
                      cf_col=cf_col, cf_row=cf_row, lse=lse, scale=scale, o_f32=o_f32)
        else:
            w_in = _pad_cols(rows_stacked(L["w_in"]), down_pad)
            down = mm_nn(f"mla_down_{i}", hb, w_in, F32)[0]
            if tables is None:
                tables = rope_tables(positions.reshape(S, 1))
            cos1, sin1, cos2, sin2 = tables
            pre1_rows = [(down[:, :q_rank], q_rank, _c0), (down[:, q_rank:q_rank + kv_rank], kv_rank, _c0),
                         (down[:, q_rank + kv_rank:], LANE, _c0), (cos1, LANE, _c0), (sin1, LANE, _c0)]
            pre1_fn = _mla_pre1_fn(q_rank, kv_rank)
            c_q, c_kv, k_rope = rowwise_fwd(
                f"mla_pre1_{i}", pre1_fn, pre1_rows, [q_norm_full, kv_norm_full],
                [(q_rank, q_rank, _c0, BF16), (kv_rank, kv_rank, _c0, BF16), (LANE, LANE, _c0, F32)], tr)
            qfull = mm_nn(f"mla_uq_{i}", c_q, L["w_uq"], F32)[0]
            kv = mm_nn(f"mla_ukv_{i}", c_kv, L["w_ukv"], F32)[0]
            qpad = jnp.pad(qfull.reshape(S, nh, MLA_QK), ((0, 0), (0, 0), (0, MLA_QK_PAD - MLA_QK))).reshape(S, nh * MLA_QK_PAD)
            gq, gk = _pad_cols(mla_q_gain[j:j + 1], MLA_QK_PAD), _pad_cols(mla_k_gain[j:j + 1], MLA_QK_PAD)
            pre2_rows = [(qpad, MLA_QK_PAD, lambda h: h), (kv, MLA_NOPE, lambda h: 2 * h), (k_rope, LANE, _c0),
                         (cos2, MLA_QK_PAD, _c0), (sin2, MLA_QK_PAD, _c0)]
            qh, kh = rowwise_fwd(f"mla_pre2_{i}", _mla_pre2_fn, pre2_rows, [gq, gk],
                                 [(nh * MLA_QK_PAD, MLA_QK_PAD, lambda h: h, BF16)] * 2, tr, nh)
            scale = 1.0 / math.sqrt(MLA_QK)
            o, o_f32, lse = attn_fwd(f"mla_attn_{i}", qh, kh, kv, nh, MLA_QK_PAD, 0, 0, 1, 2, scale)
            st.update(w_in=w_in, pre1_rows=pre1_rows, pre1_fn=pre1_fn, c_q=c_q, c_kv=c_kv, pre2_rows=pre2_rows, gq=gq, gk=gk,
                      qh=qh, kh=kh, kv=kv, lse=lse, scale=scale, o_f32=o_f32)
        x1 = mm_nn(f"mix_out_{i}", o, rows_stacked(L["w_out"]), F32, epilogue=_add, extras=(xc,))[0]
        L_mlp, token = before_part(2 * i + 1, x1)
        L.update(L_mlp)
        layers.append(L)
        g2 = mlp_norm[i:i + 1] + token
        h2 = rmsnorm_fwd(f"norm2_{i}", x1, g2)
        a, r = mm_nn(f"mlp_up_{i}", h2, L["w1"], None, epilogue=_relu2, out_dtypes=[BF16, BF16])
        xc = mm_nn(f"mlp_down_{i}", a, rows_stacked(L["w2"]), F32, epilogue=_add, extras=(x1,))[0]
        st.update(o=o, x1=x1, h2=h2, a=a, r=r, g1=g1, g2=g2)
        saved.append(st)

    loss_local, dx, dxb = loss_head(xc, loss_target[0])
    loss = lax.psum(loss_local[0, 0], ("x", "y", "c"))

    small = {}

    def stack_rows(g):
        return g.reshape(N_CHIPS, g.shape[0] // N_CHIPS, g.shape[1])

    full = [dict() for _ in range(depth)]
    sums = []

    def sums_advance(after):
        token = jnp.zeros((1, 1), F32)
        for e in sums:
            tag = e["tag"]
            if "back" in e:
                others, halves = chips_wait(f"sendh_wait_{tag}", e.pop("back"), after)
                full[e["layer"]].update(zip(e["names"], zip(halves, others)))
                e["done"] = True
            elif "chips" in e and e["hold"]:
                e["hold"] -= 1
            elif "chips" in e:
                got, parts = chips_wait(f"xchg_wait_{tag}", e.pop("chips"), after)
                halves = [sum_slabs(f"sum4_{tag}_{t}", g, p, chip_arr) for t, (g, p) in enumerate(zip(got, parts))]
                e["back"], tok = chips_start(f"sendh_start_{tag}", "whole", halves, [h.shape for h in halves])
                token = token + tok
            elif "pair" in e:
                theirs, grads = chips_wait(f"swap_wait_{tag}", e.pop("pair"), after)
                parts = [add_half(f"addh_{tag}_{t}", g, o, c_arr) for t, (g, o) in enumerate(zip(grads, theirs))]
                e["chips"], tok = chips_start(f"xchg_start_{tag}", "scatter", parts, [p.shape for p in parts])
                token = token + tok
            elif not e.get("done"):
                grads = e.pop("grads")
                e["pair"], tok = chips_start(f"swap_start_{tag}", "swap", grads,
                                             [(N_CHIPS, g.shape[1] // 2, g.shape[2]) for g in grads])
                token = token + tok
        sums[:] = [e for e in sums if not e.get("done")]
        return token

    def sums_add(i, part, gr, after):
        sums.append(dict(layer=i, tag=f"{part}_{i}", names=list(gr), grads=list(gr.values()), hold=int(part == "mlp")))
        return sums_advance(after)

    dep = None
    for i in reversed(range(depth)):
        kind, j = i % N_MIXERS, i // N_MIXERS
        L, st = layers[i], saved[i]
        gr = {}
        gr["w2"] = stack_rows(mm_tn(f"mlp_dw2_{i}", st["a"], dxb, BF16, False, dep=dep))
        du = mm_nt(f"mlp_du_{i}", dxb, rows_stacked(L["w2"]), BF16, epilogue=_times_2r, extras=(st["r"],), dep=dep)
        gr["w1"] = mm_tn(f"mlp_dw1_{i}", st["h2"], du, BF16, True)
        dh2 = mm_nt(f"mlp_dh_{i}", du, L["w1"], F32)
        dx1, dx1b, dg2 = rmsnorm_bwd(f"norm2_bwd_{i}", st["x1"], st["g2"], dh2, dx)
        small[("mlp_norm", i)] = dg2
        dep = sums_add(i, "mlp", gr, dx1)
        gr = {}
        gr["w_out"] = stack_rows(mm_tn(f"mix_dwout_{i}", st["o"], dx1b, BF16, False, dep=dep))
        do = mm_nt(f"mix_do_{i}", dx1b, rows_stacked(L["w_out"]), BF16, dep=dep)
        if kind == 0:
            dqkv = sb_bwd(f"sb_attn_bwd_{i}", st["qkv"], st["o_f32"], do, nh)
            gr["w_in"] = mm_tn(f"sb_dwin_{i}", st["hb"], dqkv, BF16, True)
            dh = mm_nt(f"sb_dh_{i}", dqkv, L["w_in"], F32)
        elif kind == 1:
            dqh, dkh, dv, dcf = attn_bwd(f"fox_attn_bwd_{i}", st["qh"], st["kh"], st["proj"], st["o_f32"], do, st["lse"], nh,
                                         HEAD_DIM, 0, 0, 2 * nh, 1, st["scale"], st["cf_col"], st["cf_row"])
            dcf_s = _pad_cols(dcf.reshape(nh, S).T, LANE)
            dlogf = seq_cumsum(f"fox_dcf_{i}", dcf_s, True)
            dgate, db = rowwise_bwd(f"fox_gate_bwd_{i}", _fox_gate_fn, st["gate_rows"], [st["b_pad"]], [(dlogf, LANE, _c0)],
                                    [(LANE, LANE, _c0, BF16, False)], tr)
            dq, dk, dgq, dgk = rowwise_bwd(
                f"fox_qk_bwd_{i}", _fox_qk_fn, st["qk_rows"], [st["gq"], st["gk"]],
                [(dqh, HEAD_DIM, lambda h: h), (dkh, HEAD_DIM, lambda h: h)], [(W, HEAD_DIM, lambda h: h, BF16, False)] * 2, tr, nh)
            small[("fox_b_f", j)] = db[:, :nh]
            small[("fox_q_gain", j)] = dgq
            small[("fox_k_gain", j)] = dgk
            dproj = jnp.concatenate([dq, dk, bf(dv), dgate], axis=1)
            dw = mm_tn(f"fox_dwin_{i}", st["hb"], dproj, BF16, False)
            n4 = n_fox_in // N_CHIPS
            gr["w_in"] = jnp.stack([dw[:, s * n4:(s + 1) * n4] for s in range(N_CHIPS)])
            dh = mm_nt(f"fox_dh_{i}", dproj, st["w_in"], F32)
        else:
            dqh, dkh, dv = attn_bwd(f"mla_attn_bwd_{i}", st["qh"], st["kh"], st["kv"], st["o_f32"], do, st["lse"], nh,
                                    MLA_QK_PAD, 0, 0, 1, 2, st["scale"])
            dqpad, dkn, dkr, dgq, dgk = rowwise_bwd(
                f"mla_pre2_bwd_{i}", _mla_pre2_fn, st["pre2_rows"], [st["gq"], st["gk"]],
                [(dqh, MLA_QK_PAD, lambda h: h), (dkh, MLA_QK_PAD, lambda h: h)],
                [(nh * MLA_QK_PAD, MLA_QK_PAD, lambda h: h, BF16, False), (W, MLA_NOPE, lambda h: h, BF16, False),
                 (LANE, LANE, _c0, F32, True)], tr, nh, n_diff=3)
            small[("mla_q_gain", j)] = dgq[:, :MLA_QK]
            small[("mla_k_gain", j)] = dgk[:, :MLA_QK]
            dqfull = dqpad.reshape(S, nh, MLA_QK_PAD)[:, :, :MLA_QK].reshape(S, nh * MLA_QK)
            dkv = jnp.stack([dkn.reshape(S, nh, MLA_NOPE), bf(dv).reshape(S, nh, MLA_V)], axis=2).reshape(S, nh * (MLA_NOPE + MLA_V))
            gr["w_uq"] = mm_tn(f"mla_dwuq_{i}", st["c_q"], dqfull, BF16, True)
            dc_q = mm_nt(f"mla_dcq_{i}", dqfull, L["w_uq"], F32)
            gr["w_ukv"] = mm_tn(f"mla_dwukv_{i}", st["c_kv"], dkv, BF16, True)
            dc_kv = mm_nt(f"mla_dckv_{i}", dkv, L["w_ukv"], F32)
            d1, d2, d3, dqn, dkvn = rowwise_bwd(
                f"mla_pre1_bwd_{i}", st["pre1_fn"], st["pre1_rows"], [q_norm_full, kv_norm_full],
                [(dc_q, q_rank, _c0), (dc_kv, kv_rank, _c0), (dkr, LANE, _c0)],
                [(q_rank, q_rank, _c0, BF16, False), (kv_rank, kv_rank, _c0, BF16, False), (LANE, LANE, _c0, BF16, False)],
                tr, n_diff=3)
            small[("mla_q_norm", j)] = dqn
            small[("mla_kv_norm", j)] = dkvn
            ddown = jnp.concatenate([d1, d2, d3], axis=1)
            dw = mm_tn(f"mla_dwin_{i}", st["hb"], ddown, BF16, False)
            gr["w_in"] = stack_rows(dw[:, :n_down])
            dh = mm_nt(f"mla_dh_{i}", ddown, st["w_in"], F32)
        dx, dxb, dg1 = rmsnorm_bwd(f"norm1_bwd_{i}", st["x"], st["g1"], dh, dx1)
        small[("mix_norm", i)] = dg1
        dep = sums_add(i, "mix", gr, dx)

    def holders(kind_of):
        return [i for i in range(depth) if kind_of is None or i % N_MIXERS == kind_of]

    tensors = {
        "sb_w_in": (sb_w_in, m_sb_w_in, v_sb_w_in, "w_in", holders(0)),
        "sb_w_out": (sb_w_out, m_sb_w_out, v_sb_w_out, "w_out", holders(0)),
        "fox_w_in": (fox_w_in, m_fox_w_in, v_fox_w_in, "w_in", holders(1)),
        "fox_w_out": (fox_w_out, m_fox_w_out, v_fox_w_out, "w_out", holders(1)),
        "mla_w_in": (mla_w_in, m_mla_w_in, v_mla_w_in, "w_in", holders(2)),
        "mla_w_uq": (mla_w_uq, m_mla_w_uq, v_mla_w_uq, "w_uq", holders(2)),
        "mla_w_ukv": (mla_w_ukv, m_mla_w_ukv, v_mla_w_ukv, "w_ukv", holders(2)),
        "mla_w_out": (mla_w_out, m_mla_w_out, v_mla_w_out, "w_out", holders(2)),
        "mlp_w1": (mlp_w1, m_mlp_w1, v_mlp_w1, "w1", holders(None)),
        "mlp_w2": (mlp_w2, m_mlp_w2, v_mlp_w2, "w2", holders(None)),
    }
    updated = {n: None for n in tensors}

    def update_ready(dep):
        outs = []
        for n, (w, m, v, key, held) in tensors.items():
            for l in reversed(range(len(held))):
                if (n, l) not in applied and key in full[held[l]]:
                    mine, other = full[held[l]][key]
                    updated[n] = adamw_big(f"adamw_{n}_{l}", w, m, v, mine, other, c_arr, l, updated[n], dep)
                    applied.add((n, l))
                    outs.append(updated[n][3][l, :8, :LANE])
        return outs

    applied = set()
    after = dx
    while sums:
        outs = update_ready(dep)
        after = outs if outs else after
        dep = sums_advance(after)

    last = update_ready(dep)

    keys = list(small)
    flat = jnp.concatenate([small[k].reshape(-1) for k in keys])
    rows_g = -(-flat.shape[0] // (8 * LANE)) * 8
    flat = jnp.pad(flat, (0, rows_g * LANE - flat.shape[0]))
    tail = last or (after if isinstance(after, list) else [after])
    summed = all_sum_small("sum_small", flat.reshape(rows_g, LANE), dep=tail[0]).reshape(-1)
    sg, off = {}, 0
    for k in keys:
        n = small[k].size
        sg[k] = summed[off:off + n].reshape(small[k].shape)
        off += n

    def update_big(name):
        return list(updated[name])

    def update_small(name, w, m, v, g):
        return [g] + list(adamw(f"adamw_{name}", w, g, m, v))

    def small_rows(name, count):
        return jnp.concatenate([sg[(name, l)] for l in range(count)], axis=0)

    def my_part(g, n):
        return lax.dynamic_slice(g, (0, chip * n), (g.shape[0], n))

    res = {
        "mix_norm": update_small("mix_norm", mix_norm, m_mix_norm, v_mix_norm, small_rows("mix_norm", depth)),
        "mlp_norm": update_small("mlp_norm", mlp_norm, m_mlp_norm, v_mlp_norm, small_rows("mlp_norm", depth)),
        "sb_w_in": update_big("sb_w_in"),
        "sb_w_out": update_big("sb_w_out"),
        "fox_w_in": update_big("fox_w_in"),
        "fox_b_f": update_small("fox_b_f", fox_b_f, m_fox_b_f, v_fox_b_f, small_rows("fox_b_f", fox_b_f.shape[0])),
        "fox_q_gain": update_small("fox_q_gain", fox_q_gain, m_fox_q_gain, v_fox_q_gain, small_rows("fox_q_gain", fox_q_gain.shape[0])),
        "fox_k_gain": update_small("fox_k_gain", fox_k_gain, m_fox_k_gain, v_fox_k_gain, small_rows("fox_k_gain", fox_k_gain.shape[0])),
        "fox_w_out": update_big("fox_w_out"),
        "mla_w_in": update_big("mla_w_in"),
        "mla_q_norm": update_small("mla_q_norm", mla_q_norm, m_mla_q_norm, v_mla_q_norm,
                                   my_part(small_rows("mla_q_norm", mla_q_norm.shape[0]), mla_q_norm.shape[1])),
        "mla_kv_norm": update_small("mla_kv_norm", mla_kv_norm, m_mla_kv_norm, v_mla_kv_norm,
                                    my_part(small_rows("mla_kv_norm", mla_kv_norm.shape[0]), mla_kv_norm.shape[1])),
        "mla_w_uq": update_big("mla_w_uq"),
        "mla_w_ukv": update_big("mla_w_ukv"),
        "mla_q_gain": update_small("mla_q_gain", mla_q_gain, m_mla_q_gain, v_mla_q_gain, small_rows("mla_q_gain", mla_q_gain.shape[0])),
        "mla_k_gain": update_small("mla_k_gain", mla_k_gain, m_mla_k_gain, v_mla_k_gain, small_rows("mla_k_gain", mla_k_gain.shape[0])),
        "mla_w_out": update_big("mla_w_out"),
        "mlp_w1": update_big("mlp_w1"),
        "mlp_w2": update_big("mlp_w2"),
    }
    order = ["mix_norm", "mlp_norm", "sb_w_in", "sb_w_out", "fox_w_in", "fox_b_f", "fox_q_gain", "fox_k_gain", "fox_w_out",
             "mla_w_in", "mla_q_norm", "mla_kv_norm", "mla_w_uq", "mla_w_ukv", "mla_q_gain", "mla_k_gain", "mla_w_out",
             "mlp_w1", "mlp_w2"]
    outs = [loss, dx.reshape(x.shape)]
    for k in range(4):
        outs += [res[n][k] for n in order]
    return tuple(outs)
```

```python
import functools
import math

import numpy as np
import jax
import jax.numpy as jnp
from jax import lax
from jax.experimental import pallas as pl
from jax.experimental.pallas import tpu as pltpu

F32 = jnp.float32
BF16 = jnp.bfloat16
MESH = pl.DeviceIdType.MESH

EPS = 1e-6
HEAD_DIM = 128
MLA_NOPE = 128
MLA_ROPE = 64
MLA_V = 128
MLA_QK = MLA_NOPE + MLA_ROPE
MLA_QK_PAD = 256
ROPE_THETA = 10000.0
N_MIXERS = 3
ADAM_LR = 0.001
ADAM_B1 = 0.9
ADAM_B2 = 0.999
ADAM_EPS = 1e-08
ADAM_WD = 0.01
ADAM_STEP = 10

LANE = 128
N_CHIPS = 4
GATHER_AHEAD = 3
VMEM_LIMIT = 48 * 1024 * 1024
NEG = -1e30


def _cp(sem):
    return pltpu.CompilerParams(dimension_semantics=sem, vmem_limit_bytes=VMEM_LIMIT)


def _pick(dim, cap):
    best = None
    b = LANE
    while b <= min(dim, cap):
        if dim % b == 0:
            best = b
        b += LANE
    return best if best is not None else dim


def _pick_rows(dim, cap):
    b = min(dim, cap)
    while dim % b:
        b -= 8
    return b


def _matmul(name, a, b, a_blk, a_map, b_blk, b_map, dn, grid, acc_shape, outs, extras=(), epilogue=None, dep=None):
    nk = grid[2]
    n_ex, n_out = len(extras), len(outs)
    n_dep = 0 if dep is None else 1

    def body(*refs):
        a_ref, b_ref = refs[0], refs[1]
        ex_refs = refs[2:2 + n_ex]
        out_refs = refs[2 + n_ex + n_dep:2 + n_ex + n_dep + n_out]
        acc = refs[-1]
        k = pl.program_id(2)

        @pl.when(k == 0)
        def _():
            acc[...] = jnp.zeros_like(acc)

        acc[...] += lax.dot_general(a_ref[...], b_ref[...], dn, preferred_element_type=F32)

        @pl.when(k == nk - 1)
        def _():
            res = acc[...]
            vals = epilogue(res, *[e[...] for e in ex_refs]) if epilogue is not None else (res,)
            for o, v in zip(out_refs, vals):
                o[...] = v.astype(o.dtype)

    in_specs = [pl.BlockSpec(a_blk, a_map), pl.BlockSpec(b_blk, b_map)]
    in_specs += [pl.BlockSpec(blk, lambda i, j, k, m=m: m(i, j)) for (_, blk, m) in extras]
    in_specs += [pl.BlockSpec(memory_space=pl.ANY)] * n_dep
    out_specs = [pl.BlockSpec(blk, lambda i, j, k, m=m: m(i, j)) for (_, _, blk, m) in outs]
    out_shape = [jax.ShapeDtypeStruct(s, d) for (s, d, _, _) in outs]
    res = pl.pallas_call(
        body, name=name, grid=grid, in_specs=in_specs, out_specs=out_specs, out_shape=out_shape,
        scratch_shapes=[pltpu.VMEM(acc_shape, F32)],
        compiler_params=_cp(("parallel", "parallel", "arbitrary")),
    )(a, b, *[e[0] for e in extras], *([] if dep is None else [dep]))
    return res


BM, BN, BK = 1024, 1024, 2048


def mm_nn(name, a, w, out_dtype, epilogue=None, extras=(), n_out=1, out_dtypes=None):
    M, K = a.shape
    stacked = w.ndim == 3
    n4 = w.shape[-1]
    N = n4 * (N_CHIPS if stacked else 1)
    bm, bn, bk = _pick_rows(M, BM), _pick(n4, BN), _pick(K, BK)
    nb = n4 // bn
    if stacked:
        b_blk, b_map = (None, bk, bn), (lambda i, j, k: (j // nb, k, j % nb))
    else:
        b_blk, b_map = (bk, bn), (lambda i, j, k: (k, j))
    dts = out_dtypes if out_dtypes is not None else [out_dtype] * n_out
    outs = [((M, N), d, (bm, bn), lambda i, j: (i, j)) for d in dts]
    exs = [(e, (bm, bn), lambda i, j: (i, j)) for e in extras]
    return _matmul(name, a, w, (bm, bk), lambda i, j, k: (i, k), b_blk, b_map,
                   (((1,), (0,)), ((), ())), (M // bm, N // bn, K // bk), (bm, bn), outs, exs, epilogue)


def mm_nt(name, dy, w, out_dtype, epilogue=None, extras=(), dep=None):
    M, N = dy.shape
    stacked = w.ndim == 3
    K, n4 = w.shape[-2], w.shape[-1]
    bm, bn, bk = _pick_rows(M, BM), _pick(K, BN), _pick(n4, BK)
    nb = n4 // bk
    if stacked:
        b_blk, b_map = (None, bn, bk), (lambda i, j, k: (k // nb, j, k % nb))
    else:
        b_blk, b_map = (bn, bk), (lambda i, j, k: (j, k))
    outs = [((M, K), out_dtype, (bm, bn), lambda i, j: (i, j))]
    exs = [(e, (bm, bn), lambda i, j: (i, j)) for e in extras]
    return _matmul(name, dy, w, (bm, bk), lambda i, j, k: (i, k), b_blk, b_map,
                   (((1,), (1,)), ((), ())), (M // bm, K // bn, N // bk), (bm, bn), outs, exs, epilogue, dep)[0]


def mm_tn(name, a, dy, out_dtype, stacked, dep=None):
    M, K = a.shape
    N = dy.shape[1]
    n4 = N // N_CHIPS if stacked else N
    bm, bn, bk = _pick(K, BM), _pick(n4, BN), _pick_rows(M, BK)
    nb = n4 // bn
    if stacked:
        outs = [((N_CHIPS, K, n4), out_dtype, (None, bm, bn), lambda i, j: (j // nb, i, j % nb))]
    else:
        outs = [((K, N), out_dtype, (bm, bn), lambda i, j: (i, j))]
    return _matmul(name, a, dy, (bk, bm), lambda i, j, k: (k, i), (bk, bn), lambda i, j, k: (k, j),
                   (((0,), (0,)), ((), ())), (K // bm, N // bn, M // bk), (bm, bn), outs, dep=dep)[0]


def _split3(x):
    hi = x.astype(BF16)
    r = x - hi.astype(F32)
    mid = r.astype(BF16)
    lo = (r - mid.astype(F32)).astype(BF16)
    return hi, mid, lo


def _exact_dot(x, m, n):
    out = None
    for p in _split3(x)[:n]:
        d = jnp.dot(p, m, preferred_element_type=F32)
        out = d if out is None else out + d
    return out


def _row_spec(tr, width, cmap):
    return pl.BlockSpec((tr, width), lambda i, h: (i, cmap(h)))


def _full_spec(p):
    return pl.BlockSpec(p.shape, lambda i, h: (0,) * p.ndim)


def rowwise_fwd(name, fn, rows, params, outs, tr, nh=1):
    S = rows[0][0].shape[0]
    nr, npar = len(rows), len(params)

    def body(*refs):
        vals = fn(*[r[...].astype(F32) for r in refs[:nr]], *[p[...] for p in refs[nr:nr + npar]])
        for o, v in zip(refs[nr + npar:], vals):
            o[...] = v.astype(o.dtype)

    return pl.pallas_call(
        body, name=name, grid=(S // tr, nh),
        in_specs=[_row_spec(tr, w, cm) for (_, w, cm) in rows] + [_full_spec(p) for p in params],
        out_specs=[_row_spec(tr, w, cm) for (_, w, cm, _) in outs],
        out_shape=[jax.ShapeDtypeStruct((S, c), d) for (c, _, _, d) in outs],
        compiler_params=_cp(("parallel", "arbitrary")),
    )(*[r[0] for r in rows], *params)


def rowwise_bwd(name, fn, rows, params, cts, grads, tr, nh=1, n_diff=None, add=None):
    S = rows[0][0].shape[0]
    nr, npar, nct = len(rows), len(params), len(cts)
    n_diff = nr if n_diff is None else n_diff
    n_add = 1 if add is not None else 0

    def body(*refs):
        row_refs = refs[:nr]
        par_refs = refs[nr:nr + npar]
        ct_refs = refs[nr + npar:nr + npar + nct]
        add_refs = refs[nr + npar + nct:nr + npar + nct + n_add]
        o = nr + npar + nct + n_add
        g_refs = refs[o:o + n_diff]
        cp_refs = refs[o + n_diff:o + n_diff + n_add]
        pg_refs = refs[o + n_diff + n_add:]
        i, h = pl.program_id(0), pl.program_id(1)
        rv = [r[...].astype(F32) for r in row_refs]
        pv = [p[...] for p in par_refs]
        aux = rv[n_diff:]

        def f(*dp):
            return fn(*dp[:n_diff], *aux, *dp[n_diff:])

        _, vjp = jax.vjp(f, *rv[:n_diff], *pv)
        gs = vjp(tuple(c[...].astype(F32) for c in ct_refs))
        for n, (g_ref, (_, _, _, _, over)) in enumerate(zip(g_refs, grads)):
            g = gs[n]
            if n == 0 and add is not None:
                g = g + add_refs[0][...]
                cp_refs[0][...] = g.astype(BF16)
            if over:
                @pl.when(h == 0)
                def _(g_ref=g_ref):
                    g_ref[...] = jnp.zeros_like(g_ref)
                g_ref[...] += g.astype(g_ref.dtype)
            else:
                g_ref[...] = g.astype(g_ref.dtype)
        for pg_ref, g in zip(pg_refs, gs[n_diff:]):
            @pl.when((i == 0) & (h == 0))
            def _(pg_ref=pg_ref):
                pg_ref[...] = jnp.zeros_like(pg_ref)
            pg_ref[...] += g

    in_specs = [_row_spec(tr, w, cm) for (_, w, cm) in rows] + [_full_spec(p) for p in params]
    in_specs += [_row_spec(tr, w, cm) for (_, w, cm) in cts]
    operands = [r[0] for r in rows] + list(params) + [c[0] for c in cts]
    out_specs = [_row_spec(tr, w, cm) for (_, w, cm, _, _) in grads]
    out_shape = [jax.ShapeDtypeStruct((S, c), d) for (c, _, _, d, _) in grads]
    if add is not None:
        in_specs.append(_row_spec(tr, add[1], add[2]))
        operands.append(add[0])
        out_specs.append(_row_spec(tr, add[1], add[2]))
        out_shape.append(jax.ShapeDtypeStruct(add[0].shape, BF16))
    out_specs += [_full_spec(p) for p in params]
    out_shape += [jax.ShapeDtypeStruct(p.shape, F32) for p in params]
    return pl.pallas_call(
        body, name=name, grid=(S // tr, nh), in_specs=in_specs, out_specs=out_specs, out_shape=out_shape,
        compiler_params=_cp(("arbitrary", "arbitrary")),
    )(*operands)


def _c0(h):
    return 0


def _rms(x, g, n):
    return x * lax.rsqrt(jnp.sum(x * x, axis=-1, keepdims=True) * (1.0 / n) + EPS) * g


def _rmsnorm_fn(d):
    def fn(x, g):
        return (_rms(x, g, d),)
    return fn


def rmsnorm_fwd(name, x, g):
    S, D = x.shape
    return rowwise_fwd(name, _rmsnorm_fn(D), [(x, D, _c0)], [g], [(D, D, _c0, BF16)], _pick_rows(S, 256))[0]


def rmsnorm_bwd(name, x, g, dh, dres):
    S, D = x.shape
    return rowwise_bwd(name, _rmsnorm_fn(D), [(x, D, _c0)], [g], [(dh, D, _c0)], [(D, D, _c0, F32, False)],
                       _pick_rows(S, 256), add=(dres, D, _c0))


def loss_head(y, t):
    S, D = y.shape
    tr = _pick_rows(S, 256)

    def body(y_ref, t_ref, l_ref, d_ref, db_ref):
        @pl.when(pl.program_id(0) == 0)
        def _():
            l_ref[...] = jnp.zeros_like(l_ref)
        e = y_ref[...] - t_ref[...]
        l_ref[...] += 0.5 * jnp.sum(jnp.sum(e * e, axis=1, keepdims=True), axis=0, keepdims=True) * (1.0 / D)
        d = e * (1.0 / D)
        d_ref[...] = d
        db_ref[...] = d.astype(BF16)

    row = pl.BlockSpec((tr, D), lambda i: (i, 0))
    return pl.pallas_call(
        body, name="loss_head", grid=(S // tr,), in_specs=[row, row],
        out_specs=[pl.BlockSpec((1, 1), lambda i: (0, 0)), row, row],
        out_shape=[jax.ShapeDtypeStruct((1, 1), F32), jax.ShapeDtypeStruct((S, D), F32), jax.ShapeDtypeStruct((S, D), BF16)],
        compiler_params=_cp(("arbitrary",)),
    )(y, t)


def _dot_nt(a, b):
    return lax.dot_general(a, b, (((1,), (1,)), ((), ())), preferred_element_type=F32)


def _dot_tn(a, b):
    return lax.dot_general(a, b, (((0,), (0,)), ((), ())), preferred_element_type=F32)


AT_TQ, AT_TK = 512, 256


def attn_fwd(name, q, k, v, nh, dq, qoff, koff, voff, vstep, scale, cf_col=None, cf_row=None):
    S = q.shape[0]
    tq = _pick_rows(S, AT_TQ)
    tk = _pick_rows(S, AT_TK)
    nq = S // tq
    bias = cf_col is not None

    def body(*refs):
        if bias:
            q_ref, k_ref, v_ref, cfc_ref, cfr_ref, o_ref, of_ref, lse_ref = refs
        else:
            q_ref, k_ref, v_ref, o_ref, of_ref, lse_ref = refs
        i = pl.program_id(1)
        qv = q_ref[...].astype(BF16)
        row = i * tq + lax.broadcasted_iota(jnp.int32, (tq, tk), 0)
        coli = lax.broadcasted_iota(jnp.int32, (tq, tk), 1)
        nkb = ((i + 1) * tq + tk - 1) // tk

        def logits(j):
            return _dot_nt(qv, k_ref[pl.ds(pl.multiple_of(j * tk, tk), tk), :].astype(BF16)) * scale

        def step(j, carry):
            s, m, l, acc, acc_lo = carry
            s_next = logits(jnp.minimum(j + 1, nkb - 1))
            vs = v_ref[pl.ds(pl.multiple_of(j * tk, tk), tk), :].astype(BF16)
            if bias:
                s = s + cfc_ref[0] - cfr_ref[0, pl.ds(j, 1), :]
            s = jnp.where(j * tk + coli <= row, s, NEG)
            m_new = jnp.maximum(m, jnp.max(s, axis=1, keepdims=True))
            alpha = jnp.exp(m - m_new)
            p = jnp.exp(s - m_new)
            pb = p.astype(BF16)
            l = alpha * l + jnp.sum(p, axis=1, keepdims=True)
            acc = alpha * acc + jnp.dot(pb, vs, preferred_element_type=F32)
            acc_lo = alpha * acc_lo + jnp.dot((p - pb.astype(F32)).astype(BF16), vs, preferred_element_type=F32)
            return s_next, m_new, l, acc, acc_lo

        z = jnp.zeros((tq, HEAD_DIM), F32)
        _, m, l, acc, acc_lo = lax.fori_loop(
            0, nkb, step, (logits(0), jnp.full((tq, 1), NEG, F32), jnp.zeros((tq, 1), F32), z, z))
        o_ref[...] = (acc / l).astype(o_ref.dtype)
        of_ref[...] = (acc + acc_lo) / l
        lse_ref[0] = m + jnp.log(l)

    in_specs = [pl.BlockSpec((tq, dq), lambda h, i: (i, qoff + h)),
                pl.BlockSpec((S, dq), lambda h, i: (0, koff + h)),
                pl.BlockSpec((S, HEAD_DIM), lambda h, i: (0, voff + vstep * h))]
    operands = [q, k, v]
    if bias:
        in_specs += [pl.BlockSpec((1, tq, 1), lambda h, i: (h, i, 0)), pl.BlockSpec((1, S // tk, tk), lambda h, i: (h, 0, 0))]
        operands += [cf_col, cf_row]
    return pl.pallas_call(
        body, name=name, grid=(nh, nq), in_specs=in_specs,
        out_specs=[pl.BlockSpec((tq, HEAD_DIM), lambda h, i: (i, h)), pl.BlockSpec((tq, HEAD_DIM), lambda h, i: (i, h)),
                   pl.BlockSpec((1, tq, 1), lambda h, i: (h, i, 0))],
        out_shape=[jax.ShapeDtypeStruct((S, nh * HEAD_DIM), BF16), jax.ShapeDtypeStruct((S, nh * HEAD_DIM), F32),
                   jax.ShapeDtypeStruct((nh, S, 1), F32)],
        compiler_params=_cp(("parallel", "arbitrary")),
    )(*operands)


def attn_bwd(name, q, k, v, o, do, lse, nh, dq, qoff, koff, voff, vstep, scale, cf_col=None, cf_row=None):
    S = q.shape[0]
    tq = _pick_rows(S, AT_TQ)
    tk = _pick_rows(S, AT_TK)
    nq = S // tq
    bias = cf_col is not None

    def body(*refs):
        if bias:
            q_ref, k_ref, v_ref, o_ref, do_ref, lse_ref, cfc_ref, cfr_ref, dq_ref, dk_ref, dv_ref, dcf_ref = refs
        else:
            q_ref, k_ref, v_ref, o_ref, do_ref, lse_ref, dq_ref, dk_ref, dv_ref = refs
        dk_ref[...] = jnp.zeros_like(dk_ref)
        dv_ref[...] = jnp.zeros_like(dv_ref)
        if bias:
            dcf_ref[...] = jnp.zeros_like(dcf_ref)
        rowi = lax.broadcasted_iota(jnp.int32, (tq, tk), 0)
        coli = lax.broadcasted_iota(jnp.int32, (tq, tk), 1)

        def outer(i, _):
            roff = pl.multiple_of(i * tq, tq)
            qv = q_ref[pl.ds(roff, tq), :].astype(BF16)
            dov = do_ref[pl.ds(roff, tq), :]
            delta = jnp.sum(dov.astype(F32) * o_ref[pl.ds(roff, tq), :].astype(F32), axis=1, keepdims=True)
            lse = lse_ref[0, pl.ds(roff, tq), :]
            if bias:
                cq = cfc_ref[0, pl.ds(roff, tq), :]

            nkb = ((i + 1) * tq + tk - 1) // tk

            def products(j):
                off = pl.multiple_of(j * tk, tk)
                return (_dot_nt(qv, k_ref[pl.ds(off, tk), :].astype(BF16)) * scale,
                        _dot_nt(dov, v_ref[pl.ds(off, tk), :].astype(BF16)))

            def inner(j, carry):
                s, dp, dq_acc = carry
                s_next, dp_next = products(jnp.minimum(j + 1, nkb - 1))
                off = pl.multiple_of(j * tk, tk)
                ks = k_ref[pl.ds(off, tk), :].astype(BF16)
                if bias:
                    s = s + cq - cfr_ref[0, pl.ds(j, 1), :]
                p = jnp.where(j * tk + coli <= i * tq + rowi, jnp.exp(s - lse), 0.0)
                ds = p * (dp - delta)
                if bias:
                    dcf_ref[0, pl.ds(j, 1), :] -= jnp.sum(ds, axis=0, keepdims=True)
                dsb = (ds * scale).astype(BF16)
                dv_ref[pl.ds(off, tk), :] += _dot_tn(p.astype(BF16), dov)
                dk_ref[pl.ds(off, tk), :] += _dot_tn(dsb, qv)
                return s_next, dp_next, dq_acc + jnp.dot(dsb, ks, preferred_element_type=F32)

            dq_ref[pl.ds(roff, tq), :] = lax.fori_loop(0, nkb, inner, products(0) + (jnp.zeros((tq, dq), F32),))[2]
            return 0

        lax.fori_loop(0, nq, outer, 0)

    in_specs = [pl.BlockSpec((S, dq), lambda h: (0, qoff + h)),
                pl.BlockSpec((S, dq), lambda h: (0, koff + h)),
                pl.BlockSpec((S, HEAD_DIM), lambda h: (0, voff + vstep * h)),
                pl.BlockSpec((S, HEAD_DIM), lambda h: (0, h)),
                pl.BlockSpec((S, HEAD_DIM), lambda h: (0, h)),
                pl.BlockSpec((1, S, 1), lambda h: (h, 0, 0))]
    operands = [q, k, v, o, do, lse]
    out_specs = [pl.BlockSpec((S, dq), lambda h: (0, h)), pl.BlockSpec((S, dq), lambda h: (0, h)),
                 pl.BlockSpec((S, HEAD_DIM), lambda h: (0, h))]
    out_shape = [jax.ShapeDtypeStruct((S, nh * dq), F32), jax.ShapeDtypeStruct((S, nh * dq), F32),
                 jax.ShapeDtypeStruct((S, nh * HEAD_DIM), F32)]
    if bias:
        in_specs += [pl.BlockSpec((1, S, 1), lambda h: (h, 0, 0)), pl.BlockSpec((1, S // tk, tk), lambda h: (h, 0, 0))]
        operands += [cf_col, cf_row]
        out_specs.append(pl.BlockSpec((1, S // tk, tk), lambda h: (h, 0, 0)))
        out_shape.append(jax.ShapeDtypeStruct((nh, S // tk, tk), F32))
    return pl.pallas_call(
        body, name=name, grid=(nh,), in_specs=in_specs, out_specs=out_specs, out_shape=out_shape,
        compiler_params=_cp(("parallel",)),
    )(*operands)


SB_TQ, SB_TK = 512, 256


def _sb_block(z, strict, later_c, tri_after):
    lsn = jnp.minimum(-z, 0.0) - jnp.log(1.0 + jnp.exp(-jnp.abs(z)))
    lsp = lsn + z
    L = lsn if strict is None else jnp.where(strict, lsn, 0.0)
    later = _exact_dot(L, tri_after, 2) + later_c
    a = jnp.exp(lsp + later)
    return lsp, lsn, L, a if strict is None else jnp.where(strict, a, 0.0)


def _tri(n, after_strict):
    r = lax.broadcasted_iota(jnp.int32, (n, n), 0)
    c = lax.broadcasted_iota(jnp.int32, (n, n), 1)
    return jnp.where(r > c if after_strict else r >= c, 1.0, 0.0).astype(BF16)


def sb_fwd(name, qkv, nh):
    S = qkv.shape[0]
    tq = _pick_rows(S, SB_TQ)
    tk = _pick_rows(S, SB_TK)
    nq = S // tq
    scale = 1.0 / math.sqrt(HEAD_DIM)

    def body(q_ref, k_ref, v_ref, o_ref, of_ref):
        i = pl.program_id(1)
        qv = q_ref[...]
        tri = _tri(tk, True)
        row = i * tq + lax.broadcasted_iota(jnp.int32, (tq, tk), 0)
        coli = lax.broadcasted_iota(jnp.int32, (tq, tk), 1)

        def logits(j):
            return _dot_nt(qv, k_ref[pl.ds(pl.multiple_of(j * tk, tk), tk), :]) * scale

        nkb = ((i + 1) * tq + tk - 1) // tk

        def step(jj, carry, masked=True):
            z, later_c, acc, acc_lo = carry
            j = nkb - 1 - jj
            z_next = logits(jnp.maximum(j - 1, 0))
            vb = v_ref[pl.ds(pl.multiple_of(j * tk, tk), tk), :]
            _, _, L, a = _sb_block(z, (j * tk + coli < row) if masked else None, later_c, tri)
            ab = a.astype(BF16)
            acc = acc + jnp.dot(ab, vb, preferred_element_type=F32)
            acc_lo = acc_lo + jnp.dot((a - ab.astype(F32)).astype(BF16), vb, preferred_element_type=F32)
            return z_next, later_c + jnp.sum(L, axis=1, keepdims=True), acc, acc_lo

        zero = jnp.zeros((tq, HEAD_DIM), F32)
        _, _, acc, acc_lo = lax.fori_loop(0, nkb, step, (logits(nkb - 1), jnp.zeros((tq, 1), F32), zero, zero))
        o_ref[...] = acc.astype(o_ref.dtype)
        of_ref[...] = acc + acc_lo

    blk = pl.BlockSpec((tq, HEAD_DIM), lambda h, i: (i, h))
    return pl.pallas_call(
        body, name=name, grid=(nh, nq),
        in_specs=[blk,
                  pl.BlockSpec((S, HEAD_DIM), lambda h, i: (0, nh + h)),
                  pl.BlockSpec((S, HEAD_DIM), lambda h, i: (0, 2 * nh + h))],
        out_specs=[blk, blk],
        out_shape=[jax.ShapeDtypeStruct((S, nh * HEAD_DIM), BF16), jax.ShapeDtypeStruct((S, nh * HEAD_DIM), F32)],
        compiler_params=_cp(("parallel", "arbitrary")),
    )(qkv, qkv, qkv)


def sb_bwd(name, qkv, o, do, nh):
    S = qkv.shape[0]
    tq = _pick_rows(S, SB_TQ)
    tk = _pick_rows(S, SB_TK)
    nq = S // tq
    scale = 1.0 / math.sqrt(HEAD_DIM)

    def body(q_ref, k_ref, v_ref, o_ref, do_ref, dq_ref, dk_ref, dv_ref, dk_acc, dv_acc):
        dk_acc[...] = jnp.zeros_like(dk_acc)
        dv_acc[...] = jnp.zeros_like(dv_acc)
        tri = _tri(tk, True)
        tri_inc = _tri(tk, False)
        rowi = lax.broadcasted_iota(jnp.int32, (tq, tk), 0)
        coli = lax.broadcasted_iota(jnp.int32, (tq, tk), 1)

        def outer(i, _):
            roff = pl.multiple_of(i * tq, tq)
            qv = q_ref[pl.ds(roff, tq), :]
            dov = do_ref[pl.ds(roff, tq), :]
            dtot = jnp.sum(dov.astype(F32) * o_ref[pl.ds(roff, tq), :].astype(F32), axis=1, keepdims=True)
            row = i * tq + rowi

            def products(j):
                off = pl.multiple_of(j * tk, tk)
                return _dot_nt(qv, k_ref[pl.ds(off, tk), :]) * scale, _dot_nt(dov, v_ref[pl.ds(off, tk), :])

            nkb = ((i + 1) * tq + tk - 1) // tk

            def inner(jj, carry, masked=True):
                z, da, later_c, suf_c, dq_acc = carry
                j = nkb - 1 - jj
                z_next, da_next = products(jnp.maximum(j - 1, 0))
                off = pl.multiple_of(j * tk, tk)
                kb = k_ref[pl.ds(off, tk), :]
                strict = (j * tk + coli < row) if masked else None
                lsp, lsn, L, a = _sb_block(z, strict, later_c, tri)
                dl = da * a
                before = dtot - (suf_c + _exact_dot(dl, tri_inc, 3))
                dz = (dl * jnp.exp(lsn) - jnp.exp(lsp) * before) * scale
                if masked:
                    dz = jnp.where(strict, dz, 0.0)
                dzb = dz.astype(BF16)
                dv_acc[pl.ds(off, tk), :] += _dot_tn(a.astype(BF16), dov)
                dk_acc[pl.ds(off, tk), :] += _dot_tn(dzb, qv)
                return (z_next, da_next, later_c + jnp.sum(L, axis=1, keepdims=True), suf_c + jnp.sum(dl, axis=1, keepdims=True),
                        dq_acc + jnp.dot(dzb, kb, preferred_element_type=F32))

            z1 = jnp.zeros((tq, 1), F32)
            res = lax.fori_loop(0, nkb, inner, products(nkb - 1) + (z1, z1, jnp.zeros((tq, HEAD_DIM), F32)))
            dq_ref[pl.ds(roff, tq), :] = res[4].astype(dq_ref.dtype)
            return 0

        lax.fori_loop(0, nq, outer, 0)
        dk_ref[...] = dk_acc[...].astype(dk_ref.dtype)
        dv_ref[...] = dv_acc[...].astype(dv_ref.dtype)

    def col(off):
        return pl.BlockSpec((S, HEAD_DIM), lambda h: (0, off + h))

    dq, dk, dv = pl.pallas_call(
        body, name=name, grid=(nh,),
        in_specs=[col(0), col(nh), col(2 * nh), col(0), col(0)],
        out_specs=[col(0), col(0), col(0)],
        out_shape=[jax.ShapeDtypeStruct((S, nh * HEAD_DIM), BF16)] * 3,
        scratch_shapes=[pltpu.VMEM((S, HEAD_DIM), F32), pltpu.VMEM((S, HEAD_DIM), F32)],
        compiler_params=_cp(("parallel",)),
    )(qkv, qkv, qkv, o, do)
    return jnp.concatenate([dq, dk, dv], axis=1)


def seq_cumsum(name, x, reverse):
    S, W = x.shape

    tb = _pick_rows(S, 256)

    def body(x_ref, o_ref):
        parts = _split3(x_ref[...])
        c = lax.broadcasted_iota(jnp.int32, (tb, S), 1)
        for b in range(S // tb):
            r = b * tb + lax.broadcasted_iota(jnp.int32, (tb, S), 0)
            t = jnp.where(r <= c if reverse else r >= c, 1.0, 0.0).astype(BF16)
            out = None
            for p in parts:
                d = jnp.dot(t, p, preferred_element_type=F32)
                out = d if out is None else out + d
            o_ref[b * tb:(b + 1) * tb, :] = out

    return pl.pallas_call(body, name=name, out_shape=jax.ShapeDtypeStruct((S, W), F32),
                          compiler_params=pltpu.CompilerParams(vmem_limit_bytes=VMEM_LIMIT))(x)


def _adamw_math(w, g, m, v):
    c1 = 1.0 - ADAM_B1 ** ADAM_STEP
    c2 = 1.0 - ADAM_B2 ** ADAM_STEP
    nm = ADAM_B1 * m + (1.0 - ADAM_B1) * g
    nv = ADAM_B2 * v + (1.0 - ADAM_B2) * (g * g)
    return -ADAM_LR * ((nm / c1) / (jnp.sqrt(nv / c2) + ADAM_EPS) + ADAM_WD * w), nm, nv


def adamw(name, w, g, m, v):
    R, C = g.shape
    tr = _pick_rows(R, 256)

    def body(w_ref, g_ref, m_ref, v_ref, d_ref, nm_ref, nv_ref):
        d_ref[...], nm_ref[...], nv_ref[...] = _adamw_math(w_ref[...], g_ref[...], m_ref[...], v_ref[...])

    row = pl.BlockSpec((tr, C), lambda i: (i, 0))
    return pl.pallas_call(
        body, name=name, grid=(R // tr,), in_specs=[row] * 4, out_specs=[row] * 3,
        out_shape=[jax.ShapeDtypeStruct((R, C), F32)] * 3, compiler_params=_cp(("parallel",)),
    )(w, g, m, v)


def adamw_big(name, w, m, v, mine, other, c_arr, layer, prev, dep=None):
    L, R, C = w.shape
    r2 = R // 2
    tr = _pick_rows(r2, 256)
    nb = r2 // tr

    def body(c_ref, w_ref, m_ref, v_ref, mine_ref, other_ref, *rest):
        g_ref, d_ref, nm_ref, nv_ref = rest[-4:]
        g = jnp.where(pl.program_id(0) == c_ref[0], mine_ref[...], other_ref[...])
        g_ref[...] = g
        d_ref[...], nm_ref[...], nv_ref[...] = _adamw_math(w_ref[...], g, m_ref[...], v_ref[...])

    sel = pl.BlockSpec((None, tr, C), lambda hf, i, c: (layer, hf * nb + i, 0))
    in_specs = [sel, sel, sel,
                pl.BlockSpec((tr, C), lambda hf, i, c: (jnp.where(hf == c[0], i, 0), 0)),
                pl.BlockSpec((tr, C), lambda hf, i, c: (jnp.where(hf == c[0], 0, i), 0))]
    operands = [c_arr, w, m, v, mine, other]
    aliases = {}
    if prev is not None:
        in_specs += [ANY] * 4
        aliases = {len(operands) + k: k for k in range(4)}
        operands += list(prev)
    if dep is not None:
        in_specs += [ANY]
        operands += [dep]
    return pl.pallas_call(
        body, name=name,
        grid_spec=pltpu.PrefetchScalarGridSpec(num_scalar_prefetch=1, grid=(2, nb), in_specs=in_specs, out_specs=[sel] * 4),
        out_shape=[jax.ShapeDtypeStruct((L, R, C), F32)] * 4, input_output_aliases=aliases,
        compiler_params=_cp(("arbitrary", "arbitrary")),
    )(*operands)


ANY = pl.BlockSpec(memory_space=pl.ANY)


def _place():
    x, y, c = lax.axis_index("x"), lax.axis_index("y"), lax.axis_index("c")
    others = [(1 - x, y), (x, 1 - y), (1 - x, 1 - y)]
    return x, y, c, others


def add_half(name, g, other, c_arr):
    _, R, C = g.shape
    r2 = R // 2
    tr = _pick_rows(r2, 512)
    nb = r2 // tr

    def body(c_ref, g_ref, o_ref, out_ref):
        out_ref[...] = (g_ref[...].astype(F32) + o_ref[...].astype(F32)).astype(out_ref.dtype)

    return pl.pallas_call(
        body, name=name,
        grid_spec=pltpu.PrefetchScalarGridSpec(
            num_scalar_prefetch=1, grid=(N_CHIPS, nb),
            in_specs=[pl.BlockSpec((None, tr, C), lambda s, i, c: (s, c[0] * nb + i, 0)),
                      pl.BlockSpec((None, tr, C), lambda s, i, c: (s, i, 0))],
            out_specs=pl.BlockSpec((None, tr, C), lambda s, i, c: (s, i, 0))),
        out_shape=jax.ShapeDtypeStruct((N_CHIPS, r2, C), BF16),
        compiler_params=_cp(("parallel", "parallel")),
    )(c_arr, g, other)


def sum_slabs(name, got, parts, chip_arr):
    _, r2, C = parts.shape
    tr = _pick_rows(r2, 512)

    def body(chip_ref, *refs):
        own_ref, out_ref = refs[N_CHIPS], refs[N_CHIPS + 1]
        acc = None
        for s in range(N_CHIPS):
            v = jnp.where(chip_ref[0] == s, own_ref[...], refs[s][...]).astype(F32)
            acc = v if acc is None else acc + v
        out_ref[...] = acc

    def slab(s):
        return pl.BlockSpec((None, tr, C), lambda i, ch: (jnp.where(ch[0] == s, (s + 1) % N_CHIPS, s), i, 0))

    return pl.pallas_call(
        body, name=name,
        grid_spec=pltpu.PrefetchScalarGridSpec(
            num_scalar_prefetch=1, grid=(r2 // tr,),
            in_specs=[slab(s) for s in range(N_CHIPS)] + [pl.BlockSpec((None, tr, C), lambda i, ch: (ch[0], i, 0))],
            out_specs=pl.BlockSpec((tr, C), lambda i, ch: (i, 0))),
        out_shape=jax.ShapeDtypeStruct((r2, C), F32), compiler_params=_cp(("parallel",)),
    )(chip_arr, got, got, got, got, parts)


HBM = pl.BlockSpec(memory_space=pltpu.HBM)
SEM = pl.BlockSpec(memory_space=pltpu.SEMAPHORE)
EFFECT = pltpu.SideEffectType.DATAFLOW_SIDE_EFFECTING


def _in_hbm(a):
    return pltpu.with_memory_space_constraint(a, pltpu.HBM)


def _chip_copies(kind, shapes, src, land, send, recv, mine):
    x, y, c, others = _place()
    me = 2 * x + y
    cps = []

    def remote(s_ref, d_ref, k, to):
        cps.append(pltpu.make_async_remote_copy(src_ref=s_ref, dst_ref=d_ref, send_sem=send.at[k], recv_sem=recv.at[k],
                                                device_id=to, device_id_type=MESH))

    for t, shape in enumerate(shapes):
        if kind == "swap":
            r2 = shape[1] // 2
            remote(src[t].at[:, pl.ds((1 - c) * r2, r2), :], land[t], t, (x, y, 1 - c))
        elif kind == "whole":
            remote(src[t], land[t], t, (x, y, 1 - c))
        for j, (px, py) in enumerate(others if kind in ("gather", "scatter", "pass") else []):
            slot = me if mine else 2 * px + py
            if kind == "gather":
                r2 = shape[0] // 2
                rows = pl.ds(c * r2, r2)
                remote(src[t].at[rows, :], land[t].at[slot, rows, :], 3 * t + j, (px, py, c))
            elif kind == "scatter":
                remote(src[t].at[2 * px + py], land[t].at[slot], 3 * t + j, (px, py, c))
            else:
                r2 = shape[1] // 2
                rows = src[t].at[2 * px + py, pl.ds((c if mine else 1 - c) * r2, r2), :]
                remote(rows, rows, 3 * t + j, (x, y, 1 - c))
    return cps


def _n_copies(kind, n):
    return n if kind in ("swap", "whole") else 3 * n


def chips_start(name, kind, srcs, land_shapes, after=None):
    n, nl = len(srcs), len(land_shapes)
    shapes = [s.shape for s in srcs]
    n_buf = n + nl
    n_in = n_buf + (0 if after is None else 1)
    n_sem = _n_copies(kind, n)

    def body(*refs):
        src, land = refs[:n], refs[n:n_buf]
        send, recv = refs[n_in], refs[n_in + 1]
        token = refs[-1]
        for cp in _chip_copies(kind, shapes, src, land, send, recv, True):
            cp.start()
        token[...] = jnp.zeros_like(token)

    lands = [lax.empty(s, srcs[0].dtype) for s in land_shapes]
    out = pl.pallas_call(
        body, name=name,
        out_shape=(pltpu.SemaphoreType.DMA((n_sem,)), pltpu.SemaphoreType.DMA((n_sem,)))
        + tuple(pltpu.HBM(s.shape, s.dtype) for s in srcs) + tuple(pltpu.HBM(l.shape, l.dtype) for l in lands)
        + (jax.ShapeDtypeStruct((8, LANE), F32),),
        in_specs=(HBM,) * n_buf + (ANY,) * (n_in - n_buf),
        out_specs=(SEM, SEM) + (HBM,) * n_buf + (pl.BlockSpec(memory_space=pltpu.VMEM),),
        input_output_aliases={k: 2 + k for k in range(n_buf)},
        compiler_params=pltpu.CompilerParams(has_side_effects=EFFECT),
    )(*[_in_hbm(s) for s in srcs], *[_in_hbm(l) for l in lands], *([] if after is None else [after]))
    return (kind, shapes, out[0], out[1], out[2:2 + n], out[2 + n:2 + n_buf]), out[-1][:1, :1]


def chips_wait(name, handle, after):
    kind, shapes, send, recv, srcs, lands = handle
    n, n_buf = len(srcs), len(srcs) + len(lands)
    after = [a for a in (list(after) if isinstance(after, (list, tuple)) else [after]) if a is not None]

    def body(*refs):
        src, land = refs[:n], refs[n:n_buf]
        for cp in _chip_copies(kind, shapes, src, land, refs[n_buf], refs[n_buf + 1], True):
            cp.wait_send()
        for cp in _chip_copies(kind, shapes, src, land, refs[n_buf], refs[n_buf + 1], False):
            cp.wait_recv()

    out = pl.pallas_call(
        body, name=name,
        out_shape=tuple(pltpu.HBM(s.shape, s.dtype) for s in srcs) + tuple(pltpu.HBM(l.shape, l.dtype) for l in lands),
        in_specs=(HBM,) * n_buf + (SEM, SEM) + (ANY,) * len(after), out_specs=(HBM,) * n_buf,
        input_output_aliases={k: k for k in range(n_buf)},
        compiler_params=pltpu.CompilerParams(has_side_effects=EFFECT),
    )(*srcs, *lands, send, recv, *after)
    return list(out[n:]), list(out[:n])


def all_sum_small(name, v, dep=None):
    R = v.shape[0]
    n_dep = 0 if dep is None else 1

    def body(v_ref, *refs):
        out_ref, slots, send, recv = refs[n_dep:]
        x, y, c, _ = _place()
        me = 4 * x + 2 * y + c
        slots[me] = v_ref[...]
        cps = []
        for k in range(1, 8):
            dx, dy, dc = (k >> 2) & 1, (k >> 1) & 1, k & 1
            to = (x ^ dx, y ^ dy, c ^ dc)
            cps.append(pltpu.make_async_remote_copy(
                src_ref=v_ref, dst_ref=slots.at[me], send_sem=send.at[k - 1], recv_sem=recv.at[k - 1],
                device_id=to, device_id_type=MESH))
        for cp in cps:
            cp.start()
        for k in range(1, 8):
            dx, dy, dc = (k >> 2) & 1, (k >> 1) & 1, k & 1
            frm = 4 * (x ^ dx) + 2 * (y ^ dy) + (c ^ dc)
            pltpu.make_async_remote_copy(
                src_ref=v_ref, dst_ref=slots.at[frm], send_sem=send.at[k - 1], recv_sem=recv.at[k - 1],
                device_id=(x, y, c), device_id_type=MESH).wait_recv()
        for cp in cps:
            cp.wait_send()
        acc = slots[0]
        for d in range(1, 8):
            acc = acc + slots[d]
        out_ref[...] = acc

    vm = pl.BlockSpec(memory_space=pltpu.VMEM)
    return pl.pallas_call(
        body, name=name, in_specs=[vm] + [ANY] * n_dep, out_specs=vm, out_shape=jax.ShapeDtypeStruct((R, LANE), F32),
        scratch_shapes=[pltpu.VMEM((8, R, LANE), F32), pltpu.SemaphoreType.DMA((7,)), pltpu.SemaphoreType.DMA((7,))],
    )(v, *([] if dep is None else [dep]))


def _rope_mat(n, lo):
    half = MLA_ROPE // 2
    r = lax.broadcasted_iota(jnp.int32, (n, n), 0)
    c = lax.broadcasted_iota(jnp.int32, (n, n), 1)
    plus = (c >= lo + half) & (c < lo + 2 * half) & (r == c - half)
    minus = (c >= lo) & (c < lo + half) & (r == c + half)
    return (jnp.where(plus, 1.0, 0.0) - jnp.where(minus, 1.0, 0.0)).astype(BF16)


def _rope(v, cos, sin, lo):
    return v * cos + _exact_dot(v, _rope_mat(v.shape[-1], lo), 3) * sin


def rope_tables(pos):
    S = pos.shape[0]
    half = MLA_ROPE // 2
    inv = (np.float32(ROPE_THETA) ** (-np.arange(0, half, dtype=np.float32) * np.float32(2.0 / MLA_ROPE))).astype(np.float32)
    f1 = np.zeros((1, LANE), np.float32)
    f1[0, :MLA_ROPE] = np.tile(inv, 2)
    f2 = np.zeros((1, MLA_QK_PAD), np.float32)
    f2[0, MLA_NOPE:MLA_QK] = np.tile(inv, 2)
    tr = _pick_rows(S, 256)

    def body(p_ref, f1_ref, f2_ref, c1, s1, c2, s2):
        p = p_ref[...].astype(F32)
        a1 = p * f1_ref[...]
        a2 = p * f2_ref[...]
        c1[...] = jnp.cos(a1)
        s1[...] = jnp.sin(a1)
        c2[...] = jnp.cos(a2)
        s2[...] = jnp.sin(a2)

    def row(w):
        return pl.BlockSpec((tr, w), lambda i: (i, 0))

    def full(w):
        return pl.BlockSpec((1, w), lambda i: (0, 0))

    return pl.pallas_call(
        body, name="rope_tables", grid=(S // tr,), in_specs=[row(1), full(LANE), full(MLA_QK_PAD)],
        out_specs=[row(LANE), row(LANE), row(MLA_QK_PAD), row(MLA_QK_PAD)],
        out_shape=[jax.ShapeDtypeStruct((S, LANE), F32)] * 2 + [jax.ShapeDtypeStruct((S, MLA_QK_PAD), F32)] * 2,
        compiler_params=_cp(("parallel",)),
    )(pos, jnp.asarray(f1), jnp.asarray(f2))


def _log_sigmoid(z):
    return jnp.minimum(z, 0.0) - jnp.log(1.0 + jnp.exp(-jnp.abs(z)))


def _fox_qk_fn(q, k, gq, gk):
    return _rms(q, gq, HEAD_DIM), _rms(k, gk, HEAD_DIM)


def _fox_gate_fn(f, b):
    return (_log_sigmoid(f + b),)


def _mla_pre1_fn(q_rank, kv_rank):
    def fn(cq, ckv, kr, cos, sin, qn, kvn):
        return _rms(cq, qn, q_rank), _rms(ckv, kvn, kv_rank), _rope(kr, cos, sin, 0)
    return fn


def _mla_pre2_fn(qb, kn, kr, cos, sin, gq, gk):
    qh = _rms(_rope(qb, cos, sin, MLA_NOPE), gq, MLA_QK)
    kh = _rms(jnp.concatenate([kn, kr], axis=1), gk, MLA_QK)
    return qh, kh


def _pad_cols(a, n):
    return jnp.pad(a, ((0, 0), (0, n - a.shape[1])))


def _relu2(acc):
    r = jnp.maximum(acc, 0.0)
    return r * r, r


def _add(acc, res):
    return (acc + res,)


def _times_2r(acc, r):
    return (acc * (2.0 * r.astype(F32)),)


def kernel(x, positions, mix_norm, mlp_norm, sb_w_in, sb_w_out, fox_w_in, fox_b_f, fox_q_gain, fox_k_gain, fox_w_out, mla_w_in, mla_q_norm, mla_kv_norm, mla_w_uq, mla_w_ukv, mla_q_gain, mla_k_gain, mla_w_out, mlp_w1, mlp_w2, loss_target, m_mix_norm, m_mlp_norm, m_sb_w_in, m_sb_w_out, m_fox_w_in, m_fox_b_f, m_fox_q_gain, m_fox_k_gain, m_fox_w_out, m_mla_w_in, m_mla_q_norm, m_mla_kv_norm, m_mla_w_uq, m_mla_w_ukv, m_mla_q_gain, m_mla_k_gain, m_mla_w_out, m_mlp_w1, m_mlp_w2, v_mix_norm, v_mlp_norm, v_sb_w_in, v_sb_w_out, v_fox_w_in, v_fox_b_f, v_fox_q_gain, v_fox_k_gain, v_fox_w_out, v_mla_w_in, v_mla_q_norm, v_mla_kv_norm, v_mla_w_uq, v_mla_w_ukv, v_mla_q_gain, v_mla_k_gain, v_mla_w_out, v_mlp_w1, v_mlp_w2):
    S, D = x.shape[1], x.shape[2]
    nh = D // HEAD_DIM
    W = nh * HEAD_DIM
    depth = mix_norm.shape[0]
    q_rank, kv_rank = mla_w_uq.shape[1], mla_w_ukv.shape[1]
    n_fox_in = 3 * W + nh
    fox_pad = -(-n_fox_in // LANE) * LANE
    n_down = q_rank + kv_rank + MLA_ROPE
    down_pad = q_rank + kv_rank + LANE
    tr = _pick_rows(S, 1024)
    tk = _pick_rows(S, AT_TK)

    ax, ay, ac = lax.axis_index("x"), lax.axis_index("y"), lax.axis_index("c")
    chip = 2 * ax + ay
    c_arr = jnp.reshape(ac, (1,)).astype(jnp.int32)
    chip_arr = jnp.reshape(chip, (1,)).astype(jnp.int32)

    def bf(a):
        return a.astype(BF16)

    n_small_in = q_rank + kv_rank
    rows_in = -(-n_small_in // (8 * LANE)) * 8
    placed = jnp.zeros((rows_in * LANE,), F32)
    placed = lax.dynamic_update_slice(placed, mla_q_norm[0], (chip * mla_q_norm.shape[1],))
    placed = lax.dynamic_update_slice(placed, mla_kv_norm[0], (q_rank + chip * mla_kv_norm.shape[1],))
    placed = placed * (ac == 0).astype(F32)
    norms = all_sum_small("gather_norms", placed.reshape(rows_in, LANE)).reshape(-1)
    q_norm_full = norms[:q_rank].reshape(1, q_rank)
    kv_norm_full = norms[q_rank:q_rank + kv_rank].reshape(1, kv_rank)

    groups = [(i, part) for i in range(depth) for part in ("mix", "mlp")]
    G = {}

    casts = {}

    def group_shards(i, part):
        if (i, part) in casts:
            return casts[(i, part)]
        kind, j = i % N_MIXERS, i // N_MIXERS
        zero = G[groups[0]]["token"] if groups[0] in G else None

        def cast(w):
            return bf(w if zero is None else w + zero.reshape((1,) * w.ndim))

        if part == "mlp":
            names, ws = ["w1", "w2"], [mlp_w1[i], mlp_w2[i]]
        elif kind == 0:
            names, ws = ["w_in", "w_out"], [sb_w_in[j], sb_w_out[j]]
        elif kind == 1:
            names, ws = ["w_in", "w_out"], [fox_w_in[j], fox_w_out[j]]
        else:
            names, ws = ["w_in", "w_uq", "w_ukv", "w_out"], [mla_w_in[j], mla_w_uq[j], mla_w_ukv[j], mla_w_out[j]]
        casts[(i, part)] = names, [cast(w) for w in ws]
        return casts[(i, part)]

    def cross_chips_start(key, after):
        names, shards = group_shards(*key)
        handle, token = chips_start(f"gather_start_{key[1]}_{key[0]}", "gather", shards,
                                    [(N_CHIPS,) + s.shape for s in shards], after)
        G[key] = dict(names=names, ici=handle, token=token)
        return token

    def cross_cores_start(key, after):
        g = G[key]
        g["lands"], g["shards"] = chips_wait(f"gather_wait_{key[1]}_{key[0]}", g["ici"], after)
        g["d2d"], token = chips_start(f"pass_start_{key[1]}_{key[0]}", "pass", g["lands"], [])
        return token

    def group_ready(key, after):
        g = G[key]
        _, bufs = chips_wait(f"pass_wait_{key[1]}_{key[0]}", g["d2d"], after)
        return dict(zip(g["names"], [lax.dynamic_update_slice(b, s[None], (chip, 0, 0)) for b, s in zip(bufs, g["shards"])]))

    def before_part(s, after):
        key = groups[s]
        if key not in G:
            cross_chips_start(key, norms)
            after = [b for k in groups[1:] for b in group_shards(*k)[1]]
        if "d2d" not in G[key]:
            cross_cores_start(key, after)
        L = group_ready(key, after)
        token = jnp.zeros((1, 1), F32)
        if s + 1 < len(groups) and groups[s + 1] in G:
            token = token + cross_cores_start(groups[s + 1], after)
        last = L[G[key]["names"][0]]
        n_ahead = GATHER_AHEAD - (s == 0)
        for ahead in groups[s + 1:s + 1 + n_ahead]:
            if ahead not in G:
                token = token + cross_chips_start(ahead, last)
            last = G[ahead]["token"]
        return L, token

    def rows_stacked(w):
        return w.reshape(w.shape[0] * w.shape[1], w.shape[2])

    xc = x[0]
    saved = []
    layers = []
    tables = None
    for i in range(depth):
        kind, j = i % N_MIXERS, i // N_MIXERS
        L, token = before_part(2 * i, xc if i else None)
        g1 = mix_norm[i:i + 1] + token
        hb = rmsnorm_fwd(f"norm1_{i}", xc, g1)
        st = dict(x=xc, hb=hb)
        if kind == 0:
            qkv = mm_nn(f"sb_proj_{i}", hb, L["w_in"], BF16)[0]
            o, o_f32 = sb_fwd(f"sb_attn_{i}", qkv, nh)
            st.update(qkv=qkv, o_f32=o_f32)
        elif kind == 1:
            w_in = _pad_cols(jnp.concatenate([L["w_in"][s] for s in range(N_CHIPS)], axis=1), fox_pad)
            proj = mm_nn(f"fox_proj_{i}", hb, w_in, F32)[0]
            gq, gk = fox_q_gain[j:j + 1], fox_k_gain[j:j + 1]
            qk_rows = [(proj, HEAD_DIM, lambda h: h), (proj, HEAD_DIM, lambda h: nh + h)]
            qh, kh = rowwise_fwd(f"fox_qk_{i}", _fox_qk_fn, qk_rows, [gq, gk],
                                 [(W, HEAD_DIM, lambda h: h, BF16)] * 2, tr, nh)
            b_pad = _pad_cols(fox_b_f[j:j + 1], LANE)
            gate_rows = [(proj, LANE, lambda h: 3 * nh)]
            logf = rowwise_fwd(f"fox_gate_{i}", _fox_gate_fn, gate_rows, [b_pad], [(LANE, LANE, _c0, F32)], tr)[0]
            cf = seq_cumsum(f"fox_cf_{i}", logf, False)[:, :nh].T
            cf_col, cf_row = cf.reshape(nh, S, 1), cf.reshape(nh, S // tk, tk)
            scale = 1.0 / math.sqrt(HEAD_DIM)
            o, o_f32, lse = attn_fwd(f"fox_attn_{i}", qh, kh, proj, nh, HEAD_DIM, 0, 0, 2 * nh, 1, scale, cf_col, cf_row)
            st.update(w_in=w_in, proj=proj, qk_rows=qk_rows, gq=gq, gk=gk, qh=qh, kh=kh, b_pad=b_pad, gate_rows=gate_rows,
                      cf_col=cf_col, cf_row=cf_row, lse=lse, scale=scale, o_f32=o_f32)
        else:
            w_in = _pad_cols(rows_stacked(L["w_in"]), down_pad)
            down = mm_nn(f"mla_down_{i}", hb, w_in, F32)[0]
            if tables is None:
                tables = rope_tables(positions.reshape(S, 1))
            cos1, sin1, cos2, sin2 = tables
            pre1_rows = [(down[:, :q_rank], q_rank, _c0), (down[:, q_rank:q_rank + kv_rank], kv_rank, _c0),
                         (down[:, q_rank + kv_rank:], LANE, _c0), (cos1, LANE, _c0), (sin1, LANE, _c0)]
            pre1_fn = _mla_pre1_fn(q_rank, kv_rank)
            c_q, c_kv, k_rope = rowwise_fwd(
                f"mla_pre1_{i}", pre1_fn, pre1_rows, [q_norm_full, kv_norm_full],
                [(q_rank, q_rank, _c0, BF16), (kv_rank, kv_rank, _c0, BF16), (LANE, LANE, _c0, F32)], tr)
            qfull = mm_nn(f"mla_uq_{i}", c_q, L["w_uq"], F32)[0]
            kv = mm_nn(f"mla_ukv_{i}", c_kv, L["w_ukv"], F32)[0]
            qpad = jnp.pad(qfull.reshape(S, nh, MLA_QK), ((0, 0), (0, 0), (0, MLA_QK_PAD - MLA_QK))).reshape(S, nh * MLA_QK_PAD)
            gq, gk = _pad_cols(mla_q_gain[j:j + 1], MLA_QK_PAD), _pad_cols(mla_k_gain[j:j + 1], MLA_QK_PAD)
            pre2_rows = [(qpad, MLA_QK_PAD, lambda h: h), (kv, MLA_NOPE, lambda h: 2 * h), (k_rope, LANE, _c0),
                         (cos2, MLA_QK_PAD, _c0), (sin2, MLA_QK_PAD, _c0)]
            qh, kh = rowwise_fwd(f"mla_pre2_{i}", _mla_pre2_fn, pre2_rows, [gq, gk],
                                 [(nh * MLA_QK_PAD, MLA_QK_PAD, lambda h: h, BF16)] * 2, tr, nh)
            scale = 1.0 / math.sqrt(MLA_QK)
            o, o_f32, lse = attn_fwd(f"mla_attn_{i}", qh, kh, kv, nh, MLA_QK_PAD, 0, 0, 1, 2, scale)
            st.update(w_in=w_in, pre1_rows=pre1_rows, pre1_fn=pre1_fn, c_q=c_q, c_kv=c_kv, pre2_rows=pre2_rows, gq=gq, gk=gk,
                      qh=qh, kh=kh, kv=kv, lse=lse, scale=scale, o_f32=o_f32)
        x1 = mm_nn(f"mix_out_{i}", o, rows_stacked(L["w_out"]), F32, epilogue=_add, extras=(xc,))[0]
        L_mlp, token = before_part(2 * i + 1, x1)
        L.update(L_mlp)
        layers.append(L)
        g2 = mlp_norm[i:i + 1] + token
        h2 = rmsnorm_fwd(f"norm2_{i}", x1, g2)
        a, r = mm_nn(f"mlp_up_{i}", h2, L["w1"], None, epilogue=_relu2, out_dtypes=[BF16, BF16])
        xc = mm_nn(f"mlp_down_{i}", a, rows_stacked(L["w2"]), F32, epilogue=_add, extras=(x1,))[0]
        st.update(o=o, x1=x1, h2=h2, a=a, r=r, g1=g1, g2=g2)
        saved.append(st)

    loss_local, dx, dxb = loss_head(xc, loss_target[0])
    loss = lax.psum(loss_local[0, 0], ("x", "y", "c"))

    small = {}

    def stack_rows(g):
        return g.reshape(N_CHIPS, g.shape[0] // N_CHIPS, g.shape[1])

    full = [dict() for _ in range(depth)]
    sums = []

    def sums_advance(after):
        token = jnp.zeros((1, 1), F32)
        for e in sums:
            tag = e["tag"]
            if "back" in e:
                others, halves = chips_wait(f"sendh_wait_{tag}", e.pop("back"), after)
                full[e["layer"]].update(zip(e["names"], zip(halves, others)))
                e["done"] = True
            elif "chips" in e and e["hold"]:
                e["hold"] -= 1
            elif "chips" in e:
                got, parts = chips_wait(f"xchg_wait_{tag}", e.pop("chips"), after)
                halves = [sum_slabs(f"sum4_{tag}_{t}", g, p, chip_arr) for t, (g, p) in enumerate(zip(got, parts))]
                e["back"], tok = chips_start(f"sendh_start_{tag}", "whole", halves, [h.shape for h in halves])
                token = token + tok
            elif "pair" in e:
                theirs, grads = chips_wait(f"swap_wait_{tag}", e.pop("pair"), after)
                parts = [add_half(f"addh_{tag}_{t}", g, o, c_arr) for t, (g, o) in enumerate(zip(grads, theirs))]
                e["chips"], tok = chips_start(f"xchg_start_{tag}", "scatter", parts, [p.shape for p in parts])
                token = token + tok
            elif not e.get("done"):
                grads = e.pop("grads")
                e["pair"], tok = chips_start(f"swap_start_{tag}", "swap", grads,
                                             [(N_CHIPS, g.shape[1] // 2, g.shape[2]) for g in grads])
                token = token + tok
        sums[:] = [e for e in sums if not e.get("done")]
        return token

    def sums_add(i, part, gr, after):
        sums.append(dict(layer=i, tag=f"{part}_{i}", names=list(gr), grads=list(gr.values()), hold=int(part == "mlp")))
        return sums_advance(after)

    dep = None
    for i in reversed(range(depth)):
        kind, j = i % N_MIXERS, i // N_MIXERS
        L, st = layers[i], saved[i]
        gr = {}
        gr["w2"] = stack_rows(mm_tn(f"mlp_dw2_{i}", st["a"], dxb, BF16, False, dep=dep))
        du = mm_nt(f"mlp_du_{i}", dxb, rows_stacked(L["w2"]), BF16, epilogue=_times_2r, extras=(st["r"],), dep=dep)
        gr["w1"] = mm_tn(f"mlp_dw1_{i}", st["h2"], du, BF16, True)
        dh2 = mm_nt(f"mlp_dh_{i}", du, L["w1"], F32)
        dx1, dx1b, dg2 = rmsnorm_bwd(f"norm2_bwd_{i}", st["x1"], st["g2"], dh2, dx)
        small[("mlp_norm", i)] = dg2
        dep = sums_add(i, "mlp", gr, dx1)
        gr = {}
        gr["w_out"] = stack_rows(mm_tn(f"mix_dwout_{i}", st["o"], dx1b, BF16, False, dep=dep))
        do = mm_nt(f"mix_do_{i}", dx1b, rows_stacked(L["w_out"]), BF16, dep=dep)
        if kind == 0:
            dqkv = sb_bwd(f"sb_attn_bwd_{i}", st["qkv"], st["o_f32"], do, nh)
            gr["w_in"] = mm_tn(f"sb_dwin_{i}", st["hb"], dqkv, BF16, True)
            dh = mm_nt(f"sb_dh_{i}", dqkv, L["w_in"], F32)
        elif kind == 1:
            dqh, dkh, dv, dcf = attn_bwd(f"fox_attn_bwd_{i}", st["qh"], st["kh"], st["proj"], st["o_f32"], do, st["lse"], nh,
                                         HEAD_DIM, 0, 0, 2 * nh, 1, st["scale"], st["cf_col"], st["cf_row"])
            dcf_s = _pad_cols(dcf.reshape(nh, S).T, LANE)
            dlogf = seq_cumsum(f"fox_dcf_{i}", dcf_s, True)
            dgate, db = rowwise_bwd(f"fox_gate_bwd_{i}", _fox_gate_fn, st["gate_rows"], [st["b_pad"]], [(dlogf, LANE, _c0)],
                                    [(LANE, LANE, _c0, BF16, False)], tr)
            dq, dk, dgq, dgk = rowwise_bwd(
                f"fox_qk_bwd_{i}", _fox_qk_fn, st["qk_rows"], [st["gq"], st["gk"]],
                [(dqh, HEAD_DIM, lambda h: h), (dkh, HEAD_DIM, lambda h: h)], [(W, HEAD_DIM, lambda h: h, BF16, False)] * 2, tr, nh)
            small[("fox_b_f", j)] = db[:, :nh]
            small[("fox_q_gain", j)] = dgq
            small[("fox_k_gain", j)] = dgk
            dproj = jnp.concatenate([dq, dk, bf(dv), dgate], axis=1)
            dw = mm_tn(f"fox_dwin_{i}", st["hb"], dproj, BF16, False)
            n4 = n_fox_in // N_CHIPS
            gr["w_in"] = jnp.stack([dw[:, s * n4:(s + 1) * n4] for s in range(N_CHIPS)])
            dh = mm_nt(f"fox_dh_{i}", dproj, st["w_in"], F32)
        else:
            dqh, dkh, dv = attn_bwd(f"mla_attn_bwd_{i}", st["qh"], st["kh"], st["kv"], st["o_f32"], do, st["lse"], nh,
                                    MLA_QK_PAD, 0, 0, 1, 2, st["scale"])
            dqpad, dkn, dkr, dgq, dgk = rowwise_bwd(
                f"mla_pre2_bwd_{i}", _mla_pre2_fn, st["pre2_rows"], [st["gq"], st["gk"]],
                [(dqh, MLA_QK_PAD, lambda h: h), (dkh, MLA_QK_PAD, lambda h: h)],
                [(nh * MLA_QK_PAD, MLA_QK_PAD, lambda h: h, BF16, False), (W, MLA_NOPE, lambda h: h, BF16, False),
                 (LANE, LANE, _c0, F32, True)], tr, nh, n_diff=3)
            small[("mla_q_gain", j)] = dgq[:, :MLA_QK]
            small[("mla_k_gain", j)] = dgk[:, :MLA_QK]
            dqfull = dqpad.reshape(S, nh, MLA_QK_PAD)[:, :, :MLA_QK].reshape(S, nh * MLA_QK)
            dkv = jnp.stack([dkn.reshape(S, nh, MLA_NOPE), bf(dv).reshape(S, nh, MLA_V)], axis=2).reshape(S, nh * (MLA_NOPE + MLA_V))
            gr["w_uq"] = mm_tn(f"mla_dwuq_{i}", st["c_q"], dqfull, BF16, True)
            dc_q = mm_nt(f"mla_dcq_{i}", dqfull, L["w_uq"], F32)
            gr["w_ukv"] = mm_tn(f"mla_dwukv_{i}", st["c_kv"], dkv, BF16, True)
            dc_kv = mm_nt(f"mla_dckv_{i}", dkv, L["w_ukv"], F32)
            d1, d2, d3, dqn, dkvn = rowwise_bwd(
                f"mla_pre1_bwd_{i}", st["pre1_fn"], st["pre1_rows"], [q_norm_full, kv_norm_full],
                [(dc_q, q_rank, _c0), (dc_kv, kv_rank, _c0), (dkr, LANE, _c0)],
                [(q_rank, q_rank, _c0, BF16, False), (kv_rank, kv_rank, _c0, BF16, False), (LANE, LANE, _c0, BF16, False)],
                tr, n_diff=3)
            small[("mla_q_norm", j)] = dqn
            small[("mla_kv_norm", j)] = dkvn
            ddown = jnp.concatenate([d1, d2, d3], axis=1)
            dw = mm_tn(f"mla_dwin_{i}", st["hb"], ddown, BF16, False)
            gr["w_in"] = stack_rows(dw[:, :n_down])
            dh = mm_nt(f"mla_dh_{i}", ddown, st["w_in"], F32)
        dx, dxb, dg1 = rmsnorm_bwd(f"norm1_bwd_{i}", st["x"], st["g1"], dh, dx1)
        small[("mix_norm", i)] = dg1
        dep = sums_add(i, "mix", gr, dx)

    def holders(kind_of):
        return [i for i in range(depth) if kind_of is None or i % N_MIXERS == kind_of]

    tensors = {
        "sb_w_in": (sb_w_in, m_sb_w_in, v_sb_w_in, "w_in", holders(0)),
        "sb_w_out": (sb_w_out, m_sb_w_out, v_sb_w_out, "w_out", holders(0)),
        "fox_w_in": (fox_w_in, m_fox_w_in, v_fox_w_in, "w_in", holders(1)),
        "fox_w_out": (fox_w_out, m_fox_w_out, v_fox_w_out, "w_out", holders(1)),
        "mla_w_in": (mla_w_in, m_mla_w_in, v_mla_w_in, "w_in", holders(2)),
        "mla_w_uq": (mla_w_uq, m_mla_w_uq, v_mla_w_uq, "w_uq", holders(2)),
        "mla_w_ukv": (mla_w_ukv, m_mla_w_ukv, v_mla_w_ukv, "w_ukv", holders(2)),
        "mla_w_out": (mla_w_out, m_mla_w_out, v_mla_w_out, "w_out", holders(2)),
        "mlp_w1": (mlp_w1, m_mlp_w1, v_mlp_w1, "w1", holders(None)),
        "mlp_w2": (mlp_w2, m_mlp_w2, v_mlp_w2, "w2", holders(None)),
    }
    updated = {n: None for n in tensors}

    def update_ready(dep):
        outs = []
        for n, (w, m, v, key, held) in tensors.items():
            for l in reversed(range(len(held))):
                if (n, l) not in applied and key in full[held[l]]:
                    mine, other = full[held[l]][key]
                    updated[n] = adamw_big(f"adamw_{n}_{l}", w, m, v, mine, other, c_arr, l, updated[n], dep)
                    applied.add((n, l))
                    outs.append(updated[n][3][l, :8, :LANE])
        return outs

    applied = set()
    after = dx
    while sums:
        outs = update_ready(dep)
        after = outs if outs else after
        dep = sums_advance(after)

    last = update_ready(dep)

    keys = list(small)
    flat = jnp.concatenate([small[k].reshape(-1) for k in keys])
    rows_g = -(-flat.shape[0] // (8 * LANE)) * 8
    flat = jnp.pad(flat, (0, rows_g * LANE - flat.shape[0]))
    tail = last or (after if isinstance(after, list) else [after])
    summed = all_sum_small("sum_small", flat.reshape(rows_g, LANE), dep=tail[0]).reshape(-1)
    sg, off = {}, 0
    for k in keys:
        n = small[k].size
        sg[k] = summed[off:off + n].reshape(small[k].shape)
        off += n

    def update_big(name):
        return list(updated[name])

    def update_small(name, w, m, v, g):
        return [g] + list(adamw(f"adamw_{name}", w, g, m, v))

    def small_rows(name, count):
        return jnp.concatenate([sg[(name, l)] for l in range(count)], axis=0)

    def my_part(g, n):
        return lax.dynamic_slice(g, (0, chip * n), (g.shape[0], n))

    res = {
        "mix_norm": update_small("mix_norm", mix_norm, m_mix_norm, v_mix_norm, small_rows("mix_norm", depth)),
        "mlp_norm": update_small("mlp_norm", mlp_norm, m_mlp_norm, v_mlp_norm, small_rows("mlp_norm", depth)),
        "sb_w_in": update_big("sb_w_in"),
        "sb_w_out": update_big("sb_w_out"),
        "fox_w_in": update_big("fox_w_in"),
        "fox_b_f": update_small("fox_b_f", fox_b_f, m_fox_b_f, v_fox_b_f, small_rows("fox_b_f", fox_b_f.shape[0])),
        "fox_q_gain": update_small("fox_q_gain", fox_q_gain, m_fox_q_gain, v_fox_q_gain, small_rows("fox_q_gain", fox_q_gain.shape[0])),
        "fox_k_gain": update_small("fox_k_gain", fox_k_gain, m_fox_k_gain, v_fox_k_gain, small_rows("fox_k_gain", fox_k_gain.shape[0])),
        "fox_w_out": update_big("fox_w_out"),
        "mla_w_in": update_big("mla_w_in"),
        "mla_q_norm": update_small("mla_q_norm", mla_q_norm, m_mla_q_norm, v_mla_q_norm,
                                   my_part(small_rows("mla_q_norm", mla_q_norm.shape[0]), mla_q_norm.shape[1])),
        "mla_kv_norm": update_small("mla_kv_norm", mla_kv_norm, m_mla_kv_norm, v_mla_kv_norm,
                                    my_part(small_rows("mla_kv_norm", mla_kv_norm.shape[0]), mla_kv_norm.shape[1])),
        "mla_w_uq": update_big("mla_w_uq"),
        "mla_w_ukv": update_big("mla_w_ukv"),
        "mla_q_gain": update_small("mla_q_gain", mla_q_gain, m_mla_q_gain, v_mla_q_gain, small_rows("mla_q_gain", mla_q_gain.shape[0])),
        "mla_k_gain": update_small("mla_k_gain", mla_k_gain, m_mla_k_gain, v_mla_k_gain, small_rows("mla_k_gain", mla_k_gain.shape[0])),
        "mla_w_out": update_big("mla_w_out"),
        "mlp_w1": update_big("mlp_w1"),
        "mlp_w2": update_big("mlp_w2"),
    }
    order = ["mix_norm", "mlp_norm", "sb_w_in", "sb_w_out", "fox_w_in", "fox_b_f", "fox_q_gain", "fox_k_gain", "fox_w_out",
             "mla_w_in", "mla_q_norm", "mla_kv_norm", "mla_w_uq", "mla_w_ukv", "mla_q_gain", "mla_k_gain", "mla_w_out",
             "mlp_w1", "mlp_w2"]
    outs = [loss, dx.reshape(x.shape)]
    for k in range(4):
        outs += [res[n][k] for n in order]
    return tuple(outs)
```
